```python
import jax
import jax.numpy as jnp
from jax import lax
import numpy as np

D_MODEL = 2048
BATCH = 16
SEQ = 256
DEPTH = 2
DEC_BATCH = 8
DEC_SEQ = 4096
PAST_LEN = 256

GRID_W = 64
N_EVEN = (DEPTH + 1) // 2
N_ODD = DEPTH // 2
N_MOD = 6
EPS = 1e-6

A_WIDTH = D_MODEL // 2
A_GROUPS = 4
A_GDIM = A_WIDTH // A_GROUPS
A_CHUNK = 128
B_HEADS = 8
B_DK = (D_MODEL // 2) // B_HEADS
B_DV = (D_MODEL // 2) // B_HEADS
B_QK = B_HEADS * B_DK
B_WIDTH = B_HEADS * B_DV
B_CHUNK = 128
ROPE_BASE = 10000.0
C_WIDTH = D_MODEL // 2
C_KSIZE = 31
D_WIDTH = D_MODEL // 2
D_HEADS = 8
D_BLK = D_WIDTH // D_HEADS
D_KSIZE = 4
LRU_C = 8.0

EVEN_IN = 2 * A_WIDTH + 2 * B_QK + 2 * B_WIDTH
EVEN_OUT = A_WIDTH + B_WIDTH
ODD_IN = 2 * C_WIDTH + 2 * D_WIDTH
ODD_OUT = C_WIDTH + D_WIDTH

N_GROUPS = 4
EXPERTS_PER_GROUP = 8
N_EXPERTS = N_GROUPS * EXPERTS_PER_GROUP
TOP_K = 2
D_EXPERT = D_MODEL // 4
MOE_BLOCK = 128

kernel_name = 'hybrid_diffusion_trunk_step'


def rmsnorm(x, w):
    xf = x.astype(jnp.float32)
    y = xf * lax.rsqrt(jnp.mean(xf * xf, axis=-1, keepdims=True) + EPS)
    return (y * w.astype(jnp.float32)).astype(x.dtype)


def layernorm(x, w, b=None):
    xf = x.astype(jnp.float32)
    mu = jnp.mean(xf, axis=-1, keepdims=True)
    var = jnp.mean(jnp.square(xf - mu), axis=-1, keepdims=True)
    y = (xf - mu) * lax.rsqrt(var + EPS) * w.astype(jnp.float32)
    if b is not None:
        y = y + b.astype(jnp.float32)
    return y.astype(x.dtype)


def modulation(cond, w_mod_l, b_mod_l):
    m = jax.nn.silu(cond) @ w_mod_l + b_mod_l
    return [t[:, None, :] for t in jnp.split(m, N_MOD, axis=-1)]


def axial_rope(n_tokens, head_dim):
    rows = n_tokens // GRID_W
    row = jnp.repeat(jnp.arange(rows, dtype=jnp.float32), GRID_W)
    col = jnp.tile(jnp.arange(GRID_W, dtype=jnp.float32), rows)
    quarter = head_dim // 4
    inv_freq = ROPE_BASE ** (-jnp.arange(quarter, dtype=jnp.float32) / quarter)
    ang = jnp.concatenate([row[:, None] * inv_freq, col[:, None] * inv_freq], axis=-1)
    return jnp.cos(ang), jnp.sin(ang)


def apply_rope(x, cos, sin):
    half = x.shape[-1] // 2
    x1, x2 = x[..., :half], x[..., half:]
    return jnp.concatenate([x1 * cos - x2 * sin, x1 * sin + x2 * cos], axis=-1)


def dwconv(x, w, b, pad_left, pad_right):
    ch = x.shape[-1]
    y = lax.conv_general_dilated(x, w[:, None, :].astype(x.dtype), window_strides=(1,),
                                 padding=[(pad_left, pad_right)],
                                 dimension_numbers=('NWC', 'WIO', 'NWC'),
                                 feature_group_count=ch)
    return y + b.astype(x.dtype)


def chunk_gmlp(u, v, norm_w, w_s, b_s):
    bsz, n, _ = u.shape
    nc = n // A_CHUNK
    u = jax.nn.gelu(u)
    v = layernorm(jax.nn.gelu(v), norm_w)
    vc = v.reshape(bsz, nc, A_CHUNK, A_GROUPS, A_GDIM)
    z = jnp.einsum('gij,bcjgd->bcigd', w_s.astype(v.dtype), vc) + b_s.T.astype(v.dtype)[None, None, :, :, None]
    return u * z.reshape(bsz, n, A_WIDTH)


def retention_scan(q, k, v, log_gamma, s0):
    bsz, nh, n, _ = q.shape
    dv = v.shape[-1]
    nc = n // B_CHUNK
    pos = jnp.arange(B_CHUNK, dtype=jnp.float32)
    rel = pos[:, None] - pos[None, :]
    lg = log_gamma[:, None, None]
    decay_in = jnp.where(rel >= 0, jnp.exp(lg * jnp.maximum(rel, 0.0)), 0.0)
    decay_q = jnp.exp(log_gamma[:, None] * (pos + 1.0))
    decay_k = jnp.exp(log_gamma[:, None] * (B_CHUNK - 1.0 - pos))
    decay_c = jnp.exp(log_gamma * B_CHUNK)

    def chunks(t):
        return t.reshape(bsz, nh, nc, B_CHUNK, t.shape[-1]).transpose(2, 0, 1, 3, 4)

    def step(s, blk):
        qb, kb, vb = blk
        scores = jnp.einsum('bhid,bhjd->bhij', qb, kb) * decay_in
        o = (jnp.einsum('bhij,bhje->bhie', scores, vb)
             + jnp.einsum('bhid,bhde->bhie', qb, s) * decay_q[None, :, :, None])
        s = s * decay_c[None, :, None, None] + jnp.einsum('bhjd,bhje->bhde', kb * decay_k[None, :, :, None], vb)
        return s, o

    s_fin, o = lax.scan(step, s0, (chunks(q), chunks(k), chunks(v)))
    o = o.transpose(1, 2, 0, 3, 4).reshape(bsz, nh, n, dv)
    return o, s_fin


def _linear_combine(e1, e2):
    a1, b1 = e1
    a2, b2 = e2
    return a1 * a2, a2 * b1 + b2


def rglru_scan(x, w_a, b_a, w_x, b_x, lam, h0):
    bsz, n, width = x.shape
    xb = x.reshape(bsz, n, D_HEADS, D_BLK)
    r = jax.nn.sigmoid(jnp.einsum('bnhi,hij->bnhj', xb, w_a.astype(jnp.float32)).reshape(bsz, n, width)
                       + b_a.astype(jnp.float32))
    i = jax.nn.sigmoid(jnp.einsum('bnhi,hij->bnhj', xb, w_x.astype(jnp.float32)).reshape(bsz, n, width)
                       + b_x.astype(jnp.float32))
    log_a = -LRU_C * r * jax.nn.softplus(-lam.astype(jnp.float32))
    a = jnp.exp(log_a)
    b = jnp.sqrt(jnp.maximum(-jnp.expm1(2.0 * log_a), 0.0)) * (i * x)
    b = b.at[:, 0].add(a[:, 0] * h0)
    _, h = lax.associative_scan(_linear_combine, (a, b), axis=1)
    return h, h[:, -1]


def even_mixer(h, rope, s_fwd0, s_bwd0, w_in, w_out, gmlp_norm_w, gmlp_w_s, gmlp_b_s,
               decay_fwd, decay_bwd, gn_w):
    bsz, n, _ = h.shape
    p = h @ w_in
    u, v, q, k, vv, g = jnp.split(
        p, [A_WIDTH, 2 * A_WIDTH, 2 * A_WIDTH + B_QK, 2 * A_WIDTH + 2 * B_QK, 2 * A_WIDTH + 2 * B_QK + B_WIDTH],
        axis=-1)
    out_a = chunk_gmlp(u, v, gmlp_norm_w, gmlp_w_s, gmlp_b_s)

    def heads(t, dh):
        return t.reshape(bsz, n, B_HEADS, dh).transpose(0, 2, 1, 3).astype(jnp.float32)

    q = heads(q, B_DK)
    k = heads(k, B_DK) * (B_DK ** -0.5)
    vv = heads(vv, B_DV)
    if rope is not None:
        q = apply_rope(q, rope[0], rope[1])
        k = apply_rope(k, rope[0], rope[1])
    lg_f = jax.nn.log_sigmoid(decay_fwd.astype(jnp.float32))
    lg_b = jax.nn.log_sigmoid(decay_bwd.astype(jnp.float32))
    o_f, s_f = retention_scan(q, k, vv, lg_f, s_fwd0.astype(jnp.float32))
    o_b, s_b = retention_scan(q[:, :, ::-1], k[:, :, ::-1], vv[:, :, ::-1], lg_b, s_bwd0.astype(jnp.float32))
    o = layernorm(o_f + o_b[:, :, ::-1], gn_w.reshape(B_HEADS, 1, B_DV))
    o = o.transpose(0, 2, 1, 3).reshape(bsz, n, B_WIDTH).astype(h.dtype)
    out_b = jax.nn.silu(g) * o
    return jnp.concatenate([out_a, out_b], axis=-1) @ w_out, s_f, s_b


def odd_mixer(h, h_fwd0, h_bwd0, w_in, w_out, conv_w, conv_b, ln_w, ln_b, lru_conv_w, lru_conv_b,
              wa_f, ba_f, wx_f, bx_f, lam_f, wa_b, ba_b, wx_b, bx_b, lam_b):
    p = h @ w_in
    ca, cg, xd, gd = jnp.split(p, [C_WIDTH, 2 * C_WIDTH, 2 * C_WIDTH + D_WIDTH], axis=-1)
    yc = ca * jax.nn.sigmoid(cg)
    yc = dwconv(yc, conv_w, conv_b, C_KSIZE // 2, C_KSIZE // 2)
    yc = jax.nn.silu(layernorm(yc, ln_w, ln_b))
    xd = dwconv(xd, lru_conv_w, lru_conv_b, D_KSIZE // 2, D_KSIZE - 1 - D_KSIZE // 2).astype(jnp.float32)
    hf, sf = rglru_scan(xd, wa_f, ba_f, wx_f, bx_f, lam_f, h_fwd0.astype(jnp.float32))
    hb, sb = rglru_scan(xd[:, ::-1], wa_b, ba_b, wx_b, bx_b, lam_b, h_bwd0.astype(jnp.float32))
    yd = (hf + hb[:, ::-1]).astype(h.dtype) * jax.nn.gelu(gd)
    return jnp.concatenate([yc, yd], axis=-1) @ w_out, sf, sb


def moe_dispatch(x, eidx, wts, w_gate, w_up, w_down):
    t, d = x.shape
    tk = t * TOP_K
    e_flat = eidx.reshape(-1).astype(jnp.int32)
    tok = jnp.repeat(jnp.arange(t, dtype=jnp.int32), TOP_K)
    w_flat = wts.reshape(-1)
    order = jnp.argsort(e_flat * tk + jnp.arange(tk, dtype=jnp.int32))
    e_s, tok_s, w_s = e_flat[order], tok[order], w_flat[order]
    counts = jnp.zeros((N_EXPERTS,), jnp.int32).at[e_flat].add(1)
    starts = jnp.cumsum(counts) - counts
    padded = ((counts + MOE_BLOCK - 1) // MOE_BLOCK) * MOE_BLOCK
    pends = jnp.cumsum(padded)
    pstarts = pends - padded
    dest = pstarts[e_s] + (jnp.arange(tk, dtype=jnp.int32) - starts[e_s])
    n_blocks = (tk + MOE_BLOCK - 1) // MOE_BLOCK + N_EXPERTS
    n_rows = n_blocks * MOE_BLOCK
    buf_tok = jnp.full((n_rows,), t, jnp.int32).at[dest].set(tok_s)
    buf_w = jnp.zeros((n_rows,), x.dtype).at[dest].set(w_s)
    block_e = jnp.clip(jnp.searchsorted(pends, jnp.arange(n_blocks, dtype=jnp.int32) * MOE_BLOCK, side='right'),
                       0, N_EXPERTS - 1)
    x_pad = jnp.concatenate([x, jnp.zeros((1, d), x.dtype)], axis=0)

    def expert_block(args):
        toks, e = args
        xb = x_pad[toks]
        hid = jax.nn.silu(xb @ w_gate[e]) * (xb @ w_up[e])
        return hid @ w_down[e]

    out = lax.map(expert_block, (buf_tok.reshape(n_blocks, MOE_BLOCK), block_e))
    out = out.reshape(n_rows, d) * buf_w[:, None]
    return jnp.zeros((t + 1, d), x.dtype).at[buf_tok].add(out)[:t]


def hier_moe(h, rg_w, rg_b, re_w, re_b, w_gate, w_up, w_down):
    bsz, n, d = h.shape
    x = h.reshape(bsz * n, d)
    lg = (x @ rg_w + rg_b).astype(jnp.float32)
    p_g = jax.nn.softmax(lg, axis=-1)
    g_star = jnp.argmax(lg, axis=-1)
    p_sel = jnp.max(p_g, axis=-1, keepdims=True)
    le = (x @ re_w + re_b).astype(jnp.float32).reshape(-1, N_GROUPS, EXPERTS_PER_GROUP)
    le = jnp.einsum('tg,tge->te', jax.nn.one_hot(g_star, N_GROUPS, dtype=jnp.float32), le)
    top_v, top_i = lax.top_k(le, TOP_K)
    wts = p_sel * jax.nn.softmax(top_v, axis=-1)
    eidx = g_star[:, None] * EXPERTS_PER_GROUP + top_i
    y = moe_dispatch(x, eidx, wts.astype(x.dtype), w_gate, w_up, w_down)
    return y.reshape(bsz, n, d)


def trunk(x, cond, rope, ret_f0, ret_b0, lru_f0, lru_b0, prm):
    ret_f, ret_b, lru_f, lru_b = [], [], [], []
    for l in range(DEPTH):
        sh1, sc1, g1, sh2, sc2, g2 = modulation(cond, prm['w_mod'][l], prm['b_mod'][l])
        hm = rmsnorm(x, prm['norm_mix_w'][l]) * (1.0 + sc1) + sh1
        if l % 2 == 0:
            e = l // 2
            y, sf, sb = even_mixer(hm, rope, ret_f0[:, e], ret_b0[:, e], prm['w_in_even'][e], prm['w_out_even'][e],
                                   prm['gmlp_norm_w'][e], prm['gmlp_w_s'][e], prm['gmlp_b_s'][e],
                                   prm['ret_decay_fwd'][e], prm['ret_decay_bwd'][e], prm['ret_gn_w'][e])
            ret_f.append(sf)
            ret_b.append(sb)
        else:
            o = l // 2
            y, sf, sb = odd_mixer(hm, lru_f0[:, o], lru_b0[:, o], prm['w_in_odd'][o], prm['w_out_odd'][o],
                                  prm['conv_w'][o], prm['conv_b'][o], prm['conv_ln_w'][o], prm['conv_ln_b'][o],
                                  prm['lru_conv_w'][o], prm['lru_conv_b'][o],
                                  prm['lru_wa_fwd'][o], prm['lru_ba_fwd'][o], prm['lru_wx_fwd'][o],
                                  prm['lru_bx_fwd'][o], prm['lru_lam_fwd'][o],
                                  prm['lru_wa_bwd'][o], prm['lru_ba_bwd'][o], prm['lru_wx_bwd'][o],
                                  prm['lru_bx_bwd'][o], prm['lru_lam_bwd'][o])
            lru_f.append(sf)
            lru_b.append(sb)
        x = x + g1 * y
        hf = rmsnorm(x, prm['norm_ffn_w'][l]) * (1.0 + sc2) + sh2
        x = x + g2 * hier_moe(hf, prm['router_grp_w'][l], prm['router_grp_b'][l], prm['router_exp_w'][l],
                              prm['router_exp_b'][l], prm['moe_w_gate'][l], prm['moe_w_up'][l], prm['moe_w_down'][l])
    y = rmsnorm(x, prm['norm_final_w'])
    dt = x.dtype
    return (y, jnp.stack(ret_f, axis=1).astype(dt), jnp.stack(ret_b, axis=1).astype(dt),
            jnp.stack(lru_f, axis=1).astype(dt), jnp.stack(lru_b, axis=1).astype(dt))


def setup_inputs(seed: int = 0) -> dict:
    key = jax.random.key(seed)
    keys = jax.random.split(key, 64)
    counter = [0]

    def nk():
        counter[0] += 1
        return keys[counter[0] - 1]

    def nrm(shape, scale):
        return jax.random.normal(nk(), shape, jnp.float32) * scale

    def gain(shape):
        return 1.0 + nrm(shape, 0.02)

    def decay_logits():
        base = jnp.log(2.0 ** (5.0 + jnp.arange(B_HEADS, dtype=jnp.float32)) - 1.0)
        return base[None, :] + nrm((N_EVEN, B_HEADS), 0.1)

    def lru_lambda():
        u = jax.random.uniform(nk(), (N_ODD, D_WIDTH), jnp.float32, 0.9, 0.999)
        s = u ** (1.0 / LRU_C)
        return jnp.log(s) - jnp.log1p(-s)

    return {
        'x_prompt': nrm((BATCH, SEQ, D_MODEL), 1.0),
        'x_sample': nrm((DEC_BATCH, DEC_SEQ, D_MODEL), 1.0),
        'c': nrm((DEC_BATCH, D_MODEL), 1.0),
        'state_ret_fwd': nrm((DEC_BATCH, N_EVEN, B_HEADS, B_DK, B_DV), 0.1),
        'state_ret_bwd': nrm((DEC_BATCH, N_EVEN, B_HEADS, B_DK, B_DV), 0.1),
        'state_lru_fwd': nrm((DEC_BATCH, N_ODD, D_WIDTH), 0.5),
        'state_lru_bwd': nrm((DEC_BATCH, N_ODD, D_WIDTH), 0.5),
        'c_ctx': nrm((D_MODEL,), 1.0),
        'w_mod': nrm((DEPTH, D_MODEL, N_MOD * D_MODEL), 0.5 * D_MODEL ** -0.5),
        'b_mod': nrm((DEPTH, N_MOD * D_MODEL), 0.02),
        'norm_mix_w': gain((DEPTH, D_MODEL)),
        'norm_ffn_w': gain((DEPTH, D_MODEL)),
        'norm_final_w': gain((D_MODEL,)),
        'w_in_even': nrm((N_EVEN, D_MODEL, EVEN_IN), D_MODEL ** -0.5),
        'w_out_even': nrm((N_EVEN, EVEN_OUT, D_MODEL), EVEN_OUT ** -0.5),
        'gmlp_norm_w': gain((N_EVEN, A_WIDTH)),
        'gmlp_w_s': nrm((N_EVEN, A_GROUPS, A_CHUNK, A_CHUNK), A_CHUNK ** -0.5),
        'gmlp_b_s': gain((N_EVEN, A_GROUPS, A_CHUNK)),
        'ret_decay_fwd': decay_logits(),
        'ret_decay_bwd': decay_logits(),
        'ret_gn_w': gain((N_EVEN, B_WIDTH)),
        'w_in_odd': nrm((N_ODD, D_MODEL, ODD_IN), D_MODEL ** -0.5),
        'w_out_odd': nrm((N_ODD, ODD_OUT, D_MODEL), ODD_OUT ** -0.5),
        'conv_w': nrm((N_ODD, C_KSIZE, C_WIDTH), C_KSIZE ** -0.5),
        'conv_b': nrm((N_ODD, C_WIDTH), 0.02),
        'conv_ln_w': gain((N_ODD, C_WIDTH)),
        'conv_ln_b': nrm((N_ODD, C_WIDTH), 0.02),
        'lru_conv_w': nrm((N_ODD, D_KSIZE, D_WIDTH), D_KSIZE ** -0.5),
        'lru_conv_b': nrm((N_ODD, D_WIDTH), 0.02),
        'lru_wa_fwd': nrm((N_ODD, D_HEADS, D_BLK, D_BLK), D_BLK ** -0.5),
        'lru_ba_fwd': nrm((N_ODD, D_WIDTH), 0.02),
        'lru_wx_fwd': nrm((N_ODD, D_HEADS, D_BLK, D_BLK), D_BLK ** -0.5),
        'lru_bx_fwd': nrm((N_ODD, D_WIDTH), 0.02),
        'lru_lam_fwd': lru_lambda(),
        'lru_wa_bwd': nrm((N_ODD, D_HEADS, D_BLK, D_BLK), D_BLK ** -0.5),
        'lru_ba_bwd': nrm((N_ODD, D_WIDTH), 0.02),
        'lru_wx_bwd': nrm((N_ODD, D_HEADS, D_BLK, D_BLK), D_BLK ** -0.5),
        'lru_bx_bwd': nrm((N_ODD, D_WIDTH), 0.02),
        'lru_lam_bwd': lru_lambda(),
        'router_grp_w': nrm((DEPTH, D_MODEL, N_GROUPS), D_MODEL ** -0.5),
        'router_grp_b': nrm((DEPTH, N_GROUPS), 0.01),
        'router_exp_w': nrm((DEPTH, D_MODEL, N_EXPERTS), D_MODEL ** -0.5),
        'router_exp_b': nrm((DEPTH, N_EXPERTS), 0.01),
        'moe_w_gate': nrm((DEPTH, N_EXPERTS, D_MODEL, D_EXPERT), D_MODEL ** -0.5),
        'moe_w_up': nrm((DEPTH, N_EXPERTS, D_MODEL, D_EXPERT), D_MODEL ** -0.5),
        'moe_w_down': nrm((DEPTH, N_EXPERTS, D_EXPERT, D_MODEL), D_EXPERT ** -0.5),
    }


def reference(x_prompt, x_sample, c, state_ret_fwd, state_ret_bwd, state_lru_fwd, state_lru_bwd, c_ctx,
              w_mod, b_mod, norm_mix_w, norm_ffn_w, norm_final_w,
              w_in_even, w_out_even, gmlp_norm_w, gmlp_w_s, gmlp_b_s, ret_decay_fwd, ret_decay_bwd, ret_gn_w,
              w_in_odd, w_out_odd, conv_w, conv_b, conv_ln_w, conv_ln_b, lru_conv_w, lru_conv_b,
              lru_wa_fwd, lru_ba_fwd, lru_wx_fwd, lru_bx_fwd, lru_lam_fwd,
              lru_wa_bwd, lru_ba_bwd, lru_wx_bwd, lru_bx_bwd, lru_lam_bwd,
              router_grp_w, router_grp_b, router_exp_w, router_exp_b, moe_w_gate, moe_w_up, moe_w_down):
    prm = {
        'w_mod': w_mod, 'b_mod': b_mod, 'norm_mix_w': norm_mix_w, 'norm_ffn_w': norm_ffn_w,
        'norm_final_w': norm_final_w,
        'w_in_even': w_in_even, 'w_out_even': w_out_even, 'gmlp_norm_w': gmlp_norm_w, 'gmlp_w_s': gmlp_w_s,
        'gmlp_b_s': gmlp_b_s, 'ret_decay_fwd': ret_decay_fwd, 'ret_decay_bwd': ret_decay_bwd, 'ret_gn_w': ret_gn_w,
        'w_in_odd': w_in_odd, 'w_out_odd': w_out_odd, 'conv_w': conv_w, 'conv_b': conv_b,
        'conv_ln_w': conv_ln_w, 'conv_ln_b': conv_ln_b, 'lru_conv_w': lru_conv_w, 'lru_conv_b': lru_conv_b,
        'lru_wa_fwd': lru_wa_fwd, 'lru_ba_fwd': lru_ba_fwd, 'lru_wx_fwd': lru_wx_fwd, 'lru_bx_fwd': lru_bx_fwd,
        'lru_lam_fwd': lru_lam_fwd, 'lru_wa_bwd': lru_wa_bwd, 'lru_ba_bwd': lru_ba_bwd, 'lru_wx_bwd': lru_wx_bwd,
        'lru_bx_bwd': lru_bx_bwd, 'lru_lam_bwd': lru_lam_bwd,
        'router_grp_w': router_grp_w, 'router_grp_b': router_grp_b, 'router_exp_w': router_exp_w,
        'router_exp_b': router_exp_b, 'moe_w_gate': moe_w_gate, 'moe_w_up': moe_w_up, 'moe_w_down': moe_w_down,
    }
    bp = x_prompt.shape[0]
    zero_ret = jnp.zeros((bp, N_EVEN, B_HEADS, B_DK, B_DV), jnp.float32)
    zero_lru = jnp.zeros((bp, N_ODD, D_WIDTH), jnp.float32)
    y_prompt, new_ret_fwd, new_ret_bwd, new_lru_fwd, new_lru_bwd = trunk(
        x_prompt, c_ctx[None, :], None, zero_ret, zero_ret, zero_lru, zero_lru, prm)
    rope = axial_rope(x_sample.shape[1], B_DK)
    y_sample, _, _, _, _ = trunk(x_sample, c, rope, state_ret_fwd, state_ret_bwd, state_lru_fwd, state_lru_bwd, prm)
    return (y_prompt, y_sample, new_ret_fwd, new_ret_bwd, new_lru_fwd, new_lru_bwd)
```

```python
import functools

import jax
import jax.numpy as jnp
from jax import lax
from jax.experimental import pallas as pl
from jax.experimental.pallas import tpu as pltpu

F32 = jnp.float32
BF16 = jnp.bfloat16

D = 2048
B_P, N_P = 16, 256
B_S, N_S = 8, 4096
T_S = B_S * N_S
T_P = B_P * N_P
T = T_S + T_P
ROWS_PER_COND = 4096
N_COND_PAD = 16
DEPTH = 2
N_MOD = 6
EPS = 1e-6
GRID_W = 64
ROPE_BASE = 10000.0

W = 1024
HEADS = 8
HD = 128
CHUNK = 128
A_GROUPS = 4
A_GDIM = W // A_GROUPS
C_KSIZE = 31
D_KSIZE = 4
LRU_C = 8.0
HALO = 16

N_GROUPS = 4
EXPERTS_PER_GROUP = 8
N_EXPERTS = 32
D_EXPERT = 512
LANES = 128
EXPERT_LANE0 = N_GROUPS

TM_IN = 1024
TN_IN = 1024
TM_OUT = 512
TM_MOE = 256
N_SLOTS = 2 * T + N_EXPERTS * TM_MOE
N_BLOCKS = N_SLOTS // TM_MOE
TM_CMB = 512

VMEM_LIMIT = 56 * 1024 * 1024


def _params(sem, vmem=VMEM_LIMIT):
    return pltpu.CompilerParams(dimension_semantics=sem, vmem_limit_bytes=vmem)


def _cond_row(i, tm):
    return (i * tm) // ROWS_PER_COND


def _mod_body(c_ref, w_ref, b_ref, o_ref):
    c = c_ref[...]
    a = (c * jax.nn.sigmoid(c)).astype(BF16)
    o_ref[...] = jnp.dot(a, w_ref[...].astype(BF16), preferred_element_type=F32) + b_ref[...]


def _modulation(cond, w_mod, b_mod):
    tn = 1024
    return pl.pallas_call(
        _mod_body,
        grid=(DEPTH, N_MOD * D // tn),
        in_specs=[pl.BlockSpec((N_COND_PAD, D), lambda l, j: (0, 0)),
                  pl.BlockSpec((None, D, tn), lambda l, j: (l, 0, j)),
                  pl.BlockSpec((None, 1, tn), lambda l, j: (l, 0, j))],
        out_specs=pl.BlockSpec((None, N_COND_PAD, tn), lambda l, j: (l, 0, j)),
        out_shape=jax.ShapeDtypeStruct((DEPTH, N_COND_PAD, N_MOD * D), F32),
        compiler_params=_params(("arbitrary", "arbitrary")),
        name="modulation",
    )(cond, w_mod, b_mod.reshape(DEPTH, 1, N_MOD * D))


def _mod_spec(k, tm, ngrid):
    if ngrid == 1:
        return pl.BlockSpec((None, None, 1, D), lambda i: (k, _cond_row(i, tm), 0, 0))
    return pl.BlockSpec((None, None, 1, D), lambda i, j: (k, _cond_row(i, tm), 0, 0))


def _norm_in_body(x_ref, nw_ref, sc_ref, sh_ref, w_ref, p_ref, h_scr):
    @pl.when(pl.program_id(1) == 0)
    def _():
        x = x_ref[...]
        ms = jnp.mean(x * x, axis=-1, keepdims=True)
        h = x * lax.rsqrt(ms + EPS) * nw_ref[...]
        h = h * (1.0 + sc_ref[...]) + sh_ref[...]
        h_scr[...] = h.astype(BF16)

    p_ref[...] = jnp.dot(h_scr[...], w_ref[...], preferred_element_type=F32).astype(p_ref.dtype)


def _norm_in(x, nw, mods, w_bf16):
    n = w_bf16.shape[1]
    return pl.pallas_call(
        _norm_in_body,
        grid=(T // TM_IN, n // TN_IN),
        in_specs=[pl.BlockSpec((TM_IN, D), lambda i, j: (i, 0)),
                  pl.BlockSpec((1, D), lambda i, j: (0, 0)),
                  _mod_spec(1, TM_IN, 2),
                  _mod_spec(0, TM_IN, 2),
                  pl.BlockSpec((D, TN_IN), lambda i, j: (0, j))],
        out_specs=pl.BlockSpec((TM_IN, TN_IN), lambda i, j: (i, j)),
        out_shape=jax.ShapeDtypeStruct((T, n), BF16),
        scratch_shapes=[pltpu.VMEM((TM_IN, D), BF16)],
        compiler_params=_params(("arbitrary", "arbitrary")),
        name="norm_in",
    )(x, nw.reshape(1, D), mods, mods, w_bf16)


GMLP_ROWS = 512


def _gmlp_body(u_ref, v_ref, lnw_ref, ws_ref, b_ref, o_ref):
    for c in range(GMLP_ROWS // CHUNK):
        rows = pl.ds(c * CHUNK, CHUNK)
        v = jax.nn.gelu(v_ref[rows, :].astype(F32))
        mu = jnp.mean(v, axis=-1, keepdims=True)
        var = jnp.mean(jnp.square(v - mu), axis=-1, keepdims=True)
        vn = ((v - mu) * lax.rsqrt(var + EPS) * lnw_ref[...]).astype(BF16)
        z = jnp.concatenate(
            [jnp.dot(ws_ref[g], vn[:, g * A_GDIM:(g + 1) * A_GDIM], preferred_element_type=F32)
             for g in range(A_GROUPS)], axis=1) + b_ref[...]
        u = jax.nn.gelu(u_ref[rows, :].astype(F32))
        o_ref[rows, :] = (u * z).astype(BF16)


def _gmlp(p, lnw, ws_bf16, b_full):
    return pl.pallas_call(
        _gmlp_body,
        grid=(T // GMLP_ROWS,),
        in_specs=[pl.BlockSpec((GMLP_ROWS, W), lambda i: (i, 0)),
                  pl.BlockSpec((GMLP_ROWS, W), lambda i: (i, 1)),
                  pl.BlockSpec((1, W), lambda i: (0, 0)),
                  pl.BlockSpec((A_GROUPS, CHUNK, CHUNK), lambda i: (0, 0, 0)),
                  pl.BlockSpec((CHUNK, W), lambda i: (0, 0))],
        out_specs=pl.BlockSpec((GMLP_ROWS, W), lambda i: (i, 0)),
        out_shape=jax.ShapeDtypeStruct((T, W), BF16),
        compiler_params=_params(("arbitrary",)),
        name="gmlp",
    )(p, p, lnw.reshape(1, W), ws_bf16, b_full)


def _roped_qk(q_ref, k_ref, cos_ref, sin_ref, h, rope):
    cols = slice(h * HD, (h + 1) * HD)
    q = q_ref[:, cols].astype(F32)
    k = k_ref[:, cols].astype(F32) * (HD ** -0.5)
    if rope:
        c = cos_ref[...]
        s = sin_ref[...]
        q = q * c + pltpu.roll(q, HD // 2, axis=1) * s
        k = k * c + pltpu.roll(k, HD // 2, axis=1) * s
    return q, k


def _kt_v(kd, v):
    return lax.dot_general(kd, v, (((0,), (0,)), ((), ())), preferred_element_type=F32)


def _ret_bwd_body(q_ref, k_ref, v_ref, cos_ref, sin_ref, s0_ref, dq_ref, dk_ref, dc_ref,
                  ob_ref, sfin_ref, s_scr, *, rope, nc):
    c = pl.program_id(1)

    @pl.when(c == 0)
    def _():
        s_scr[...] = s0_ref[...]

    for h in range(HEADS):
        cols = slice(h * HD, (h + 1) * HD)
        q, k = _roped_qk(q_ref, k_ref, cos_ref, sin_ref, h, rope)
        v = v_ref[:, cols]
        s = s_scr[h]
        ob_ref[:, cols] = jnp.dot(q.astype(BF16), s.astype(BF16), preferred_element_type=F32) * dq_ref[h]
        kd = (k * dk_ref[h]).astype(BF16)
        s_scr[h] = s * dc_ref[h] + _kt_v(kd, v)

    @pl.when(c == nc - 1)
    def _():
        sfin_ref[...] = s_scr[...]


def _ret_fwd_body(q_ref, k_ref, v_ref, g_ref, cos_ref, sin_ref, ob_ref, s0_ref, m_ref, dq_ref, dk_ref, dc_ref,
                  gn_ref, o_ref, sfin_ref, s_scr, *, rope, nc):
    c = pl.program_id(1)

    @pl.when(c == 0)
    def _():
        s_scr[...] = s0_ref[...]

    for h in range(HEADS):
        cols = slice(h * HD, (h + 1) * HD)
        q, k = _roped_qk(q_ref, k_ref, cos_ref, sin_ref, h, rope)
        qb = q.astype(BF16)
        v = v_ref[:, cols]
        s = s_scr[h]
        scores = lax.dot_general(qb, k.astype(BF16), (((1,), (1,)), ((), ())),
                                 preferred_element_type=F32) * m_ref[h]
        o = (jnp.dot(scores.astype(BF16), v, preferred_element_type=F32)
             + jnp.dot(qb, s.astype(BF16), preferred_element_type=F32) * dq_ref[h]
             + ob_ref[:, cols])
        kd = (k * dk_ref[h]).astype(BF16)
        s_scr[h] = s * dc_ref[h] + _kt_v(kd, v)
        mu = jnp.mean(o, axis=-1, keepdims=True)
        var = jnp.mean(jnp.square(o - mu), axis=-1, keepdims=True)
        on = (o - mu) * lax.rsqrt(var + EPS) * gn_ref[:, cols]
        g = g_ref[:, cols].astype(F32)
        o_ref[:, cols] = (g * jax.nn.sigmoid(g) * on).astype(BF16)

    @pl.when(c == nc - 1)
    def _():
        sfin_ref[...] = s_scr[...]


def _retention(p, s_f0, s_b0, tabs, gn_w, cos, sin, *, base_chunk, nb, nc, rope):
    m_tab, dq_f, dk_f, dc_f, dq_b, dk_b, dc_b = tabs
    state_spec = pl.BlockSpec((None, HEADS, HD, HD), lambda b, c: (b, 0, 0, 0))
    tab_spec = pl.BlockSpec((HEADS, HD, HD), lambda b, c: (0, 0, 0))
    dc_spec = pl.BlockSpec((HEADS, 1, HD), lambda b, c: (0, 0, 0))
    state_shape = jax.ShapeDtypeStruct((nb, HEADS, HD, HD), F32)

    def col(j, rev):
        if rev:
            return pl.BlockSpec((CHUNK, W), lambda b, c: (base_chunk + b * nc + nc - 1 - c, j))
        return pl.BlockSpec((CHUNK, W), lambda b, c: (base_chunk + b * nc + c, j))

    def rope_spec(rev):
        if not rope:
            return pl.BlockSpec((CHUNK, HD), lambda b, c: (0, 0))
        if rev:
            return pl.BlockSpec((CHUNK, HD), lambda b, c: (nc - 1 - c, 0))
        return pl.BlockSpec((CHUNK, HD), lambda b, c: (c, 0))

    n_rows = nb * nc * CHUNK
    ob, s_b = pl.pallas_call(
        functools.partial(_ret_bwd_body, rope=rope, nc=nc),
        grid=(nb, nc),
        in_specs=[col(2, True), col(3, True), col(4, True), rope_spec(True), rope_spec(True),
                  state_spec, tab_spec, tab_spec, dc_spec],
        out_specs=[pl.BlockSpec((CHUNK, W), lambda b, c: (b * nc + nc - 1 - c, 0)), state_spec],
        out_shape=[jax.ShapeDtypeStruct((n_rows, W), F32), state_shape],
        scratch_shapes=[pltpu.VMEM((HEADS, HD, HD), F32)],
        compiler_params=_params(("arbitrary", "arbitrary")),
        name="retention_bwd",
    )(p, p, p, cos, sin, s_b0, dq_b, dk_b, dc_b)

    o, s_f = pl.pallas_call(
        functools.partial(_ret_fwd_body, rope=rope, nc=nc),
        grid=(nb, nc),
        in_specs=[col(2, False), col(3, False), col(4, False), col(5, False), rope_spec(False), rope_spec(False),
                  pl.BlockSpec((CHUNK, W), lambda b, c: (b * nc + c, 0)),
                  state_spec, tab_spec, tab_spec, tab_spec, dc_spec,
                  pl.BlockSpec((1, W), lambda b, c: (0, 0))],
        out_specs=[pl.BlockSpec((CHUNK, W), lambda b, c: (b * nc + c, 0)), state_spec],
        out_shape=[jax.ShapeDtypeStruct((n_rows, W), BF16), state_shape],
        scratch_shapes=[pltpu.VMEM((HEADS, HD, HD), F32)],
        compiler_params=_params(("arbitrary", "arbitrary")),
        name="retention_fwd",
    )(p, p, p, p, cos, sin, ob, s_f0, m_tab, dq_f, dk_f, dc_f, gn_w.reshape(1, W))
    return o, s_f, s_b


def _retention_tables(decay_fwd, decay_bwd):
    lg_f = jax.nn.log_sigmoid(decay_fwd.astype(F32))[:, None, None]
    lg_b = jax.nn.log_sigmoid(decay_bwd.astype(F32))[:, None, None]
    pos = jnp.arange(CHUNK, dtype=F32)
    rel = pos[:, None] - pos[None, :]
    m_tab = (jnp.where(rel >= 0, jnp.exp(lg_f * jnp.maximum(rel, 0.0)), 0.0)
             + jnp.where(rel <= 0, jnp.exp(lg_b * jnp.maximum(-rel, 0.0)), 0.0))
    ones = jnp.ones((1, 1, HD), F32)
    col = pos[None, :, None]
    dq_f = jnp.exp(lg_f * (col + 1.0)) * ones
    dk_f = jnp.exp(lg_f * (CHUNK - 1.0 - col)) * ones
    dq_b = jnp.exp(lg_b * (CHUNK - col)) * ones
    dk_b = jnp.exp(lg_b * col) * ones
    dc_f = jnp.exp(lg_f * CHUNK) * ones
    dc_b = jnp.exp(lg_b * CHUNK) * ones
    return m_tab, dq_f, dk_f, dc_f, dq_b, dk_b, dc_b


def _rope_tables():
    rows = N_S // GRID_W
    row = jnp.repeat(jnp.arange(rows, dtype=F32), GRID_W)
    colp = jnp.tile(jnp.arange(GRID_W, dtype=F32), rows)
    quarter = HD // 4
    inv_freq = ROPE_BASE ** (-jnp.arange(quarter, dtype=F32) / quarter)
    ang = jnp.concatenate([row[:, None] * inv_freq, colp[:, None] * inv_freq], axis=-1)
    cos, sin = jnp.cos(ang), jnp.sin(ang)
    return jnp.concatenate([cos, cos], axis=-1), jnp.concatenate([-sin, sin], axis=-1)


CONV_ROWS = 256
CONV_RB = 64


def _convc_body(flags_ref, a_ref, g_ref, ap_ref, gp_ref, an_ref, gn_ref, cw_ref, cb_ref, lnw_ref, lnb_ref,
                o_ref, ybuf, acc_scr):
    i = pl.program_id(0)

    def glu(a, g):
        return a[...].astype(F32) * jax.nn.sigmoid(g[...].astype(F32))

    ybuf[HALO:HALO + CONV_ROWS, :] = glu(a_ref, g_ref)
    ybuf[0:HALO, :] = jnp.where(flags_ref[i, 0] > 0, glu(ap_ref, gp_ref), 0.0)
    ybuf[HALO + CONV_ROWS:2 * HALO + CONV_ROWS, :] = jnp.where(flags_ref[i, 1] > 0, glu(an_ref, gn_ref), 0.0)

    def strip(s, carry):
        lanes = pl.ds(pl.multiple_of(s * LANES, LANES), LANES)
        for rb in range(CONV_ROWS // CONV_RB):
            acc = jnp.zeros((CONV_RB, LANES), F32)
            for k in range(C_KSIZE):
                r0 = rb * CONV_RB + HALO - C_KSIZE // 2 + k
                acc = acc + cw_ref[k:k + 1, lanes] * ybuf[r0:r0 + CONV_RB, lanes]
            acc_scr[rb * CONV_RB:(rb + 1) * CONV_RB, lanes] = acc
        return carry

    lax.fori_loop(0, W // LANES, strip, 0)
    y = acc_scr[...] + cb_ref[...]
    mu = jnp.mean(y, axis=-1, keepdims=True)
    var = jnp.mean(jnp.square(y - mu), axis=-1, keepdims=True)
    yn = (y - mu) * lax.rsqrt(var + EPS) * lnw_ref[...] + lnb_ref[...]
    o_ref[...] = (yn * jax.nn.sigmoid(yn)).astype(BF16)


def _halo_flags(rows):
    t0 = jnp.arange(T // rows, dtype=jnp.int32) * rows
    seq_len = jnp.where(t0 < T_S, N_S, N_P)
    off = jnp.where(t0 < T_S, t0 % N_S, (t0 - T_S) % N_P)
    return jnp.stack([off > 0, off + rows < seq_len], axis=1).astype(jnp.int32)


def _convc(p, cw, cb, lnw, lnb):
    hb = CONV_ROWS // HALO
    last = T // HALO - 1
    main = lambda j: pl.BlockSpec((CONV_ROWS, W), lambda i, f: (i, j))
    prev = lambda j: pl.BlockSpec((HALO, W), lambda i, f: (jnp.maximum(i * hb - 1, 0), j))
    nxt = lambda j: pl.BlockSpec((HALO, W), lambda i, f: (jnp.minimum((i + 1) * hb, last), j))
    vec = pl.BlockSpec((1, W), lambda i, f: (0, 0))
    cw_pad = jnp.concatenate([cw, jnp.zeros((1, W), F32)], axis=0)
    return pl.pallas_call(
        _convc_body,
        grid_spec=pltpu.PrefetchScalarGridSpec(
            num_scalar_prefetch=1,
            grid=(T // CONV_ROWS,),
            in_specs=[main(0), main(1), prev(0), prev(1), nxt(0), nxt(1),
                      pl.BlockSpec((C_KSIZE + 1, W), lambda i, f: (0, 0)), vec, vec, vec],
            out_specs=pl.BlockSpec((CONV_ROWS, W), lambda i, f: (i, 0)),
            scratch_shapes=[pltpu.VMEM((CONV_ROWS + 2 * HALO, W), F32), pltpu.VMEM((CONV_ROWS, W), F32)]),
        out_shape=jax.ShapeDtypeStruct((T, W), BF16),
        compiler_params=_params(("arbitrary",)),
        name="conv_module",
    )(_halo_flags(CONV_ROWS), p, p, p, p, p, p, cw_pad, cb.reshape(1, W), lnw.reshape(1, W), lnb.reshape(1, W))


LRU_TT = 128
LRU_G = 8
N_STRIPS = W // LANES


def _lru_body(*refs, nt, reverse):
    if reverse:
        (x_ref, xp_ref, xn_ref, hf_ref, gd_ref, cw_ref, cb_ref, wa_ref, ba_ref, wx_ref, bx_ref, sp_ref, h0_ref,
         o_ref, hfin_ref, xpad, a_scr, b_scr, hs_scr, h_scr) = refs
    else:
        (x_ref, xp_ref, xn_ref, cw_ref, cb_ref, wa_ref, ba_ref, wx_ref, bx_ref, sp_ref, h0_ref,
         o_ref, hfin_ref, xpad, a_scr, b_scr, hs_scr, h_scr) = refs
    tt = LRU_TT
    j = pl.program_id(1)
    jt = (nt - 1 - j) if reverse else j

    @pl.when(j == 0)
    def _():
        h_scr[...] = h0_ref[...]

    has_prev = jt > 0
    has_next = jt < nt - 1
    for b in range(LRU_G):
        rows = slice(b * tt, (b + 1) * tt)
        xpad[HALO:HALO + tt, :] = x_ref[b].astype(F32)
        xpad[0:HALO, :] = jnp.where(has_prev, xp_ref[b].astype(F32), 0.0)
        xpad[HALO + tt:2 * HALO + tt, :] = jnp.where(has_next, xn_ref[b].astype(F32), 0.0)
        xc = cb_ref[...]
        for k in range(D_KSIZE):
            r0 = HALO - D_KSIZE // 2 + k
            xc = xc + cw_ref[k:k + 1, :] * xpad[r0:r0 + tt, :]
        xb = xc.astype(BF16)
        r = jnp.concatenate([jnp.dot(xb[:, h * HD:(h + 1) * HD], wa_ref[h], preferred_element_type=F32)
                             for h in range(HEADS)], axis=1) + ba_ref[...]
        g = jnp.concatenate([jnp.dot(xb[:, h * HD:(h + 1) * HD], wx_ref[h], preferred_element_type=F32)
                             for h in range(HEADS)], axis=1) + bx_ref[...]
        log_a = (-LRU_C) * jax.nn.sigmoid(r) * sp_ref[...]
        th = jnp.tanh(log_a)
        one_minus_a2 = -2.0 * th / (1.0 - th)
        a = jnp.exp(log_a)
        bb = jnp.sqrt(jnp.maximum(one_minus_a2, 0.0)) * (jax.nn.sigmoid(g) * xc)
        for s in range(N_STRIPS):
            a_scr[s, rows, :] = a[:, s * LANES:(s + 1) * LANES]
            b_scr[s, rows, :] = bb[:, s * LANES:(s + 1) * LANES]

    def step(t, hs):
        tr = (tt - 1 - t) if reverse else t
        idx = pl.ds(tr, LRU_G, stride=tt)
        out = []
        for s in range(N_STRIPS):
            h = a_scr[s, idx, :] * hs[s] + b_scr[s, idx, :]
            hs_scr[s, idx, :] = h
            out.append(h)
        return tuple(out)

    h_init = tuple(h_scr[:, s * LANES:(s + 1) * LANES] for s in range(N_STRIPS))
    hs = lax.fori_loop(0, tt, step, h_init, unroll=8)
    h = jnp.concatenate(hs, axis=1)
    h_scr[...] = h

    for b in range(LRU_G):
        rows = slice(b * tt, (b + 1) * tt)
        hb = jnp.concatenate([hs_scr[s, rows, :] for s in range(N_STRIPS)], axis=1)
        if reverse:
            o_ref[b] = ((hf_ref[b] + hb) * jax.nn.gelu(gd_ref[b].astype(F32))).astype(BF16)
        else:
            o_ref[b] = hb

    @pl.when(j == nt - 1)
    def _():
        hfin_ref[...] = h


def _lru_pass(p3, hf3, prm, h0, *, g0, ng, n, reverse):
    cw, cb, wa, ba, wx, bx, sp = prm
    tt = LRU_TT
    nt = n // tt
    hb = tt // HALO
    lasth = n // HALO - 1
    tix = (lambda j: nt - 1 - j) if reverse else (lambda j: j)
    main = lambda col: pl.BlockSpec((LRU_G, tt, W), lambda g, j: (g0 + g, tix(j), col))
    prev = pl.BlockSpec((LRU_G, HALO, W), lambda g, j: (g0 + g, jnp.maximum(tix(j) * hb - 1, 0), 2))
    nxt = pl.BlockSpec((LRU_G, HALO, W), lambda g, j: (g0 + g, jnp.minimum((tix(j) + 1) * hb, lasth), 2))
    vec = pl.BlockSpec((1, W), lambda g, j: (0, 0))
    wsp = pl.BlockSpec((HEADS, HD, HD), lambda g, j: (0, 0, 0))
    local = pl.BlockSpec((LRU_G, tt, W), lambda g, j: (g, tix(j), 0))
    in_specs = [main(2), prev, nxt]
    args = [p3, p3, p3]
    if reverse:
        in_specs += [local, main(3)]
        args += [hf3, p3]
    in_specs += [pl.BlockSpec((8, W), lambda g, j: (0, 0)), vec, wsp, vec, wsp, vec, vec,
                 pl.BlockSpec((LRU_G, W), lambda g, j: (g, 0))]
    args += [cw, cb, wa, ba, wx, bx, sp, h0]
    return pl.pallas_call(
        functools.partial(_lru_body, nt=nt, reverse=reverse),
        grid=(ng, nt),
        in_specs=in_specs,
        out_specs=[local, pl.BlockSpec((LRU_G, W), lambda g, j: (g, 0))],
        out_shape=[jax.ShapeDtypeStruct((ng * LRU_G, n, W), BF16 if reverse else F32),
                   jax.ShapeDtypeStruct((ng * LRU_G, W), F32)],
        scratch_shapes=[pltpu.VMEM((tt + 2 * HALO, W), F32)] + [pltpu.VMEM((N_STRIPS, LRU_G * tt, LANES), F32)] * 3
        + [pltpu.VMEM((LRU_G, W), F32)],
        compiler_params=_params(("arbitrary", "arbitrary")),
        name="rglru_bwd" if reverse else "rglru_fwd",
    )(*args)


def _lru_params(cw, cb, wa, ba, wx, bx, lam):
    cw_pad = jnp.concatenate([cw, jnp.zeros((8 - D_KSIZE, W), F32)], axis=0)
    sp = jax.nn.softplus(-lam.astype(F32)).reshape(1, W)
    return (cw_pad, cb.reshape(1, W), wa.astype(BF16), ba.reshape(1, W), wx.astype(BF16), bx.reshape(1, W), sp)


def _rglru(p, prm_f, prm_b, h0_f, h0_b, *, seq0, nseq, n):
    p3 = p.reshape(T // n, n, 4 * W)
    g0, ng = seq0 // LRU_G, nseq // LRU_G
    hf, s_f = _lru_pass(p3, None, prm_f, h0_f, g0=g0, ng=ng, n=n, reverse=False)
    yd, s_b = _lru_pass(p3, hf, prm_b, h0_b, g0=g0, ng=ng, n=n, reverse=True)
    return yd.reshape(nseq * n, W), s_f, s_b


def _out_body(a_ref, b_ref, x_ref, g1_ref, wo_ref, nw_ref, sc_ref, sh_ref, rw_ref, rb_ref,
              x1_ref, hf_ref, info_ref, cnt_ref, carry):
    i = pl.program_id(0)

    @pl.when(i == 0)
    def _():
        carry[...] = jnp.zeros_like(carry)

    y = (jnp.dot(a_ref[...], wo_ref[0:W, :], preferred_element_type=F32)
         + jnp.dot(b_ref[...], wo_ref[W:2 * W, :], preferred_element_type=F32))
    x1 = x_ref[...] + g1_ref[...] * y
    x1_ref[...] = x1
    ms = jnp.mean(x1 * x1, axis=-1, keepdims=True)
    hf = x1 * lax.rsqrt(ms + EPS) * nw_ref[...]
    hf = hf * (1.0 + sc_ref[...]) + sh_ref[...]
    hf_ref[...] = hf

    logits = jnp.dot(hf, rw_ref[...], preferred_element_type=F32, precision=lax.Precision.HIGHEST) + rb_ref[...]
    lane = lax.broadcasted_iota(jnp.int32, (TM_OUT, LANES), 1).astype(F32)
    neg = -jnp.inf
    gmask = lane < N_GROUPS
    mg = jnp.max(jnp.where(gmask, logits, neg), axis=-1, keepdims=True)
    gstar = jnp.min(jnp.where(gmask & (logits == mg), lane, float(LANES)), axis=-1, keepdims=True)
    denom = jnp.sum(jnp.where(gmask, jnp.exp(logits - mg), 0.0), axis=-1, keepdims=True)
    psel = 1.0 / denom
    lo = EXPERT_LANE0 + EXPERTS_PER_GROUP * gstar
    emask = (lane >= lo) & (lane < lo + EXPERTS_PER_GROUP)
    v1 = jnp.max(jnp.where(emask, logits, neg), axis=-1, keepdims=True)
    i1 = jnp.min(jnp.where(emask & (logits == v1), lane, float(LANES)), axis=-1, keepdims=True)
    em2 = emask & (lane != i1)
    v2 = jnp.max(jnp.where(em2, logits, neg), axis=-1, keepdims=True)
    i2 = jnp.min(jnp.where(em2 & (logits == v2), lane, float(LANES)), axis=-1, keepdims=True)
    e21 = jnp.exp(v2 - v1)
    w1 = psel / (1.0 + e21)
    w2 = psel * e21 / (1.0 + e21)

    memb = jnp.where((lane == i1) | (lane == i2), 1.0, 0.0)
    r_i = lax.broadcasted_iota(jnp.int32, (TM_OUT, TM_OUT), 0)
    c_i = lax.broadcasted_iota(jnp.int32, (TM_OUT, TM_OUT), 1)
    lower = jnp.where(r_i > c_i, 1.0, 0.0).astype(BF16)
    before = jnp.dot(lower, memb.astype(BF16), preferred_element_type=F32) + carry[0:1, :]
    rank1 = jnp.sum(jnp.where(lane == i1, before, 0.0), axis=-1, keepdims=True)
    rank2 = jnp.sum(jnp.where(lane == i2, before, 0.0), axis=-1, keepdims=True)
    new_carry = carry[...] + jnp.sum(memb, axis=0, keepdims=True)
    carry[...] = new_carry
    cnt_ref[...] = new_carry

    e1 = i1 - EXPERT_LANE0
    e2 = i2 - EXPERT_LANE0
    info = jnp.where(lane == 0, e1, jnp.where(lane == 1, e2, jnp.where(lane == 2, w1, jnp.where(
        lane == 3, w2, jnp.where(lane == 4, rank1, jnp.where(lane == 5, rank2, 0.0))))))
    info_ref[...] = info


def _out_proj(a, b, x, mods, wo_bf16, nw, rw, rb):
    tm = TM_OUT
    row = lambda n: pl.BlockSpec((tm, n), lambda i: (i, 0))
    vec = pl.BlockSpec((1, D), lambda i: (0, 0))
    return pl.pallas_call(
        _out_body,
        grid=(T // tm,),
        in_specs=[row(W), row(W), row(D), _mod_spec(2, tm, 1),
                  pl.BlockSpec((D, D), lambda i: (0, 0)), vec, _mod_spec(4, tm, 1), _mod_spec(3, tm, 1),
                  pl.BlockSpec((D, LANES), lambda i: (0, 0)), pl.BlockSpec((1, LANES), lambda i: (0, 0))],
        out_specs=[row(D), row(D), row(LANES), pl.BlockSpec((8, LANES), lambda i: (0, 0))],
        out_shape=[jax.ShapeDtypeStruct((T, D), F32), jax.ShapeDtypeStruct((T, D), F32),
                   jax.ShapeDtypeStruct((T, LANES), F32), jax.ShapeDtypeStruct((8, LANES), F32)],
        scratch_shapes=[pltpu.VMEM((8, LANES), F32)],
        compiler_params=_params(("arbitrary",)),
        name="out_proj_router",
    )(a, b, x, mods, wo_bf16, nw.reshape(1, D), mods, mods, rw, rb)


def _moe_body(be_ref, nused_ref, codes_ref, hf_hbm, wg_ref, wu_ref, wd_ref, y_hbm, xbuf, obuf, sem_in, sem_out):
    i = pl.program_id(0)
    tm = TM_MOE

    def in_copy(tok, r):
        return pltpu.make_async_copy(hf_hbm.at[pl.ds(tok, 1)], xbuf.at[pl.ds(r, 1)], sem_in)

    def out_copy(row, r):
        return pltpu.make_async_copy(obuf.at[pl.ds(r, 1)], y_hbm.at[pl.ds(row, 1)], sem_out)

    @pl.when(i < nused_ref[0])
    def _():
        def issue(r, carry):
            code = codes_ref[0, r]
            in_copy(jnp.maximum(code, 0) >> 1, r).start()
            return carry

        lax.fori_loop(0, tm, issue, 0)

        def wait_in(r, carry):
            in_copy(0, r).wait()
            return carry

        lax.fori_loop(0, tm, wait_in, 0)

        x = xbuf[...].astype(BF16)
        gate = jnp.dot(x, wg_ref[...].astype(BF16), preferred_element_type=F32)
        up = jnp.dot(x, wu_ref[...].astype(BF16), preferred_element_type=F32)
        hid = (gate * jax.nn.sigmoid(gate) * up).astype(BF16)
        obuf[...] = jnp.dot(hid, wd_ref[...].astype(BF16), preferred_element_type=F32)

        def scatter(r, carry):
            code = codes_ref[0, r]

            @pl.when(code >= 0)
            def _():
                out_copy((code & 1) * T + (code >> 1), r).start()

            return carry

        lax.fori_loop(0, tm, scatter, 0)

        def wait_out(r, carry):
            @pl.when(codes_ref[0, r] >= 0)
            def _():
                out_copy(0, r).wait()

            return carry

        lax.fori_loop(0, tm, wait_out, 0)


def _moe(hf, codes, block_e, nused, wg, wu, wd, layer):
    wspec = lambda shape: pl.BlockSpec((None, None) + shape, lambda i, be, nu: (layer, be[i], 0, 0))
    return pl.pallas_call(
        _moe_body,
        grid_spec=pltpu.PrefetchScalarGridSpec(
            num_scalar_prefetch=2,
            grid=(N_BLOCKS,),
            in_specs=[pl.BlockSpec((None, 1, TM_MOE), lambda i, be, nu: (i, 0, 0), memory_space=pltpu.SMEM),
                      pl.BlockSpec(memory_space=pl.ANY),
                      wspec((D, D_EXPERT)), wspec((D, D_EXPERT)), wspec((D_EXPERT, D))],
            out_specs=pl.BlockSpec(memory_space=pl.ANY),
            scratch_shapes=[pltpu.VMEM((TM_MOE, D), F32), pltpu.VMEM((TM_MOE, D), F32),
                            pltpu.SemaphoreType.DMA, pltpu.SemaphoreType.DMA]),
        out_shape=jax.ShapeDtypeStruct((2 * T, D), F32),
        compiler_params=_params(("arbitrary",)),
        name="moe_experts",
    )(block_e, nused, codes.reshape(N_BLOCKS, 1, TM_MOE), hf, wg, wu, wd)


def _dispatch_plan(info, cnt):
    e = info[:, 0:2].astype(jnp.int32)
    rank = info[:, 4:6].astype(jnp.int32)
    counts = cnt[0, EXPERT_LANE0:EXPERT_LANE0 + N_EXPERTS].astype(jnp.int32)
    padded = ((counts + TM_MOE - 1) // TM_MOE) * TM_MOE
    pends = jnp.cumsum(padded)
    pstarts = pends - padded
    dest = pstarts[e] + rank
    codes = jnp.full((N_SLOTS,), -1, jnp.int32).at[dest.reshape(-1)].set(
        jnp.arange(2 * T, dtype=jnp.int32), unique_indices=True)
    block_e = jnp.clip(jnp.searchsorted(pends, jnp.arange(N_BLOCKS, dtype=jnp.int32) * TM_MOE, side='right'),
                       0, N_EXPERTS - 1).astype(jnp.int32)
    nused = (pends[-1] // TM_MOE).astype(jnp.int32).reshape(1)
    return codes, block_e, nused


def _combine_body(x_ref, y0_ref, y1_ref, info_ref, g2_ref, nw_ref, o_ref, *, final):
    info = info_ref[...]
    lane = lax.broadcasted_iota(jnp.int32, info.shape, 1)
    w1 = jnp.sum(jnp.where(lane == 2, info, 0.0), axis=-1, keepdims=True)
    w2 = jnp.sum(jnp.where(lane == 3, info, 0.0), axis=-1, keepdims=True)
    x = x_ref[...] + g2_ref[...] * (y0_ref[...] * w1 + y1_ref[...] * w2)
    if final:
        ms = jnp.mean(x * x, axis=-1, keepdims=True)
        x = x * lax.rsqrt(ms + EPS) * nw_ref[...]
    o_ref[...] = x


def _combine(x1, y, info, mods, nw, *, final):
    tm = TM_CMB
    nt = T // tm
    return pl.pallas_call(
        functools.partial(_combine_body, final=final),
        grid=(nt,),
        in_specs=[pl.BlockSpec((tm, D), lambda i: (i, 0)),
                  pl.BlockSpec((tm, D), lambda i: (i, 0)),
                  pl.BlockSpec((tm, D), lambda i: (i + nt, 0)),
                  pl.BlockSpec((tm, LANES), lambda i: (i, 0)),
                  _mod_spec(5, tm, 1),
                  pl.BlockSpec((1, D), lambda i: (0, 0))],
        out_specs=pl.BlockSpec((tm, D), lambda i: (i, 0)),
        out_shape=jax.ShapeDtypeStruct((T, D), F32),
        compiler_params=_params(("arbitrary",)),
        name="combine_final" if final else "combine",
    )(x1, y, y, info, mods, nw.reshape(1, D))


def kernel(x_prompt, x_sample, c, state_ret_fwd, state_ret_bwd, state_lru_fwd, state_lru_bwd, c_ctx, w_mod, b_mod, norm_mix_w, norm_ffn_w, norm_final_w, w_in_even, w_out_even, gmlp_norm_w, gmlp_w_s, gmlp_b_s, ret_decay_fwd, ret_decay_bwd, ret_gn_w, w_in_odd, w_out_odd, conv_w, conv_b, conv_ln_w, conv_ln_b, lru_conv_w, lru_conv_b, lru_wa_fwd, lru_ba_fwd, lru_wx_fwd, lru_bx_fwd, lru_lam_fwd, lru_wa_bwd, lru_ba_bwd, lru_wx_bwd, lru_bx_bwd, lru_lam_bwd, router_grp_w, router_grp_b, router_exp_w, router_exp_b, moe_w_gate, moe_w_up, moe_w_down):
    x = jnp.concatenate([x_sample.reshape(T_S, D), x_prompt.reshape(T_P, D)], axis=0)
    cond = jnp.concatenate([c, c_ctx[None, :], jnp.zeros((N_COND_PAD - B_S - 1, D), F32)], axis=0)
    m = _modulation(cond, w_mod, b_mod)
    mods_all = m.reshape(DEPTH, N_COND_PAD, N_MOD, D).transpose(0, 2, 1, 3)[:, :, :, None, :]

    ret_f = ret_b = lru_f = lru_b = None
    for l in range(DEPTH):
        mods = mods_all[l]
        if l % 2 == 0:
            e = l // 2
            p = _norm_in(x, norm_mix_w[l], mods, w_in_even[e].astype(BF16))
            b_full = jnp.repeat(gmlp_b_s[e].T.astype(F32), A_GDIM, axis=1)
            out_a = _gmlp(p, gmlp_norm_w[e], gmlp_w_s[e].astype(BF16), b_full)
            tabs = _retention_tables(ret_decay_fwd[e], ret_decay_bwd[e])
            cos, sin = _rope_tables()
            ob_s, _, _ = _retention(p, state_ret_fwd[:, e], state_ret_bwd[:, e], tabs, ret_gn_w[e], cos, sin,
                                    base_chunk=0, nb=B_S, nc=N_S // CHUNK, rope=True)
            zero_state = jnp.zeros((B_P, HEADS, HD, HD), F32)
            ob_p, ret_f, ret_b = _retention(p, zero_state, zero_state, tabs, ret_gn_w[e], cos, sin,
                                            base_chunk=T_S // CHUNK, nb=B_P, nc=N_P // CHUNK, rope=False)
            out_b = jnp.concatenate([ob_s, ob_p], axis=0)
            w_out = w_out_even[e]
        else:
            o = l // 2
            p = _norm_in(x, norm_mix_w[l], mods, w_in_odd[o].astype(BF16))
            out_a = _convc(p, conv_w[o], conv_b[o], conv_ln_w[o], conv_ln_b[o])
            prm_f = _lru_params(lru_conv_w[o], lru_conv_b[o], lru_wa_fwd[o], lru_ba_fwd[o], lru_wx_fwd[o],
                                lru_bx_fwd[o], lru_lam_fwd[o])
            prm_b = _lru_params(lru_conv_w[o], lru_conv_b[o], lru_wa_bwd[o], lru_ba_bwd[o], lru_wx_bwd[o],
                                lru_bx_bwd[o], lru_lam_bwd[o])
            yd_s, _, _ = _rglru(p, prm_f, prm_b, state_lru_fwd[:, o], state_lru_bwd[:, o],
                                seq0=0, nseq=B_S, n=N_S)
            zero_h = jnp.zeros((B_P, W), F32)
            yd_p, lru_f, lru_b = _rglru(p, prm_f, prm_b, zero_h, zero_h, seq0=T_S // N_P, nseq=B_P, n=N_P)
            out_b = jnp.concatenate([yd_s, yd_p], axis=0)
            w_out = w_out_odd[o]

        rw = jnp.concatenate([router_grp_w[l], router_exp_w[l],
                              jnp.zeros((D, LANES - N_GROUPS - N_EXPERTS), F32)], axis=1)
        rb = jnp.concatenate([router_grp_b[l], router_exp_b[l],
                              jnp.zeros((LANES - N_GROUPS - N_EXPERTS,), F32)]).reshape(1, LANES)
        x1, hf, info, cnt = _out_proj(out_a, out_b, x, mods, w_out.astype(BF16), norm_ffn_w[l], rw, rb)
        codes, block_e, nused = _dispatch_plan(info, cnt)
        y = _moe(hf, codes, block_e, nused, moe_w_gate, moe_w_up, moe_w_down, l)
        x = _combine(x1, y, info, mods, norm_final_w, final=(l == DEPTH - 1))

    y_sample = x[:T_S].reshape(B_S, N_S, D)
    y_prompt = x[T_S:].reshape(B_P, N_P, D)
    return (y_prompt, y_sample, ret_f[:, None], ret_b[:, None], lru_f[:, None], lru_b[:, None])
```

```python
import functools

import jax
import jax.numpy as jnp
from jax import lax
from jax.experimental import pallas as pl
from jax.experimental.pallas import tpu as pltpu

F32 = jnp.float32
BF16 = jnp.bfloat16
U32 = jnp.uint32

D = 2048
B_P, N_P = 16, 256
B_S, N_S = 8, 4096
T_S = B_S * N_S
T_P = B_P * N_P
T = T_S + T_P
ROWS_PER_COND = 4096
N_COND_PAD = 16
DEPTH = 2
N_MOD = 6
EPS = 1e-6
GRID_W = 64
ROPE_BASE = 10000.0

W = 1024
HW = W // 2
HEADS = 8
HD = 128
CHUNK = 128
A_GROUPS = 4
A_GDIM = W // A_GROUPS
C_KSIZE = 31
D_KSIZE = 4
LRU_C = 8.0

N_GROUPS = 4
EXPERTS_PER_GROUP = 8
N_EXPERTS = 32
D_EXPERT = 512
LANES = 128
N_STRIPS = W // LANES
EXPERT_LANE0 = N_GROUPS

TM_IN = 512
TN_IN = 2048
TM_OUT = 512
TM_MOE = 512
N_SLOTS = 2 * T + N_EXPERTS * TM_MOE
N_BLOCKS = N_SLOTS // TM_MOE
TM_CMB = 512

VMEM_LIMIT = 56 * 1024 * 1024
MOE_VMEM_LIMIT = 60 * 1024 * 1024


def _params(sem, vmem=VMEM_LIMIT):
    return pltpu.CompilerParams(dimension_semantics=sem, vmem_limit_bytes=vmem)


def _cond_row(i, tm):
    return (i * tm) // ROWS_PER_COND


def _pack_pair(lo, hi):
    lo_b = lax.bitcast_convert_type(lo.astype(BF16).astype(F32), U32)
    hi_b = lax.bitcast_convert_type(hi.astype(BF16).astype(F32), U32)
    return (lo_b >> 16) | hi_b


def _unpack_pair(w, dtype):
    return (lax.bitcast_convert_type(w << 16, F32).astype(dtype),
            lax.bitcast_convert_type(w & U32(0xFFFF0000), F32).astype(dtype))


def _pack_halves(x):
    k = x.shape[1] // 2
    return _pack_pair(x[:, :k], x[:, k:])


def _unpack_halves(w, dtype):
    return jnp.concatenate(_unpack_pair(w, dtype), axis=1)


TT_SUB = 8


def _store_tt(ref, x, rows):
    for c in range(TT_SUB):
        ref[pl.ds(c, rows, stride=TT_SUB), :] = _pack_pair(x[:, c * LANES:(c + 1) * LANES],
                                                           x[:, W + c * LANES:W + (c + 1) * LANES])


def _load_tt(ref, rows, dtype):
    lo, hi = [], []
    for c in range(TT_SUB):
        a, b = _unpack_pair(ref[pl.ds(c, rows, stride=TT_SUB), :], dtype)
        lo.append(a)
        hi.append(b)
    return jnp.concatenate(lo + hi, axis=1)


def _row_specs(arrs, tm, ngrid):
    if len(arrs) == 1:
        maps = [lambda i: i]
    else:
        n_s = T_S // tm
        maps = [lambda i: jnp.minimum(i, n_s - 1), lambda i: jnp.maximum(i - n_s, 0)]
    width = arrs[0].shape[1]
    if ngrid == 1:
        return [pl.BlockSpec((tm, width), lambda i, f=f: (f(i), 0)) for f in maps]
    return [pl.BlockSpec((tm, width), lambda i, j, f=f: (f(i), 0)) for f in maps]


def _tile_value(refs, tm):
    if len(refs) == 1:
        return refs[0][...]
    return jnp.where(pl.program_id(0) < T_S // tm, refs[0][...], refs[1][...])


def _for_tile(refs, tm, fn):
    if len(refs) == 1:
        fn(refs[0])
        return
    i = pl.program_id(0)

    @pl.when(i < T_S // tm)
    def _():
        fn(refs[0])

    @pl.when(i >= T_S // tm)
    def _():
        fn(refs[1])


def _mod_body(c_ref, w_ref, b_ref, o_ref):
    c = c_ref[...]
    a = (c * jax.nn.sigmoid(c)).astype(BF16)
    o_ref[...] = jnp.dot(a, w_ref[...].astype(BF16), preferred_element_type=F32) + b_ref[...]


def _modulation(cond, w_mod, b_mod):
    tn = 1024
    return pl.pallas_call(
        _mod_body,
        grid=(DEPTH, N_MOD * D // tn),
        in_specs=[pl.BlockSpec((N_COND_PAD, D), lambda l, j: (0, 0)),
                  pl.BlockSpec((None, D, tn), lambda l, j: (l, 0, j)),
                  pl.BlockSpec((None, 1, tn), lambda l, j: (l, 0, j))],
        out_specs=pl.BlockSpec((None, N_COND_PAD, tn), lambda l, j: (l, 0, j)),
        out_shape=jax.ShapeDtypeStruct((DEPTH, N_COND_PAD, N_MOD * D), F32),
        compiler_params=_params(("arbitrary", "arbitrary")),
        name="modulation",
    )(cond, w_mod, b_mod.reshape(DEPTH, 1, N_MOD * D))


def _mod_spec(k, tm, ngrid):
    if ngrid == 1:
        return pl.BlockSpec((None, None, 1, D), lambda i: (k, _cond_row(i, tm), 0, 0))
    return pl.BlockSpec((None, None, 1, D), lambda i, j: (k, _cond_row(i, tm), 0, 0))


def _norm_in_body(*refs, pack):
    nw_ref, sc_ref, sh_ref, w_ref, p_ref, h_scr = refs[-6:]

    @pl.when(pl.program_id(1) == 0)
    def _():
        x = _tile_value(refs[:-6], TM_IN)
        ms = jnp.mean(x * x, axis=-1, keepdims=True)
        h = x * lax.rsqrt(ms + EPS) * nw_ref[...]
        h = h * (1.0 + sc_ref[...]) + sh_ref[...]
        h_scr[...] = h.astype(BF16)

    r = jnp.dot(h_scr[...], w_ref[...], preferred_element_type=F32)
    if pack:
        p_ref[...] = jnp.concatenate([_pack_halves(r[:, k * W:(k + 1) * W]) for k in range(TN_IN // W)], axis=1)
    else:
        p_ref[...] = r.astype(BF16)


def _norm_in(xs, nw, mods, w_bf16, *, pack):
    n = w_bf16.shape[1]
    if pack:
        out_spec = pl.BlockSpec((TM_IN, TN_IN // 2), lambda i, j: (i, j))
        out_shape = jax.ShapeDtypeStruct((T, n // 2), U32)
    else:
        out_spec = pl.BlockSpec((TM_IN, TN_IN), lambda i, j: (i, j))
        out_shape = jax.ShapeDtypeStruct((T, n), BF16)
    return pl.pallas_call(
        functools.partial(_norm_in_body, pack=pack),
        grid=(T // TM_IN, n // TN_IN),
        in_specs=_row_specs(xs, TM_IN, 2) + [
            pl.BlockSpec((1, D), lambda i, j: (0, 0)),
            _mod_spec(1, TM_IN, 2),
            _mod_spec(0, TM_IN, 2),
            pl.BlockSpec((D, TN_IN), lambda i, j: (0, j))],
        out_specs=out_spec,
        out_shape=out_shape,
        scratch_shapes=[pltpu.VMEM((TM_IN, D), BF16)],
        compiler_params=_params(("arbitrary", "arbitrary")),
        name="norm_in",
    )(*xs, nw.reshape(1, D), mods, mods, w_bf16)


GMLP_ROWS = 512


def _gmlp_body(u_ref, v_ref, lnw_ref, ws_ref, b_ref, o_ref):
    for c in range(GMLP_ROWS // CHUNK):
        rows = pl.ds(c * CHUNK, CHUNK)
        v = jax.nn.gelu(v_ref[rows, :].astype(F32))
        mu = jnp.mean(v, axis=-1, keepdims=True)
        var = jnp.mean(jnp.square(v - mu), axis=-1, keepdims=True)
        vn = ((v - mu) * lax.rsqrt(var + EPS) * lnw_ref[...]).astype(BF16)
        z = jnp.concatenate(
            [jnp.dot(ws_ref[g], vn[:, g * A_GDIM:(g + 1) * A_GDIM], preferred_element_type=F32)
             for g in range(A_GROUPS)], axis=1) + b_ref[...]
        u = jax.nn.gelu(u_ref[rows, :].astype(F32))
        o_ref[rows, :] = (u * z).astype(BF16)


def _gmlp(p, lnw, ws_bf16, b_full):
    return pl.pallas_call(
        _gmlp_body,
        grid=(T // GMLP_ROWS,),
        in_specs=[pl.BlockSpec((GMLP_ROWS, W), lambda i: (i, 0)),
                  pl.BlockSpec((GMLP_ROWS, W), lambda i: (i, 1)),
                  pl.BlockSpec((1, W), lambda i: (0, 0)),
                  pl.BlockSpec((A_GROUPS, CHUNK, CHUNK), lambda i: (0, 0, 0)),
                  pl.BlockSpec((CHUNK, W), lambda i: (0, 0))],
        out_specs=pl.BlockSpec((GMLP_ROWS, W), lambda i: (i, 0)),
        out_shape=jax.ShapeDtypeStruct((T, W), BF16),
        compiler_params=_params(("arbitrary",)),
        name="gmlp",
    )(p, p, lnw.reshape(1, W), ws_bf16, b_full)


def _roped_qk(q_ref, k_ref, cos_ref, sin_ref, h, rope):
    cols = slice(h * HD, (h + 1) * HD)
    q = q_ref[:, cols].astype(F32)
    k = k_ref[:, cols].astype(F32) * (HD ** -0.5)
    if rope:
        c = cos_ref[...]
        s = sin_ref[...]
        q = q * c + pltpu.roll(q, HD // 2, axis=1) * s
        k = k * c + pltpu.roll(k, HD // 2, axis=1) * s
    return q, k


def _kt_v(kd, v):
    return lax.dot_general(kd, v, (((0,), (0,)), ((), ())), preferred_element_type=F32)


def _ret_bwd_body(q_ref, k_ref, v_ref, cos_ref, sin_ref, s0_ref, dq_ref, dk_ref, dc_ref,
                  ob_ref, sfin_ref, s_scr, *, rope, nc):
    c = pl.program_id(1)

    @pl.when(c == 0)
    def _():
        s_scr[...] = s0_ref[...]

    for h in range(HEADS):
        cols = slice(h * HD, (h + 1) * HD)
        q, k = _roped_qk(q_ref, k_ref, cos_ref, sin_ref, h, rope)
        v = v_ref[:, cols]
        s = s_scr[h]
        ob_ref[:, cols] = jnp.dot(q.astype(BF16), s.astype(BF16), preferred_element_type=F32) * dq_ref[h]
        kd = (k * dk_ref[h]).astype(BF16)
        s_scr[h] = s * dc_ref[h] + _kt_v(kd, v)

    @pl.when(c == nc - 1)
    def _():
        sfin_ref[...] = s_scr[...]


def _ret_fwd_body(q_ref, k_ref, v_ref, g_ref, cos_ref, sin_ref, ob_ref, s0_ref, m_ref, dq_ref, dk_ref, dc_ref,
                  gn_ref, o_ref, sfin_ref, s_scr, *, rope, nc):
    c = pl.program_id(1)

    @pl.when(c == 0)
    def _():
        s_scr[...] = s0_ref[...]

    for h in range(HEADS):
        cols = slice(h * HD, (h + 1) * HD)
        q, k = _roped_qk(q_ref, k_ref, cos_ref, sin_ref, h, rope)
        qb = q.astype(BF16)
        v = v_ref[:, cols]
        s = s_scr[h]
        scores = lax.dot_general(qb, k.astype(BF16), (((1,), (1,)), ((), ())),
                                 preferred_element_type=F32) * m_ref[h]
        o = (jnp.dot(scores.astype(BF16), v, preferred_element_type=F32)
             + jnp.dot(qb, s.astype(BF16), preferred_element_type=F32) * dq_ref[h]
             + ob_ref[:, cols])
        kd = (k * dk_ref[h]).astype(BF16)
        s_scr[h] = s * dc_ref[h] + _kt_v(kd, v)
        mu = jnp.mean(o, axis=-1, keepdims=True)
        var = jnp.mean(jnp.square(o - mu), axis=-1, keepdims=True)
        on = (o - mu) * lax.rsqrt(var + EPS) * gn_ref[:, cols]
        g = g_ref[:, cols].astype(F32)
        o_ref[:, cols] = (g * jax.nn.sigmoid(g) * on).astype(BF16)

    @pl.when(c == nc - 1)
    def _():
        sfin_ref[...] = s_scr[...]


def _retention(p, s_f0, s_b0, tabs, gn_w, cos, sin, *, base_chunk, nb, nc, rope):
    m_tab, dq_f, dk_f, dc_f, dq_b, dk_b, dc_b = tabs
    state_spec = pl.BlockSpec((None, HEADS, HD, HD), lambda b, c: (b, 0, 0, 0))
    tab_spec = pl.BlockSpec((HEADS, HD, HD), lambda b, c: (0, 0, 0))
    dc_spec = pl.BlockSpec((HEADS, 1, HD), lambda b, c: (0, 0, 0))
    state_shape = jax.ShapeDtypeStruct((nb, HEADS, HD, HD), F32)

    def col(j, rev):
        if rev:
            return pl.BlockSpec((CHUNK, W), lambda b, c: (base_chunk + b * nc + nc - 1 - c, j))
        return pl.BlockSpec((CHUNK, W), lambda b, c: (base_chunk + b * nc + c, j))

    def rope_spec(rev):
        if not rope:
            return pl.BlockSpec((CHUNK, HD), lambda b, c: (0, 0))
        if rev:
            return pl.BlockSpec((CHUNK, HD), lambda b, c: (nc - 1 - c, 0))
        return pl.BlockSpec((CHUNK, HD), lambda b, c: (c, 0))

    n_rows = nb * nc * CHUNK
    ob, s_b = pl.pallas_call(
        functools.partial(_ret_bwd_body, rope=rope, nc=nc),
        grid=(nb, nc),
        in_specs=[col(2, True), col(3, True), col(4, True), rope_spec(True), rope_spec(True),
                  state_spec, tab_spec, tab_spec, dc_spec],
        out_specs=[pl.BlockSpec((CHUNK, W), lambda b, c: (b * nc + nc - 1 - c, 0)), state_spec],
        out_shape=[jax.ShapeDtypeStruct((n_rows, W), F32), state_shape],
        scratch_shapes=[pltpu.VMEM((HEADS, HD, HD), F32)],
        compiler_params=_params(("arbitrary", "arbitrary")),
        name="retention_bwd",
    )(p, p, p, cos, sin, s_b0, dq_b, dk_b, dc_b)

    o, s_f = pl.pallas_call(
        functools.partial(_ret_fwd_body, rope=rope, nc=nc),
        grid=(nb, nc),
        in_specs=[col(2, False), col(3, False), col(4, False), col(5, False), rope_spec(False), rope_spec(False),
                  pl.BlockSpec((CHUNK, W), lambda b, c: (b * nc + c, 0)),
                  state_spec, tab_spec, tab_spec, tab_spec, dc_spec,
                  pl.BlockSpec((1, W), lambda b, c: (0, 0))],
        out_specs=[pl.BlockSpec((CHUNK, W), lambda b, c: (b * nc + c, 0)), state_spec],
        out_shape=[jax.ShapeDtypeStruct((n_rows, W), BF16), state_shape],
        scratch_shapes=[pltpu.VMEM((HEADS, HD, HD), F32)],
        compiler_params=_params(("arbitrary", "arbitrary")),
        name="retention_fwd",
    )(p, p, p, p, cos, sin, ob, s_f0, m_tab, dq_f, dk_f, dc_f, gn_w.reshape(1, W))
    return o, s_f, s_b


def _retention_tables(decay_fwd, decay_bwd):
    lg_f = jax.nn.log_sigmoid(decay_fwd.astype(F32))[:, None, None]
    lg_b = jax.nn.log_sigmoid(decay_bwd.astype(F32))[:, None, None]
    pos = jnp.arange(CHUNK, dtype=F32)
    rel = pos[:, None] - pos[None, :]
    m_tab = (jnp.where(rel >= 0, jnp.exp(lg_f * jnp.maximum(rel, 0.0)), 0.0)
             + jnp.where(rel <= 0, jnp.exp(lg_b * jnp.maximum(-rel, 0.0)), 0.0))
    ones = jnp.ones((1, 1, HD), F32)
    col = pos[None, :, None]
    dq_f = jnp.exp(lg_f * (col + 1.0)) * ones
    dk_f = jnp.exp(lg_f * (CHUNK - 1.0 - col)) * ones
    dq_b = jnp.exp(lg_b * (CHUNK - col)) * ones
    dk_b = jnp.exp(lg_b * col) * ones
    dc_f = jnp.exp(lg_f * CHUNK) * ones
    dc_b = jnp.exp(lg_b * CHUNK) * ones
    return m_tab, dq_f, dk_f, dc_f, dq_b, dk_b, dc_b


def _rope_tables():
    rows = N_S // GRID_W
    row = jnp.repeat(jnp.arange(rows, dtype=F32), GRID_W)
    colp = jnp.tile(jnp.arange(GRID_W, dtype=F32), rows)
    quarter = HD // 4
    inv_freq = ROPE_BASE ** (-jnp.arange(quarter, dtype=F32) / quarter)
    ang = jnp.concatenate([row[:, None] * inv_freq, colp[:, None] * inv_freq], axis=-1)
    cos, sin = jnp.cos(ang), jnp.sin(ang)
    return jnp.concatenate([cos, cos], axis=-1), jnp.concatenate([-sin, sin], axis=-1)


OD_TT = 128
OD_G = 8
CONV_HALO = 16
LRU_HALO = 8


def _od_rows(seq_row0, n, g, tj):
    return [seq_row0 + (g * OD_G + b) * n + tj * OD_TT for b in range(OD_G)]


def _od_in_copies(p_hbm, buf, sem, *, col, halo, seq_row0, n, g, tj):
    nt = n // OD_TT
    out = []
    for b, base in enumerate(_od_rows(seq_row0, n, g, tj)):
        def src(r, k):
            return p_hbm.at[pl.ds(r, k), pl.ds(col * HW, HW)]

        out.append((None, pltpu.make_async_copy(src(base, OD_TT), buf.at[pl.ds(halo, OD_TT), b, :], sem)))
        if halo:
            out.append((tj > 0, pltpu.make_async_copy(src(base - halo, halo), buf.at[pl.ds(0, halo), b, :], sem)))
            out.append((tj < nt - 1, pltpu.make_async_copy(src(base + OD_TT, halo),
                                                           buf.at[pl.ds(halo + OD_TT, halo), b, :], sem)))
    return out


def _od_run(copies, op):
    for cond, cp in copies:
        fn = cp.start if op == "start" else cp.wait
        if cond is None:
            fn()
        else:
            pl.when(cond)(fn)


def _od_zero_halo(buf, halo, n, tj):
    nt = n // OD_TT
    zeros = jnp.zeros((halo, OD_G, HW), U32)

    @pl.when(tj == 0)
    def _():
        buf[0:halo] = zeros

    @pl.when(tj == nt - 1)
    def _():
        buf[halo + OD_TT:2 * halo + OD_TT] = zeros


def _od_out_copies(obuf, out_hbm, sem, *, seq_row0, n, g, tj):
    return [(None, pltpu.make_async_copy(obuf.at[:, b, :], out_hbm.at[pl.ds(base - seq_row0, OD_TT), :], sem))
            for b, base in enumerate(_od_rows(seq_row0, n, g, tj))]


def _od_unpack(buf, rows):
    return _unpack_halves(buf[...].reshape(rows, HW), F32)


def _od_pipeline(q, nq, in_copies, sem_slots=2):
    s = q % 2

    @pl.when(q == 0)
    def _():
        _od_run(in_copies(q, s), "start")

    _od_run(in_copies(q, s), "wait")

    @pl.when(q + 1 < nq)
    def _():
        _od_run(in_copies(q + 1, 1 - s), "start")

    return s


def _od_emit(q, nq, s, obuf, words, out_copies):
    @pl.when(q >= 2)
    def _():
        _od_run(out_copies(s), "wait")

    obuf[s] = words.reshape(OD_TT, OD_G, HW)
    _od_run(out_copies(s), "start")

    @pl.when(q == nq - 1)
    def _():
        _od_run(out_copies(s), "wait")

        @pl.when(q >= 1)
        def _():
            _od_run(out_copies(1 - s), "wait")


CONV_RB = 64


def _od_conv_body(p_hbm, cw_ref, cb_ref, lnw_ref, lnb_ref, yc_hbm, xin, glu, ybuf, obuf, sem_in, sem_out,
                  *, seq_row0, n, ng):
    nt = n // OD_TT
    nq = ng * nt
    q = pl.program_id(0)
    halo = CONV_HALO
    rows_in = (OD_TT + 2 * halo) * OD_G
    rows = OD_TT * OD_G

    def in_copies(step, slot):
        out = []
        for br in range(2):
            out += _od_in_copies(p_hbm, xin.at[br, slot], sem_in.at[slot], col=br, halo=halo,
                                 seq_row0=seq_row0, n=n, g=step // nt, tj=step % nt)
        return out

    s = _od_pipeline(q, nq, in_copies)
    g, tj = q // nt, q % nt
    for br in range(2):
        _od_zero_halo(xin.at[br, s], halo, n, tj)
    glu[...] = _od_unpack(xin.at[0, s], rows_in) * jax.nn.sigmoid(_od_unpack(xin.at[1, s], rows_in))

    def strip(si, carry):
        lanes = pl.ds(pl.multiple_of(si * LANES, LANES), LANES)
        for rb in range(rows // CONV_RB):
            acc = jnp.zeros((CONV_RB // 8, 8, LANES), F32)
            for k in range(C_KSIZE):
                wk = cw_ref[si, pl.ds(k, 1), :]
                r0 = rb * CONV_RB + (halo - C_KSIZE // 2 + k) * OD_G
                acc = acc + glu[r0:r0 + CONV_RB, lanes].reshape(CONV_RB // 8, 8, LANES) * wk[None]
            ybuf[rb * CONV_RB:(rb + 1) * CONV_RB, lanes] = acc.reshape(CONV_RB, LANES)
        return carry

    lax.fori_loop(0, N_STRIPS, strip, 0)
    y = ybuf[...] + cb_ref[...]
    mu = jnp.mean(y, axis=-1, keepdims=True)
    var = jnp.mean(jnp.square(y - mu), axis=-1, keepdims=True)
    yn = (y - mu) * lax.rsqrt(var + EPS) * lnw_ref[...] + lnb_ref[...]
    yo = yn * jax.nn.sigmoid(yn)

    def out_copies(slot):
        return _od_out_copies(obuf.at[slot], yc_hbm, sem_out.at[slot], seq_row0=seq_row0, n=n, g=g, tj=tj)

    _od_emit(q, nq, s, obuf, _pack_halves(yo), out_copies)


def _od_conv(p, cw3, cb, lnw, lnb, *, seq_row0, nseq, n):
    ng = nseq // OD_G
    nq = ng * (n // OD_TT)
    vec = pl.BlockSpec((1, W), lambda q: (0, 0))
    t_in = OD_TT + 2 * CONV_HALO
    return pl.pallas_call(
        functools.partial(_od_conv_body, seq_row0=seq_row0, n=n, ng=ng),
        grid=(nq,),
        in_specs=[pl.BlockSpec(memory_space=pl.ANY),
                  pl.BlockSpec((N_STRIPS, C_KSIZE + 1, LANES), lambda q: (0, 0, 0)), vec, vec, vec],
        out_specs=pl.BlockSpec(memory_space=pl.ANY),
        out_shape=jax.ShapeDtypeStruct((nseq * n, HW), U32),
        scratch_shapes=[pltpu.VMEM((2, 2, t_in, OD_G, HW), U32),
                        pltpu.VMEM((t_in * OD_G, W), F32),
                        pltpu.VMEM((OD_TT * OD_G, W), F32),
                        pltpu.VMEM((2, OD_TT, OD_G, HW), U32),
                        pltpu.SemaphoreType.DMA((2,)), pltpu.SemaphoreType.DMA((2,))],
        compiler_params=_params(("arbitrary",)),
        name="conv_module",
    )(p, cw3, cb.reshape(1, W), lnw.reshape(1, W), lnb.reshape(1, W))


def _od_gates(x, cw_ref, cb_ref, wa_ref, ba_ref, wx_ref, bx_ref, sp_ref, a_scr, b_scr):
    rows = OD_TT * OD_G
    xc = cb_ref[...]
    for k in range(D_KSIZE):
        r0 = (LRU_HALO - D_KSIZE // 2 + k) * OD_G
        xc = xc + cw_ref[k:k + 1, :] * x[r0:r0 + rows, :]
    xb = xc.astype(BF16)
    r = jnp.concatenate([jnp.dot(xb[:, h * HD:(h + 1) * HD], wa_ref[h], preferred_element_type=F32)
                         for h in range(HEADS)], axis=1) + ba_ref[...]
    g = jnp.concatenate([jnp.dot(xb[:, h * HD:(h + 1) * HD], wx_ref[h], preferred_element_type=F32)
                         for h in range(HEADS)], axis=1) + bx_ref[...]
    log_a = (-LRU_C) * jax.nn.sigmoid(r) * sp_ref[...]
    th = jnp.tanh(log_a)
    one_minus_a2 = -2.0 * th / (1.0 - th)
    a_scr[...] = jnp.exp(log_a)
    b_scr[...] = jnp.sqrt(jnp.maximum(one_minus_a2, 0.0)) * (jax.nn.sigmoid(g) * xc)


def _od_scan(a_scr, b_scr, out_ref, h, reverse):
    def step(t, h):
        tr = (OD_TT - 1 - t) if reverse else t
        rows = pl.ds(pl.multiple_of(tr * OD_G, OD_G), OD_G)
        h = a_scr[rows, :] * h + b_scr[rows, :]
        out_ref[rows, :] = h
        return h

    return lax.fori_loop(0, OD_TT, step, h, unroll=8)


def _od_lru_fwd_body(p_hbm, cw_ref, cb_ref, wa_ref, ba_ref, wx_ref, bx_ref, sp_ref, h0_ref,
                     hf_ref, hfin_ref, xin, a_scr, b_scr, h_scr, sem_in, *, seq_row0, n, ng):
    nt = n // OD_TT
    nq = ng * nt
    q = pl.program_id(0)

    def in_copies(step, slot):
        return _od_in_copies(p_hbm, xin.at[slot], sem_in.at[slot], col=2, halo=LRU_HALO,
                             seq_row0=seq_row0, n=n, g=step // nt, tj=step % nt)

    s = _od_pipeline(q, nq, in_copies)
    tj = q % nt
    _od_zero_halo(xin.at[s], LRU_HALO, n, tj)

    @pl.when(tj == 0)
    def _():
        h_scr[...] = h0_ref[...]

    x = _od_unpack(xin.at[s], (OD_TT + 2 * LRU_HALO) * OD_G)
    _od_gates(x, cw_ref, cb_ref, wa_ref, ba_ref, wx_ref, bx_ref, sp_ref, a_scr, b_scr)
    h = _od_scan(a_scr, b_scr, hf_ref, h_scr[...], False)
    h_scr[...] = h

    @pl.when(tj == nt - 1)
    def _():
        hfin_ref[...] = h


def _od_lru_bwd_body(p_hbm, hf_ref, cw_ref, cb_ref, wa_ref, ba_ref, wx_ref, bx_ref, sp_ref, h0_ref,
                     yd_hbm, hfin_ref, xin, gin, a_scr, b_scr, hb_scr, obuf, h_scr, sem_in, sem_out,
                     *, seq_row0, n, ng):
    nt = n // OD_TT
    nq = ng * nt
    q = pl.program_id(0)

    def in_copies(step, slot):
        where = dict(seq_row0=seq_row0, n=n, g=step // nt, tj=nt - 1 - step % nt)
        return (_od_in_copies(p_hbm, xin.at[slot], sem_in.at[slot], col=2, halo=LRU_HALO, **where)
                + _od_in_copies(p_hbm, gin.at[slot], sem_in.at[slot], col=3, halo=0, **where))

    s = _od_pipeline(q, nq, in_copies)
    g, tj = q // nt, nt - 1 - q % nt
    _od_zero_halo(xin.at[s], LRU_HALO, n, tj)

    @pl.when(tj == nt - 1)
    def _():
        h_scr[...] = h0_ref[...]

    x = _od_unpack(xin.at[s], (OD_TT + 2 * LRU_HALO) * OD_G)
    _od_gates(x, cw_ref, cb_ref, wa_ref, ba_ref, wx_ref, bx_ref, sp_ref, a_scr, b_scr)
    h = _od_scan(a_scr, b_scr, hb_scr, h_scr[...], True)
    h_scr[...] = h

    @pl.when(tj == 0)
    def _():
        hfin_ref[...] = h

    gd = _od_unpack(gin.at[s], OD_TT * OD_G)
    yd = (hf_ref[...] + hb_scr[...]) * jax.nn.gelu(gd)

    def out_copies(slot):
        return _od_out_copies(obuf.at[slot], yd_hbm, sem_out.at[slot], seq_row0=seq_row0, n=n, g=g, tj=tj)

    _od_emit(q, nq, s, obuf, _pack_halves(yd), out_copies)


def _od_rglru(p, prm_f, prm_b, h0_f, h0_b, *, seq_row0, nseq, n):
    ng = nseq // OD_G
    nt = n // OD_TT
    nq = ng * nt
    rows = OD_TT * OD_G
    vec = pl.BlockSpec((1, W), lambda q: (0, 0))
    wsp = pl.BlockSpec((HEADS, HD, HD), lambda q: (0, 0, 0))
    prm_specs = [pl.BlockSpec((8, W), lambda q: (0, 0)), vec, wsp, vec, wsp, vec, vec,
                 pl.BlockSpec((OD_G, W), lambda q: (q // nt, 0))]
    state_spec = pl.BlockSpec((OD_G, W), lambda q: (q // nt, 0))
    state_shape = jax.ShapeDtypeStruct((nseq, W), F32)
    t_in = OD_TT + 2 * LRU_HALO
    tile_f32 = pltpu.VMEM((rows, W), F32)

    hf, s_f = pl.pallas_call(
        functools.partial(_od_lru_fwd_body, seq_row0=seq_row0, n=n, ng=ng),
        grid=(nq,),
        in_specs=[pl.BlockSpec(memory_space=pl.ANY)] + prm_specs,
        out_specs=[pl.BlockSpec((rows, W), lambda q: (q, 0)), state_spec],
        out_shape=[jax.ShapeDtypeStruct((nq * rows, W), F32), state_shape],
        scratch_shapes=[pltpu.VMEM((2, t_in, OD_G, HW), U32), tile_f32, tile_f32, pltpu.VMEM((OD_G, W), F32),
                        pltpu.SemaphoreType.DMA((2,))],
        compiler_params=_params(("arbitrary",)),
        name="rglru_fwd",
    )(p, *prm_f, h0_f)

    yd, s_b = pl.pallas_call(
        functools.partial(_od_lru_bwd_body, seq_row0=seq_row0, n=n, ng=ng),
        grid=(nq,),
        in_specs=[pl.BlockSpec(memory_space=pl.ANY),
                  pl.BlockSpec((rows, W), lambda q: ((q // nt) * nt + nt - 1 - q % nt, 0))] + prm_specs,
        out_specs=[pl.BlockSpec(memory_space=pl.ANY), state_spec],
        out_shape=[jax.ShapeDtypeStruct((nseq * n, HW), U32), state_shape],
        scratch_shapes=[pltpu.VMEM((2, t_in, OD_G, HW), U32), pltpu.VMEM((2, OD_TT, OD_G, HW), U32),
                        tile_f32, tile_f32, tile_f32, pltpu.VMEM((2, OD_TT, OD_G, HW), U32),
                        pltpu.VMEM((OD_G, W), F32),
                        pltpu.SemaphoreType.DMA((2,)), pltpu.SemaphoreType.DMA((2,))],
        compiler_params=_params(("arbitrary",)),
        name="rglru_bwd",
    )(p, hf, *prm_b, h0_b)
    return yd, s_f, s_b


def _lru_params(cw, cb, wa, ba, wx, bx, lam):
    cw_pad = jnp.concatenate([cw, jnp.zeros((8 - D_KSIZE, W), F32)], axis=0)
    sp = jax.nn.softplus(-lam.astype(F32)).reshape(1, W)
    return (cw_pad, cb.reshape(1, W), wa.astype(BF16), ba.reshape(1, W), wx.astype(BF16), bx.reshape(1, W), sp)


def _out_body(*refs, n_a, n_b, n_x, packed):
    a_refs = refs[:n_a]
    b_refs = refs[n_a:n_a + n_b]
    x_refs = refs[n_a + n_b:n_a + n_b + n_x]
    (g1_ref, wo_ref, nw_ref, sc_ref, sh_ref, rw_ref, rb_ref,
     x1_ref, hf_ref, info_ref, cnt_ref, carry) = refs[n_a + n_b + n_x:]
    i = pl.program_id(0)

    @pl.when(i == 0)
    def _():
        carry[...] = jnp.zeros_like(carry)

    def branch(refs_, row0):
        v = _tile_value(refs_, TM_OUT)
        if packed:
            lo, hi = _unpack_pair(v, BF16)
            return (jnp.dot(lo, wo_ref[row0:row0 + HW, :], preferred_element_type=F32)
                    + jnp.dot(hi, wo_ref[row0 + HW:row0 + W, :], preferred_element_type=F32))
        return jnp.dot(v, wo_ref[row0:row0 + W, :], preferred_element_type=F32)

    x1 = _tile_value(x_refs, TM_OUT) + g1_ref[...] * (branch(a_refs, 0) + branch(b_refs, W))
    x1_ref[...] = x1
    ms = jnp.mean(x1 * x1, axis=-1, keepdims=True)
    hf = x1 * lax.rsqrt(ms + EPS) * nw_ref[...]
    hf = hf * (1.0 + sc_ref[...]) + sh_ref[...]
    _store_tt(hf_ref, hf, TM_OUT)

    hf_hi = hf.astype(BF16)
    hf_lo = (hf - hf_hi.astype(F32)).astype(BF16)
    l2 = jnp.dot(hf_hi, rw_ref[...], preferred_element_type=F32)
    logits = (l2[:, :LANES] + l2[:, LANES:]
              + jnp.dot(hf_lo, rw_ref[:, :LANES], preferred_element_type=F32) + rb_ref[...])
    lane = lax.broadcasted_iota(jnp.int32, (TM_OUT, LANES), 1).astype(F32)
    neg = -jnp.inf
    gmask = lane < N_GROUPS
    mg = jnp.max(jnp.where(gmask, logits, neg), axis=-1, keepdims=True)
    gstar = jnp.min(jnp.where(gmask & (logits == mg), lane, float(LANES)), axis=-1, keepdims=True)
    denom = jnp.sum(jnp.where(gmask, jnp.exp(logits - mg), 0.0), axis=-1, keepdims=True)
    psel = 1.0 / denom
    lo = EXPERT_LANE0 + EXPERTS_PER_GROUP * gstar
    emask = (lane >= lo) & (lane < lo + EXPERTS_PER_GROUP)
    v1 = jnp.max(jnp.where(emask, logits, neg), axis=-1, keepdims=True)
    i1 = jnp.min(jnp.where(emask & (logits == v1), lane, float(LANES)), axis=-1, keepdims=True)
    em2 = emask & (lane != i1)
    v2 = jnp.max(jnp.where(em2, logits, neg), axis=-1, keepdims=True)
    i2 = jnp.min(jnp.where(em2 & (logits == v2), lane, float(LANES)), axis=-1, keepdims=True)
    e21 = jnp.exp(v2 - v1)
    w1 = psel / (1.0 + e21)
    w2 = psel * e21 / (1.0 + e21)

    memb = jnp.where((lane == i1) | (lane == i2), 1.0, 0.0)
    r_i = lax.broadcasted_iota(jnp.int32, (TM_OUT, TM_OUT), 0)
    c_i = lax.broadcasted_iota(jnp.int32, (TM_OUT, TM_OUT), 1)
    lower = jnp.where(r_i > c_i, 1.0, 0.0).astype(BF16)
    before = jnp.dot(lower, memb.astype(BF16), preferred_element_type=F32) + carry[0:1, :]
    rank1 = jnp.sum(jnp.where(lane == i1, before, 0.0), axis=-1, keepdims=True)
    rank2 = jnp.sum(jnp.where(lane == i2, before, 0.0), axis=-1, keepdims=True)
    new_carry = carry[...] + jnp.sum(memb, axis=0, keepdims=True)
    carry[...] = new_carry
    cnt_ref[...] = new_carry

    e1 = i1 - EXPERT_LANE0
    e2 = i2 - EXPERT_LANE0
    info = jnp.where(lane == 0, e1, jnp.where(lane == 1, e2, jnp.where(lane == 2, w1, jnp.where(
        lane == 3, w2, jnp.where(lane == 4, rank1, jnp.where(lane == 5, rank2, 0.0))))))
    info_ref[...] = info


def _out_proj(a_parts, b_parts, xs, mods, wo_bf16, nw, rw, rb, *, packed):
    tm = TM_OUT
    row = lambda n: pl.BlockSpec((tm, n), lambda i: (i, 0))
    vec = pl.BlockSpec((1, D), lambda i: (0, 0))
    return pl.pallas_call(
        functools.partial(_out_body, n_a=len(a_parts), n_b=len(b_parts), n_x=len(xs), packed=packed),
        grid=(T // tm,),
        in_specs=_row_specs(a_parts, tm, 1) + _row_specs(b_parts, tm, 1) + _row_specs(xs, tm, 1) + [
            _mod_spec(2, tm, 1),
            pl.BlockSpec((D, D), lambda i: (0, 0)), vec, _mod_spec(4, tm, 1), _mod_spec(3, tm, 1),
            pl.BlockSpec((D, 2 * LANES), lambda i: (0, 0)), pl.BlockSpec((1, LANES), lambda i: (0, 0))],
        out_specs=[row(D), pl.BlockSpec((tm * TT_SUB, LANES), lambda i: (i, 0)), row(LANES),
                   pl.BlockSpec((8, LANES), lambda i: (0, 0))],
        out_shape=[jax.ShapeDtypeStruct((T, D), F32), jax.ShapeDtypeStruct((T * TT_SUB, LANES), U32),
                   jax.ShapeDtypeStruct((T, LANES), F32), jax.ShapeDtypeStruct((8, LANES), F32)],
        scratch_shapes=[pltpu.VMEM((8, LANES), F32)],
        compiler_params=_params(("arbitrary",)),
        name="out_proj_router",
    )(*a_parts, *b_parts, *xs, mods, wo_bf16, nw.reshape(1, D), mods, mods, rw, rb)


def _moe_body(be_ref, nused_ref, nv_ref, src_cur, src_nxt, dst_cur, hf_hbm, wg_ref, wu_ref, wd_ref, y_hbm,
              xbuf, obuf, wgc, wuc, wdc, sem_in, sem_out):
    i = pl.program_id(0)
    nused = nused_ref[0]
    slot = i % 2

    def tile(ref, row8):
        return ref.at[pl.ds(pl.multiple_of(row8, TT_SUB), TT_SUB)]

    def rows(ref, n):
        return ref.at[pl.ds(0, n * TT_SUB)]

    def gather(src_ref, s, n):
        def body(r, carry):
            pltpu.make_async_copy(tile(hf_hbm, src_ref[0, r]), tile(xbuf.at[s], r * TT_SUB), sem_in.at[s]).start()
            return carry

        lax.fori_loop(0, n, body, 0)

    def wait_scatter(n):
        pltpu.make_async_copy(rows(obuf, n), rows(y_hbm, n), sem_out).wait()

    @pl.when(i < nused)
    def _():
        nv = nv_ref[i]

        @pl.when(i == 0)
        def _():
            xbuf[...] = jnp.zeros(xbuf.shape, U32)
            gather(src_cur, 0, nv)

        pltpu.make_async_copy(rows(hf_hbm, nv), rows(xbuf.at[slot], nv), sem_in.at[slot]).wait()

        @pl.when(i + 1 < nused)
        def _():
            gather(src_nxt, 1 - slot, nv_ref[i + 1])

        @pl.when((i == 0) | (be_ref[i] != be_ref[jnp.maximum(i - 1, 0)]))
        def _():
            wgc[...] = wg_ref[...].astype(BF16)
            wuc[...] = wu_ref[...].astype(BF16)
            wdc[...] = wd_ref[...].astype(BF16)

        x = _load_tt(xbuf.at[slot], TM_MOE, BF16)
        gate = jnp.dot(x, wgc[...], preferred_element_type=F32)
        up = jnp.dot(x, wuc[...], preferred_element_type=F32)
        hid = (gate * jax.nn.sigmoid(gate) * up).astype(BF16)

        @pl.when(i > 0)
        def _():
            wait_scatter(nv_ref[i - 1])

        _store_tt(obuf, jnp.dot(hid, wdc[...], preferred_element_type=F32), TM_MOE)

        def scatter(r, carry):
            pltpu.make_async_copy(tile(obuf, r * TT_SUB), tile(y_hbm, dst_cur[0, r]), sem_out).start()
            return carry

        lax.fori_loop(0, nv, scatter, 0)

        @pl.when(i == nused - 1)
        def _():
            wait_scatter(nv)


def _moe(hf, plan, wg, wu, wd, layer):
    src, dst, block_e, nused, nvalid = plan
    wspec = lambda shape: pl.BlockSpec((None, None) + shape, lambda i, be, nu, nv: (layer, be[i], 0, 0))
    rows = lambda f: pl.BlockSpec((None, 1, TM_MOE), lambda i, be, nu, nv: (f(i), 0, 0), memory_space=pltpu.SMEM)
    return pl.pallas_call(
        _moe_body,
        grid_spec=pltpu.PrefetchScalarGridSpec(
            num_scalar_prefetch=3,
            grid=(N_BLOCKS,),
            in_specs=[rows(lambda i: i), rows(lambda i: jnp.minimum(i + 1, N_BLOCKS - 1)), rows(lambda i: i),
                      pl.BlockSpec(memory_space=pl.ANY),
                      wspec((D, D_EXPERT)), wspec((D, D_EXPERT)), wspec((D_EXPERT, D))],
            out_specs=pl.BlockSpec(memory_space=pl.ANY),
            scratch_shapes=[pltpu.VMEM((2, TM_MOE * TT_SUB, LANES), U32),
                            pltpu.VMEM((TM_MOE * TT_SUB, LANES), U32),
                            pltpu.VMEM((D, D_EXPERT), BF16), pltpu.VMEM((D, D_EXPERT), BF16),
                            pltpu.VMEM((D_EXPERT, D), BF16),
                            pltpu.SemaphoreType.DMA((2,)), pltpu.SemaphoreType.DMA]),
        out_shape=jax.ShapeDtypeStruct((2 * T * TT_SUB, LANES), U32),
        compiler_params=_params(("arbitrary",), MOE_VMEM_LIMIT),
        name="moe_experts",
    )(block_e, nused, nvalid, src.reshape(N_BLOCKS, 1, TM_MOE), src.reshape(N_BLOCKS, 1, TM_MOE),
      dst.reshape(N_BLOCKS, 1, TM_MOE), hf, wg, wu, wd)


def _dispatch_plan(info, cnt):
    e = info[:, 0:2].astype(jnp.int32)
    rank = info[:, 4:6].astype(jnp.int32)
    counts = cnt[0, EXPERT_LANE0:EXPERT_LANE0 + N_EXPERTS].astype(jnp.int32)
    padded = ((counts + TM_MOE - 1) // TM_MOE) * TM_MOE
    pends = jnp.cumsum(padded)
    pstarts = pends - padded
    dest = pstarts[e] + rank
    codes = jnp.zeros((N_SLOTS,), jnp.int32).at[dest.reshape(-1)].set(
        jnp.arange(2 * T, dtype=jnp.int32), unique_indices=True)
    src = (codes >> 1) * TT_SUB
    dst = ((codes & 1) * T + (codes >> 1)) * TT_SUB
    block0 = jnp.arange(N_BLOCKS, dtype=jnp.int32) * TM_MOE
    block_e = jnp.clip(jnp.searchsorted(pends, block0, side='right', method='compare_all'),
                       0, N_EXPERTS - 1).astype(jnp.int32)
    nvalid = jnp.clip(pstarts[block_e] + counts[block_e] - block0, 0, TM_MOE).astype(jnp.int32)
    nused = (pends[-1] // TM_MOE).astype(jnp.int32).reshape(1)
    return src, dst, block_e, nused, nvalid


def _combine_body(x_ref, y0_ref, y1_ref, info_ref, g2_ref, nw_ref, *o_refs, final):
    info = info_ref[...]
    lane = lax.broadcasted_iota(jnp.int32, info.shape, 1)
    w1 = jnp.sum(jnp.where(lane == 2, info, 0.0), axis=-1, keepdims=True)
    w2 = jnp.sum(jnp.where(lane == 3, info, 0.0), axis=-1, keepdims=True)
    y0 = _load_tt(y0_ref, TM_CMB, F32)
    y1 = _load_tt(y1_ref, TM_CMB, F32)
    x = x_ref[...] + g2_ref[...] * (y0 * w1 + y1 * w2)
    if final:
        ms = jnp.mean(x * x, axis=-1, keepdims=True)
        x = x * lax.rsqrt(ms + EPS) * nw_ref[...]

    def emit(ref):
        ref[...] = x

    _for_tile(o_refs, TM_CMB, emit)


def _combine(x1, y, info, mods, nw, *, final):
    tm = TM_CMB
    nt = T // tm
    if final:
        out_shape = [jax.ShapeDtypeStruct((T_S, D), F32), jax.ShapeDtypeStruct((T_P, D), F32)]
    else:
        out_shape = [jax.ShapeDtypeStruct((T, D), F32)]
    return pl.pallas_call(
        functools.partial(_combine_body, final=final),
        grid=(nt,),
        in_specs=[pl.BlockSpec((tm, D), lambda i: (i, 0)),
                  pl.BlockSpec((tm * TT_SUB, LANES), lambda i: (i, 0)),
                  pl.BlockSpec((tm * TT_SUB, LANES), lambda i: (i + nt, 0)),
                  pl.BlockSpec((tm, LANES), lambda i: (i, 0)),
                  _mod_spec(5, tm, 1),
                  pl.BlockSpec((1, D), lambda i: (0, 0))],
        out_specs=_row_specs(out_shape, tm, 1),
        out_shape=out_shape,
        compiler_params=_params(("arbitrary",)),
        name="combine_final" if final else "combine",
    )(x1, y, y, info, mods, nw.reshape(1, D))


def kernel(x_prompt, x_sample, c, state_ret_fwd, state_ret_bwd, state_lru_fwd, state_lru_bwd, c_ctx, w_mod, b_mod, norm_mix_w, norm_ffn_w, norm_final_w, w_in_even, w_out_even, gmlp_norm_w, gmlp_w_s, gmlp_b_s, ret_decay_fwd, ret_decay_bwd, ret_gn_w, w_in_odd, w_out_odd, conv_w, conv_b, conv_ln_w, conv_ln_b, lru_conv_w, lru_conv_b, lru_wa_fwd, lru_ba_fwd, lru_wx_fwd, lru_bx_fwd, lru_lam_fwd, lru_wa_bwd, lru_ba_bwd, lru_wx_bwd, lru_bx_bwd, lru_lam_bwd, router_grp_w, router_grp_b, router_exp_w, router_exp_b, moe_w_gate, moe_w_up, moe_w_down):
    xs = (x_sample.reshape(T_S, D), x_prompt.reshape(T_P, D))
    cond = jnp.concatenate([c, c_ctx[None, :], jnp.zeros((N_COND_PAD - B_S - 1, D), F32)], axis=0)
    m = _modulation(cond, w_mod, b_mod)
    mods_all = m.reshape(DEPTH, N_COND_PAD, N_MOD, D).transpose(0, 2, 1, 3)[:, :, :, None, :]

    ret_f = ret_b = lru_f = lru_b = None
    for l in range(DEPTH):
        mods = mods_all[l]
        if l % 2 == 0:
            e = l // 2
            p = _norm_in(xs, norm_mix_w[l], mods, w_in_even[e].astype(BF16), pack=False)
            b_full = jnp.repeat(gmlp_b_s[e].T.astype(F32), A_GDIM, axis=1)
            out_a = (_gmlp(p, gmlp_norm_w[e], gmlp_w_s[e].astype(BF16), b_full),)
            tabs = _retention_tables(ret_decay_fwd[e], ret_decay_bwd[e])
            cos, sin = _rope_tables()
            ob_s, _, _ = _retention(p, state_ret_fwd[:, e], state_ret_bwd[:, e], tabs, ret_gn_w[e], cos, sin,
                                    base_chunk=0, nb=B_S, nc=N_S // CHUNK, rope=True)
            zero_state = jnp.zeros((B_P, HEADS, HD, HD), F32)
            ob_p, ret_f, ret_b = _retention(p, zero_state, zero_state, tabs, ret_gn_w[e], cos, sin,
                                            base_chunk=T_S // CHUNK, nb=B_P, nc=N_P // CHUNK, rope=False)
            out_b = (ob_s, ob_p)
            w_out = w_out_even[e]
        else:
            o = l // 2
            p = _norm_in(xs, norm_mix_w[l], mods, w_in_odd[o].astype(BF16), pack=True)
            cw3 = jnp.concatenate([conv_w[o], jnp.zeros((1, W), F32)], axis=0).reshape(
                C_KSIZE + 1, N_STRIPS, LANES).transpose(1, 0, 2)
            conv_args = (cw3, conv_b[o], conv_ln_w[o], conv_ln_b[o])
            prm_f = _lru_params(lru_conv_w[o], lru_conv_b[o], lru_wa_fwd[o], lru_ba_fwd[o], lru_wx_fwd[o],
                                lru_bx_fwd[o], lru_lam_fwd[o])
            prm_b = _lru_params(lru_conv_w[o], lru_conv_b[o], lru_wa_bwd[o], lru_ba_bwd[o], lru_wx_bwd[o],
                                lru_bx_bwd[o], lru_lam_bwd[o])
            latent = dict(seq_row0=0, nseq=B_S, n=N_S)
            context = dict(seq_row0=T_S, nseq=B_P, n=N_P)
            out_a = (_od_conv(p, *conv_args, **latent), _od_conv(p, *conv_args, **context))
            yd_s, _, _ = _od_rglru(p, prm_f, prm_b, state_lru_fwd[:, o], state_lru_bwd[:, o], **latent)
            zero_h = jnp.zeros((B_P, W), F32)
            yd_p, lru_f, lru_b = _od_rglru(p, prm_f, prm_b, zero_h, zero_h, **context)
            out_b = (yd_s, yd_p)
            w_out = w_out_odd[o]

        rw = jnp.concatenate([router_grp_w[l], router_exp_w[l],
                              jnp.zeros((D, LANES - N_GROUPS - N_EXPERTS), F32)], axis=1)
        rw_hi = rw.astype(BF16)
        rw = jnp.concatenate([rw_hi, (rw - rw_hi.astype(F32)).astype(BF16)], axis=1)
        rb = jnp.concatenate([router_grp_b[l], router_exp_b[l],
                              jnp.zeros((LANES - N_GROUPS - N_EXPERTS,), F32)]).reshape(1, LANES)
        x1, hf, info, cnt = _out_proj(out_a, out_b, xs, mods, w_out.astype(BF16), norm_ffn_w[l], rw, rb,
                                      packed=(l % 2 == 1))
        y = _moe(hf, _dispatch_plan(info, cnt), moe_w_gate, moe_w_up, moe_w_down, l)
        xs = tuple(_combine(x1, y, info, mods, norm_final_w, final=(l == DEPTH - 1)))

    y_sample = xs[0].reshape(B_S, N_S, D)
    y_prompt = xs[1].reshape(B_P, N_P, D)
    return (y_prompt, y_sample, ret_f[:, None], ret_b[:, None], lru_f[:, None], lru_b[:, None])
```

```python
import functools

import jax
import jax.numpy as jnp
from jax import lax
from jax.experimental import pallas as pl
from jax.experimental.pallas import tpu as pltpu

F32 = jnp.float32
BF16 = jnp.bfloat16
U32 = jnp.uint32

D = 2048
B_P, N_P = 16, 256
B_S, N_S = 8, 4096
T_S = B_S * N_S
T_P = B_P * N_P
T = T_S + T_P
ROWS_PER_COND = 4096
N_COND_PAD = 16
DEPTH = 2
N_MOD = 6
EPS = 1e-6
GRID_W = 64
ROPE_BASE = 10000.0

W = 1024
HW = W // 2
HEADS = 8
HD = 128
CHUNK = 128
A_GROUPS = 4
A_GDIM = W // A_GROUPS
C_KSIZE = 31
D_KSIZE = 4
LRU_C = 8.0

N_GROUPS = 4
EXPERTS_PER_GROUP = 8
N_EXPERTS = 32
D_EXPERT = 512
LANES = 128
N_STRIPS = W // LANES
EXPERT_LANE0 = N_GROUPS

TM_IN = 512
TN_IN = 2048
TM_OUT = 512
TM_MOE = 512
N_SLOTS = 2 * T + N_EXPERTS * TM_MOE
N_BLOCKS = N_SLOTS // TM_MOE
TM_CMB = 512

VMEM_LIMIT = 56 * 1024 * 1024
MOE_VMEM_LIMIT = 60 * 1024 * 1024


def _params(sem, vmem=VMEM_LIMIT):
    return pltpu.CompilerParams(dimension_semantics=sem, vmem_limit_bytes=vmem)


def _cond_row(i, tm):
    return (i * tm) // ROWS_PER_COND


def _pack_pair(lo, hi):
    lo_b = lax.bitcast_convert_type(lo.astype(BF16).astype(F32), U32)
    hi_b = lax.bitcast_convert_type(hi.astype(BF16).astype(F32), U32)
    return (lo_b >> 16) | hi_b


def _unpack_pair(w, dtype):
    return (lax.bitcast_convert_type(w << 16, F32).astype(dtype),
            lax.bitcast_convert_type(w & U32(0xFFFF0000), F32).astype(dtype))


def _pack_halves(x):
    k = x.shape[1] // 2
    return _pack_pair(x[:, :k], x[:, k:])


def _unpack_halves(w, dtype):
    return jnp.concatenate(_unpack_pair(w, dtype), axis=1)


TT_SUB = 8


def _store_tt(ref, x, rows):
    for c in range(TT_SUB):
        ref[pl.ds(c, rows, stride=TT_SUB), :] = _pack_pair(x[:, c * LANES:(c + 1) * LANES],
                                                           x[:, W + c * LANES:W + (c + 1) * LANES])


def _load_tt(ref, rows, dtype):
    lo, hi = [], []
    for c in range(TT_SUB):
        a, b = _unpack_pair(ref[pl.ds(c, rows, stride=TT_SUB), :], dtype)
        lo.append(a)
        hi.append(b)
    return jnp.concatenate(lo + hi, axis=1)


def _row_specs(arrs, tm, ngrid):
    if len(arrs) == 1:
        maps = [lambda i: i]
    else:
        n_s = T_S // tm
        maps = [lambda i: jnp.minimum(i, n_s - 1), lambda i: jnp.maximum(i - n_s, 0)]
    width = arrs[0].shape[1]
    if ngrid == 1:
        return [pl.BlockSpec((tm, width), lambda i, f=f: (f(i), 0)) for f in maps]
    return [pl.BlockSpec((tm, width), lambda i, j, f=f: (f(i), 0)) for f in maps]


def _tile_value(refs, tm):
    if len(refs) == 1:
        return refs[0][...]
    return jnp.where(pl.program_id(0) < T_S // tm, refs[0][...], refs[1][...])


def _for_tile(refs, tm, fn):
    if len(refs) == 1:
        fn(refs[0])
        return
    i = pl.program_id(0)

    @pl.when(i < T_S // tm)
    def _():
        fn(refs[0])

    @pl.when(i >= T_S // tm)
    def _():
        fn(refs[1])


def _mod_body(c_ref, w_ref, b_ref, o_ref):
    c = c_ref[...]
    a = (c * jax.nn.sigmoid(c)).astype(BF16)
    o_ref[...] = jnp.dot(a, w_ref[...].astype(BF16), preferred_element_type=F32) + b_ref[...]


def _modulation(cond, w_mod, b_mod):
    tn = 1024
    return pl.pallas_call(
        _mod_body,
        grid=(DEPTH, N_MOD * D // tn),
        in_specs=[pl.BlockSpec((N_COND_PAD, D), lambda l, j: (0, 0)),
                  pl.BlockSpec((None, D, tn), lambda l, j: (l, 0, j)),
                  pl.BlockSpec((None, 1, tn), lambda l, j: (l, 0, j))],
        out_specs=pl.BlockSpec((None, N_COND_PAD, tn), lambda l, j: (l, 0, j)),
        out_shape=jax.ShapeDtypeStruct((DEPTH, N_COND_PAD, N_MOD * D), F32),
        compiler_params=_params(("arbitrary", "arbitrary")),
        name="modulation",
    )(cond, w_mod, b_mod.reshape(DEPTH, 1, N_MOD * D))


def _mod_spec(k, tm, ngrid):
    if ngrid == 1:
        return pl.BlockSpec((None, None, 1, D), lambda i: (k, _cond_row(i, tm), 0, 0))
    return pl.BlockSpec((None, None, 1, D), lambda i, j: (k, _cond_row(i, tm), 0, 0))


def _norm_in_body(*refs, pack):
    nw_ref, sc_ref, sh_ref, w_ref, p_ref, h_scr = refs[-6:]

    @pl.when(pl.program_id(1) == 0)
    def _():
        x = _tile_value(refs[:-6], TM_IN)
        ms = jnp.mean(x * x, axis=-1, keepdims=True)
        h = x * lax.rsqrt(ms + EPS) * nw_ref[...]
        h = h * (1.0 + sc_ref[...]) + sh_ref[...]
        h_scr[...] = h.astype(BF16)

    r = jnp.dot(h_scr[...], w_ref[...], preferred_element_type=F32)
    if pack:
        p_ref[...] = jnp.concatenate([_pack_halves(r[:, k * W:(k + 1) * W]) for k in range(TN_IN // W)], axis=1)
    else:
        p_ref[...] = r.astype(BF16)


def _norm_in(xs, nw, mods, w_bf16, *, pack):
    n = w_bf16.shape[1]
    if pack:
        out_spec = pl.BlockSpec((TM_IN, TN_IN // 2), lambda i, j: (i, j))
        out_shape = jax.ShapeDtypeStruct((T, n // 2), U32)
    else:
        out_spec = pl.BlockSpec((TM_IN, TN_IN), lambda i, j: (i, j))
        out_shape = jax.ShapeDtypeStruct((T, n), BF16)
    return pl.pallas_call(
        functools.partial(_norm_in_body, pack=pack),
        grid=(T // TM_IN, n // TN_IN),
        in_specs=_row_specs(xs, TM_IN, 2) + [
            pl.BlockSpec((1, D), lambda i, j: (0, 0)),
            _mod_spec(1, TM_IN, 2),
            _mod_spec(0, TM_IN, 2),
            pl.BlockSpec((D, TN_IN), lambda i, j: (0, j))],
        out_specs=out_spec,
        out_shape=out_shape,
        scratch_shapes=[pltpu.VMEM((TM_IN, D), BF16)],
        compiler_params=_params(("arbitrary", "arbitrary")),
        name="norm_in",
    )(*xs, nw.reshape(1, D), mods, mods, w_bf16)


GMLP_ROWS = 512


def _gmlp_body(u_ref, v_ref, lnw_ref, ws_ref, b_ref, o_ref):
    for c in range(GMLP_ROWS // CHUNK):
        rows = pl.ds(c * CHUNK, CHUNK)
        v = jax.nn.gelu(v_ref[rows, :].astype(F32))
        mu = jnp.mean(v, axis=-1, keepdims=True)
        var = jnp.mean(jnp.square(v - mu), axis=-1, keepdims=True)
        vn = ((v - mu) * lax.rsqrt(var + EPS) * lnw_ref[...]).astype(BF16)
        z = jnp.concatenate(
            [jnp.dot(ws_ref[g], vn[:, g * A_GDIM:(g + 1) * A_GDIM], preferred_element_type=F32)
             for g in range(A_GROUPS)], axis=1) + b_ref[...]
        u = jax.nn.gelu(u_ref[rows, :].astype(F32))
        o_ref[rows, :] = (u * z).astype(BF16)


def _gmlp(p, lnw, ws_bf16, b_full):
    return pl.pallas_call(
        _gmlp_body,
        grid=(T // GMLP_ROWS,),
        in_specs=[pl.BlockSpec((GMLP_ROWS, W), lambda i: (i, 0)),
                  pl.BlockSpec((GMLP_ROWS, W), lambda i: (i, 1)),
                  pl.BlockSpec((1, W), lambda i: (0, 0)),
                  pl.BlockSpec((A_GROUPS, CHUNK, CHUNK), lambda i: (0, 0, 0)),
                  pl.BlockSpec((CHUNK, W), lambda i: (0, 0))],
        out_specs=pl.BlockSpec((GMLP_ROWS, W), lambda i: (i, 0)),
        out_shape=jax.ShapeDtypeStruct((T, W), BF16),
        compiler_params=_params(("arbitrary",)),
        name="gmlp",
    )(p, p, lnw.reshape(1, W), ws_bf16, b_full)


def _roped_qk(q_ref, k_ref, cos_ref, sin_ref, h, rope):
    cols = slice(h * HD, (h + 1) * HD)
    q = q_ref[:, cols].astype(F32)
    k = k_ref[:, cols].astype(F32) * (HD ** -0.5)
    if rope:
        c = cos_ref[...]
        s = sin_ref[...]
        q = q * c + pltpu.roll(q, HD // 2, axis=1) * s
        k = k * c + pltpu.roll(k, HD // 2, axis=1) * s
    return q, k


def _kt_v(kd, v):
    return lax.dot_general(kd, v, (((0,), (0,)), ((), ())), preferred_element_type=F32)


def _ret_bwd_body(q_ref, k_ref, v_ref, cos_ref, sin_ref, s0_ref, dq_ref, dk_ref, dc_ref,
                  ob_ref, sfin_ref, s_scr, *, rope, nc):
    c = pl.program_id(1)

    @pl.when(c == 0)
    def _():
        s_scr[...] = s0_ref[...]

    for h in range(HEADS):
        cols = slice(h * HD, (h + 1) * HD)
        q, k = _roped_qk(q_ref, k_ref, cos_ref, sin_ref, h, rope)
        v = v_ref[:, cols]
        s = s_scr[h]
        ob_ref[:, cols] = jnp.dot(q.astype(BF16), s.astype(BF16), preferred_element_type=F32) * dq_ref[h]
        kd = (k * dk_ref[h]).astype(BF16)
        s_scr[h] = s * dc_ref[h] + _kt_v(kd, v)

    @pl.when(c == nc - 1)
    def _():
        sfin_ref[...] = s_scr[...]


def _ret_fwd_body(q_ref, k_ref, v_ref, g_ref, cos_ref, sin_ref, ob_ref, s0_ref, m_ref, dq_ref, dk_ref, dc_ref,
                  gn_ref, o_ref, sfin_ref, s_scr, *, rope, nc):
    c = pl.program_id(1)

    @pl.when(c == 0)
    def _():
        s_scr[...] = s0_ref[...]

    for h in range(HEADS):
        cols = slice(h * HD, (h + 1) * HD)
        q, k = _roped_qk(q_ref, k_ref, cos_ref, sin_ref, h, rope)
        qb = q.astype(BF16)
        v = v_ref[:, cols]
        s = s_scr[h]
        scores = lax.dot_general(qb, k.astype(BF16), (((1,), (1,)), ((), ())),
                                 preferred_element_type=F32) * m_ref[h]
        o = (jnp.dot(scores.astype(BF16), v, preferred_element_type=F32)
             + jnp.dot(qb, s.astype(BF16), preferred_element_type=F32) * dq_ref[h]
             + ob_ref[:, cols])
        kd = (k * dk_ref[h]).astype(BF16)
        s_scr[h] = s * dc_ref[h] + _kt_v(kd, v)
        mu = jnp.mean(o, axis=-1, keepdims=True)
        var = jnp.mean(jnp.square(o - mu), axis=-1, keepdims=True)
        on = (o - mu) * lax.rsqrt(var + EPS) * gn_ref[:, cols]
        g = g_ref[:, cols].astype(F32)
        o_ref[:, cols] = (g * jax.nn.sigmoid(g) * on).astype(BF16)

    @pl.when(c == nc - 1)
    def _():
        sfin_ref[...] = s_scr[...]


def _retention(p, s_f0, s_b0, tabs, gn_w, cos, sin, *, base_chunk, nb, nc, rope):
    m_tab, dq_f, dk_f, dc_f, dq_b, dk_b, dc_b = tabs
    state_spec = pl.BlockSpec((None, HEADS, HD, HD), lambda b, c: (b, 0, 0, 0))
    tab_spec = pl.BlockSpec((HEADS, HD, HD), lambda b, c: (0, 0, 0))
    dc_spec = pl.BlockSpec((HEADS, 1, HD), lambda b, c: (0, 0, 0))
    state_shape = jax.ShapeDtypeStruct((nb, HEADS, HD, HD), F32)

    def col(j, rev):
        if rev:
            return pl.BlockSpec((CHUNK, W), lambda b, c: (base_chunk + b * nc + nc - 1 - c, j))
        return pl.BlockSpec((CHUNK, W), lambda b, c: (base_chunk + b * nc + c, j))

    def rope_spec(rev):
        if not rope:
            return pl.BlockSpec((CHUNK, HD), lambda b, c: (0, 0))
        if rev:
            return pl.BlockSpec((CHUNK, HD), lambda b, c: (nc - 1 - c, 0))
        return pl.BlockSpec((CHUNK, HD), lambda b, c: (c, 0))

    n_rows = nb * nc * CHUNK
    ob, s_b = pl.pallas_call(
        functools.partial(_ret_bwd_body, rope=rope, nc=nc),
        grid=(nb, nc),
        in_specs=[col(2, True), col(3, True), col(4, True), rope_spec(True), rope_spec(True),
                  state_spec, tab_spec, tab_spec, dc_spec],
        out_specs=[pl.BlockSpec((CHUNK, W), lambda b, c: (b * nc + nc - 1 - c, 0)), state_spec],
        out_shape=[jax.ShapeDtypeStruct((n_rows, W), F32), state_shape],
        scratch_shapes=[pltpu.VMEM((HEADS, HD, HD), F32)],
        compiler_params=_params(("arbitrary", "arbitrary")),
        name="retention_bwd",
    )(p, p, p, cos, sin, s_b0, dq_b, dk_b, dc_b)

    o, s_f = pl.pallas_call(
        functools.partial(_ret_fwd_body, rope=rope, nc=nc),
        grid=(nb, nc),
        in_specs=[col(2, False), col(3, False), col(4, False), col(5, False), rope_spec(False), rope_spec(False),
                  pl.BlockSpec((CHUNK, W), lambda b, c: (b * nc + c, 0)),
                  state_spec, tab_spec, tab_spec, tab_spec, dc_spec,
                  pl.BlockSpec((1, W), lambda b, c: (0, 0))],
        out_specs=[pl.BlockSpec((CHUNK, W), lambda b, c: (b * nc + c, 0)), state_spec],
        out_shape=[jax.ShapeDtypeStruct((n_rows, W), BF16), state_shape],
        scratch_shapes=[pltpu.VMEM((HEADS, HD, HD), F32)],
        compiler_params=_params(("arbitrary", "arbitrary")),
        name="retention_fwd",
    )(p, p, p, p, cos, sin, ob, s_f0, m_tab, dq_f, dk_f, dc_f, gn_w.reshape(1, W))
    return o, s_f, s_b


def _retention_tables(decay_fwd, decay_bwd):
    lg_f = jax.nn.log_sigmoid(decay_fwd.astype(F32))[:, None, None]
    lg_b = jax.nn.log_sigmoid(decay_bwd.astype(F32))[:, None, None]
    pos = jnp.arange(CHUNK, dtype=F32)
    rel = pos[:, None] - pos[None, :]
    m_tab = (jnp.where(rel >= 0, jnp.exp(lg_f * jnp.maximum(rel, 0.0)), 0.0)
             + jnp.where(rel <= 0, jnp.exp(lg_b * jnp.maximum(-rel, 0.0)), 0.0))
    ones = jnp.ones((1, 1, HD), F32)
    col = pos[None, :, None]
    dq_f = jnp.exp(lg_f * (col + 1.0)) * ones
    dk_f = jnp.exp(lg_f * (CHUNK - 1.0 - col)) * ones
    dq_b = jnp.exp(lg_b * (CHUNK - col)) * ones
    dk_b = jnp.exp(lg_b * col) * ones
    dc_f = jnp.exp(lg_f * CHUNK) * ones
    dc_b = jnp.exp(lg_b * CHUNK) * ones
    return m_tab, dq_f, dk_f, dc_f, dq_b, dk_b, dc_b


def _rope_tables():
    rows = N_S // GRID_W
    row = jnp.repeat(jnp.arange(rows, dtype=F32), GRID_W)
    colp = jnp.tile(jnp.arange(GRID_W, dtype=F32), rows)
    quarter = HD // 4
    inv_freq = ROPE_BASE ** (-jnp.arange(quarter, dtype=F32) / quarter)
    ang = jnp.concatenate([row[:, None] * inv_freq, colp[:, None] * inv_freq], axis=-1)
    cos, sin = jnp.cos(ang), jnp.sin(ang)
    return jnp.concatenate([cos, cos], axis=-1), jnp.concatenate([-sin, sin], axis=-1)


OD_TT = 128
OD_G = 8
CONV_HALO = 16
LRU_HALO = 8


def _od_rows(seq_row0, n, g, tj):
    return [seq_row0 + (g * OD_G + b) * n + tj * OD_TT for b in range(OD_G)]


def _od_in_copies(p_hbm, buf, sem, *, col, halo, seq_row0, n, g, tj):
    nt = n // OD_TT
    out = []
    for b, base in enumerate(_od_rows(seq_row0, n, g, tj)):
        def src(r, k):
            return p_hbm.at[pl.ds(r, k), pl.ds(col * HW, HW)]

        out.append((None, pltpu.make_async_copy(src(base, OD_TT), buf.at[pl.ds(halo, OD_TT), b, :], sem)))
        if halo:
            out.append((tj > 0, pltpu.make_async_copy(src(base - halo, halo), buf.at[pl.ds(0, halo), b, :], sem)))
            out.append((tj < nt - 1, pltpu.make_async_copy(src(base + OD_TT, halo),
                                                           buf.at[pl.ds(halo + OD_TT, halo), b, :], sem)))
    return out


def _od_run(copies, op):
    for cond, cp in copies:
        fn = cp.start if op == "start" else cp.wait
        if cond is None:
            fn()
        else:
            pl.when(cond)(fn)


def _od_zero_halo(buf, halo, n, tj):
    nt = n // OD_TT
    zeros = jnp.zeros((halo, OD_G, HW), U32)

    @pl.when(tj == 0)
    def _():
        buf[0:halo] = zeros

    @pl.when(tj == nt - 1)
    def _():
        buf[halo + OD_TT:2 * halo + OD_TT] = zeros


def _od_out_copies(obuf, out_hbm, sem, *, seq_row0, n, g, tj):
    return [(None, pltpu.make_async_copy(obuf.at[:, b, :], out_hbm.at[pl.ds(base - seq_row0, OD_TT), :], sem))
            for b, base in enumerate(_od_rows(seq_row0, n, g, tj))]


def _od_unpack(buf, rows):
    return _unpack_halves(buf[...].reshape(rows, HW), F32)


def _od_pipeline(q, nq, in_copies, sem_slots=2):
    s = q % 2

    @pl.when(q == 0)
    def _():
        _od_run(in_copies(q, s), "start")

    _od_run(in_copies(q, s), "wait")

    @pl.when(q + 1 < nq)
    def _():
        _od_run(in_copies(q + 1, 1 - s), "start")

    return s


def _od_emit(q, nq, s, obuf, words, out_copies):
    @pl.when(q >= 2)
    def _():
        _od_run(out_copies(s), "wait")

    obuf[s] = words.reshape(OD_TT, OD_G, HW)
    _od_run(out_copies(s), "start")

    @pl.when(q == nq - 1)
    def _():
        _od_run(out_copies(s), "wait")

        @pl.when(q >= 1)
        def _():
            _od_run(out_copies(1 - s), "wait")


CONV_RB = 64


def _od_conv_body(p_hbm, cw_ref, cb_ref, lnw_ref, lnb_ref, yc_hbm, xin, glu, ybuf, obuf, sem_in, sem_out,
                  *, seq_row0, n, ng):
    nt = n // OD_TT
    nq = ng * nt
    q = pl.program_id(0)
    halo = CONV_HALO
    rows_in = (OD_TT + 2 * halo) * OD_G
    rows = OD_TT * OD_G

    def in_copies(step, slot):
        out = []
        for br in range(2):
            out += _od_in_copies(p_hbm, xin.at[br, slot], sem_in.at[slot], col=br, halo=halo,
                                 seq_row0=seq_row0, n=n, g=step // nt, tj=step % nt)
        return out

    s = _od_pipeline(q, nq, in_copies)
    g, tj = q // nt, q % nt
    for br in range(2):
        _od_zero_halo(xin.at[br, s], halo, n, tj)
    glu[...] = _od_unpack(xin.at[0, s], rows_in) * jax.nn.sigmoid(_od_unpack(xin.at[1, s], rows_in))

    def strip(si, carry):
        lanes = pl.ds(pl.multiple_of(si * LANES, LANES), LANES)
        for rb in range(rows // CONV_RB):
            acc = jnp.zeros((CONV_RB // 8, 8, LANES), F32)
            for k in range(C_KSIZE):
                wk = cw_ref[si, pl.ds(k, 1), :]
                r0 = rb * CONV_RB + (halo - C_KSIZE // 2 + k) * OD_G
                acc = acc + glu[r0:r0 + CONV_RB, lanes].reshape(CONV_RB // 8, 8, LANES) * wk[None]
            ybuf[rb * CONV_RB:(rb + 1) * CONV_RB, lanes] = acc.reshape(CONV_RB, LANES)
        return carry

    lax.fori_loop(0, N_STRIPS, strip, 0)
    y = ybuf[...] + cb_ref[...]
    mu = jnp.mean(y, axis=-1, keepdims=True)
    var = jnp.mean(jnp.square(y - mu), axis=-1, keepdims=True)
    yn = (y - mu) * lax.rsqrt(var + EPS) * lnw_ref[...] + lnb_ref[...]
    yo = yn * jax.nn.sigmoid(yn)

    def out_copies(slot):
        return _od_out_copies(obuf.at[slot], yc_hbm, sem_out.at[slot], seq_row0=seq_row0, n=n, g=g, tj=tj)

    _od_emit(q, nq, s, obuf, _pack_halves(yo), out_copies)


def _od_conv(p, cw3, cb, lnw, lnb, *, seq_row0, nseq, n):
    ng = nseq // OD_G
    nq = ng * (n // OD_TT)
    vec = pl.BlockSpec((1, W), lambda q: (0, 0))
    t_in = OD_TT + 2 * CONV_HALO
    return pl.pallas_call(
        functools.partial(_od_conv_body, seq_row0=seq_row0, n=n, ng=ng),
        grid=(nq,),
        in_specs=[pl.BlockSpec(memory_space=pl.ANY),
                  pl.BlockSpec((N_STRIPS, C_KSIZE + 1, LANES), lambda q: (0, 0, 0)), vec, vec, vec],
        out_specs=pl.BlockSpec(memory_space=pl.ANY),
        out_shape=jax.ShapeDtypeStruct((nseq * n, HW), U32),
        scratch_shapes=[pltpu.VMEM((2, 2, t_in, OD_G, HW), U32),
                        pltpu.VMEM((t_in * OD_G, W), F32),
                        pltpu.VMEM((OD_TT * OD_G, W), F32),
                        pltpu.VMEM((2, OD_TT, OD_G, HW), U32),
                        pltpu.SemaphoreType.DMA((2,)), pltpu.SemaphoreType.DMA((2,))],
        compiler_params=_params(("arbitrary",)),
        name="conv_module",
    )(p, cw3, cb.reshape(1, W), lnw.reshape(1, W), lnb.reshape(1, W))


def _od_gates(x, cw_ref, cb_ref, wa_ref, ba_ref, wx_ref, bx_ref, sp_ref, a_scr, b_scr):
    rows = OD_TT * OD_G
    xc = cb_ref[...]
    for k in range(D_KSIZE):
        r0 = (LRU_HALO - D_KSIZE // 2 + k) * OD_G
        xc = xc + cw_ref[k:k + 1, :] * x[r0:r0 + rows, :]
    xb = xc.astype(BF16)
    r = jnp.concatenate([jnp.dot(xb[:, h * HD:(h + 1) * HD], wa_ref[h], preferred_element_type=F32)
                         for h in range(HEADS)], axis=1) + ba_ref[...]
    g = jnp.concatenate([jnp.dot(xb[:, h * HD:(h + 1) * HD], wx_ref[h], preferred_element_type=F32)
                         for h in range(HEADS)], axis=1) + bx_ref[...]
    log_a = (-LRU_C) * jax.nn.sigmoid(r) * sp_ref[...]
    th = jnp.tanh(log_a)
    one_minus_a2 = -2.0 * th / (1.0 - th)
    a_scr[...] = jnp.exp(log_a)
    b_scr[...] = jnp.sqrt(jnp.maximum(one_minus_a2, 0.0)) * (jax.nn.sigmoid(g) * xc)


def _od_scan(a_scr, b_scr, out_ref, h, reverse):
    def step(t, h):
        tr = (OD_TT - 1 - t) if reverse else t
        rows = pl.ds(pl.multiple_of(tr * OD_G, OD_G), OD_G)
        h = a_scr[rows, :] * h + b_scr[rows, :]
        out_ref[rows, :] = h
        return h

    return lax.fori_loop(0, OD_TT, step, h, unroll=8)


def _od_lru_fwd_body(p_hbm, cw_ref, cb_ref, wa_ref, ba_ref, wx_ref, bx_ref, sp_ref, h0_ref,
                     hf_ref, hfin_ref, xin, a_scr, b_scr, h_scr, sem_in, *, seq_row0, n, ng):
    nt = n // OD_TT
    nq = ng * nt
    q = pl.program_id(0)

    def in_copies(step, slot):
        return _od_in_copies(p_hbm, xin.at[slot], sem_in.at[slot], col=2, halo=LRU_HALO,
                             seq_row0=seq_row0, n=n, g=step // nt, tj=step % nt)

    s = _od_pipeline(q, nq, in_copies)
    tj = q % nt
    _od_zero_halo(xin.at[s], LRU_HALO, n, tj)

    @pl.when(tj == 0)
    def _():
        h_scr[...] = h0_ref[...]

    x = _od_unpack(xin.at[s], (OD_TT + 2 * LRU_HALO) * OD_G)
    _od_gates(x, cw_ref, cb_ref, wa_ref, ba_ref, wx_ref, bx_ref, sp_ref, a_scr, b_scr)
    h = _od_scan(a_scr, b_scr, hf_ref, h_scr[...], False)
    h_scr[...] = h

    @pl.when(tj == nt - 1)
    def _():
        hfin_ref[...] = h


def _od_lru_bwd_body(p_hbm, hf_ref, cw_ref, cb_ref, wa_ref, ba_ref, wx_ref, bx_ref, sp_ref, h0_ref,
                     yd_hbm, hfin_ref, xin, gin, a_scr, b_scr, hb_scr, obuf, h_scr, sem_in, sem_out,
                     *, seq_row0, n, ng):
    nt = n // OD_TT
    nq = ng * nt
    q = pl.program_id(0)

    def in_copies(step, slot):
        where = dict(seq_row0=seq_row0, n=n, g=step // nt, tj=nt - 1 - step % nt)
        return (_od_in_copies(p_hbm, xin.at[slot], sem_in.at[slot], col=2, halo=LRU_HALO, **where)
                + _od_in_copies(p_hbm, gin.at[slot], sem_in.at[slot], col=3, halo=0, **where))

    s = _od_pipeline(q, nq, in_copies)
    g, tj = q // nt, nt - 1 - q % nt
    _od_zero_halo(xin.at[s], LRU_HALO, n, tj)

    @pl.when(tj == nt - 1)
    def _():
        h_scr[...] = h0_ref[...]

    x = _od_unpack(xin.at[s], (OD_TT + 2 * LRU_HALO) * OD_G)
    _od_gates(x, cw_ref, cb_ref, wa_ref, ba_ref, wx_ref, bx_ref, sp_ref, a_scr, b_scr)
    h = _od_scan(a_scr, b_scr, hb_scr, h_scr[...], True)
    h_scr[...] = h

    @pl.when(tj == 0)
    def _():
        hfin_ref[...] = h

    gd = _od_unpack(gin.at[s], OD_TT * OD_G)
    yd = (hf_ref[...] + hb_scr[...]) * jax.nn.gelu(gd)

    def out_copies(slot):
        return _od_out_copies(obuf.at[slot], yd_hbm, sem_out.at[slot], seq_row0=seq_row0, n=n, g=g, tj=tj)

    _od_emit(q, nq, s, obuf, _pack_halves(yd), out_copies)


def _od_rglru(p, prm_f, prm_b, h0_f, h0_b, *, seq_row0, nseq, n):
    ng = nseq // OD_G
    nt = n // OD_TT
    nq = ng * nt
    rows = OD_TT * OD_G
    vec = pl.BlockSpec((1, W), lambda q: (0, 0))
    wsp = pl.BlockSpec((HEADS, HD, HD), lambda q: (0, 0, 0))
    prm_specs = [pl.BlockSpec((8, W), lambda q: (0, 0)), vec, wsp, vec, wsp, vec, vec,
                 pl.BlockSpec((OD_G, W), lambda q: (q // nt, 0))]
    state_spec = pl.BlockSpec((OD_G, W), lambda q: (q // nt, 0))
    state_shape = jax.ShapeDtypeStruct((nseq, W), F32)
    t_in = OD_TT + 2 * LRU_HALO
    tile_f32 = pltpu.VMEM((rows, W), F32)

    hf, s_f = pl.pallas_call(
        functools.partial(_od_lru_fwd_body, seq_row0=seq_row0, n=n, ng=ng),
        grid=(nq,),
        in_specs=[pl.BlockSpec(memory_space=pl.ANY)] + prm_specs,
        out_specs=[pl.BlockSpec((rows, W), lambda q: (q, 0)), state_spec],
        out_shape=[jax.ShapeDtypeStruct((nq * rows, W), F32), state_shape],
        scratch_shapes=[pltpu.VMEM((2, t_in, OD_G, HW), U32), tile_f32, tile_f32, pltpu.VMEM((OD_G, W), F32),
                        pltpu.SemaphoreType.DMA((2,))],
        compiler_params=_params(("arbitrary",)),
        name="rglru_fwd",
    )(p, *prm_f, h0_f)

    yd, s_b = pl.pallas_call(
        functools.partial(_od_lru_bwd_body, seq_row0=seq_row0, n=n, ng=ng),
        grid=(nq,),
        in_specs=[pl.BlockSpec(memory_space=pl.ANY),
                  pl.BlockSpec((rows, W), lambda q: ((q // nt) * nt + nt - 1 - q % nt, 0))] + prm_specs,
        out_specs=[pl.BlockSpec(memory_space=pl.ANY), state_spec],
        out_shape=[jax.ShapeDtypeStruct((nseq * n, HW), U32), state_shape],
        scratch_shapes=[pltpu.VMEM((2, t_in, OD_G, HW), U32), pltpu.VMEM((2, OD_TT, OD_G, HW), U32),
                        tile_f32, tile_f32, tile_f32, pltpu.VMEM((2, OD_TT, OD_G, HW), U32),
                        pltpu.VMEM((OD_G, W), F32),
                        pltpu.SemaphoreType.DMA((2,)), pltpu.SemaphoreType.DMA((2,))],
        compiler_params=_params(("arbitrary",)),
        name="rglru_bwd",
    )(p, hf, *prm_b, h0_b)
    return yd, s_f, s_b


def _lru_params(cw, cb, wa, ba, wx, bx, lam):
    cw_pad = jnp.concatenate([cw, jnp.zeros((8 - D_KSIZE, W), F32)], axis=0)
    sp = jax.nn.softplus(-lam.astype(F32)).reshape(1, W)
    return (cw_pad, cb.reshape(1, W), wa.astype(BF16), ba.reshape(1, W), wx.astype(BF16), bx.reshape(1, W), sp)


def _out_body(*refs, n_a, n_b, n_x, packed):
    a_refs = refs[:n_a]
    b_refs = refs[n_a:n_a + n_b]
    x_refs = refs[n_a + n_b:n_a + n_b + n_x]
    (g1_ref, wo_ref, nw_ref, sc_ref, sh_ref, rw_ref, rb_ref,
     x1_ref, hf_ref, info_ref, cnt_ref, carry) = refs[n_a + n_b + n_x:]
    i = pl.program_id(0)

    @pl.when(i == 0)
    def _():
        carry[...] = jnp.zeros_like(carry)

    def branch(refs_, row0):
        v = _tile_value(refs_, TM_OUT)
        if packed:
            lo, hi = _unpack_pair(v, BF16)
            return (jnp.dot(lo, wo_ref[row0:row0 + HW, :], preferred_element_type=F32)
                    + jnp.dot(hi, wo_ref[row0 + HW:row0 + W, :], preferred_element_type=F32))
        return jnp.dot(v, wo_ref[row0:row0 + W, :], preferred_element_type=F32)

    x1 = _tile_value(x_refs, TM_OUT) + g1_ref[...] * (branch(a_refs, 0) + branch(b_refs, W))
    x1_ref[...] = x1
    ms = jnp.mean(x1 * x1, axis=-1, keepdims=True)
    hf = x1 * lax.rsqrt(ms + EPS) * nw_ref[...]
    hf = hf * (1.0 + sc_ref[...]) + sh_ref[...]
    _store_tt(hf_ref, hf, TM_OUT)

    hf_hi = hf.astype(BF16)
    hf_lo = (hf - hf_hi.astype(F32)).astype(BF16)
    l2 = jnp.dot(hf_hi, rw_ref[...], preferred_element_type=F32)
    logits = (l2[:, :LANES] + l2[:, LANES:]
              + jnp.dot(hf_lo, rw_ref[:, :LANES], preferred_element_type=F32) + rb_ref[...])
    lane = lax.broadcasted_iota(jnp.int32, (TM_OUT, LANES), 1).astype(F32)
    neg = -jnp.inf
    gmask = lane < N_GROUPS
    mg = jnp.max(jnp.where(gmask, logits, neg), axis=-1, keepdims=True)
    gstar = jnp.min(jnp.where(gmask & (logits == mg), lane, float(LANES)), axis=-1, keepdims=True)
    denom = jnp.sum(jnp.where(gmask, jnp.exp(logits - mg), 0.0), axis=-1, keepdims=True)
    psel = 1.0 / denom
    lo = EXPERT_LANE0 + EXPERTS_PER_GROUP * gstar
    emask = (lane >= lo) & (lane < lo + EXPERTS_PER_GROUP)
    v1 = jnp.max(jnp.where(emask, logits, neg), axis=-1, keepdims=True)
    i1 = jnp.min(jnp.where(emask & (logits == v1), lane, float(LANES)), axis=-1, keepdims=True)
    em2 = emask & (lane != i1)
    v2 = jnp.max(jnp.where(em2, logits, neg), axis=-1, keepdims=True)
    i2 = jnp.min(jnp.where(em2 & (logits == v2), lane, float(LANES)), axis=-1, keepdims=True)
    e21 = jnp.exp(v2 - v1)
    w1 = psel / (1.0 + e21)
    w2 = psel * e21 / (1.0 + e21)

    memb = jnp.where((lane == i1) | (lane == i2), 1.0, 0.0)
    r_i = lax.broadcasted_iota(jnp.int32, (TM_OUT, TM_OUT), 0)
    c_i = lax.broadcasted_iota(jnp.int32, (TM_OUT, TM_OUT), 1)
    lower = jnp.where(r_i > c_i, 1.0, 0.0).astype(BF16)
    before = jnp.dot(lower, memb.astype(BF16), preferred_element_type=F32) + carry[0:1, :]
    rank1 = jnp.sum(jnp.where(lane == i1, before, 0.0), axis=-1, keepdims=True)
    rank2 = jnp.sum(jnp.where(lane == i2, before, 0.0), axis=-1, keepdims=True)
    new_carry = carry[...] + jnp.sum(memb, axis=0, keepdims=True)
    carry[...] = new_carry
    cnt_ref[...] = new_carry

    e1 = i1 - EXPERT_LANE0
    e2 = i2 - EXPERT_LANE0
    info = jnp.where(lane == 0, e1, jnp.where(lane == 1, e2, jnp.where(lane == 2, w1, jnp.where(
        lane == 3, w2, jnp.where(lane == 4, rank1, jnp.where(lane == 5, rank2, 0.0))))))
    info_ref[...] = info


def _out_proj(a_parts, b_parts, xs, mods, wo_bf16, nw, rw, rb, *, packed):
    tm = TM_OUT
    row = lambda n: pl.BlockSpec((tm, n), lambda i: (i, 0))
    vec = pl.BlockSpec((1, D), lambda i: (0, 0))
    return pl.pallas_call(
        functools.partial(_out_body, n_a=len(a_parts), n_b=len(b_parts), n_x=len(xs), packed=packed),
        grid=(T // tm,),
        in_specs=_row_specs(a_parts, tm, 1) + _row_specs(b_parts, tm, 1) + _row_specs(xs, tm, 1) + [
            _mod_spec(2, tm, 1),
            pl.BlockSpec((D, D), lambda i: (0, 0)), vec, _mod_spec(4, tm, 1), _mod_spec(3, tm, 1),
            pl.BlockSpec((D, 2 * LANES), lambda i: (0, 0)), pl.BlockSpec((1, LANES), lambda i: (0, 0))],
        out_specs=[row(D), pl.BlockSpec((tm * TT_SUB, LANES), lambda i: (i, 0)), row(LANES),
                   pl.BlockSpec((8, LANES), lambda i: (0, 0))],
        out_shape=[jax.ShapeDtypeStruct((T, D), F32), jax.ShapeDtypeStruct((T * TT_SUB, LANES), U32),
                   jax.ShapeDtypeStruct((T, LANES), F32), jax.ShapeDtypeStruct((8, LANES), F32)],
        scratch_shapes=[pltpu.VMEM((8, LANES), F32)],
        compiler_params=_params(("arbitrary",)),
        name="out_proj_router",
    )(*a_parts, *b_parts, *xs, mods, wo_bf16, nw.reshape(1, D), mods, mods, rw, rb)


COPY_UNROLL = 16


def _moe_body(be_ref, nused_ref, nv_ref, src_cur, src_nxt, dst_cur, hf_hbm, wg_ref, wu_ref, wd_ref, y_hbm,
              xbuf, obuf, wgc, wuc, wdc, sem_in, sem_out):
    i = pl.program_id(0)
    nused = nused_ref[0]
    slot = i % 2

    def tile(ref, row8):
        return ref.at[pl.ds(pl.multiple_of(row8, TT_SUB), TT_SUB)]

    def rows(ref, n):
        return ref.at[pl.ds(0, n * TT_SUB)]

    def for_rows(n, fn):
        groups = n // COPY_UNROLL

        def group(c, carry):
            for u in range(COPY_UNROLL):
                fn(c * COPY_UNROLL + u)
            return carry

        def single(r, carry):
            fn(r)
            return carry

        lax.fori_loop(0, groups, group, 0)
        lax.fori_loop(groups * COPY_UNROLL, n, single, 0)

    def gather(src_ref, s, n):
        def row(r):
            pltpu.make_async_copy(tile(hf_hbm, src_ref[0, r]), tile(xbuf.at[s], r * TT_SUB), sem_in.at[s]).start()

        for_rows(n, row)

    def wait_scatter(n):
        pltpu.make_async_copy(rows(obuf, n), rows(y_hbm, n), sem_out).wait()

    @pl.when(i < nused)
    def _():
        nv = nv_ref[i]

        @pl.when(i == 0)
        def _():
            xbuf[...] = jnp.zeros(xbuf.shape, U32)
            gather(src_cur, 0, nv)

        pltpu.make_async_copy(rows(hf_hbm, nv), rows(xbuf.at[slot], nv), sem_in.at[slot]).wait()

        @pl.when(i + 1 < nused)
        def _():
            gather(src_nxt, 1 - slot, nv_ref[i + 1])

        @pl.when((i == 0) | (be_ref[i] != be_ref[jnp.maximum(i - 1, 0)]))
        def _():
            wgc[...] = wg_ref[...].astype(BF16)
            wuc[...] = wu_ref[...].astype(BF16)
            wdc[...] = wd_ref[...].astype(BF16)

        x = _load_tt(xbuf.at[slot], TM_MOE, BF16)
        gate = jnp.dot(x, wgc[...], preferred_element_type=F32)
        up = jnp.dot(x, wuc[...], preferred_element_type=F32)
        hid = (gate * jax.nn.sigmoid(gate) * up).astype(BF16)

        @pl.when(i > 0)
        def _():
            wait_scatter(nv_ref[i - 1])

        _store_tt(obuf, jnp.dot(hid, wdc[...], preferred_element_type=F32), TM_MOE)

        def scatter_row(r):
            pltpu.make_async_copy(tile(obuf, r * TT_SUB), tile(y_hbm, dst_cur[0, r]), sem_out).start()

        for_rows(nv, scatter_row)

        @pl.when(i == nused - 1)
        def _():
            wait_scatter(nv)


def _moe(hf, plan, wg, wu, wd, layer):
    src, dst, block_e, nused, nvalid = plan
    wspec = lambda shape: pl.BlockSpec((None, None) + shape, lambda i, be, nu, nv: (layer, be[i], 0, 0))
    rows = lambda f: pl.BlockSpec((None, 1, TM_MOE), lambda i, be, nu, nv: (f(i), 0, 0), memory_space=pltpu.SMEM)
    return pl.pallas_call(
        _moe_body,
        grid_spec=pltpu.PrefetchScalarGridSpec(
            num_scalar_prefetch=3,
            grid=(N_BLOCKS,),
            in_specs=[rows(lambda i: i), rows(lambda i: jnp.minimum(i + 1, N_BLOCKS - 1)), rows(lambda i: i),
                      pl.BlockSpec(memory_space=pl.ANY),
                      wspec((D, D_EXPERT)), wspec((D, D_EXPERT)), wspec((D_EXPERT, D))],
            out_specs=pl.BlockSpec(memory_space=pl.ANY),
            scratch_shapes=[pltpu.VMEM((2, TM_MOE * TT_SUB, LANES), U32),
                            pltpu.VMEM((TM_MOE * TT_SUB, LANES), U32),
                            pltpu.VMEM((D, D_EXPERT), BF16), pltpu.VMEM((D, D_EXPERT), BF16),
                            pltpu.VMEM((D_EXPERT, D), BF16),
                            pltpu.SemaphoreType.DMA((2,)), pltpu.SemaphoreType.DMA]),
        out_shape=jax.ShapeDtypeStruct((2 * T * TT_SUB, LANES), U32),
        compiler_params=_params(("arbitrary",), MOE_VMEM_LIMIT),
        name="moe_experts",
    )(block_e, nused, nvalid, src.reshape(N_BLOCKS, 1, TM_MOE), src.reshape(N_BLOCKS, 1, TM_MOE),
      dst.reshape(N_BLOCKS, 1, TM_MOE), hf, wg, wu, wd)


INV_ROWS = 2048


def _invert_body(dest_ref, codes_ref):
    i = pl.program_id(0)

    @pl.when(i == 0)
    def _():
        def clear(s, carry):
            codes_ref[s] = 0
            return carry

        lax.fori_loop(0, N_SLOTS, clear, 0, unroll=8)

    def put(j, carry):
        codes_ref[dest_ref[0, j]] = i * INV_ROWS + j
        return carry

    lax.fori_loop(0, INV_ROWS, put, 0, unroll=8)


def _invert_slots(dest3):
    return pl.pallas_call(
        _invert_body,
        grid=(dest3.shape[0],),
        in_specs=[pl.BlockSpec((None, 1, INV_ROWS), lambda i: (i, 0, 0), memory_space=pltpu.SMEM)],
        out_specs=pl.BlockSpec(memory_space=pltpu.SMEM),
        out_shape=jax.ShapeDtypeStruct((N_SLOTS,), jnp.int32),
        compiler_params=_params(("arbitrary",)),
        name="invert_slots",
    )(dest3)


def _dispatch_plan(info, cnt):
    e = info[:, 0:2].astype(jnp.int32)
    rank = info[:, 4:6].astype(jnp.int32)
    counts = cnt[0, EXPERT_LANE0:EXPERT_LANE0 + N_EXPERTS].astype(jnp.int32)
    padded = ((counts + TM_MOE - 1) // TM_MOE) * TM_MOE
    pends = jnp.cumsum(padded)
    pstarts = pends - padded
    first = jnp.sum(jnp.where(e[..., None] == jnp.arange(N_EXPERTS, dtype=jnp.int32), pstarts, 0), axis=-1)
    dest = first + rank
    codes = _invert_slots(dest.reshape(2 * T // INV_ROWS, 1, INV_ROWS))
    src = (codes >> 1) * TT_SUB
    dst = ((codes & 1) * T + (codes >> 1)) * TT_SUB
    block0 = jnp.arange(N_BLOCKS, dtype=jnp.int32) * TM_MOE
    block_e = jnp.clip(jnp.searchsorted(pends, block0, side='right', method='compare_all'),
                       0, N_EXPERTS - 1).astype(jnp.int32)
    nvalid = jnp.clip(pstarts[block_e] + counts[block_e] - block0, 0, TM_MOE).astype(jnp.int32)
    nused = (pends[-1] // TM_MOE).astype(jnp.int32).reshape(1)
    return src, dst, block_e, nused, nvalid


def _combine_body(x_ref, y0_ref, y1_ref, info_ref, g2_ref, nw_ref, *o_refs, final):
    info = info_ref[...]
    lane = lax.broadcasted_iota(jnp.int32, info.shape, 1)
    w1 = jnp.sum(jnp.where(lane == 2, info, 0.0), axis=-1, keepdims=True)
    w2 = jnp.sum(jnp.where(lane == 3, info, 0.0), axis=-1, keepdims=True)
    y0 = _load_tt(y0_ref, TM_CMB, F32)
    y1 = _load_tt(y1_ref, TM_CMB, F32)
    x = x_ref[...] + g2_ref[...] * (y0 * w1 + y1 * w2)
    if final:
        ms = jnp.mean(x * x, axis=-1, keepdims=True)
        x = x * lax.rsqrt(ms + EPS) * nw_ref[...]

    def emit(ref):
        ref[...] = x

    _for_tile(o_refs, TM_CMB, emit)


def _combine(x1, y, info, mods, nw, *, final):
    tm = TM_CMB
    nt = T // tm
    if final:
        out_shape = [jax.ShapeDtypeStruct((T_S, D), F32), jax.ShapeDtypeStruct((T_P, D), F32)]
    else:
        out_shape = [jax.ShapeDtypeStruct((T, D), F32)]
    return pl.pallas_call(
        functools.partial(_combine_body, final=final),
        grid=(nt,),
        in_specs=[pl.BlockSpec((tm, D), lambda i: (i, 0)),
                  pl.BlockSpec((tm * TT_SUB, LANES), lambda i: (i, 0)),
                  pl.BlockSpec((tm * TT_SUB, LANES), lambda i: (i + nt, 0)),
                  pl.BlockSpec((tm, LANES), lambda i: (i, 0)),
                  _mod_spec(5, tm, 1),
                  pl.BlockSpec((1, D), lambda i: (0, 0))],
        out_specs=_row_specs(out_shape, tm, 1),
        out_shape=out_shape,
        compiler_params=_params(("arbitrary",)),
        name="combine_final" if final else "combine",
    )(x1, y, y, info, mods, nw.reshape(1, D))


def kernel(x_prompt, x_sample, c, state_ret_fwd, state_ret_bwd, state_lru_fwd, state_lru_bwd, c_ctx, w_mod, b_mod, norm_mix_w, norm_ffn_w, norm_final_w, w_in_even, w_out_even, gmlp_norm_w, gmlp_w_s, gmlp_b_s, ret_decay_fwd, ret_decay_bwd, ret_gn_w, w_in_odd, w_out_odd, conv_w, conv_b, conv_ln_w, conv_ln_b, lru_conv_w, lru_conv_b, lru_wa_fwd, lru_ba_fwd, lru_wx_fwd, lru_bx_fwd, lru_lam_fwd, lru_wa_bwd, lru_ba_bwd, lru_wx_bwd, lru_bx_bwd, lru_lam_bwd, router_grp_w, router_grp_b, router_exp_w, router_exp_b, moe_w_gate, moe_w_up, moe_w_down):
    xs = (x_sample.reshape(T_S, D), x_prompt.reshape(T_P, D))
    cond = jnp.concatenate([c, c_ctx[None, :], jnp.zeros((N_COND_PAD - B_S - 1, D), F32)], axis=0)
    m = _modulation(cond, w_mod, b_mod)
    mods_all = m.reshape(DEPTH, N_COND_PAD, N_MOD, D).transpose(0, 2, 1, 3)[:, :, :, None, :]

    ret_f = ret_b = lru_f = lru_b = None
    for l in range(DEPTH):
        mods = mods_all[l]
        if l % 2 == 0:
            e = l // 2
            p = _norm_in(xs, norm_mix_w[l], mods, w_in_even[e].astype(BF16), pack=False)
            b_full = jnp.repeat(gmlp_b_s[e].T.astype(F32), A_GDIM, axis=1)
            out_a = (_gmlp(p, gmlp_norm_w[e], gmlp_w_s[e].astype(BF16), b_full),)
            tabs = _retention_tables(ret_decay_fwd[e], ret_decay_bwd[e])
            cos, sin = _rope_tables()
            ob_s, _, _ = _retention(p, state_ret_fwd[:, e], state_ret_bwd[:, e], tabs, ret_gn_w[e], cos, sin,
                                    base_chunk=0, nb=B_S, nc=N_S // CHUNK, rope=True)
            zero_state = jnp.zeros((B_P, HEADS, HD, HD), F32)
            ob_p, ret_f, ret_b = _retention(p, zero_state, zero_state, tabs, ret_gn_w[e], cos, sin,
                                            base_chunk=T_S // CHUNK, nb=B_P, nc=N_P // CHUNK, rope=False)
            out_b = (ob_s, ob_p)
            w_out = w_out_even[e]
        else:
            o = l // 2
            p = _norm_in(xs, norm_mix_w[l], mods, w_in_odd[o].astype(BF16), pack=True)
            cw3 = jnp.concatenate([conv_w[o], jnp.zeros((1, W), F32)], axis=0).reshape(
                C_KSIZE + 1, N_STRIPS, LANES).transpose(1, 0, 2)
            conv_args = (cw3, conv_b[o], conv_ln_w[o], conv_ln_b[o])
            prm_f = _lru_params(lru_conv_w[o], lru_conv_b[o], lru_wa_fwd[o], lru_ba_fwd[o], lru_wx_fwd[o],
                                lru_bx_fwd[o], lru_lam_fwd[o])
            prm_b = _lru_params(lru_conv_w[o], lru_conv_b[o], lru_wa_bwd[o], lru_ba_bwd[o], lru_wx_bwd[o],
                                lru_bx_bwd[o], lru_lam_bwd[o])
            latent = dict(seq_row0=0, nseq=B_S, n=N_S)
            context = dict(seq_row0=T_S, nseq=B_P, n=N_P)
            out_a = (_od_conv(p, *conv_args, **latent), _od_conv(p, *conv_args, **context))
            yd_s, _, _ = _od_rglru(p, prm_f, prm_b, state_lru_fwd[:, o], state_lru_bwd[:, o], **latent)
            zero_h = jnp.zeros((B_P, W), F32)
            yd_p, lru_f, lru_b = _od_rglru(p, prm_f, prm_b, zero_h, zero_h, **context)
            out_b = (yd_s, yd_p)
            w_out = w_out_odd[o]

        rw = jnp.concatenate([router_grp_w[l], router_exp_w[l],
                              jnp.zeros((D, LANES - N_GROUPS - N_EXPERTS), F32)], axis=1)
        rw_hi = rw.astype(BF16)
        rw = jnp.concatenate([rw_hi, (rw - rw_hi.astype(F32)).astype(BF16)], axis=1)
        rb = jnp.concatenate([router_grp_b[l], router_exp_b[l],
                              jnp.zeros((LANES - N_GROUPS - N_EXPERTS,), F32)]).reshape(1, LANES)
        x1, hf, info, cnt = _out_proj(out_a, out_b, xs, mods, w_out.astype(BF16), norm_ffn_w[l], rw, rb,
                                      packed=(l % 2 == 1))
        y = _moe(hf, _dispatch_plan(info, cnt), moe_w_gate, moe_w_up, moe_w_down, l)
        xs = tuple(_combine(x1, y, info, mods, norm_final_w, final=(l == DEPTH - 1)))

    y_sample = xs[0].reshape(B_S, N_S, D)
    y_prompt = xs[1].reshape(B_P, N_P, D)
    return (y_prompt, y_sample, ret_f[:, None], ret_b[:, None], lru_f[:, None], lru_b[:, None])
```

```python
import functools

import jax
import jax.numpy as jnp
from jax import lax
from jax.experimental import pallas as pl
from jax.experimental.pallas import tpu as pltpu

F32 = jnp.float32
BF16 = jnp.bfloat16
U32 = jnp.uint32

D = 2048
B_P, N_P = 16, 256
B_S, N_S = 8, 4096
T_S = B_S * N_S
T_P = B_P * N_P
T = T_S + T_P
ROWS_PER_COND = 4096
N_COND_PAD = 16
DEPTH = 2
N_MOD = 6
EPS = 1e-6
GRID_W = 64
ROPE_BASE = 10000.0

W = 1024
HW = W // 2
HEADS = 8
HD = 128
CHUNK = 128
A_GROUPS = 4
A_GDIM = W // A_GROUPS
C_KSIZE = 31
D_KSIZE = 4
LRU_C = 8.0

N_GROUPS = 4
EXPERTS_PER_GROUP = 8
N_EXPERTS = 32
D_EXPERT = 512
LANES = 128
N_STRIPS = W // LANES
EXPERT_LANE0 = N_GROUPS

TM_IN = 512
TN_IN = 2048
TM_OUT = 512
TM_MOE = 512
N_SLOTS = 2 * T + N_EXPERTS * TM_MOE
N_BLOCKS = N_SLOTS // TM_MOE
TM_CMB = 512

VMEM_LIMIT = 56 * 1024 * 1024
MOE_VMEM_LIMIT = 60 * 1024 * 1024


def _params(sem, vmem=VMEM_LIMIT):
    return pltpu.CompilerParams(dimension_semantics=sem, vmem_limit_bytes=vmem)


def _cond_row(i, tm):
    return (i * tm) // ROWS_PER_COND


def _pack_pair(lo, hi):
    lo_b = lax.bitcast_convert_type(lo.astype(BF16).astype(F32), U32)
    hi_b = lax.bitcast_convert_type(hi.astype(BF16).astype(F32), U32)
    return (lo_b >> 16) | hi_b


def _unpack_pair(w, dtype):
    return (lax.bitcast_convert_type(w << 16, F32).astype(dtype),
            lax.bitcast_convert_type(w & U32(0xFFFF0000), F32).astype(dtype))


def _pack_halves(x):
    k = x.shape[1] // 2
    return _pack_pair(x[:, :k], x[:, k:])


def _unpack_halves(w, dtype):
    return jnp.concatenate(_unpack_pair(w, dtype), axis=1)


TT_SUB = 8


def _store_tt(ref, x, rows):
    for c in range(TT_SUB):
        ref[pl.ds(c, rows, stride=TT_SUB), :] = _pack_pair(x[:, c * LANES:(c + 1) * LANES],
                                                           x[:, W + c * LANES:W + (c + 1) * LANES])


def _load_tt(ref, rows, dtype):
    lo, hi = [], []
    for c in range(TT_SUB):
        a, b = _unpack_pair(ref[pl.ds(c, rows, stride=TT_SUB), :], dtype)
        lo.append(a)
        hi.append(b)
    return jnp.concatenate(lo + hi, axis=1)


def _row_specs(arrs, tm, ngrid):
    if len(arrs) == 1:
        maps = [lambda i: i]
    else:
        n_s = T_S // tm
        maps = [lambda i: jnp.minimum(i, n_s - 1), lambda i: jnp.maximum(i - n_s, 0)]
    width = arrs[0].shape[1]
    if ngrid == 1:
        return [pl.BlockSpec((tm, width), lambda i, f=f: (f(i), 0)) for f in maps]
    return [pl.BlockSpec((tm, width), lambda i, j, f=f: (f(i), 0)) for f in maps]


def _tile_value(refs, tm):
    if len(refs) == 1:
        return refs[0][...]
    return jnp.where(pl.program_id(0) < T_S // tm, refs[0][...], refs[1][...])


def _for_tile(refs, tm, fn):
    if len(refs) == 1:
        fn(refs[0])
        return
    i = pl.program_id(0)

    @pl.when(i < T_S // tm)
    def _():
        fn(refs[0])

    @pl.when(i >= T_S // tm)
    def _():
        fn(refs[1])


def _mod_body(c_ref, w_ref, b_ref, o_ref):
    c = c_ref[...]
    a = (c * jax.nn.sigmoid(c)).astype(BF16)
    o_ref[...] = jnp.dot(a, w_ref[...].astype(BF16), preferred_element_type=F32) + b_ref[...]


def _modulation(cond, w_mod, b_mod):
    tn = 1024
    return pl.pallas_call(
        _mod_body,
        grid=(DEPTH, N_MOD * D // tn),
        in_specs=[pl.BlockSpec((N_COND_PAD, D), lambda l, j: (0, 0)),
                  pl.BlockSpec((None, D, tn), lambda l, j: (l, 0, j)),
                  pl.BlockSpec((None, 1, tn), lambda l, j: (l, 0, j))],
        out_specs=pl.BlockSpec((None, N_COND_PAD, tn), lambda l, j: (l, 0, j)),
        out_shape=jax.ShapeDtypeStruct((DEPTH, N_COND_PAD, N_MOD * D), F32),
        compiler_params=_params(("arbitrary", "arbitrary")),
        name="modulation",
    )(cond, w_mod, b_mod.reshape(DEPTH, 1, N_MOD * D))


def _mod_spec(k, tm, ngrid):
    if ngrid == 1:
        return pl.BlockSpec((None, None, 1, D), lambda i: (k, _cond_row(i, tm), 0, 0))
    return pl.BlockSpec((None, None, 1, D), lambda i, j: (k, _cond_row(i, tm), 0, 0))


def _norm_in_body(*refs, pack):
    nw_ref, sc_ref, sh_ref, w_ref, p_ref, h_scr = refs[-6:]

    @pl.when(pl.program_id(1) == 0)
    def _():
        x = _tile_value(refs[:-6], TM_IN)
        ms = jnp.mean(x * x, axis=-1, keepdims=True)
        h = x * lax.rsqrt(ms + EPS) * nw_ref[...]
        h = h * (1.0 + sc_ref[...]) + sh_ref[...]
        h_scr[...] = h.astype(BF16)

    r = jnp.dot(h_scr[...], w_ref[...], preferred_element_type=F32)
    if pack:
        p_ref[...] = jnp.concatenate([_pack_halves(r[:, k * W:(k + 1) * W]) for k in range(TN_IN // W)], axis=1)
    else:
        p_ref[...] = r.astype(BF16)


def _norm_in(xs, nw, mods, w_bf16, *, pack):
    n = w_bf16.shape[1]
    if pack:
        out_spec = pl.BlockSpec((TM_IN, TN_IN // 2), lambda i, j: (i, j))
        out_shape = jax.ShapeDtypeStruct((T, n // 2), U32)
    else:
        out_spec = pl.BlockSpec((TM_IN, TN_IN), lambda i, j: (i, j))
        out_shape = jax.ShapeDtypeStruct((T, n), BF16)
    return pl.pallas_call(
        functools.partial(_norm_in_body, pack=pack),
        grid=(T // TM_IN, n // TN_IN),
        in_specs=_row_specs(xs, TM_IN, 2) + [
            pl.BlockSpec((1, D), lambda i, j: (0, 0)),
            _mod_spec(1, TM_IN, 2),
            _mod_spec(0, TM_IN, 2),
            pl.BlockSpec((D, TN_IN), lambda i, j: (0, j))],
        out_specs=out_spec,
        out_shape=out_shape,
        scratch_shapes=[pltpu.VMEM((TM_IN, D), BF16)],
        compiler_params=_params(("arbitrary", "arbitrary")),
        name="norm_in",
    )(*xs, nw.reshape(1, D), mods, mods, w_bf16)


GMLP_ROWS = 512


def _gmlp_body(u_ref, v_ref, lnw_ref, ws_ref, b_ref, o_ref):
    for c in range(GMLP_ROWS // CHUNK):
        rows = pl.ds(c * CHUNK, CHUNK)
        v = jax.nn.gelu(v_ref[rows, :].astype(F32))
        mu = jnp.mean(v, axis=-1, keepdims=True)
        var = jnp.mean(jnp.square(v - mu), axis=-1, keepdims=True)
        vn = ((v - mu) * lax.rsqrt(var + EPS) * lnw_ref[...]).astype(BF16)
        z = jnp.concatenate(
            [jnp.dot(ws_ref[g], vn[:, g * A_GDIM:(g + 1) * A_GDIM], preferred_element_type=F32)
             for g in range(A_GROUPS)], axis=1) + b_ref[...]
        u = jax.nn.gelu(u_ref[rows, :].astype(F32))
        o_ref[rows, :] = (u * z).astype(BF16)


def _gmlp(p, lnw, ws_bf16, b_full):
    return pl.pallas_call(
        _gmlp_body,
        grid=(T // GMLP_ROWS,),
        in_specs=[pl.BlockSpec((GMLP_ROWS, W), lambda i: (i, 0)),
                  pl.BlockSpec((GMLP_ROWS, W), lambda i: (i, 1)),
                  pl.BlockSpec((1, W), lambda i: (0, 0)),
                  pl.BlockSpec((A_GROUPS, CHUNK, CHUNK), lambda i: (0, 0, 0)),
                  pl.BlockSpec((CHUNK, W), lambda i: (0, 0))],
        out_specs=pl.BlockSpec((GMLP_ROWS, W), lambda i: (i, 0)),
        out_shape=jax.ShapeDtypeStruct((T, W), BF16),
        compiler_params=_params(("arbitrary",)),
        name="gmlp",
    )(p, p, lnw.reshape(1, W), ws_bf16, b_full)


RET_SUB = 2
RET_ROWS = RET_SUB * CHUNK


def _roped_qk(q_ref, k_ref, cos_ref, sin_ref, rows, h, rope):
    cols = slice(h * HD, (h + 1) * HD)
    q = q_ref[rows, cols].astype(F32)
    k = k_ref[rows, cols].astype(F32) * (HD ** -0.5)
    if rope:
        c = cos_ref[rows, :]
        s = sin_ref[rows, :]
        q = q * c + pltpu.roll(q, HD // 2, axis=1) * s
        k = k * c + pltpu.roll(k, HD // 2, axis=1) * s
    return q, k


def _kt_v(kd, v):
    return lax.dot_general(kd, v, (((0,), (0,)), ((), ())), preferred_element_type=F32)


def _ret_bwd_body(q_ref, k_ref, v_ref, cos_ref, sin_ref, s0_ref, dq_ref, dk_ref, dc_ref,
                  ob_ref, sfin_ref, s_scr, *, rope, nc):
    c = pl.program_id(1)

    @pl.when(c == 0)
    def _():
        s_scr[...] = s0_ref[...]

    for sub in reversed(range(RET_SUB)):
        rows = slice(sub * CHUNK, (sub + 1) * CHUNK)
        for h in range(HEADS):
            cols = slice(h * HD, (h + 1) * HD)
            q, k = _roped_qk(q_ref, k_ref, cos_ref, sin_ref, rows, h, rope)
            v = v_ref[rows, cols]
            s = s_scr[h]
            ob_ref[rows, cols] = jnp.dot(q.astype(BF16), s.astype(BF16), preferred_element_type=F32) * dq_ref[h]
            kd = (k * dk_ref[h]).astype(BF16)
            s_scr[h] = s * dc_ref[h] + _kt_v(kd, v)

    @pl.when(c == nc - 1)
    def _():
        sfin_ref[...] = s_scr[...]


def _ret_fwd_body(q_ref, k_ref, v_ref, g_ref, cos_ref, sin_ref, ob_ref, s0_ref, m_ref, dq_ref, dk_ref, dc_ref,
                  gn_ref, o_ref, sfin_ref, s_scr, *, rope, nc):
    c = pl.program_id(1)

    @pl.when(c == 0)
    def _():
        s_scr[...] = s0_ref[...]

    for sub in range(RET_SUB):
        rows = slice(sub * CHUNK, (sub + 1) * CHUNK)
        for h in range(HEADS):
            cols = slice(h * HD, (h + 1) * HD)
            q, k = _roped_qk(q_ref, k_ref, cos_ref, sin_ref, rows, h, rope)
            qb = q.astype(BF16)
            v = v_ref[rows, cols]
            s = s_scr[h]
            scores = lax.dot_general(qb, k.astype(BF16), (((1,), (1,)), ((), ())),
                                     preferred_element_type=F32) * m_ref[h]
            o = (jnp.dot(scores.astype(BF16), v, preferred_element_type=F32)
                 + jnp.dot(qb, s.astype(BF16), preferred_element_type=F32) * dq_ref[h]
                 + ob_ref[rows, cols])
            kd = (k * dk_ref[h]).astype(BF16)
            s_scr[h] = s * dc_ref[h] + _kt_v(kd, v)
            mu = jnp.mean(o, axis=-1, keepdims=True)
            var = jnp.mean(jnp.square(o - mu), axis=-1, keepdims=True)
            on = (o - mu) * lax.rsqrt(var + EPS) * gn_ref[:, cols]
            g = g_ref[rows, cols].astype(F32)
            o_ref[rows, cols] = (g * jax.nn.sigmoid(g) * on).astype(BF16)

    @pl.when(c == nc - 1)
    def _():
        sfin_ref[...] = s_scr[...]


def _retention(p, s_f0, s_b0, tabs, gn_w, cos, sin, *, base_chunk, nb, nc, rope):
    m_tab, dq_f, dk_f, dc_f, dq_b, dk_b, dc_b = tabs
    ns = nc // RET_SUB
    base = base_chunk // RET_SUB
    state_spec = pl.BlockSpec((None, HEADS, HD, HD), lambda b, c: (b, 0, 0, 0))
    tab_spec = pl.BlockSpec((HEADS, HD, HD), lambda b, c: (0, 0, 0))
    dc_spec = pl.BlockSpec((HEADS, 1, HD), lambda b, c: (0, 0, 0))
    state_shape = jax.ShapeDtypeStruct((nb, HEADS, HD, HD), F32)

    def step(c, rev):
        return ns - 1 - c if rev else c

    def col(j, rev):
        return pl.BlockSpec((RET_ROWS, W), lambda b, c: (base + b * ns + step(c, rev), j))

    def rope_spec(rev):
        if not rope:
            return pl.BlockSpec((RET_ROWS, HD), lambda b, c: (0, 0))
        return pl.BlockSpec((RET_ROWS, HD), lambda b, c: (step(c, rev), 0))

    def local(rev):
        return pl.BlockSpec((RET_ROWS, W), lambda b, c: (b * ns + step(c, rev), 0))

    n_rows = nb * nc * CHUNK
    ob, s_b = pl.pallas_call(
        functools.partial(_ret_bwd_body, rope=rope, nc=ns),
        grid=(nb, ns),
        in_specs=[col(2, True), col(3, True), col(4, True), rope_spec(True), rope_spec(True),
                  state_spec, tab_spec, tab_spec, dc_spec],
        out_specs=[local(True), state_spec],
        out_shape=[jax.ShapeDtypeStruct((n_rows, W), F32), state_shape],
        scratch_shapes=[pltpu.VMEM((HEADS, HD, HD), F32)],
        compiler_params=_params(("arbitrary", "arbitrary")),
        name="retention_bwd",
    )(p, p, p, cos, sin, s_b0, dq_b, dk_b, dc_b)

    o, s_f = pl.pallas_call(
        functools.partial(_ret_fwd_body, rope=rope, nc=ns),
        grid=(nb, ns),
        in_specs=[col(2, False), col(3, False), col(4, False), col(5, False), rope_spec(False), rope_spec(False),
                  local(False), state_spec, tab_spec, tab_spec, tab_spec, dc_spec,
                  pl.BlockSpec((1, W), lambda b, c: (0, 0))],
        out_specs=[local(False), state_spec],
        out_shape=[jax.ShapeDtypeStruct((n_rows, W), BF16), state_shape],
        scratch_shapes=[pltpu.VMEM((HEADS, HD, HD), F32)],
        compiler_params=_params(("arbitrary", "arbitrary")),
        name="retention_fwd",
    )(p, p, p, p, cos, sin, ob, s_f0, m_tab, dq_f, dk_f, dc_f, gn_w.reshape(1, W))
    return o, s_f, s_b


def _retention_tables(decay_fwd, decay_bwd):
    lg_f = jax.nn.log_sigmoid(decay_fwd.astype(F32))[:, None, None]
    lg_b = jax.nn.log_sigmoid(decay_bwd.astype(F32))[:, None, None]
    pos = jnp.arange(CHUNK, dtype=F32)
    rel = pos[:, None] - pos[None, :]
    m_tab = (jnp.where(rel >= 0, jnp.exp(lg_f * jnp.maximum(rel, 0.0)), 0.0)
             + jnp.where(rel <= 0, jnp.exp(lg_b * jnp.maximum(-rel, 0.0)), 0.0))
    ones = jnp.ones((1, 1, HD), F32)
    col = pos[None, :, None]
    dq_f = jnp.exp(lg_f * (col + 1.0)) * ones
    dk_f = jnp.exp(lg_f * (CHUNK - 1.0 - col)) * ones
    dq_b = jnp.exp(lg_b * (CHUNK - col)) * ones
    dk_b = jnp.exp(lg_b * col) * ones
    dc_f = jnp.exp(lg_f * CHUNK) * ones
    dc_b = jnp.exp(lg_b * CHUNK) * ones
    return m_tab, dq_f, dk_f, dc_f, dq_b, dk_b, dc_b


def _rope_tables():
    rows = N_S // GRID_W
    row = jnp.repeat(jnp.arange(rows, dtype=F32), GRID_W)
    colp = jnp.tile(jnp.arange(GRID_W, dtype=F32), rows)
    quarter = HD // 4
    inv_freq = ROPE_BASE ** (-jnp.arange(quarter, dtype=F32) / quarter)
    ang = jnp.concatenate([row[:, None] * inv_freq, colp[:, None] * inv_freq], axis=-1)
    cos, sin = jnp.cos(ang), jnp.sin(ang)
    return jnp.concatenate([cos, cos], axis=-1), jnp.concatenate([-sin, sin], axis=-1)


OD_TT = 128
OD_G = 8
CONV_HALO = 16
LRU_HALO = 8


def _od_rows(seq_row0, n, g, tj):
    return [seq_row0 + (g * OD_G + b) * n + tj * OD_TT for b in range(OD_G)]


def _od_in_copies(p_hbm, buf, sem, *, col, halo, seq_row0, n, g, tj):
    nt = n // OD_TT
    out = []
    for b, base in enumerate(_od_rows(seq_row0, n, g, tj)):
        def src(r, k):
            return p_hbm.at[pl.ds(r, k), pl.ds(col * HW, HW)]

        out.append((None, pltpu.make_async_copy(src(base, OD_TT), buf.at[pl.ds(halo, OD_TT), b, :], sem)))
        if halo:
            out.append((tj > 0, pltpu.make_async_copy(src(base - halo, halo), buf.at[pl.ds(0, halo), b, :], sem)))
            out.append((tj < nt - 1, pltpu.make_async_copy(src(base + OD_TT, halo),
                                                           buf.at[pl.ds(halo + OD_TT, halo), b, :], sem)))
    return out


def _od_run(copies, op):
    for cond, cp in copies:
        fn = cp.start if op == "start" else cp.wait
        if cond is None:
            fn()
        else:
            pl.when(cond)(fn)


def _od_zero_halo(buf, halo, n, tj):
    nt = n // OD_TT
    zeros = jnp.zeros((halo, OD_G, HW), U32)

    @pl.when(tj == 0)
    def _():
        buf[0:halo] = zeros

    @pl.when(tj == nt - 1)
    def _():
        buf[halo + OD_TT:2 * halo + OD_TT] = zeros


def _od_out_copies(obuf, out_hbm, sem, *, seq_row0, n, g, tj):
    return [(None, pltpu.make_async_copy(obuf.at[:, b, :], out_hbm.at[pl.ds(base - seq_row0, OD_TT), :], sem))
            for b, base in enumerate(_od_rows(seq_row0, n, g, tj))]


def _od_unpack(buf, rows):
    return _unpack_halves(buf[...].reshape(rows, HW), F32)


def _od_pipeline(q, nq, in_copies, sem_slots=2):
    s = q % 2

    @pl.when(q == 0)
    def _():
        _od_run(in_copies(q, s), "start")

    _od_run(in_copies(q, s), "wait")

    @pl.when(q + 1 < nq)
    def _():
        _od_run(in_copies(q + 1, 1 - s), "start")

    return s


def _od_emit(q, nq, s, obuf, words, out_copies):
    @pl.when(q >= 2)
    def _():
        _od_run(out_copies(s), "wait")

    obuf[s] = words.reshape(OD_TT, OD_G, HW)
    _od_run(out_copies(s), "start")

    @pl.when(q == nq - 1)
    def _():
        _od_run(out_copies(s), "wait")

        @pl.when(q >= 1)
        def _():
            _od_run(out_copies(1 - s), "wait")


CONV_RB = 64


def _od_conv_body(p_hbm, cw_ref, cb_ref, lnw_ref, lnb_ref, yc_hbm, xin, glu, ybuf, obuf, sem_in, sem_out,
                  *, seq_row0, n, ng):
    nt = n // OD_TT
    nq = ng * nt
    q = pl.program_id(0)
    halo = CONV_HALO
    rows_in = (OD_TT + 2 * halo) * OD_G
    rows = OD_TT * OD_G

    def in_copies(step, slot):
        out = []
        for br in range(2):
            out += _od_in_copies(p_hbm, xin.at[br, slot], sem_in.at[slot], col=br, halo=halo,
                                 seq_row0=seq_row0, n=n, g=step // nt, tj=step % nt)
        return out

    s = _od_pipeline(q, nq, in_copies)
    g, tj = q // nt, q % nt
    for br in range(2):
        _od_zero_halo(xin.at[br, s], halo, n, tj)
    glu[...] = _od_unpack(xin.at[0, s], rows_in) * jax.nn.sigmoid(_od_unpack(xin.at[1, s], rows_in))

    def strip(si, carry):
        lanes = pl.ds(pl.multiple_of(si * LANES, LANES), LANES)
        for rb in range(rows // CONV_RB):
            acc = jnp.zeros((CONV_RB // 8, 8, LANES), F32)
            for k in range(C_KSIZE):
                wk = cw_ref[si, pl.ds(k, 1), :]
                r0 = rb * CONV_RB + (halo - C_KSIZE // 2 + k) * OD_G
                acc = acc + glu[r0:r0 + CONV_RB, lanes].reshape(CONV_RB // 8, 8, LANES) * wk[None]
            ybuf[rb * CONV_RB:(rb + 1) * CONV_RB, lanes] = acc.reshape(CONV_RB, LANES)
        return carry

    lax.fori_loop(0, N_STRIPS, strip, 0)
    y = ybuf[...] + cb_ref[...]
    mu = jnp.mean(y, axis=-1, keepdims=True)
    var = jnp.mean(jnp.square(y - mu), axis=-1, keepdims=True)
    yn = (y - mu) * lax.rsqrt(var + EPS) * lnw_ref[...] + lnb_ref[...]
    yo = yn * jax.nn.sigmoid(yn)

    def out_copies(slot):
        return _od_out_copies(obuf.at[slot], yc_hbm, sem_out.at[slot], seq_row0=seq_row0, n=n, g=g, tj=tj)

    _od_emit(q, nq, s, obuf, _pack_halves(yo), out_copies)


def _od_conv(p, cw3, cb, lnw, lnb, *, seq_row0, nseq, n):
    ng = nseq // OD_G
    nq = ng * (n // OD_TT)
    vec = pl.BlockSpec((1, W), lambda q: (0, 0))
    t_in = OD_TT + 2 * CONV_HALO
    return pl.pallas_call(
        functools.partial(_od_conv_body, seq_row0=seq_row0, n=n, ng=ng),
        grid=(nq,),
        in_specs=[pl.BlockSpec(memory_space=pl.ANY),
                  pl.BlockSpec((N_STRIPS, C_KSIZE + 1, LANES), lambda q: (0, 0, 0)), vec, vec, vec],
        out_specs=pl.BlockSpec(memory_space=pl.ANY),
        out_shape=jax.ShapeDtypeStruct((nseq * n, HW), U32),
        scratch_shapes=[pltpu.VMEM((2, 2, t_in, OD_G, HW), U32),
                        pltpu.VMEM((t_in * OD_G, W), F32),
                        pltpu.VMEM((OD_TT * OD_G, W), F32),
                        pltpu.VMEM((2, OD_TT, OD_G, HW), U32),
                        pltpu.SemaphoreType.DMA((2,)), pltpu.SemaphoreType.DMA((2,))],
        compiler_params=_params(("arbitrary",)),
        name="conv_module",
    )(p, cw3, cb.reshape(1, W), lnw.reshape(1, W), lnb.reshape(1, W))


def _od_gates(x, cw_ref, cb_ref, wa_ref, ba_ref, wx_ref, bx_ref, sp_ref, a_scr, b_scr):
    rows = OD_TT * OD_G
    xc = cb_ref[...]
    for k in range(D_KSIZE):
        r0 = (LRU_HALO - D_KSIZE // 2 + k) * OD_G
        xc = xc + cw_ref[k:k + 1, :] * x[r0:r0 + rows, :]
    xb = xc.astype(BF16)
    r = jnp.concatenate([jnp.dot(xb[:, h * HD:(h + 1) * HD], wa_ref[h], preferred_element_type=F32)
                         for h in range(HEADS)], axis=1) + ba_ref[...]
    g = jnp.concatenate([jnp.dot(xb[:, h * HD:(h + 1) * HD], wx_ref[h], preferred_element_type=F32)
                         for h in range(HEADS)], axis=1) + bx_ref[...]
    log_a = (-LRU_C) * jax.nn.sigmoid(r) * sp_ref[...]
    th = jnp.tanh(log_a)
    one_minus_a2 = -2.0 * th / (1.0 - th)
    a_scr[...] = jnp.exp(log_a)
    b_scr[...] = jnp.sqrt(jnp.maximum(one_minus_a2, 0.0)) * (jax.nn.sigmoid(g) * xc)


def _od_scan(a_scr, b_scr, out_ref, h, reverse):
    def step(t, h):
        tr = (OD_TT - 1 - t) if reverse else t
        rows = pl.ds(pl.multiple_of(tr * OD_G, OD_G), OD_G)
        h = a_scr[rows, :] * h + b_scr[rows, :]
        out_ref[rows, :] = h
        return h

    return lax.fori_loop(0, OD_TT, step, h, unroll=8)


def _od_lru_fwd_body(p_hbm, cw_ref, cb_ref, wa_ref, ba_ref, wx_ref, bx_ref, sp_ref, h0_ref,
                     hf_ref, hfin_ref, xin, a_scr, b_scr, h_scr, sem_in, *, seq_row0, n, ng):
    nt = n // OD_TT
    nq = ng * nt
    q = pl.program_id(0)

    def in_copies(step, slot):
        return _od_in_copies(p_hbm, xin.at[slot], sem_in.at[slot], col=2, halo=LRU_HALO,
                             seq_row0=seq_row0, n=n, g=step // nt, tj=step % nt)

    s = _od_pipeline(q, nq, in_copies)
    tj = q % nt
    _od_zero_halo(xin.at[s], LRU_HALO, n, tj)

    @pl.when(tj == 0)
    def _():
        h_scr[...] = h0_ref[...]

    x = _od_unpack(xin.at[s], (OD_TT + 2 * LRU_HALO) * OD_G)
    _od_gates(x, cw_ref, cb_ref, wa_ref, ba_ref, wx_ref, bx_ref, sp_ref, a_scr, b_scr)
    h = _od_scan(a_scr, b_scr, hf_ref, h_scr[...], False)
    h_scr[...] = h

    @pl.when(tj == nt - 1)
    def _():
        hfin_ref[...] = h


def _od_lru_bwd_body(p_hbm, hf_ref, cw_ref, cb_ref, wa_ref, ba_ref, wx_ref, bx_ref, sp_ref, h0_ref,
                     yd_hbm, hfin_ref, xin, gin, a_scr, b_scr, hb_scr, obuf, h_scr, sem_in, sem_out,
                     *, seq_row0, n, ng):
    nt = n // OD_TT
    nq = ng * nt
    q = pl.program_id(0)

    def in_copies(step, slot):
        where = dict(seq_row0=seq_row0, n=n, g=step // nt, tj=nt - 1 - step % nt)
        return (_od_in_copies(p_hbm, xin.at[slot], sem_in.at[slot], col=2, halo=LRU_HALO, **where)
                + _od_in_copies(p_hbm, gin.at[slot], sem_in.at[slot], col=3, halo=0, **where))

    s = _od_pipeline(q, nq, in_copies)
    g, tj = q // nt, nt - 1 - q % nt
    _od_zero_halo(xin.at[s], LRU_HALO, n, tj)

    @pl.when(tj == nt - 1)
    def _():
        h_scr[...] = h0_ref[...]

    x = _od_unpack(xin.at[s], (OD_TT + 2 * LRU_HALO) * OD_G)
    _od_gates(x, cw_ref, cb_ref, wa_ref, ba_ref, wx_ref, bx_ref, sp_ref, a_scr, b_scr)
    h = _od_scan(a_scr, b_scr, hb_scr, h_scr[...], True)
    h_scr[...] = h

    @pl.when(tj == 0)
    def _():
        hfin_ref[...] = h

    gd = _od_unpack(gin.at[s], OD_TT * OD_G)
    yd = (hf_ref[...] + hb_scr[...]) * jax.nn.gelu(gd)

    def out_copies(slot):
        return _od_out_copies(obuf.at[slot], yd_hbm, sem_out.at[slot], seq_row0=seq_row0, n=n, g=g, tj=tj)

    _od_emit(q, nq, s, obuf, _pack_halves(yd), out_copies)


def _od_rglru(p, prm_f, prm_b, h0_f, h0_b, *, seq_row0, nseq, n):
    ng = nseq // OD_G
    nt = n // OD_TT
    nq = ng * nt
    rows = OD_TT * OD_G
    vec = pl.BlockSpec((1, W), lambda q: (0, 0))
    wsp = pl.BlockSpec((HEADS, HD, HD), lambda q: (0, 0, 0))
    prm_specs = [pl.BlockSpec((8, W), lambda q: (0, 0)), vec, wsp, vec, wsp, vec, vec,
                 pl.BlockSpec((OD_G, W), lambda q: (q // nt, 0))]
    state_spec = pl.BlockSpec((OD_G, W), lambda q: (q // nt, 0))
    state_shape = jax.ShapeDtypeStruct((nseq, W), F32)
    t_in = OD_TT + 2 * LRU_HALO
    tile_f32 = pltpu.VMEM((rows, W), F32)

    hf, s_f = pl.pallas_call(
        functools.partial(_od_lru_fwd_body, seq_row0=seq_row0, n=n, ng=ng),
        grid=(nq,),
        in_specs=[pl.BlockSpec(memory_space=pl.ANY)] + prm_specs,
        out_specs=[pl.BlockSpec((rows, W), lambda q: (q, 0)), state_spec],
        out_shape=[jax.ShapeDtypeStruct((nq * rows, W), F32), state_shape],
        scratch_shapes=[pltpu.VMEM((2, t_in, OD_G, HW), U32), tile_f32, tile_f32, pltpu.VMEM((OD_G, W), F32),
                        pltpu.SemaphoreType.DMA((2,))],
        compiler_params=_params(("arbitrary",)),
        name="rglru_fwd",
    )(p, *prm_f, h0_f)

    yd, s_b = pl.pallas_call(
        functools.partial(_od_lru_bwd_body, seq_row0=seq_row0, n=n, ng=ng),
        grid=(nq,),
        in_specs=[pl.BlockSpec(memory_space=pl.ANY),
                  pl.BlockSpec((rows, W), lambda q: ((q // nt) * nt + nt - 1 - q % nt, 0))] + prm_specs,
        out_specs=[pl.BlockSpec(memory_space=pl.ANY), state_spec],
        out_shape=[jax.ShapeDtypeStruct((nseq * n, HW), U32), state_shape],
        scratch_shapes=[pltpu.VMEM((2, t_in, OD_G, HW), U32), pltpu.VMEM((2, OD_TT, OD_G, HW), U32),
                        tile_f32, tile_f32, tile_f32, pltpu.VMEM((2, OD_TT, OD_G, HW), U32),
                        pltpu.VMEM((OD_G, W), F32),
                        pltpu.SemaphoreType.DMA((2,)), pltpu.SemaphoreType.DMA((2,))],
        compiler_params=_params(("arbitrary",)),
        name="rglru_bwd",
    )(p, hf, *prm_b, h0_b)
    return yd, s_f, s_b


def _lru_params(cw, cb, wa, ba, wx, bx, lam):
    cw_pad = jnp.concatenate([cw, jnp.zeros((8 - D_KSIZE, W), F32)], axis=0)
    sp = jax.nn.softplus(-lam.astype(F32)).reshape(1, W)
    return (cw_pad, cb.reshape(1, W), wa.astype(BF16), ba.reshape(1, W), wx.astype(BF16), bx.reshape(1, W), sp)


def _out_body(*refs, n_a, n_b, n_x, packed):
    a_refs = refs[:n_a]
    b_refs = refs[n_a:n_a + n_b]
    x_refs = refs[n_a + n_b:n_a + n_b + n_x]
    (g1_ref, wo_ref, nw_ref, sc_ref, sh_ref, rw_ref, rb_ref,
     x1_ref, hf_ref, info_ref, cnt_ref, carry) = refs[n_a + n_b + n_x:]
    i = pl.program_id(0)

    @pl.when(i == 0)
    def _():
        carry[...] = jnp.zeros_like(carry)

    def branch(refs_, row0):
        v = _tile_value(refs_, TM_OUT)
        if packed:
            lo, hi = _unpack_pair(v, BF16)
            return (jnp.dot(lo, wo_ref[row0:row0 + HW, :], preferred_element_type=F32)
                    + jnp.dot(hi, wo_ref[row0 + HW:row0 + W, :], preferred_element_type=F32))
        return jnp.dot(v, wo_ref[row0:row0 + W, :], preferred_element_type=F32)

    x1 = _tile_value(x_refs, TM_OUT) + g1_ref[...] * (branch(a_refs, 0) + branch(b_refs, W))
    x1_ref[...] = x1
    ms = jnp.mean(x1 * x1, axis=-1, keepdims=True)
    hf = x1 * lax.rsqrt(ms + EPS) * nw_ref[...]
    hf = hf * (1.0 + sc_ref[...]) + sh_ref[...]
    _store_tt(hf_ref, hf, TM_OUT)

    hf_hi = hf.astype(BF16)
    hf_lo = (hf - hf_hi.astype(F32)).astype(BF16)
    l2 = jnp.dot(hf_hi, rw_ref[...], preferred_element_type=F32)
    logits = (l2[:, :LANES] + l2[:, LANES:]
              + jnp.dot(hf_lo, rw_ref[:, :LANES], preferred_element_type=F32) + rb_ref[...])
    lane = lax.broadcasted_iota(jnp.int32, (TM_OUT, LANES), 1).astype(F32)
    neg = -jnp.inf
    gmask = lane < N_GROUPS
    mg = jnp.max(jnp.where(gmask, logits, neg), axis=-1, keepdims=True)
    gstar = jnp.min(jnp.where(gmask & (logits == mg), lane, float(LANES)), axis=-1, keepdims=True)
    denom = jnp.sum(jnp.where(gmask, jnp.exp(logits - mg), 0.0), axis=-1, keepdims=True)
    psel = 1.0 / denom
    lo = EXPERT_LANE0 + EXPERTS_PER_GROUP * gstar
    emask = (lane >= lo) & (lane < lo + EXPERTS_PER_GROUP)
    v1 = jnp.max(jnp.where(emask, logits, neg), axis=-1, keepdims=True)
    i1 = jnp.min(jnp.where(emask & (logits == v1), lane, float(LANES)), axis=-1, keepdims=True)
    em2 = emask & (lane != i1)
    v2 = jnp.max(jnp.where(em2, logits, neg), axis=-1, keepdims=True)
    i2 = jnp.min(jnp.where(em2 & (logits == v2), lane, float(LANES)), axis=-1, keepdims=True)
    e21 = jnp.exp(v2 - v1)
    w1 = psel / (1.0 + e21)
    w2 = psel * e21 / (1.0 + e21)

    memb = jnp.where((lane == i1) | (lane == i2), 1.0, 0.0)
    r_i = lax.broadcasted_iota(jnp.int32, (TM_OUT, TM_OUT), 0)
    c_i = lax.broadcasted_iota(jnp.int32, (TM_OUT, TM_OUT), 1)
    lower = jnp.where(r_i > c_i, 1.0, 0.0).astype(BF16)
    before = jnp.dot(lower, memb.astype(BF16), preferred_element_type=F32) + carry[0:1, :]
    rank1 = jnp.sum(jnp.where(lane == i1, before, 0.0), axis=-1, keepdims=True)
    rank2 = jnp.sum(jnp.where(lane == i2, before, 0.0), axis=-1, keepdims=True)
    new_carry = carry[...] + jnp.sum(memb, axis=0, keepdims=True)
    carry[...] = new_carry
    cnt_ref[...] = new_carry

    e1 = i1 - EXPERT_LANE0
    e2 = i2 - EXPERT_LANE0
    info = jnp.where(lane == 0, e1, jnp.where(lane == 1, e2, jnp.where(lane == 2, w1, jnp.where(
        lane == 3, w2, jnp.where(lane == 4, rank1, jnp.where(lane == 5, rank2, 0.0))))))
    info_ref[...] = info


def _out_proj(a_parts, b_parts, xs, mods, wo_bf16, nw, rw, rb, *, packed):
    tm = TM_OUT
    row = lambda n: pl.BlockSpec((tm, n), lambda i: (i, 0))
    vec = pl.BlockSpec((1, D), lambda i: (0, 0))
    return pl.pallas_call(
        functools.partial(_out_body, n_a=len(a_parts), n_b=len(b_parts), n_x=len(xs), packed=packed),
        grid=(T // tm,),
        in_specs=_row_specs(a_parts, tm, 1) + _row_specs(b_parts, tm, 1) + _row_specs(xs, tm, 1) + [
            _mod_spec(2, tm, 1),
            pl.BlockSpec((D, D), lambda i: (0, 0)), vec, _mod_spec(4, tm, 1), _mod_spec(3, tm, 1),
            pl.BlockSpec((D, 2 * LANES), lambda i: (0, 0)), pl.BlockSpec((1, LANES), lambda i: (0, 0))],
        out_specs=[row(D), pl.BlockSpec((tm * TT_SUB, LANES), lambda i: (i, 0)), row(LANES),
                   pl.BlockSpec((8, LANES), lambda i: (0, 0))],
        out_shape=[jax.ShapeDtypeStruct((T, D), F32), jax.ShapeDtypeStruct((T * TT_SUB, LANES), U32),
                   jax.ShapeDtypeStruct((T, LANES), F32), jax.ShapeDtypeStruct((8, LANES), F32)],
        scratch_shapes=[pltpu.VMEM((8, LANES), F32)],
        compiler_params=_params(("arbitrary",)),
        name="out_proj_router",
    )(*a_parts, *b_parts, *xs, mods, wo_bf16, nw.reshape(1, D), mods, mods, rw, rb)


COPY_UNROLL = 32


def _moe_body(be_ref, nused_ref, nv_ref, src_cur, src_nxt, dst_cur, hf_hbm, wg_ref, wu_ref, wd_ref, y_hbm,
              xbuf, obuf, wgc, wuc, wdc, sem_in, sem_out):
    i = pl.program_id(0)
    nused = nused_ref[0]
    slot = i % 2

    def tile(ref, row8):
        return ref.at[pl.ds(pl.multiple_of(row8, TT_SUB), TT_SUB)]

    def rows(ref, n):
        return ref.at[pl.ds(0, n * TT_SUB)]

    def for_rows(n, fn):
        groups = n // COPY_UNROLL

        def group(c, carry):
            for u in range(COPY_UNROLL):
                fn(c * COPY_UNROLL + u)
            return carry

        def single(r, carry):
            fn(r)
            return carry

        lax.fori_loop(0, groups, group, 0)
        lax.fori_loop(groups * COPY_UNROLL, n, single, 0)

    def gather(src_ref, s, n):
        def row(r):
            pltpu.make_async_copy(tile(hf_hbm, src_ref[0, r]), tile(xbuf.at[s], r * TT_SUB), sem_in.at[s]).start()

        for_rows(n, row)

    def wait_scatter(n):
        pltpu.make_async_copy(rows(obuf, n), rows(y_hbm, n), sem_out).wait()

    @pl.when(i < nused)
    def _():
        nv = nv_ref[i]

        @pl.when(i == 0)
        def _():
            xbuf[...] = jnp.zeros(xbuf.shape, U32)
            gather(src_cur, 0, nv)

        pltpu.make_async_copy(rows(hf_hbm, nv), rows(xbuf.at[slot], nv), sem_in.at[slot]).wait()

        @pl.when(i + 1 < nused)
        def _():
            gather(src_nxt, 1 - slot, nv_ref[i + 1])

        @pl.when((i == 0) | (be_ref[i] != be_ref[jnp.maximum(i - 1, 0)]))
        def _():
            wgc[...] = wg_ref[...].astype(BF16)
            wuc[...] = wu_ref[...].astype(BF16)
            wdc[...] = wd_ref[...].astype(BF16)

        x = _load_tt(xbuf.at[slot], TM_MOE, BF16)
        gate = jnp.dot(x, wgc[...], preferred_element_type=F32)
        up = jnp.dot(x, wuc[...], preferred_element_type=F32)
        hid = (gate * jax.nn.sigmoid(gate) * up).astype(BF16)

        @pl.when(i > 0)
        def _():
            wait_scatter(nv_ref[i - 1])

        _store_tt(obuf, jnp.dot(hid, wdc[...], preferred_element_type=F32), TM_MOE)

        def scatter_row(r):
            pltpu.make_async_copy(tile(obuf, r * TT_SUB), tile(y_hbm, dst_cur[0, r]), sem_out).start()

        for_rows(nv, scatter_row)

        @pl.when(i == nused - 1)
        def _():
            wait_scatter(nv)


def _moe(hf, plan, wg, wu, wd, layer):
    src, dst, block_e, nused, nvalid = plan
    wspec = lambda shape: pl.BlockSpec((None, None) + shape, lambda i, be, nu, nv: (layer, be[i], 0, 0))
    rows = lambda f: pl.BlockSpec((None, 1, TM_MOE), lambda i, be, nu, nv: (f(i), 0, 0), memory_space=pltpu.SMEM)
    return pl.pallas_call(
        _moe_body,
        grid_spec=pltpu.PrefetchScalarGridSpec(
            num_scalar_prefetch=3,
            grid=(N_BLOCKS,),
            in_specs=[rows(lambda i: i), rows(lambda i: jnp.minimum(i + 1, N_BLOCKS - 1)), rows(lambda i: i),
                      pl.BlockSpec(memory_space=pl.ANY),
                      wspec((D, D_EXPERT)), wspec((D, D_EXPERT)), wspec((D_EXPERT, D))],
            out_specs=pl.BlockSpec(memory_space=pl.ANY),
            scratch_shapes=[pltpu.VMEM((2, TM_MOE * TT_SUB, LANES), U32),
                            pltpu.VMEM((TM_MOE * TT_SUB, LANES), U32),
                            pltpu.VMEM((D, D_EXPERT), BF16), pltpu.VMEM((D, D_EXPERT), BF16),
                            pltpu.VMEM((D_EXPERT, D), BF16),
                            pltpu.SemaphoreType.DMA((2,)), pltpu.SemaphoreType.DMA]),
        out_shape=jax.ShapeDtypeStruct((2 * T * TT_SUB, LANES), U32),
        compiler_params=_params(("arbitrary",), MOE_VMEM_LIMIT),
        name="moe_experts",
    )(block_e, nused, nvalid, src.reshape(N_BLOCKS, 1, TM_MOE), src.reshape(N_BLOCKS, 1, TM_MOE),
      dst.reshape(N_BLOCKS, 1, TM_MOE), hf, wg, wu, wd)


INV_ROWS = 2048


def _invert_body(dest_ref, zeros_hbm, codes_ref, sem):
    i = pl.program_id(0)

    @pl.when(i == 0)
    def _():
        clear = pltpu.make_async_copy(zeros_hbm, codes_ref, sem)
        clear.start()
        clear.wait()

    def put(j, carry):
        codes_ref[dest_ref[0, j]] = i * INV_ROWS + j
        return carry

    lax.fori_loop(0, INV_ROWS, put, 0, unroll=16)


def _invert_slots(dest3):
    return pl.pallas_call(
        _invert_body,
        grid=(dest3.shape[0],),
        in_specs=[pl.BlockSpec((None, 1, INV_ROWS), lambda i: (i, 0, 0), memory_space=pltpu.SMEM),
                  pl.BlockSpec(memory_space=pl.ANY)],
        out_specs=pl.BlockSpec(memory_space=pltpu.SMEM),
        out_shape=jax.ShapeDtypeStruct((N_SLOTS,), jnp.int32),
        scratch_shapes=[pltpu.SemaphoreType.DMA],
        compiler_params=_params(("arbitrary",)),
        name="invert_slots",
    )(dest3, jnp.zeros((N_SLOTS,), jnp.int32))


def _dispatch_plan(info, cnt):
    e = info[:, 0:2].astype(jnp.int32)
    rank = info[:, 4:6].astype(jnp.int32)
    counts = cnt[0, EXPERT_LANE0:EXPERT_LANE0 + N_EXPERTS].astype(jnp.int32)
    padded = ((counts + TM_MOE - 1) // TM_MOE) * TM_MOE
    pends = jnp.cumsum(padded)
    pstarts = pends - padded
    first = jnp.sum(jnp.where(e[..., None] == jnp.arange(N_EXPERTS, dtype=jnp.int32), pstarts, 0), axis=-1)
    dest = first + rank
    codes = _invert_slots(dest.reshape(2 * T // INV_ROWS, 1, INV_ROWS))
    src = (codes >> 1) * TT_SUB
    dst = ((codes & 1) * T + (codes >> 1)) * TT_SUB
    block0 = jnp.arange(N_BLOCKS, dtype=jnp.int32) * TM_MOE
    block_e = jnp.clip(jnp.searchsorted(pends, block0, side='right', method='compare_all'),
                       0, N_EXPERTS - 1).astype(jnp.int32)
    nvalid = jnp.clip(pstarts[block_e] + counts[block_e] - block0, 0, TM_MOE).astype(jnp.int32)
    nused = (pends[-1] // TM_MOE).astype(jnp.int32).reshape(1)
    return src, dst, block_e, nused, nvalid


def _combine_body(x_ref, y0_ref, y1_ref, info_ref, g2_ref, nw_ref, *o_refs, final):
    info = info_ref[...]
    lane = lax.broadcasted_iota(jnp.int32, info.shape, 1)
    w1 = jnp.sum(jnp.where(lane == 2, info, 0.0), axis=-1, keepdims=True)
    w2 = jnp.sum(jnp.where(lane == 3, info, 0.0), axis=-1, keepdims=True)
    y0 = _load_tt(y0_ref, TM_CMB, F32)
    y1 = _load_tt(y1_ref, TM_CMB, F32)
    x = x_ref[...] + g2_ref[...] * (y0 * w1 + y1 * w2)
    if final:
        ms = jnp.mean(x * x, axis=-1, keepdims=True)
        x = x * lax.rsqrt(ms + EPS) * nw_ref[...]

    def emit(ref):
        ref[...] = x

    _for_tile(o_refs, TM_CMB, emit)


def _combine(x1, y, info, mods, nw, *, final):
    tm = TM_CMB
    nt = T // tm
    if final:
        out_shape = [jax.ShapeDtypeStruct((T_S, D), F32), jax.ShapeDtypeStruct((T_P, D), F32)]
    else:
        out_shape = [jax.ShapeDtypeStruct((T, D), F32)]
    return pl.pallas_call(
        functools.partial(_combine_body, final=final),
        grid=(nt,),
        in_specs=[pl.BlockSpec((tm, D), lambda i: (i, 0)),
                  pl.BlockSpec((tm * TT_SUB, LANES), lambda i: (i, 0)),
                  pl.BlockSpec((tm * TT_SUB, LANES), lambda i: (i + nt, 0)),
                  pl.BlockSpec((tm, LANES), lambda i: (i, 0)),
                  _mod_spec(5, tm, 1),
                  pl.BlockSpec((1, D), lambda i: (0, 0))],
        out_specs=_row_specs(out_shape, tm, 1),
        out_shape=out_shape,
        compiler_params=_params(("arbitrary",)),
        name="combine_final" if final else "combine",
    )(x1, y, y, info, mods, nw.reshape(1, D))


def kernel(x_prompt, x_sample, c, state_ret_fwd, state_ret_bwd, state_lru_fwd, state_lru_bwd, c_ctx, w_mod, b_mod, norm_mix_w, norm_ffn_w, norm_final_w, w_in_even, w_out_even, gmlp_norm_w, gmlp_w_s, gmlp_b_s, ret_decay_fwd, ret_decay_bwd, ret_gn_w, w_in_odd, w_out_odd, conv_w, conv_b, conv_ln_w, conv_ln_b, lru_conv_w, lru_conv_b, lru_wa_fwd, lru_ba_fwd, lru_wx_fwd, lru_bx_fwd, lru_lam_fwd, lru_wa_bwd, lru_ba_bwd, lru_wx_bwd, lru_bx_bwd, lru_lam_bwd, router_grp_w, router_grp_b, router_exp_w, router_exp_b, moe_w_gate, moe_w_up, moe_w_down):
    xs = (x_sample.reshape(T_S, D), x_prompt.reshape(T_P, D))
    cond = jnp.concatenate([c, c_ctx[None, :], jnp.zeros((N_COND_PAD - B_S - 1, D), F32)], axis=0)
    m = _modulation(cond, w_mod, b_mod)
    mods_all = m.reshape(DEPTH, N_COND_PAD, N_MOD, D).transpose(0, 2, 1, 3)[:, :, :, None, :]

    ret_f = ret_b = lru_f = lru_b = None
    for l in range(DEPTH):
        mods = mods_all[l]
        if l % 2 == 0:
            e = l // 2
            p = _norm_in(xs, norm_mix_w[l], mods, w_in_even[e].astype(BF16), pack=False)
            b_full = jnp.repeat(gmlp_b_s[e].T.astype(F32), A_GDIM, axis=1)
            out_a = (_gmlp(p, gmlp_norm_w[e], gmlp_w_s[e].astype(BF16), b_full),)
            tabs = _retention_tables(ret_decay_fwd[e], ret_decay_bwd[e])
            cos, sin = _rope_tables()
            ob_s, _, _ = _retention(p, state_ret_fwd[:, e], state_ret_bwd[:, e], tabs, ret_gn_w[e], cos, sin,
                                    base_chunk=0, nb=B_S, nc=N_S // CHUNK, rope=True)
            zero_state = jnp.zeros((B_P, HEADS, HD, HD), F32)
            ob_p, ret_f, ret_b = _retention(p, zero_state, zero_state, tabs, ret_gn_w[e], cos, sin,
                                            base_chunk=T_S // CHUNK, nb=B_P, nc=N_P // CHUNK, rope=False)
            out_b = (ob_s, ob_p)
            w_out = w_out_even[e]
        else:
            o = l // 2
            p = _norm_in(xs, norm_mix_w[l], mods, w_in_odd[o].astype(BF16), pack=True)
            cw3 = jnp.concatenate([conv_w[o], jnp.zeros((1, W), F32)], axis=0).reshape(
                C_KSIZE + 1, N_STRIPS, LANES).transpose(1, 0, 2)
            conv_args = (cw3, conv_b[o], conv_ln_w[o], conv_ln_b[o])
            prm_f = _lru_params(lru_conv_w[o], lru_conv_b[o], lru_wa_fwd[o], lru_ba_fwd[o], lru_wx_fwd[o],
                                lru_bx_fwd[o], lru_lam_fwd[o])
            prm_b = _lru_params(lru_conv_w[o], lru_conv_b[o], lru_wa_bwd[o], lru_ba_bwd[o], lru_wx_bwd[o],
                                lru_bx_bwd[o], lru_lam_bwd[o])
            latent = dict(seq_row0=0, nseq=B_S, n=N_S)
            context = dict(seq_row0=T_S, nseq=B_P, n=N_P)
            out_a = (_od_conv(p, *conv_args, **latent), _od_conv(p, *conv_args, **context))
            yd_s, _, _ = _od_rglru(p, prm_f, prm_b, state_lru_fwd[:, o], state_lru_bwd[:, o], **latent)
            zero_h = jnp.zeros((B_P, W), F32)
            yd_p, lru_f, lru_b = _od_rglru(p, prm_f, prm_b, zero_h, zero_h, **context)
            out_b = (yd_s, yd_p)
            w_out = w_out_odd[o]

        rw = jnp.concatenate([router_grp_w[l], router_exp_w[l],
                              jnp.zeros((D, LANES - N_GROUPS - N_EXPERTS), F32)], axis=1)
        rw_hi = rw.astype(BF16)
        rw = jnp.concatenate([rw_hi, (rw - rw_hi.astype(F32)).astype(BF16)], axis=1)
        rb = jnp.concatenate([router_grp_b[l], router_exp_b[l],
                              jnp.zeros((LANES - N_GROUPS - N_EXPERTS,), F32)]).reshape(1, LANES)
        x1, hf, info, cnt = _out_proj(out_a, out_b, xs, mods, w_out.astype(BF16), norm_ffn_w[l], rw, rb,
                                      packed=(l % 2 == 1))
        y = _moe(hf, _dispatch_plan(info, cnt), moe_w_gate, moe_w_up, moe_w_down, l)
        xs = tuple(_combine(x1, y, info, mods, norm_final_w, final=(l == DEPTH - 1)))

    y_sample = xs[0].reshape(B_S, N_S, D)
    y_prompt = xs[1].reshape(B_P, N_P, D)
    return (y_prompt, y_sample, ret_f[:, None], ret_b[:, None], lru_f[:, None], lru_b[:, None])
```

```python
import functools

import jax
import jax.numpy as jnp
from jax import lax
from jax.experimental import pallas as pl
from jax.experimental.pallas import tpu as pltpu

F32 = jnp.float32
BF16 = jnp.bfloat16
U32 = jnp.uint32

D = 2048
B_P, N_P = 16, 256
B_S, N_S = 8, 4096
T_S = B_S * N_S
T_P = B_P * N_P
T = T_S + T_P
ROWS_PER_COND = 4096
N_COND_PAD = 16
DEPTH = 2
N_MOD = 6
EPS = 1e-6
GRID_W = 64
ROPE_BASE = 10000.0

W = 1024
HW = W // 2
HEADS = 8
HD = 128
CHUNK = 128
A_GROUPS = 4
A_GDIM = W // A_GROUPS
C_KSIZE = 31
D_KSIZE = 4
LRU_C = 8.0

N_GROUPS = 4
EXPERTS_PER_GROUP = 8
N_EXPERTS = 32
D_EXPERT = 512
LANES = 128
N_STRIPS = W // LANES
EXPERT_LANE0 = N_GROUPS

TM_IN = 512
TN_IN = 2048
TM_OUT = 512
TM_MOE = 512
N_SLOTS = 2 * T + N_EXPERTS * TM_MOE
N_BLOCKS = N_SLOTS // TM_MOE
TM_CMB = 512

VMEM_LIMIT = 56 * 1024 * 1024
MOE_VMEM_LIMIT = 60 * 1024 * 1024


def _params(sem, vmem=VMEM_LIMIT):
    return pltpu.CompilerParams(dimension_semantics=sem, vmem_limit_bytes=vmem)


def _cond_row(i, tm):
    return (i * tm) // ROWS_PER_COND


def _sigmoid(x):
    return 0.5 * jnp.tanh(0.5 * x) + 0.5


def _pack_pair(lo, hi):
    lo_b = lax.bitcast_convert_type(lo.astype(BF16).astype(F32), U32)
    hi_b = lax.bitcast_convert_type(hi.astype(BF16).astype(F32), U32)
    return (lo_b >> 16) | hi_b


def _unpack_pair(w, dtype):
    return (lax.bitcast_convert_type(w << 16, F32).astype(dtype),
            lax.bitcast_convert_type(w & U32(0xFFFF0000), F32).astype(dtype))


def _pack_halves(x):
    k = x.shape[1] // 2
    return _pack_pair(x[:, :k], x[:, k:])


def _unpack_halves(w, dtype):
    return jnp.concatenate(_unpack_pair(w, dtype), axis=1)


TT_SUB = 8


def _store_tt(ref, x, rows):
    for c in range(TT_SUB):
        ref[pl.ds(c, rows, stride=TT_SUB), :] = _pack_pair(x[:, c * LANES:(c + 1) * LANES],
                                                           x[:, W + c * LANES:W + (c + 1) * LANES])


def _load_tt(ref, rows, dtype):
    lo, hi = [], []
    for c in range(TT_SUB):
        a, b = _unpack_pair(ref[pl.ds(c, rows, stride=TT_SUB), :], dtype)
        lo.append(a)
        hi.append(b)
    return jnp.concatenate(lo + hi, axis=1)


def _row_specs(arrs, tm, ngrid, tile=lambda i: i):
    if len(arrs) == 1:
        maps = [tile]
    else:
        n_s = T_S // tm
        maps = [lambda i: jnp.minimum(tile(i), n_s - 1), lambda i: jnp.maximum(tile(i) - n_s, 0)]
    width = arrs[0].shape[1]
    if ngrid == 1:
        return [pl.BlockSpec((tm, width), lambda i, f=f: (f(i), 0)) for f in maps]
    return [pl.BlockSpec((tm, width), lambda i, j, f=f: (f(i), 0)) for f in maps]


def _tile_value(refs, tm):
    if len(refs) == 1:
        return refs[0][...]
    return jnp.where(pl.program_id(0) < T_S // tm, refs[0][...], refs[1][...])


def _for_tile(refs, tm, fn):
    if len(refs) == 1:
        fn(refs[0])
        return
    i = pl.program_id(0)

    @pl.when(i < T_S // tm)
    def _():
        fn(refs[0])

    @pl.when(i >= T_S // tm)
    def _():
        fn(refs[1])


def _mod_body(c_ref, w_ref, b_ref, o_ref):
    c = c_ref[...]
    a = (c * _sigmoid(c)).astype(BF16)
    o_ref[...] = jnp.dot(a, w_ref[...].astype(BF16), preferred_element_type=F32) + b_ref[...]


def _modulation(cond, w_mod, b_mod):
    tn = 1024
    return pl.pallas_call(
        _mod_body,
        grid=(DEPTH, N_MOD * D // tn),
        in_specs=[pl.BlockSpec((N_COND_PAD, D), lambda l, j: (0, 0)),
                  pl.BlockSpec((None, D, tn), lambda l, j: (l, 0, j)),
                  pl.BlockSpec((None, 1, tn), lambda l, j: (l, 0, j))],
        out_specs=pl.BlockSpec((None, N_COND_PAD, tn), lambda l, j: (l, 0, j)),
        out_shape=jax.ShapeDtypeStruct((DEPTH, N_COND_PAD, N_MOD * D), F32),
        compiler_params=_params(("arbitrary", "arbitrary")),
        name="modulation",
    )(cond, w_mod, b_mod.reshape(DEPTH, 1, N_MOD * D))


def _mod_spec(k, tm, ngrid, tile=lambda i: i):
    if ngrid == 1:
        return pl.BlockSpec((None, None, 1, D), lambda i: (k, _cond_row(tile(i), tm), 0, 0))
    return pl.BlockSpec((None, None, 1, D), lambda i, j: (k, _cond_row(i, tm), 0, 0))


def _combined(x_ref, y0_ref, y1_ref, info_ref, g2_ref, rows):
    info = info_ref[...]
    lane = lax.broadcasted_iota(jnp.int32, info.shape, 1)
    w1 = jnp.sum(jnp.where(lane == 2, info, 0.0), axis=-1, keepdims=True)
    w2 = jnp.sum(jnp.where(lane == 3, info, 0.0), axis=-1, keepdims=True)
    y0 = _load_tt(y0_ref, rows, F32)
    y1 = _load_tt(y1_ref, rows, F32)
    return x_ref[...] + g2_ref[...] * (y0 * w1 + y1 * w2)


def _norm_in_body(*refs, pack, fused, tn):
    if fused:
        x_refs, (nw_ref, sc_ref, sh_ref, w_ref, p_ref, x2_ref, h_scr) = refs[:5], refs[5:]
    else:
        x_refs, (nw_ref, sc_ref, sh_ref, w_ref, p_ref, h_scr) = refs[:-6], refs[-6:]

    @pl.when(pl.program_id(1) == 0)
    def _():
        if fused:
            x = _combined(*x_refs, TM_IN)
            x2_ref[...] = x
        else:
            x = _tile_value(x_refs, TM_IN)
        ms = jnp.mean(x * x, axis=-1, keepdims=True)
        h = x * lax.rsqrt(ms + EPS) * nw_ref[...]
        h = h * (1.0 + sc_ref[...]) + sh_ref[...]
        h_scr[...] = h.astype(BF16)

    r = jnp.dot(h_scr[...], w_ref[...], preferred_element_type=F32)
    if pack:
        p_ref[...] = jnp.concatenate([_pack_halves(r[:, k * W:(k + 1) * W]) for k in range(tn // W)], axis=1)
    else:
        p_ref[...] = r.astype(BF16)


def _norm_in(xs, nw, mods, w_bf16, *, pack, moe=None):
    n = w_bf16.shape[1]
    fused = moe is not None
    tn = W if fused else TN_IN
    nt = T // TM_IN
    if pack:
        out_specs = [pl.BlockSpec((TM_IN, tn // 2), lambda i, j: (i, j))]
        out_shape = [jax.ShapeDtypeStruct((T, n // 2), U32)]
    else:
        out_specs = [pl.BlockSpec((TM_IN, tn), lambda i, j: (i, j))]
        out_shape = [jax.ShapeDtypeStruct((T, n), BF16)]
    if fused:
        x1, y, info, mods_prev = moe
        x_specs = [pl.BlockSpec((TM_IN, D), lambda i, j: (i, 0)),
                   pl.BlockSpec((TM_IN * TT_SUB, LANES), lambda i, j: (i, 0)),
                   pl.BlockSpec((TM_IN * TT_SUB, LANES), lambda i, j: (i + nt, 0)),
                   pl.BlockSpec((TM_IN, LANES), lambda i, j: (i, 0)),
                   _mod_spec(5, TM_IN, 2)]
        x_args = [x1, y, y, info, mods_prev]
        out_specs.append(pl.BlockSpec((TM_IN, D), lambda i, j: (i, 0)))
        out_shape.append(jax.ShapeDtypeStruct((T, D), F32))
    else:
        x_specs = _row_specs(xs, TM_IN, 2)
        x_args = list(xs)
    return pl.pallas_call(
        functools.partial(_norm_in_body, pack=pack, fused=fused, tn=tn),
        grid=(nt, n // tn),
        in_specs=x_specs + [
            pl.BlockSpec((1, D), lambda i, j: (0, 0)),
            _mod_spec(1, TM_IN, 2),
            _mod_spec(0, TM_IN, 2),
            pl.BlockSpec((D, tn), lambda i, j: (0, j))],
        out_specs=out_specs,
        out_shape=out_shape,
        scratch_shapes=[pltpu.VMEM((TM_IN, D), BF16)],
        compiler_params=_params(("arbitrary", "arbitrary")),
        name="norm_in",
    )(*x_args, nw.reshape(1, D), mods, mods, w_bf16)


GMLP_ROWS = 512


def _gmlp_body(u_ref, v_ref, lnw_ref, ws_ref, b_ref, o_ref):
    for c in range(GMLP_ROWS // CHUNK):
        rows = pl.ds(c * CHUNK, CHUNK)
        v = jax.nn.gelu(v_ref[rows, :].astype(F32))
        mu = jnp.mean(v, axis=-1, keepdims=True)
        var = jnp.mean(jnp.square(v - mu), axis=-1, keepdims=True)
        vn = ((v - mu) * lax.rsqrt(var + EPS) * lnw_ref[...]).astype(BF16)
        z = jnp.concatenate(
            [jnp.dot(ws_ref[g], vn[:, g * A_GDIM:(g + 1) * A_GDIM], preferred_element_type=F32)
             for g in range(A_GROUPS)], axis=1) + b_ref[...]
        u = jax.nn.gelu(u_ref[rows, :].astype(F32))
        o_ref[rows, :] = (u * z).astype(BF16)


def _gmlp(p, lnw, ws_bf16, b_full):
    return pl.pallas_call(
        _gmlp_body,
        grid=(T // GMLP_ROWS,),
        in_specs=[pl.BlockSpec((GMLP_ROWS, W), lambda i: (i, 0)),
                  pl.BlockSpec((GMLP_ROWS, W), lambda i: (i, 1)),
                  pl.BlockSpec((1, W), lambda i: (0, 0)),
                  pl.BlockSpec((A_GROUPS, CHUNK, CHUNK), lambda i: (0, 0, 0)),
                  pl.BlockSpec((CHUNK, W), lambda i: (0, 0))],
        out_specs=pl.BlockSpec((GMLP_ROWS, W), lambda i: (i, 0)),
        out_shape=jax.ShapeDtypeStruct((T, W), BF16),
        compiler_params=_params(("arbitrary",)),
        name="gmlp",
    )(p, p, lnw.reshape(1, W), ws_bf16, b_full)


RET_SUB = 2
RET_ROWS = RET_SUB * CHUNK


def _roped_qk(q_ref, k_ref, cos_ref, sin_ref, rows, h, rope):
    cols = slice(h * HD, (h + 1) * HD)
    q = q_ref[rows, cols].astype(F32)
    k = k_ref[rows, cols].astype(F32) * (HD ** -0.5)
    if rope:
        c = cos_ref[rows, :]
        s = sin_ref[rows, :]
        q = q * c + pltpu.roll(q, HD // 2, axis=1) * s
        k = k * c + pltpu.roll(k, HD // 2, axis=1) * s
    return q, k


def _kt_v(kd, v):
    return lax.dot_general(kd, v, (((0,), (0,)), ((), ())), preferred_element_type=F32)


def _ret_bwd_body(q_ref, k_ref, v_ref, cos_ref, sin_ref, s0_ref, dq_ref, dk_ref, dc_ref,
                  ob_ref, sfin_ref, s_scr, *, rope, nc):
    c = pl.program_id(1)

    @pl.when(c == 0)
    def _():
        s_scr[...] = s0_ref[...]

    for sub in reversed(range(RET_SUB)):
        rows = slice(sub * CHUNK, (sub + 1) * CHUNK)
        for h in range(HEADS):
            cols = slice(h * HD, (h + 1) * HD)
            q, k = _roped_qk(q_ref, k_ref, cos_ref, sin_ref, rows, h, rope)
            v = v_ref[rows, cols]
            s = s_scr[h]
            ob_ref[rows, cols] = jnp.dot(q.astype(BF16), s.astype(BF16), preferred_element_type=F32) * dq_ref[h]
            kd = (k * dk_ref[h]).astype(BF16)
            s_scr[h] = s * dc_ref[h] + _kt_v(kd, v)

    @pl.when(c == nc - 1)
    def _():
        sfin_ref[...] = s_scr[...]


def _ret_fwd_body(q_ref, k_ref, v_ref, g_ref, cos_ref, sin_ref, ob_ref, s0_ref, m_ref, dq_ref, dk_ref, dc_ref,
                  gn_ref, o_ref, sfin_ref, s_scr, *, rope, nc):
    c = pl.program_id(1)

    @pl.when(c == 0)
    def _():
        s_scr[...] = s0_ref[...]

    for sub in range(RET_SUB):
        rows = slice(sub * CHUNK, (sub + 1) * CHUNK)
        for h in range(HEADS):
            cols = slice(h * HD, (h + 1) * HD)
            q, k = _roped_qk(q_ref, k_ref, cos_ref, sin_ref, rows, h, rope)
            qb = q.astype(BF16)
            v = v_ref[rows, cols]
            s = s_scr[h]
            scores = lax.dot_general(qb, k.astype(BF16), (((1,), (1,)), ((), ())),
                                     preferred_element_type=F32) * m_ref[h]
            o = (jnp.dot(scores.astype(BF16), v, preferred_element_type=F32)
                 + jnp.dot(qb, s.astype(BF16), preferred_element_type=F32) * dq_ref[h]
                 + ob_ref[rows, cols])
            kd = (k * dk_ref[h]).astype(BF16)
            s_scr[h] = s * dc_ref[h] + _kt_v(kd, v)
            mu = jnp.mean(o, axis=-1, keepdims=True)
            var = jnp.mean(jnp.square(o - mu), axis=-1, keepdims=True)
            on = (o - mu) * lax.rsqrt(var + EPS) * gn_ref[:, cols]
            g = g_ref[rows, cols].astype(F32)
            o_ref[rows, cols] = (g * _sigmoid(g) * on).astype(BF16)

    @pl.when(c == nc - 1)
    def _():
        sfin_ref[...] = s_scr[...]


def _retention(p, s_f0, s_b0, tabs, gn_w, cos, sin, *, base_chunk, nb, nc, rope):
    m_tab, dq_f, dk_f, dc_f, dq_b, dk_b, dc_b = tabs
    ns = nc // RET_SUB
    base = base_chunk // RET_SUB
    state_spec = pl.BlockSpec((None, HEADS, HD, HD), lambda b, c: (b, 0, 0, 0))
    tab_spec = pl.BlockSpec((HEADS, HD, HD), lambda b, c: (0, 0, 0))
    dc_spec = pl.BlockSpec((HEADS, 1, HD), lambda b, c: (0, 0, 0))
    state_shape = jax.ShapeDtypeStruct((nb, HEADS, HD, HD), F32)

    def step(c, rev):
        return ns - 1 - c if rev else c

    def col(j, rev):
        return pl.BlockSpec((RET_ROWS, W), lambda b, c: (base + b * ns + step(c, rev), j))

    def rope_spec(rev):
        if not rope:
            return pl.BlockSpec((RET_ROWS, HD), lambda b, c: (0, 0))
        return pl.BlockSpec((RET_ROWS, HD), lambda b, c: (step(c, rev), 0))

    def local(rev):
        return pl.BlockSpec((RET_ROWS, W), lambda b, c: (b * ns + step(c, rev), 0))

    n_rows = nb * nc * CHUNK
    ob, s_b = pl.pallas_call(
        functools.partial(_ret_bwd_body, rope=rope, nc=ns),
        grid=(nb, ns),
        in_specs=[col(2, True), col(3, True), col(4, True), rope_spec(True), rope_spec(True),
                  state_spec, tab_spec, tab_spec, dc_spec],
        out_specs=[local(True), state_spec],
        out_shape=[jax.ShapeDtypeStruct((n_rows, W), F32), state_shape],
        scratch_shapes=[pltpu.VMEM((HEADS, HD, HD), F32)],
        compiler_params=_params(("arbitrary", "arbitrary")),
        name="retention_bwd",
    )(p, p, p, cos, sin, s_b0, dq_b, dk_b, dc_b)

    o, s_f = pl.pallas_call(
        functools.partial(_ret_fwd_body, rope=rope, nc=ns),
        grid=(nb, ns),
        in_specs=[col(2, False), col(3, False), col(4, False), col(5, False), rope_spec(False), rope_spec(False),
                  local(False), state_spec, tab_spec, tab_spec, tab_spec, dc_spec,
                  pl.BlockSpec((1, W), lambda b, c: (0, 0))],
        out_specs=[local(False), state_spec],
        out_shape=[jax.ShapeDtypeStruct((n_rows, W), BF16), state_shape],
        scratch_shapes=[pltpu.VMEM((HEADS, HD, HD), F32)],
        compiler_params=_params(("arbitrary", "arbitrary")),
        name="retention_fwd",
    )(p, p, p, p, cos, sin, ob, s_f0, m_tab, dq_f, dk_f, dc_f, gn_w.reshape(1, W))
    return o, s_f, s_b


def _retention_tables(decay_fwd, decay_bwd):
    lg_f = jax.nn.log_sigmoid(decay_fwd.astype(F32))[:, None, None]
    lg_b = jax.nn.log_sigmoid(decay_bwd.astype(F32))[:, None, None]
    pos = jnp.arange(CHUNK, dtype=F32)
    rel = pos[:, None] - pos[None, :]
    m_tab = (jnp.where(rel >= 0, jnp.exp(lg_f * jnp.maximum(rel, 0.0)), 0.0)
             + jnp.where(rel <= 0, jnp.exp(lg_b * jnp.maximum(-rel, 0.0)), 0.0))
    ones = jnp.ones((1, 1, HD), F32)
    col = pos[None, :, None]
    dq_f = jnp.exp(lg_f * (col + 1.0)) * ones
    dk_f = jnp.exp(lg_f * (CHUNK - 1.0 - col)) * ones
    dq_b = jnp.exp(lg_b * (CHUNK - col)) * ones
    dk_b = jnp.exp(lg_b * col) * ones
    dc_f = jnp.exp(lg_f * CHUNK) * ones
    dc_b = jnp.exp(lg_b * CHUNK) * ones
    return m_tab, dq_f, dk_f, dc_f, dq_b, dk_b, dc_b


def _rope_tables():
    rows = N_S // GRID_W
    row = jnp.repeat(jnp.arange(rows, dtype=F32), GRID_W)
    colp = jnp.tile(jnp.arange(GRID_W, dtype=F32), rows)
    quarter = HD // 4
    inv_freq = ROPE_BASE ** (-jnp.arange(quarter, dtype=F32) / quarter)
    ang = jnp.concatenate([row[:, None] * inv_freq, colp[:, None] * inv_freq], axis=-1)
    cos, sin = jnp.cos(ang), jnp.sin(ang)
    return jnp.concatenate([cos, cos], axis=-1), jnp.concatenate([-sin, sin], axis=-1)


OD_TT = 128
OD_G = 8
CONV_HALO = 16
LRU_HALO = 8


def _od_rows(seq_row0, n, g, tj):
    return [seq_row0 + (g * OD_G + b) * n + tj * OD_TT for b in range(OD_G)]


def _od_in_copies(p_hbm, buf, sem, *, col, halo, seq_row0, n, g, tj):
    nt = n // OD_TT
    out = []
    for b, base in enumerate(_od_rows(seq_row0, n, g, tj)):
        def src(r, k):
            return p_hbm.at[pl.ds(r, k), pl.ds(col * HW, HW)]

        out.append((None, pltpu.make_async_copy(src(base, OD_TT), buf.at[pl.ds(halo, OD_TT), b, :], sem)))
        if halo:
            out.append((tj > 0, pltpu.make_async_copy(src(base - halo, halo), buf.at[pl.ds(0, halo), b, :], sem)))
            out.append((tj < nt - 1, pltpu.make_async_copy(src(base + OD_TT, halo),
                                                           buf.at[pl.ds(halo + OD_TT, halo), b, :], sem)))
    return out


def _od_run(copies, op):
    for cond, cp in copies:
        fn = cp.start if op == "start" else cp.wait
        if cond is None:
            fn()
        else:
            pl.when(cond)(fn)


def _od_zero_halo(buf, halo, n, tj):
    nt = n // OD_TT
    zeros = jnp.zeros((halo, OD_G, HW), U32)

    @pl.when(tj == 0)
    def _():
        buf[0:halo] = zeros

    @pl.when(tj == nt - 1)
    def _():
        buf[halo + OD_TT:2 * halo + OD_TT] = zeros


def _od_out_copies(obuf, out_hbm, sem, *, seq_row0, n, g, tj):
    return [(None, pltpu.make_async_copy(obuf.at[:, b, :], out_hbm.at[pl.ds(base - seq_row0, OD_TT), :], sem))
            for b, base in enumerate(_od_rows(seq_row0, n, g, tj))]


def _od_unpack(buf, rows):
    return _unpack_halves(buf[...].reshape(rows, HW), F32)


def _od_pipeline(q, nq, in_copies, sem_slots=2):
    s = q % 2

    @pl.when(q == 0)
    def _():
        _od_run(in_copies(q, s), "start")

    _od_run(in_copies(q, s), "wait")

    @pl.when(q + 1 < nq)
    def _():
        _od_run(in_copies(q + 1, 1 - s), "start")

    return s


def _od_emit(q, nq, s, obuf, words, out_copies):
    @pl.when(q >= 2)
    def _():
        _od_run(out_copies(s), "wait")

    obuf[s] = words.reshape(OD_TT, OD_G, HW)
    _od_run(out_copies(s), "start")

    @pl.when(q == nq - 1)
    def _():
        _od_run(out_copies(s), "wait")

        @pl.when(q >= 1)
        def _():
            _od_run(out_copies(1 - s), "wait")


CONV_RB = 64


def _od_conv_body(p_hbm, cw_ref, cb_ref, lnw_ref, lnb_ref, yc_hbm, xin, glu, ybuf, obuf, sem_in, sem_out,
                  *, seq_row0, n, ng):
    nt = n // OD_TT
    nq = ng * nt
    q = pl.program_id(0)
    halo = CONV_HALO
    rows_in = (OD_TT + 2 * halo) * OD_G
    rows = OD_TT * OD_G

    def in_copies(step, slot):
        out = []
        for br in range(2):
            out += _od_in_copies(p_hbm, xin.at[br, slot], sem_in.at[slot], col=br, halo=halo,
                                 seq_row0=seq_row0, n=n, g=step // nt, tj=step % nt)
        return out

    s = _od_pipeline(q, nq, in_copies)
    g, tj = q // nt, q % nt
    for br in range(2):
        _od_zero_halo(xin.at[br, s], halo, n, tj)
    glu[...] = _od_unpack(xin.at[0, s], rows_in) * _sigmoid(_od_unpack(xin.at[1, s], rows_in))

    def strip(si, carry):
        lanes = pl.ds(pl.multiple_of(si * LANES, LANES), LANES)
        for rb in range(rows // CONV_RB):
            acc = jnp.zeros((CONV_RB // 8, 8, LANES), F32)
            for k in range(C_KSIZE):
                wk = cw_ref[si, pl.ds(k, 1), :]
                r0 = rb * CONV_RB + (halo - C_KSIZE // 2 + k) * OD_G
                acc = acc + glu[r0:r0 + CONV_RB, lanes].reshape(CONV_RB // 8, 8, LANES) * wk[None]
            ybuf[rb * CONV_RB:(rb + 1) * CONV_RB, lanes] = acc.reshape(CONV_RB, LANES)
        return carry

    lax.fori_loop(0, N_STRIPS, strip, 0)
    y = ybuf[...] + cb_ref[...]
    mu = jnp.mean(y, axis=-1, keepdims=True)
    var = jnp.mean(jnp.square(y - mu), axis=-1, keepdims=True)
    yn = (y - mu) * lax.rsqrt(var + EPS) * lnw_ref[...] + lnb_ref[...]
    yo = yn * _sigmoid(yn)

    def out_copies(slot):
        return _od_out_copies(obuf.at[slot], yc_hbm, sem_out.at[slot], seq_row0=seq_row0, n=n, g=g, tj=tj)

    _od_emit(q, nq, s, obuf, _pack_halves(yo), out_copies)


def _od_conv(p, cw3, cb, lnw, lnb, *, seq_row0, nseq, n):
    ng = nseq // OD_G
    nq = ng * (n // OD_TT)
    vec = pl.BlockSpec((1, W), lambda q: (0, 0))
    t_in = OD_TT + 2 * CONV_HALO
    return pl.pallas_call(
        functools.partial(_od_conv_body, seq_row0=seq_row0, n=n, ng=ng),
        grid=(nq,),
        in_specs=[pl.BlockSpec(memory_space=pl.ANY),
                  pl.BlockSpec((N_STRIPS, C_KSIZE + 1, LANES), lambda q: (0, 0, 0)), vec, vec, vec],
        out_specs=pl.BlockSpec(memory_space=pl.ANY),
        out_shape=jax.ShapeDtypeStruct((nseq * n, HW), U32),
        scratch_shapes=[pltpu.VMEM((2, 2, t_in, OD_G, HW), U32),
                        pltpu.VMEM((t_in * OD_G, W), F32),
                        pltpu.VMEM((OD_TT * OD_G, W), F32),
                        pltpu.VMEM((2, OD_TT, OD_G, HW), U32),
                        pltpu.SemaphoreType.DMA((2,)), pltpu.SemaphoreType.DMA((2,))],
        compiler_params=_params(("arbitrary",)),
        name="conv_module",
    )(p, cw3, cb.reshape(1, W), lnw.reshape(1, W), lnb.reshape(1, W))


def _od_gates(x, cw_ref, cb_ref, wa_ref, ba_ref, wx_ref, bx_ref, sp_ref, a_scr, b_scr):
    rows = OD_TT * OD_G
    xc = cb_ref[...]
    for k in range(D_KSIZE):
        r0 = (LRU_HALO - D_KSIZE // 2 + k) * OD_G
        xc = xc + cw_ref[k:k + 1, :] * x[r0:r0 + rows, :]
    xb = xc.astype(BF16)
    r = jnp.concatenate([jnp.dot(xb[:, h * HD:(h + 1) * HD], wa_ref[h], preferred_element_type=F32)
                         for h in range(HEADS)], axis=1) + ba_ref[...]
    g = jnp.concatenate([jnp.dot(xb[:, h * HD:(h + 1) * HD], wx_ref[h], preferred_element_type=F32)
                         for h in range(HEADS)], axis=1) + bx_ref[...]
    log_a = (-LRU_C) * _sigmoid(r) * sp_ref[...]
    th = jnp.tanh(log_a)
    one_minus_a2 = -2.0 * th / (1.0 - th)
    a_scr[...] = jnp.exp(log_a)
    b_scr[...] = jnp.sqrt(jnp.maximum(one_minus_a2, 0.0)) * (_sigmoid(g) * xc)


def _od_scan(a_scr, b_scr, out_ref, h, reverse):
    def step(t, h):
        tr = (OD_TT - 1 - t) if reverse else t
        rows = pl.ds(pl.multiple_of(tr * OD_G, OD_G), OD_G)
        h = a_scr[rows, :] * h + b_scr[rows, :]
        out_ref[rows, :] = h
        return h

    return lax.fori_loop(0, OD_TT, step, h, unroll=8)


def _od_lru_fwd_body(p_hbm, cw_ref, cb_ref, wa_ref, ba_ref, wx_ref, bx_ref, sp_ref, h0_ref,
                     hf_ref, hfin_ref, xin, a_scr, b_scr, h_scr, sem_in, *, seq_row0, n, ng):
    nt = n // OD_TT
    nq = ng * nt
    q = pl.program_id(0)

    def in_copies(step, slot):
        return _od_in_copies(p_hbm, xin.at[slot], sem_in.at[slot], col=2, halo=LRU_HALO,
                             seq_row0=seq_row0, n=n, g=step // nt, tj=step % nt)

    s = _od_pipeline(q, nq, in_copies)
    tj = q % nt
    _od_zero_halo(xin.at[s], LRU_HALO, n, tj)

    @pl.when(tj == 0)
    def _():
        h_scr[...] = h0_ref[...]

    x = _od_unpack(xin.at[s], (OD_TT + 2 * LRU_HALO) * OD_G)
    _od_gates(x, cw_ref, cb_ref, wa_ref, ba_ref, wx_ref, bx_ref, sp_ref, a_scr, b_scr)
    h = _od_scan(a_scr, b_scr, hf_ref, h_scr[...], False)
    h_scr[...] = h

    @pl.when(tj == nt - 1)
    def _():
        hfin_ref[...] = h


def _od_lru_bwd_body(p_hbm, hf_ref, cw_ref, cb_ref, wa_ref, ba_ref, wx_ref, bx_ref, sp_ref, h0_ref,
                     yd_hbm, hfin_ref, xin, gin, a_scr, b_scr, hb_scr, obuf, h_scr, sem_in, sem_out,
                     *, seq_row0, n, ng):
    nt = n // OD_TT
    nq = ng * nt
    q = pl.program_id(0)

    def in_copies(step, slot):
        where = dict(seq_row0=seq_row0, n=n, g=step // nt, tj=nt - 1 - step % nt)
        return (_od_in_copies(p_hbm, xin.at[slot], sem_in.at[slot], col=2, halo=LRU_HALO, **where)
                + _od_in_copies(p_hbm, gin.at[slot], sem_in.at[slot], col=3, halo=0, **where))

    s = _od_pipeline(q, nq, in_copies)
    g, tj = q // nt, nt - 1 - q % nt
    _od_zero_halo(xin.at[s], LRU_HALO, n, tj)

    @pl.when(tj == nt - 1)
    def _():
        h_scr[...] = h0_ref[...]

    x = _od_unpack(xin.at[s], (OD_TT + 2 * LRU_HALO) * OD_G)
    _od_gates(x, cw_ref, cb_ref, wa_ref, ba_ref, wx_ref, bx_ref, sp_ref, a_scr, b_scr)
    h = _od_scan(a_scr, b_scr, hb_scr, h_scr[...], True)
    h_scr[...] = h

    @pl.when(tj == 0)
    def _():
        hfin_ref[...] = h

    gd = _od_unpack(gin.at[s], OD_TT * OD_G)
    yd = (hf_ref[...] + hb_scr[...]) * jax.nn.gelu(gd)

    def out_copies(slot):
        return _od_out_copies(obuf.at[slot], yd_hbm, sem_out.at[slot], seq_row0=seq_row0, n=n, g=g, tj=tj)

    _od_emit(q, nq, s, obuf, _pack_halves(yd), out_copies)


def _od_rglru(p, prm_f, prm_b, h0_f, h0_b, *, seq_row0, nseq, n):
    ng = nseq // OD_G
    nt = n // OD_TT
    nq = ng * nt
    rows = OD_TT * OD_G
    vec = pl.BlockSpec((1, W), lambda q: (0, 0))
    wsp = pl.BlockSpec((HEADS, HD, HD), lambda q: (0, 0, 0))
    prm_specs = [pl.BlockSpec((8, W), lambda q: (0, 0)), vec, wsp, vec, wsp, vec, vec,
                 pl.BlockSpec((OD_G, W), lambda q: (q // nt, 0))]
    state_spec = pl.BlockSpec((OD_G, W), lambda q: (q // nt, 0))
    state_shape = jax.ShapeDtypeStruct((nseq, W), F32)
    t_in = OD_TT + 2 * LRU_HALO
    tile_f32 = pltpu.VMEM((rows, W), F32)

    hf, s_f = pl.pallas_call(
        functools.partial(_od_lru_fwd_body, seq_row0=seq_row0, n=n, ng=ng),
        grid=(nq,),
        in_specs=[pl.BlockSpec(memory_space=pl.ANY)] + prm_specs,
        out_specs=[pl.BlockSpec((rows, W), lambda q: (q, 0)), state_spec],
        out_shape=[jax.ShapeDtypeStruct((nq * rows, W), F32), state_shape],
        scratch_shapes=[pltpu.VMEM((2, t_in, OD_G, HW), U32), tile_f32, tile_f32, pltpu.VMEM((OD_G, W), F32),
                        pltpu.SemaphoreType.DMA((2,))],
        compiler_params=_params(("arbitrary",)),
        name="rglru_fwd",
    )(p, *prm_f, h0_f)

    yd, s_b = pl.pallas_call(
        functools.partial(_od_lru_bwd_body, seq_row0=seq_row0, n=n, ng=ng),
        grid=(nq,),
        in_specs=[pl.BlockSpec(memory_space=pl.ANY),
                  pl.BlockSpec((rows, W), lambda q: ((q // nt) * nt + nt - 1 - q % nt, 0))] + prm_specs,
        out_specs=[pl.BlockSpec(memory_space=pl.ANY), state_spec],
        out_shape=[jax.ShapeDtypeStruct((nseq * n, HW), U32), state_shape],
        scratch_shapes=[pltpu.VMEM((2, t_in, OD_G, HW), U32), pltpu.VMEM((2, OD_TT, OD_G, HW), U32),
                        tile_f32, tile_f32, tile_f32, pltpu.VMEM((2, OD_TT, OD_G, HW), U32),
                        pltpu.VMEM((OD_G, W), F32),
                        pltpu.SemaphoreType.DMA((2,)), pltpu.SemaphoreType.DMA((2,))],
        compiler_params=_params(("arbitrary",)),
        name="rglru_bwd",
    )(p, hf, *prm_b, h0_b)
    return yd, s_f, s_b


def _lru_params(cw, cb, wa, ba, wx, bx, lam):
    cw_pad = jnp.concatenate([cw, jnp.zeros((8 - D_KSIZE, W), F32)], axis=0)
    sp = jax.nn.softplus(-lam.astype(F32)).reshape(1, W)
    return (cw_pad, cb.reshape(1, W), wa.astype(BF16), ba.reshape(1, W), wx.astype(BF16), bx.reshape(1, W), sp)


def _out_body(*refs, n_a, n_b, n_x, packed):
    a_refs = refs[:n_a]
    b_refs = refs[n_a:n_a + n_b]
    x_refs = refs[n_a + n_b:n_a + n_b + n_x]
    (g1_ref, wo_ref, nw_ref, sc_ref, sh_ref, rw_ref, rb_ref,
     x1_ref, hf_ref, info_ref, cnt_ref, carry) = refs[n_a + n_b + n_x:]
    i = pl.program_id(0)

    @pl.when(i == 0)
    def _():
        carry[...] = jnp.zeros_like(carry)

    def branch(refs_, row0):
        v = _tile_value(refs_, TM_OUT)
        if packed:
            lo, hi = _unpack_pair(v, BF16)
            return (jnp.dot(lo, wo_ref[row0:row0 + HW, :], preferred_element_type=F32)
                    + jnp.dot(hi, wo_ref[row0 + HW:row0 + W, :], preferred_element_type=F32))
        return jnp.dot(v, wo_ref[row0:row0 + W, :], preferred_element_type=F32)

    x1 = _tile_value(x_refs, TM_OUT) + g1_ref[...] * (branch(a_refs, 0) + branch(b_refs, W))
    x1_ref[...] = x1
    ms = jnp.mean(x1 * x1, axis=-1, keepdims=True)
    hf = x1 * lax.rsqrt(ms + EPS) * nw_ref[...]
    hf = hf * (1.0 + sc_ref[...]) + sh_ref[...]
    _store_tt(hf_ref, hf, TM_OUT)

    hf_hi = hf.astype(BF16)
    hf_lo = (hf - hf_hi.astype(F32)).astype(BF16)
    l2 = jnp.dot(hf_hi, rw_ref[...], preferred_element_type=F32)
    logits = (l2[:, :LANES] + l2[:, LANES:]
              + jnp.dot(hf_lo, rw_ref[:, :LANES], preferred_element_type=F32) + rb_ref[...])
    lane = lax.broadcasted_iota(jnp.int32, (TM_OUT, LANES), 1).astype(F32)
    neg = -jnp.inf
    gmask = lane < N_GROUPS
    mg = jnp.max(jnp.where(gmask, logits, neg), axis=-1, keepdims=True)
    gstar = jnp.min(jnp.where(gmask & (logits == mg), lane, float(LANES)), axis=-1, keepdims=True)
    denom = jnp.sum(jnp.where(gmask, jnp.exp(logits - mg), 0.0), axis=-1, keepdims=True)
    psel = 1.0 / denom
    lo = EXPERT_LANE0 + EXPERTS_PER_GROUP * gstar
    emask = (lane >= lo) & (lane < lo + EXPERTS_PER_GROUP)
    v1 = jnp.max(jnp.where(emask, logits, neg), axis=-1, keepdims=True)
    i1 = jnp.min(jnp.where(emask & (logits == v1), lane, float(LANES)), axis=-1, keepdims=True)
    em2 = emask & (lane != i1)
    v2 = jnp.max(jnp.where(em2, logits, neg), axis=-1, keepdims=True)
    i2 = jnp.min(jnp.where(em2 & (logits == v2), lane, float(LANES)), axis=-1, keepdims=True)
    e21 = jnp.exp(v2 - v1)
    w1 = psel / (1.0 + e21)
    w2 = psel * e21 / (1.0 + e21)

    memb = jnp.where((lane == i1) | (lane == i2), 1.0, 0.0)
    r_i = lax.broadcasted_iota(jnp.int32, (TM_OUT, TM_OUT), 0)
    c_i = lax.broadcasted_iota(jnp.int32, (TM_OUT, TM_OUT), 1)
    lower = jnp.where(r_i > c_i, 1.0, 0.0).astype(BF16)
    before = jnp.dot(lower, memb.astype(BF16), preferred_element_type=F32) + carry[0:1, :]
    rank1 = jnp.sum(jnp.where(lane == i1, before, 0.0), axis=-1, keepdims=True)
    rank2 = jnp.sum(jnp.where(lane == i2, before, 0.0), axis=-1, keepdims=True)
    new_carry = carry[...] + jnp.sum(memb, axis=0, keepdims=True)
    carry[...] = new_carry
    cnt_ref[...] = new_carry

    e1 = i1 - EXPERT_LANE0
    e2 = i2 - EXPERT_LANE0
    info = jnp.where(lane == 0, e1, jnp.where(lane == 1, e2, jnp.where(lane == 2, w1, jnp.where(
        lane == 3, w2, jnp.where(lane == 4, rank1, jnp.where(lane == 5, rank2, 0.0))))))
    info_ref[...] = info


def _out_proj(a_parts, b_parts, xs, mods, wo_bf16, nw, rw, rb, *, packed):
    tm = TM_OUT
    row = lambda n: pl.BlockSpec((tm, n), lambda i: (i, 0))
    vec = pl.BlockSpec((1, D), lambda i: (0, 0))
    return pl.pallas_call(
        functools.partial(_out_body, n_a=len(a_parts), n_b=len(b_parts), n_x=len(xs), packed=packed),
        grid=(T // tm,),
        in_specs=_row_specs(a_parts, tm, 1) + _row_specs(b_parts, tm, 1) + _row_specs(xs, tm, 1) + [
            _mod_spec(2, tm, 1),
            pl.BlockSpec((D, D), lambda i: (0, 0)), vec, _mod_spec(4, tm, 1), _mod_spec(3, tm, 1),
            pl.BlockSpec((D, 2 * LANES), lambda i: (0, 0)), pl.BlockSpec((1, LANES), lambda i: (0, 0))],
        out_specs=[row(D), pl.BlockSpec((tm * TT_SUB, LANES), lambda i: (i, 0)), row(LANES),
                   pl.BlockSpec((8, LANES), lambda i: (0, 0))],
        out_shape=[jax.ShapeDtypeStruct((T, D), F32), jax.ShapeDtypeStruct((T * TT_SUB, LANES), U32),
                   jax.ShapeDtypeStruct((T, LANES), F32), jax.ShapeDtypeStruct((8, LANES), F32)],
        scratch_shapes=[pltpu.VMEM((8, LANES), F32)],
        compiler_params=_params(("arbitrary",)),
        name="out_proj_router",
    )(*a_parts, *b_parts, *xs, mods, wo_bf16, nw.reshape(1, D), mods, mods, rw, rb)


COPY_UNROLL = 32


def _moe_body(be_ref, nused_ref, nv_ref, src_cur, src_nxt, dst_cur, hf_hbm, wg_ref, wu_ref, wd_ref, y_hbm,
              xbuf, obuf, wgc, wuc, wdc, sem_in, sem_out):
    i = pl.program_id(0)
    nused = nused_ref[0]
    slot = i % 2

    def tile(ref, row8):
        return ref.at[pl.ds(pl.multiple_of(row8, TT_SUB), TT_SUB)]

    def rows(ref, n):
        return ref.at[pl.ds(0, n * TT_SUB)]

    def for_rows(n, fn):
        groups = n // COPY_UNROLL

        def group(c, carry):
            for u in range(COPY_UNROLL):
                fn(c * COPY_UNROLL + u)
            return carry

        def single(r, carry):
            fn(r)
            return carry

        lax.fori_loop(0, groups, group, 0)
        lax.fori_loop(groups * COPY_UNROLL, n, single, 0)

    def gather(src_ref, s, n):
        def row(r):
            pltpu.make_async_copy(tile(hf_hbm, src_ref[0, r]), tile(xbuf.at[s], r * TT_SUB), sem_in.at[s]).start()

        for_rows(n, row)

    def wait_scatter(n):
        pltpu.make_async_copy(rows(obuf, n), rows(y_hbm, n), sem_out).wait()

    @pl.when(i < nused)
    def _():
        nv = nv_ref[i]

        @pl.when(i == 0)
        def _():
            xbuf[...] = jnp.zeros(xbuf.shape, U32)
            gather(src_cur, 0, nv)

        pltpu.make_async_copy(rows(hf_hbm, nv), rows(xbuf.at[slot], nv), sem_in.at[slot]).wait()

        @pl.when(i + 1 < nused)
        def _():
            gather(src_nxt, 1 - slot, nv_ref[i + 1])

        @pl.when((i == 0) | (be_ref[i] != be_ref[jnp.maximum(i - 1, 0)]))
        def _():
            wgc[...] = wg_ref[...].astype(BF16)
            wuc[...] = wu_ref[...].astype(BF16)
            wdc[...] = wd_ref[...].astype(BF16)

        x = _load_tt(xbuf.at[slot], TM_MOE, BF16)
        gate = jnp.dot(x, wgc[...], preferred_element_type=F32)
        up = jnp.dot(x, wuc[...], preferred_element_type=F32)
        hid = (gate * _sigmoid(gate) * up).astype(BF16)

        @pl.when(i > 0)
        def _():
            wait_scatter(nv_ref[i - 1])

        _store_tt(obuf, jnp.dot(hid, wdc[...], preferred_element_type=F32), TM_MOE)

        def scatter_row(r):
            pltpu.make_async_copy(tile(obuf, r * TT_SUB), tile(y_hbm, dst_cur[0, r]), sem_out).start()

        for_rows(nv, scatter_row)

        @pl.when(i == nused - 1)
        def _():
            wait_scatter(nv)


def _moe(hf, plan, wg, wu, wd, layer):
    src, dst, block_e, nused, nvalid = plan
    wspec = lambda shape: pl.BlockSpec((None, None) + shape, lambda i, be, nu, nv: (layer, be[i], 0, 0))
    rows = lambda f: pl.BlockSpec((None, 1, TM_MOE), lambda i, be, nu, nv: (f(i), 0, 0), memory_space=pltpu.SMEM)
    return pl.pallas_call(
        _moe_body,
        grid_spec=pltpu.PrefetchScalarGridSpec(
            num_scalar_prefetch=3,
            grid=(N_BLOCKS,),
            in_specs=[rows(lambda i: i), rows(lambda i: jnp.minimum(i + 1, N_BLOCKS - 1)), rows(lambda i: i),
                      pl.BlockSpec(memory_space=pl.ANY),
                      wspec((D, D_EXPERT)), wspec((D, D_EXPERT)), wspec((D_EXPERT, D))],
            out_specs=pl.BlockSpec(memory_space=pl.ANY),
            scratch_shapes=[pltpu.VMEM((2, TM_MOE * TT_SUB, LANES), U32),
                            pltpu.VMEM((TM_MOE * TT_SUB, LANES), U32),
                            pltpu.VMEM((D, D_EXPERT), BF16), pltpu.VMEM((D, D_EXPERT), BF16),
                            pltpu.VMEM((D_EXPERT, D), BF16),
                            pltpu.SemaphoreType.DMA((2,)), pltpu.SemaphoreType.DMA]),
        out_shape=jax.ShapeDtypeStruct((2 * T * TT_SUB, LANES), U32),
        compiler_params=_params(("arbitrary",), MOE_VMEM_LIMIT),
        name="moe_experts",
    )(block_e, nused, nvalid, src.reshape(N_BLOCKS, 1, TM_MOE), src.reshape(N_BLOCKS, 1, TM_MOE),
      dst.reshape(N_BLOCKS, 1, TM_MOE), hf, wg, wu, wd)


INV_ROWS = 2048


def _invert_body(dest_ref, zeros_hbm, codes_ref, sem):
    i = pl.program_id(0)

    @pl.when(i == 0)
    def _():
        clear = pltpu.make_async_copy(zeros_hbm, codes_ref, sem)
        clear.start()
        clear.wait()

    def put(j, carry):
        codes_ref[dest_ref[0, j]] = i * INV_ROWS + j
        return carry

    lax.fori_loop(0, INV_ROWS, put, 0, unroll=16)


def _invert_slots(dest3):
    return pl.pallas_call(
        _invert_body,
        grid=(dest3.shape[0],),
        in_specs=[pl.BlockSpec((None, 1, INV_ROWS), lambda i: (i, 0, 0), memory_space=pltpu.SMEM),
                  pl.BlockSpec(memory_space=pl.ANY)],
        out_specs=pl.BlockSpec(memory_space=pltpu.SMEM),
        out_shape=jax.ShapeDtypeStruct((N_SLOTS,), jnp.int32),
        scratch_shapes=[pltpu.SemaphoreType.DMA],
        compiler_params=_params(("arbitrary",)),
        name="invert_slots",
    )(dest3, jnp.zeros((N_SLOTS,), jnp.int32))


def _dispatch_plan(info, cnt):
    e = info[:, 0:2].astype(jnp.int32)
    rank = info[:, 4:6].astype(jnp.int32)
    counts = cnt[0, EXPERT_LANE0:EXPERT_LANE0 + N_EXPERTS].astype(jnp.int32)
    padded = ((counts + TM_MOE - 1) // TM_MOE) * TM_MOE
    pends = jnp.cumsum(padded)
    pstarts = pends - padded
    first = jnp.sum(jnp.where(e[..., None] == jnp.arange(N_EXPERTS, dtype=jnp.int32), pstarts, 0), axis=-1)
    dest = first + rank
    codes = _invert_slots(dest.reshape(2 * T // INV_ROWS, 1, INV_ROWS))
    src = (codes >> 1) * TT_SUB
    dst = ((codes & 1) * T + (codes >> 1)) * TT_SUB
    block0 = jnp.arange(N_BLOCKS, dtype=jnp.int32) * TM_MOE
    block_e = jnp.clip(jnp.searchsorted(pends, block0, side='right', method='compare_all'),
                       0, N_EXPERTS - 1).astype(jnp.int32)
    nvalid = jnp.clip(pstarts[block_e] + counts[block_e] - block0, 0, TM_MOE).astype(jnp.int32)
    nused = (pends[-1] // TM_MOE).astype(jnp.int32).reshape(1)
    return src, dst, block_e, nused, nvalid


def _combine_body(x_ref, y0_ref, y1_ref, info_ref, g2_ref, nw_ref, *o_refs, final):
    x = _combined(x_ref, y0_ref, y1_ref, info_ref, g2_ref, TM_CMB)
    if final:
        ms = jnp.mean(x * x, axis=-1, keepdims=True)
        x = x * lax.rsqrt(ms + EPS) * nw_ref[...]

    def emit(ref):
        ref[...] = x

    _for_tile(o_refs, TM_CMB, emit)


def _combine(x1, y, info, mods, nw, *, final):
    tm = TM_CMB
    nt = T // tm
    if final:
        out_shape = [jax.ShapeDtypeStruct((T_S, D), F32), jax.ShapeDtypeStruct((T_P, D), F32)]
    else:
        out_shape = [jax.ShapeDtypeStruct((T, D), F32)]
    return pl.pallas_call(
        functools.partial(_combine_body, final=final),
        grid=(nt,),
        in_specs=[pl.BlockSpec((tm, D), lambda i: (i, 0)),
                  pl.BlockSpec((tm * TT_SUB, LANES), lambda i: (i, 0)),
                  pl.BlockSpec((tm * TT_SUB, LANES), lambda i: (i + nt, 0)),
                  pl.BlockSpec((tm, LANES), lambda i: (i, 0)),
                  _mod_spec(5, tm, 1),
                  pl.BlockSpec((1, D), lambda i: (0, 0))],
        out_specs=_row_specs(out_shape, tm, 1),
        out_shape=out_shape,
        compiler_params=_params(("arbitrary",)),
        name="combine_final" if final else "combine",
    )(x1, y, y, info, mods, nw.reshape(1, D))


def kernel(x_prompt, x_sample, c, state_ret_fwd, state_ret_bwd, state_lru_fwd, state_lru_bwd, c_ctx, w_mod, b_mod, norm_mix_w, norm_ffn_w, norm_final_w, w_in_even, w_out_even, gmlp_norm_w, gmlp_w_s, gmlp_b_s, ret_decay_fwd, ret_decay_bwd, ret_gn_w, w_in_odd, w_out_odd, conv_w, conv_b, conv_ln_w, conv_ln_b, lru_conv_w, lru_conv_b, lru_wa_fwd, lru_ba_fwd, lru_wx_fwd, lru_bx_fwd, lru_lam_fwd, lru_wa_bwd, lru_ba_bwd, lru_wx_bwd, lru_bx_bwd, lru_lam_bwd, router_grp_w, router_grp_b, router_exp_w, router_exp_b, moe_w_gate, moe_w_up, moe_w_down):
    xs = (x_sample.reshape(T_S, D), x_prompt.reshape(T_P, D))
    cond = jnp.concatenate([c, c_ctx[None, :], jnp.zeros((N_COND_PAD - B_S - 1, D), F32)], axis=0)
    m = _modulation(cond, w_mod, b_mod)
    mods_all = m.reshape(DEPTH, N_COND_PAD, N_MOD, D).transpose(0, 2, 1, 3)[:, :, :, None, :]

    ret_f = ret_b = lru_f = lru_b = None
    for l in range(DEPTH):
        mods = mods_all[l]
        if l % 2 == 0:
            e = l // 2
            p, = _norm_in(xs, norm_mix_w[l], mods, w_in_even[e].astype(BF16), pack=False)
            b_full = jnp.repeat(gmlp_b_s[e].T.astype(F32), A_GDIM, axis=1)
            out_a = (_gmlp(p, gmlp_norm_w[e], gmlp_w_s[e].astype(BF16), b_full),)
            tabs = _retention_tables(ret_decay_fwd[e], ret_decay_bwd[e])
            cos, sin = _rope_tables()
            ob_s, _, _ = _retention(p, state_ret_fwd[:, e], state_ret_bwd[:, e], tabs, ret_gn_w[e], cos, sin,
                                    base_chunk=0, nb=B_S, nc=N_S // CHUNK, rope=True)
            zero_state = jnp.zeros((B_P, HEADS, HD, HD), F32)
            ob_p, ret_f, ret_b = _retention(p, zero_state, zero_state, tabs, ret_gn_w[e], cos, sin,
                                            base_chunk=T_S // CHUNK, nb=B_P, nc=N_P // CHUNK, rope=False)
            out_b = (ob_s, ob_p)
            w_out = w_out_even[e]
        else:
            o = l // 2
            p, x2 = _norm_in(None, norm_mix_w[l], mods, w_in_odd[o].astype(BF16), pack=True, moe=pending)
            xs = (x2,)
            cw3 = jnp.concatenate([conv_w[o], jnp.zeros((1, W), F32)], axis=0).reshape(
                C_KSIZE + 1, N_STRIPS, LANES).transpose(1, 0, 2)
            conv_args = (cw3, conv_b[o], conv_ln_w[o], conv_ln_b[o])
            prm_f = _lru_params(lru_conv_w[o], lru_conv_b[o], lru_wa_fwd[o], lru_ba_fwd[o], lru_wx_fwd[o],
                                lru_bx_fwd[o], lru_lam_fwd[o])
            prm_b = _lru_params(lru_conv_w[o], lru_conv_b[o], lru_wa_bwd[o], lru_ba_bwd[o], lru_wx_bwd[o],
                                lru_bx_bwd[o], lru_lam_bwd[o])
            latent = dict(seq_row0=0, nseq=B_S, n=N_S)
            context = dict(seq_row0=T_S, nseq=B_P, n=N_P)
            out_a = (_od_conv(p, *conv_args, **latent), _od_conv(p, *conv_args, **context))
            yd_s, _, _ = _od_rglru(p, prm_f, prm_b, state_lru_fwd[:, o], state_lru_bwd[:, o], **latent)
            zero_h = jnp.zeros((B_P, W), F32)
            yd_p, lru_f, lru_b = _od_rglru(p, prm_f, prm_b, zero_h, zero_h, **context)
            out_b = (yd_s, yd_p)
            w_out = w_out_odd[o]

        rw = jnp.concatenate([router_grp_w[l], router_exp_w[l],
                              jnp.zeros((D, LANES - N_GROUPS - N_EXPERTS), F32)], axis=1)
        rw_hi = rw.astype(BF16)
        rw = jnp.concatenate([rw_hi, (rw - rw_hi.astype(F32)).astype(BF16)], axis=1)
        rb = jnp.concatenate([router_grp_b[l], router_exp_b[l],
                              jnp.zeros((LANES - N_GROUPS - N_EXPERTS,), F32)]).reshape(1, LANES)
        x1, hf, info, cnt = _out_proj(out_a, out_b, xs, mods, w_out.astype(BF16), norm_ffn_w[l], rw, rb,
                                      packed=(l % 2 == 1))
        y = _moe(hf, _dispatch_plan(info, cnt), moe_w_gate, moe_w_up, moe_w_down, l)
        if l == DEPTH - 1:
            xs = tuple(_combine(x1, y, info, mods, norm_final_w, final=True))
        else:
            pending = (x1, y, info, mods)

    y_sample = xs[0].reshape(B_S, N_S, D)
    y_prompt = xs[1].reshape(B_P, N_P, D)
    return (y_prompt, y_sample, ret_f[:, None], ret_b[:, None], lru_f[:, None], lru_b[:, None])
```

```python
import functools

import jax
import jax.numpy as jnp
from jax import lax
from jax.experimental import pallas as pl
from jax.experimental.pallas import tpu as pltpu

F32 = jnp.float32
BF16 = jnp.bfloat16
U32 = jnp.uint32

D = 2048
B_P, N_P = 16, 256
B_S, N_S = 8, 4096
T_S = B_S * N_S
T_P = B_P * N_P
T = T_S + T_P
ROWS_PER_COND = 4096
N_COND_PAD = 16
DEPTH = 2
N_MOD = 6
EPS = 1e-6
GRID_W = 64
ROPE_BASE = 10000.0

W = 1024
HW = W // 2
HEADS = 8
HD = 128
CHUNK = 128
A_GROUPS = 4
A_GDIM = W // A_GROUPS
C_KSIZE = 31
D_KSIZE = 4
LRU_C = 8.0

N_GROUPS = 4
EXPERTS_PER_GROUP = 8
N_EXPERTS = 32
D_EXPERT = 512
LANES = 128
N_STRIPS = W // LANES
EXPERT_LANE0 = N_GROUPS

TM_IN = 512
TN_IN = 2048
TM_OUT = 512
TM_MOE = 512
N_SLOTS = 2 * T + N_EXPERTS * TM_MOE
N_BLOCKS = N_SLOTS // TM_MOE
TM_CMB = 512

VMEM_LIMIT = 56 * 1024 * 1024
MOE_VMEM_LIMIT = 60 * 1024 * 1024


def _params(sem, vmem=VMEM_LIMIT):
    return pltpu.CompilerParams(dimension_semantics=sem, vmem_limit_bytes=vmem)


def _cond_row(i, tm):
    return (i * tm) // ROWS_PER_COND


def _sigmoid(x):
    return 0.5 * jnp.tanh(0.5 * x) + 0.5


def _pack_pair(lo, hi, rounded=False):
    if not rounded:
        lo, hi = lo.astype(BF16).astype(F32), hi.astype(BF16).astype(F32)
    return (lax.bitcast_convert_type(lo, U32) >> 16) | lax.bitcast_convert_type(hi, U32)


def _unpack_pair(w, dtype):
    return (lax.bitcast_convert_type(w << 16, F32).astype(dtype),
            lax.bitcast_convert_type(w & U32(0xFFFF0000), F32).astype(dtype))


def _pack_halves(x):
    k = x.shape[1] // 2
    return _pack_pair(x[:, :k], x[:, k:])


def _unpack_halves(w, dtype):
    return jnp.concatenate(_unpack_pair(w, dtype), axis=1)


TT_SUB = 8


def _store_tt(ref, x, rows, rounded=False):
    for c in range(TT_SUB):
        ref[pl.ds(c, rows, stride=TT_SUB), :] = _pack_pair(x[:, c * LANES:(c + 1) * LANES],
                                                           x[:, W + c * LANES:W + (c + 1) * LANES], rounded)


def _load_tt(ref, rows, dtype):
    lo, hi = [], []
    for c in range(TT_SUB):
        a, b = _unpack_pair(ref[pl.ds(c, rows, stride=TT_SUB), :], dtype)
        lo.append(a)
        hi.append(b)
    return jnp.concatenate(lo + hi, axis=1)


def _row_specs(arrs, tm, ngrid):
    if len(arrs) == 1:
        maps = [lambda i: i]
    else:
        n_s = T_S // tm
        maps = [lambda i: jnp.minimum(i, n_s - 1), lambda i: jnp.maximum(i - n_s, 0)]
    width = arrs[0].shape[1]
    if ngrid == 1:
        return [pl.BlockSpec((tm, width), lambda i, f=f: (f(i), 0)) for f in maps]
    return [pl.BlockSpec((tm, width), lambda i, j, f=f: (f(i), 0)) for f in maps]


def _tile_value(refs, tm):
    if len(refs) == 1:
        return refs[0][...]
    return jnp.where(pl.program_id(0) < T_S // tm, refs[0][...], refs[1][...])


def _for_tile(refs, tm, fn):
    if len(refs) == 1:
        fn(refs[0])
        return
    i = pl.program_id(0)

    @pl.when(i < T_S // tm)
    def _():
        fn(refs[0])

    @pl.when(i >= T_S // tm)
    def _():
        fn(refs[1])


def _mod_body(c_ref, w_ref, b_ref, o_ref):
    c = c_ref[...]
    a = (c * _sigmoid(c)).astype(BF16)
    o_ref[...] = jnp.dot(a, w_ref[...].astype(BF16), preferred_element_type=F32) + b_ref[...]


def _modulation(cond, w_mod, b_mod):
    tn = 1024
    return pl.pallas_call(
        _mod_body,
        grid=(DEPTH, N_MOD * D // tn),
        in_specs=[pl.BlockSpec((N_COND_PAD, D), lambda l, j: (0, 0)),
                  pl.BlockSpec((None, D, tn), lambda l, j: (l, 0, j)),
                  pl.BlockSpec((None, 1, tn), lambda l, j: (l, 0, j))],
        out_specs=pl.BlockSpec((None, N_COND_PAD, tn), lambda l, j: (l, 0, j)),
        out_shape=jax.ShapeDtypeStruct((DEPTH, N_COND_PAD, N_MOD * D), F32),
        compiler_params=_params(("arbitrary", "arbitrary")),
        name="modulation",
    )(cond, w_mod, b_mod.reshape(DEPTH, 1, N_MOD * D))


def _mod_spec(k, tm, ngrid):
    if ngrid == 1:
        return pl.BlockSpec((None, None, 1, D), lambda i: (k, _cond_row(i, tm), 0, 0))
    return pl.BlockSpec((None, None, 1, D), lambda i, j: (k, _cond_row(i, tm), 0, 0))


def _norm_in_body(*refs, pack):
    nw_ref, sc_ref, sh_ref, w_ref, p_ref, h_scr = refs[-6:]

    @pl.when(pl.program_id(1) == 0)
    def _():
        x = _tile_value(refs[:-6], TM_IN)
        ms = jnp.mean(x * x, axis=-1, keepdims=True)
        h = x * lax.rsqrt(ms + EPS) * nw_ref[...]
        h = h * (1.0 + sc_ref[...]) + sh_ref[...]
        h_scr[...] = h.astype(BF16)

    r = jnp.dot(h_scr[...], w_ref[...], preferred_element_type=F32)
    if pack:
        p_ref[...] = jnp.concatenate([_pack_halves(r[:, k * W:(k + 1) * W]) for k in range(TN_IN // W)], axis=1)
    else:
        p_ref[...] = r.astype(BF16)


def _norm_in(xs, nw, mods, w_bf16, *, pack):
    n = w_bf16.shape[1]
    if pack:
        out_spec = pl.BlockSpec((TM_IN, TN_IN // 2), lambda i, j: (i, j))
        out_shape = jax.ShapeDtypeStruct((T, n // 2), U32)
    else:
        out_spec = pl.BlockSpec((TM_IN, TN_IN), lambda i, j: (i, j))
        out_shape = jax.ShapeDtypeStruct((T, n), BF16)
    return pl.pallas_call(
        functools.partial(_norm_in_body, pack=pack),
        grid=(T // TM_IN, n // TN_IN),
        in_specs=_row_specs(xs, TM_IN, 2) + [
            pl.BlockSpec((1, D), lambda i, j: (0, 0)),
            _mod_spec(1, TM_IN, 2),
            _mod_spec(0, TM_IN, 2),
            pl.BlockSpec((D, TN_IN), lambda i, j: (0, j))],
        out_specs=out_spec,
        out_shape=out_shape,
        scratch_shapes=[pltpu.VMEM((TM_IN, D), BF16)],
        compiler_params=_params(("arbitrary", "arbitrary")),
        name="norm_in",
    )(*xs, nw.reshape(1, D), mods, mods, w_bf16)


GMLP_ROWS = 512


def _gmlp_body(u_ref, v_ref, lnw_ref, ws_ref, b_ref, o_ref):
    for c in range(GMLP_ROWS // CHUNK):
        rows = pl.ds(c * CHUNK, CHUNK)
        v = jax.nn.gelu(v_ref[rows, :].astype(F32))
        mu = jnp.mean(v, axis=-1, keepdims=True)
        var = jnp.mean(jnp.square(v - mu), axis=-1, keepdims=True)
        vn = ((v - mu) * lax.rsqrt(var + EPS) * lnw_ref[...]).astype(BF16)
        z = jnp.concatenate(
            [jnp.dot(ws_ref[g], vn[:, g * A_GDIM:(g + 1) * A_GDIM], preferred_element_type=F32)
             for g in range(A_GROUPS)], axis=1) + b_ref[...]
        u = jax.nn.gelu(u_ref[rows, :].astype(F32))
        o_ref[rows, :] = (u * z).astype(BF16)


def _gmlp(p, lnw, ws_bf16, b_full):
    return pl.pallas_call(
        _gmlp_body,
        grid=(T // GMLP_ROWS,),
        in_specs=[pl.BlockSpec((GMLP_ROWS, W), lambda i: (i, 0)),
                  pl.BlockSpec((GMLP_ROWS, W), lambda i: (i, 1)),
                  pl.BlockSpec((1, W), lambda i: (0, 0)),
                  pl.BlockSpec((A_GROUPS, CHUNK, CHUNK), lambda i: (0, 0, 0)),
                  pl.BlockSpec((CHUNK, W), lambda i: (0, 0))],
        out_specs=pl.BlockSpec((GMLP_ROWS, W), lambda i: (i, 0)),
        out_shape=jax.ShapeDtypeStruct((T, W), BF16),
        compiler_params=_params(("arbitrary",)),
        name="gmlp",
    )(p, p, lnw.reshape(1, W), ws_bf16, b_full)


RET_SUB = 2
RET_ROWS = RET_SUB * CHUNK


def _roped_qk(q_ref, k_ref, cos_ref, sin_ref, rows, h, rope):
    cols = slice(h * HD, (h + 1) * HD)
    q = q_ref[rows, cols].astype(F32)
    k = k_ref[rows, cols].astype(F32) * (HD ** -0.5)
    if rope:
        c = cos_ref[rows, :]
        s = sin_ref[rows, :]
        q = q * c + pltpu.roll(q, HD // 2, axis=1) * s
        k = k * c + pltpu.roll(k, HD // 2, axis=1) * s
    return q, k


def _kt_v(kd, v):
    return lax.dot_general(kd, v, (((0,), (0,)), ((), ())), preferred_element_type=F32)


def _ret_bwd_body(q_ref, k_ref, v_ref, cos_ref, sin_ref, s0_ref, dq_ref, dk_ref, dc_ref,
                  ob_ref, sfin_ref, s_scr, *, rope, nc):
    c = pl.program_id(1)

    @pl.when(c == 0)
    def _():
        s_scr[...] = s0_ref[...]

    for sub in reversed(range(RET_SUB)):
        rows = slice(sub * CHUNK, (sub + 1) * CHUNK)
        for h in range(HEADS):
            cols = slice(h * HD, (h + 1) * HD)
            q, k = _roped_qk(q_ref, k_ref, cos_ref, sin_ref, rows, h, rope)
            v = v_ref[rows, cols]
            s = s_scr[h]
            ob_ref[rows, cols] = jnp.dot(q.astype(BF16), s.astype(BF16), preferred_element_type=F32) * dq_ref[h]
            kd = (k * dk_ref[h]).astype(BF16)
            s_scr[h] = s * dc_ref[h] + _kt_v(kd, v)

    @pl.when(c == nc - 1)
    def _():
        sfin_ref[...] = s_scr[...]


def _ret_fwd_body(q_ref, k_ref, v_ref, g_ref, cos_ref, sin_ref, ob_ref, s0_ref, m_ref, dq_ref, dk_ref, dc_ref,
                  gn_ref, o_ref, sfin_ref, s_scr, *, rope, nc):
    c = pl.program_id(1)

    @pl.when(c == 0)
    def _():
        s_scr[...] = s0_ref[...]

    for sub in range(RET_SUB):
        rows = slice(sub * CHUNK, (sub + 1) * CHUNK)
        for h in range(HEADS):
            cols = slice(h * HD, (h + 1) * HD)
            q, k = _roped_qk(q_ref, k_ref, cos_ref, sin_ref, rows, h, rope)
            qb = q.astype(BF16)
            v = v_ref[rows, cols]
            s = s_scr[h]
            scores = lax.dot_general(qb, k.astype(BF16), (((1,), (1,)), ((), ())),
                                     preferred_element_type=F32) * m_ref[h]
            o = (jnp.dot(scores.astype(BF16), v, preferred_element_type=F32)
                 + jnp.dot(qb, s.astype(BF16), preferred_element_type=F32) * dq_ref[h]
                 + ob_ref[rows, cols])
            kd = (k * dk_ref[h]).astype(BF16)
            s_scr[h] = s * dc_ref[h] + _kt_v(kd, v)
            mu = jnp.mean(o, axis=-1, keepdims=True)
            var = jnp.mean(jnp.square(o - mu), axis=-1, keepdims=True)
            on = (o - mu) * lax.rsqrt(var + EPS) * gn_ref[:, cols]
            g = g_ref[rows, cols].astype(F32)
            o_ref[rows, cols] = (g * _sigmoid(g) * on).astype(BF16)

    @pl.when(c == nc - 1)
    def _():
        sfin_ref[...] = s_scr[...]


def _retention(p, s_f0, s_b0, tabs, gn_w, cos, sin, *, base_chunk, nb, nc, rope):
    m_tab, dq_f, dk_f, dc_f, dq_b, dk_b, dc_b = tabs
    ns = nc // RET_SUB
    base = base_chunk // RET_SUB
    state_spec = pl.BlockSpec((None, HEADS, HD, HD), lambda b, c: (b, 0, 0, 0))
    tab_spec = pl.BlockSpec((HEADS, HD, HD), lambda b, c: (0, 0, 0))
    dc_spec = pl.BlockSpec((HEADS, 1, HD), lambda b, c: (0, 0, 0))
    state_shape = jax.ShapeDtypeStruct((nb, HEADS, HD, HD), F32)

    def step(c, rev):
        return ns - 1 - c if rev else c

    def col(j, rev):
        return pl.BlockSpec((RET_ROWS, W), lambda b, c: (base + b * ns + step(c, rev), j))

    def rope_spec(rev):
        if not rope:
            return pl.BlockSpec((RET_ROWS, HD), lambda b, c: (0, 0))
        return pl.BlockSpec((RET_ROWS, HD), lambda b, c: (step(c, rev), 0))

    def local(rev):
        return pl.BlockSpec((RET_ROWS, W), lambda b, c: (b * ns + step(c, rev), 0))

    n_rows = nb * nc * CHUNK
    ob, s_b = pl.pallas_call(
        functools.partial(_ret_bwd_body, rope=rope, nc=ns),
        grid=(nb, ns),
        in_specs=[col(2, True), col(3, True), col(4, True), rope_spec(True), rope_spec(True),
                  state_spec, tab_spec, tab_spec, dc_spec],
        out_specs=[local(True), state_spec],
        out_shape=[jax.ShapeDtypeStruct((n_rows, W), F32), state_shape],
        scratch_shapes=[pltpu.VMEM((HEADS, HD, HD), F32)],
        compiler_params=_params(("arbitrary", "arbitrary")),
        name="retention_bwd",
    )(p, p, p, cos, sin, s_b0, dq_b, dk_b, dc_b)

    o, s_f = pl.pallas_call(
        functools.partial(_ret_fwd_body, rope=rope, nc=ns),
        grid=(nb, ns),
        in_specs=[col(2, False), col(3, False), col(4, False), col(5, False), rope_spec(False), rope_spec(False),
                  local(False), state_spec, tab_spec, tab_spec, tab_spec, dc_spec,
                  pl.BlockSpec((1, W), lambda b, c: (0, 0))],
        out_specs=[local(False), state_spec],
        out_shape=[jax.ShapeDtypeStruct((n_rows, W), BF16), state_shape],
        scratch_shapes=[pltpu.VMEM((HEADS, HD, HD), F32)],
        compiler_params=_params(("arbitrary", "arbitrary")),
        name="retention_fwd",
    )(p, p, p, p, cos, sin, ob, s_f0, m_tab, dq_f, dk_f, dc_f, gn_w.reshape(1, W))
    return o, s_f, s_b


def _retention_tables(decay_fwd, decay_bwd):
    lg_f = jax.nn.log_sigmoid(decay_fwd.astype(F32))[:, None, None]
    lg_b = jax.nn.log_sigmoid(decay_bwd.astype(F32))[:, None, None]
    pos = jnp.arange(CHUNK, dtype=F32)
    rel = pos[:, None] - pos[None, :]
    m_tab = (jnp.where(rel >= 0, jnp.exp(lg_f * jnp.maximum(rel, 0.0)), 0.0)
             + jnp.where(rel <= 0, jnp.exp(lg_b * jnp.maximum(-rel, 0.0)), 0.0))
    ones = jnp.ones((1, 1, HD), F32)
    col = pos[None, :, None]
    dq_f = jnp.exp(lg_f * (col + 1.0)) * ones
    dk_f = jnp.exp(lg_f * (CHUNK - 1.0 - col)) * ones
    dq_b = jnp.exp(lg_b * (CHUNK - col)) * ones
    dk_b = jnp.exp(lg_b * col) * ones
    dc_f = jnp.exp(lg_f * CHUNK) * ones
    dc_b = jnp.exp(lg_b * CHUNK) * ones
    return m_tab, dq_f, dk_f, dc_f, dq_b, dk_b, dc_b


def _rope_tables():
    rows = N_S // GRID_W
    row = jnp.repeat(jnp.arange(rows, dtype=F32), GRID_W)
    colp = jnp.tile(jnp.arange(GRID_W, dtype=F32), rows)
    quarter = HD // 4
    inv_freq = ROPE_BASE ** (-jnp.arange(quarter, dtype=F32) / quarter)
    ang = jnp.concatenate([row[:, None] * inv_freq, colp[:, None] * inv_freq], axis=-1)
    cos, sin = jnp.cos(ang), jnp.sin(ang)
    return jnp.concatenate([cos, cos], axis=-1), jnp.concatenate([-sin, sin], axis=-1)


OD_TT = 128
OD_G = 8
CONV_HALO = 16
LRU_HALO = 8


def _od_rows(seq_row0, n, g, tj):
    return [seq_row0 + (g * OD_G + b) * n + tj * OD_TT for b in range(OD_G)]


def _od_in_copies(p_hbm, buf, sem, *, col, halo, seq_row0, n, g, tj):
    nt = n // OD_TT
    out = []
    for b, base in enumerate(_od_rows(seq_row0, n, g, tj)):
        def src(r, k):
            return p_hbm.at[pl.ds(r, k), pl.ds(col * HW, HW)]

        out.append((None, pltpu.make_async_copy(src(base, OD_TT), buf.at[pl.ds(halo, OD_TT), b, :], sem)))
        if halo:
            out.append((tj > 0, pltpu.make_async_copy(src(base - halo, halo), buf.at[pl.ds(0, halo), b, :], sem)))
            out.append((tj < nt - 1, pltpu.make_async_copy(src(base + OD_TT, halo),
                                                           buf.at[pl.ds(halo + OD_TT, halo), b, :], sem)))
    return out


def _od_run(copies, op):
    for cond, cp in copies:
        fn = cp.start if op == "start" else cp.wait
        if cond is None:
            fn()
        else:
            pl.when(cond)(fn)


def _od_zero_halo(buf, halo, n, tj):
    nt = n // OD_TT
    zeros = jnp.zeros((halo, OD_G, HW), U32)

    @pl.when(tj == 0)
    def _():
        buf[0:halo] = zeros

    @pl.when(tj == nt - 1)
    def _():
        buf[halo + OD_TT:2 * halo + OD_TT] = zeros


def _od_out_copies(obuf, out_hbm, sem, *, seq_row0, n, g, tj):
    return [(None, pltpu.make_async_copy(obuf.at[:, b, :], out_hbm.at[pl.ds(base - seq_row0, OD_TT), :], sem))
            for b, base in enumerate(_od_rows(seq_row0, n, g, tj))]


def _od_unpack(buf, rows):
    return _unpack_halves(buf[...].reshape(rows, HW), F32)


def _od_pipeline(q, nq, in_copies, sem_slots=2):
    s = q % 2

    @pl.when(q == 0)
    def _():
        _od_run(in_copies(q, s), "start")

    _od_run(in_copies(q, s), "wait")

    @pl.when(q + 1 < nq)
    def _():
        _od_run(in_copies(q + 1, 1 - s), "start")

    return s


def _od_emit(q, nq, s, obuf, words, out_copies):
    @pl.when(q >= 2)
    def _():
        _od_run(out_copies(s), "wait")

    obuf[s] = words.reshape(OD_TT, OD_G, HW)
    _od_run(out_copies(s), "start")

    @pl.when(q == nq - 1)
    def _():
        _od_run(out_copies(s), "wait")

        @pl.when(q >= 1)
        def _():
            _od_run(out_copies(1 - s), "wait")


CONV_RB = 64


def _od_conv_body(p_hbm, cw_ref, cb_ref, lnw_ref, lnb_ref, yc_hbm, xin, glu, ybuf, obuf, sem_in, sem_out,
                  *, seq_row0, n, ng):
    nt = n // OD_TT
    nq = ng * nt
    q = pl.program_id(0)
    halo = CONV_HALO
    rows_in = (OD_TT + 2 * halo) * OD_G
    rows = OD_TT * OD_G

    def in_copies(step, slot):
        out = []
        for br in range(2):
            out += _od_in_copies(p_hbm, xin.at[br, slot], sem_in.at[slot], col=br, halo=halo,
                                 seq_row0=seq_row0, n=n, g=step // nt, tj=step % nt)
        return out

    s = _od_pipeline(q, nq, in_copies)
    g, tj = q // nt, q % nt
    for br in range(2):
        _od_zero_halo(xin.at[br, s], halo, n, tj)
    glu[...] = _od_unpack(xin.at[0, s], rows_in) * _sigmoid(_od_unpack(xin.at[1, s], rows_in))

    def strip(si, carry):
        lanes = pl.ds(pl.multiple_of(si * LANES, LANES), LANES)
        for rb in range(rows // CONV_RB):
            acc = jnp.zeros((CONV_RB // 8, 8, LANES), F32)
            for k in range(C_KSIZE):
                wk = cw_ref[si, pl.ds(k, 1), :]
                r0 = rb * CONV_RB + (halo - C_KSIZE // 2 + k) * OD_G
                acc = acc + glu[r0:r0 + CONV_RB, lanes].reshape(CONV_RB // 8, 8, LANES) * wk[None]
            ybuf[rb * CONV_RB:(rb + 1) * CONV_RB, lanes] = acc.reshape(CONV_RB, LANES)
        return carry

    lax.fori_loop(0, N_STRIPS, strip, 0)
    y = ybuf[...] + cb_ref[...]
    mu = jnp.mean(y, axis=-1, keepdims=True)
    var = jnp.mean(jnp.square(y - mu), axis=-1, keepdims=True)
    yn = (y - mu) * lax.rsqrt(var + EPS) * lnw_ref[...] + lnb_ref[...]
    yo = yn * _sigmoid(yn)

    def out_copies(slot):
        return _od_out_copies(obuf.at[slot], yc_hbm, sem_out.at[slot], seq_row0=seq_row0, n=n, g=g, tj=tj)

    _od_emit(q, nq, s, obuf, _pack_halves(yo), out_copies)


def _od_conv(p, cw3, cb, lnw, lnb, *, seq_row0, nseq, n):
    ng = nseq // OD_G
    nq = ng * (n // OD_TT)
    vec = pl.BlockSpec((1, W), lambda q: (0, 0))
    t_in = OD_TT + 2 * CONV_HALO
    return pl.pallas_call(
        functools.partial(_od_conv_body, seq_row0=seq_row0, n=n, ng=ng),
        grid=(nq,),
        in_specs=[pl.BlockSpec(memory_space=pl.ANY),
                  pl.BlockSpec((N_STRIPS, C_KSIZE + 1, LANES), lambda q: (0, 0, 0)), vec, vec, vec],
        out_specs=pl.BlockSpec(memory_space=pl.ANY),
        out_shape=jax.ShapeDtypeStruct((nseq * n, HW), U32),
        scratch_shapes=[pltpu.VMEM((2, 2, t_in, OD_G, HW), U32),
                        pltpu.VMEM((t_in * OD_G, W), F32),
                        pltpu.VMEM((OD_TT * OD_G, W), F32),
                        pltpu.VMEM((2, OD_TT, OD_G, HW), U32),
                        pltpu.SemaphoreType.DMA((2,)), pltpu.SemaphoreType.DMA((2,))],
        compiler_params=_params(("arbitrary",)),
        name="conv_module",
    )(p, cw3, cb.reshape(1, W), lnw.reshape(1, W), lnb.reshape(1, W))


def _od_gates(x, cw_ref, cb_ref, wa_ref, ba_ref, wx_ref, bx_ref, sp_ref, a_scr, b_scr):
    rows = OD_TT * OD_G
    xc = cb_ref[...]
    for k in range(D_KSIZE):
        r0 = (LRU_HALO - D_KSIZE // 2 + k) * OD_G
        xc = xc + cw_ref[k:k + 1, :] * x[r0:r0 + rows, :]
    xb = xc.astype(BF16)
    r = jnp.concatenate([jnp.dot(xb[:, h * HD:(h + 1) * HD], wa_ref[h], preferred_element_type=F32)
                         for h in range(HEADS)], axis=1) + ba_ref[...]
    g = jnp.concatenate([jnp.dot(xb[:, h * HD:(h + 1) * HD], wx_ref[h], preferred_element_type=F32)
                         for h in range(HEADS)], axis=1) + bx_ref[...]
    log_a = (-LRU_C) * _sigmoid(r) * sp_ref[...]
    th = jnp.tanh(log_a)
    one_minus_a2 = -2.0 * th / (1.0 - th)
    a_scr[...] = jnp.exp(log_a)
    b_scr[...] = jnp.sqrt(jnp.maximum(one_minus_a2, 0.0)) * (_sigmoid(g) * xc)


def _od_scan(a_scr, b_scr, out_ref, h, reverse):
    def step(t, h):
        tr = (OD_TT - 1 - t) if reverse else t
        rows = pl.ds(pl.multiple_of(tr * OD_G, OD_G), OD_G)
        h = a_scr[rows, :] * h + b_scr[rows, :]
        out_ref[rows, :] = h
        return h

    return lax.fori_loop(0, OD_TT, step, h, unroll=8)


def _od_lru_fwd_body(p_hbm, cw_ref, cb_ref, wa_ref, ba_ref, wx_ref, bx_ref, sp_ref, h0_ref,
                     hf_ref, hfin_ref, xin, a_scr, b_scr, h_scr, sem_in, *, seq_row0, n, ng):
    nt = n // OD_TT
    nq = ng * nt
    q = pl.program_id(0)

    def in_copies(step, slot):
        return _od_in_copies(p_hbm, xin.at[slot], sem_in.at[slot], col=2, halo=LRU_HALO,
                             seq_row0=seq_row0, n=n, g=step // nt, tj=step % nt)

    s = _od_pipeline(q, nq, in_copies)
    tj = q % nt
    _od_zero_halo(xin.at[s], LRU_HALO, n, tj)

    @pl.when(tj == 0)
    def _():
        h_scr[...] = h0_ref[...]

    x = _od_unpack(xin.at[s], (OD_TT + 2 * LRU_HALO) * OD_G)
    _od_gates(x, cw_ref, cb_ref, wa_ref, ba_ref, wx_ref, bx_ref, sp_ref, a_scr, b_scr)
    h = _od_scan(a_scr, b_scr, hf_ref, h_scr[...], False)
    h_scr[...] = h

    @pl.when(tj == nt - 1)
    def _():
        hfin_ref[...] = h


def _od_lru_bwd_body(p_hbm, hf_ref, cw_ref, cb_ref, wa_ref, ba_ref, wx_ref, bx_ref, sp_ref, h0_ref,
                     yd_hbm, hfin_ref, xin, gin, a_scr, b_scr, hb_scr, obuf, h_scr, sem_in, sem_out,
                     *, seq_row0, n, ng):
    nt = n // OD_TT
    nq = ng * nt
    q = pl.program_id(0)

    def in_copies(step, slot):
        where = dict(seq_row0=seq_row0, n=n, g=step // nt, tj=nt - 1 - step % nt)
        return (_od_in_copies(p_hbm, xin.at[slot], sem_in.at[slot], col=2, halo=LRU_HALO, **where)
                + _od_in_copies(p_hbm, gin.at[slot], sem_in.at[slot], col=3, halo=0, **where))

    s = _od_pipeline(q, nq, in_copies)
    g, tj = q // nt, nt - 1 - q % nt
    _od_zero_halo(xin.at[s], LRU_HALO, n, tj)

    @pl.when(tj == nt - 1)
    def _():
        h_scr[...] = h0_ref[...]

    x = _od_unpack(xin.at[s], (OD_TT + 2 * LRU_HALO) * OD_G)
    _od_gates(x, cw_ref, cb_ref, wa_ref, ba_ref, wx_ref, bx_ref, sp_ref, a_scr, b_scr)
    h = _od_scan(a_scr, b_scr, hb_scr, h_scr[...], True)
    h_scr[...] = h

    @pl.when(tj == 0)
    def _():
        hfin_ref[...] = h

    gd = _od_unpack(gin.at[s], OD_TT * OD_G)
    yd = (hf_ref[...] + hb_scr[...]) * jax.nn.gelu(gd)

    def out_copies(slot):
        return _od_out_copies(obuf.at[slot], yd_hbm, sem_out.at[slot], seq_row0=seq_row0, n=n, g=g, tj=tj)

    _od_emit(q, nq, s, obuf, _pack_halves(yd), out_copies)


def _od_rglru(p, prm_f, prm_b, h0_f, h0_b, *, seq_row0, nseq, n):
    ng = nseq // OD_G
    nt = n // OD_TT
    nq = ng * nt
    rows = OD_TT * OD_G
    vec = pl.BlockSpec((1, W), lambda q: (0, 0))
    wsp = pl.BlockSpec((HEADS, HD, HD), lambda q: (0, 0, 0))
    prm_specs = [pl.BlockSpec((8, W), lambda q: (0, 0)), vec, wsp, vec, wsp, vec, vec,
                 pl.BlockSpec((OD_G, W), lambda q: (q // nt, 0))]
    state_spec = pl.BlockSpec((OD_G, W), lambda q: (q // nt, 0))
    state_shape = jax.ShapeDtypeStruct((nseq, W), F32)
    t_in = OD_TT + 2 * LRU_HALO
    tile_f32 = pltpu.VMEM((rows, W), F32)

    hf, s_f = pl.pallas_call(
        functools.partial(_od_lru_fwd_body, seq_row0=seq_row0, n=n, ng=ng),
        grid=(nq,),
        in_specs=[pl.BlockSpec(memory_space=pl.ANY)] + prm_specs,
        out_specs=[pl.BlockSpec((rows, W), lambda q: (q, 0)), state_spec],
        out_shape=[jax.ShapeDtypeStruct((nq * rows, W), F32), state_shape],
        scratch_shapes=[pltpu.VMEM((2, t_in, OD_G, HW), U32), tile_f32, tile_f32, pltpu.VMEM((OD_G, W), F32),
                        pltpu.SemaphoreType.DMA((2,))],
        compiler_params=_params(("arbitrary",)),
        name="rglru_fwd",
    )(p, *prm_f, h0_f)

    yd, s_b = pl.pallas_call(
        functools.partial(_od_lru_bwd_body, seq_row0=seq_row0, n=n, ng=ng),
        grid=(nq,),
        in_specs=[pl.BlockSpec(memory_space=pl.ANY),
                  pl.BlockSpec((rows, W), lambda q: ((q // nt) * nt + nt - 1 - q % nt, 0))] + prm_specs,
        out_specs=[pl.BlockSpec(memory_space=pl.ANY), state_spec],
        out_shape=[jax.ShapeDtypeStruct((nseq * n, HW), U32), state_shape],
        scratch_shapes=[pltpu.VMEM((2, t_in, OD_G, HW), U32), pltpu.VMEM((2, OD_TT, OD_G, HW), U32),
                        tile_f32, tile_f32, tile_f32, pltpu.VMEM((2, OD_TT, OD_G, HW), U32),
                        pltpu.VMEM((OD_G, W), F32),
                        pltpu.SemaphoreType.DMA((2,)), pltpu.SemaphoreType.DMA((2,))],
        compiler_params=_params(("arbitrary",)),
        name="rglru_bwd",
    )(p, hf, *prm_b, h0_b)
    return yd, s_f, s_b


def _lru_params(cw, cb, wa, ba, wx, bx, lam):
    cw_pad = jnp.concatenate([cw, jnp.zeros((8 - D_KSIZE, W), F32)], axis=0)
    sp = jax.nn.softplus(-lam.astype(F32)).reshape(1, W)
    return (cw_pad, cb.reshape(1, W), wa.astype(BF16), ba.reshape(1, W), wx.astype(BF16), bx.reshape(1, W), sp)


def _out_body(*refs, n_a, n_b, n_x, packed):
    a_refs = refs[:n_a]
    b_refs = refs[n_a:n_a + n_b]
    x_refs = refs[n_a + n_b:n_a + n_b + n_x]
    (g1_ref, wo_ref, nw_ref, sc_ref, sh_ref, rw_ref, rb_ref,
     x1_ref, hf_ref, info_ref, cnt_ref, carry) = refs[n_a + n_b + n_x:]
    i = pl.program_id(0)

    @pl.when(i == 0)
    def _():
        carry[...] = jnp.zeros_like(carry)

    def branch(refs_, row0):
        v = _tile_value(refs_, TM_OUT)
        if packed:
            lo, hi = _unpack_pair(v, BF16)
            return (jnp.dot(lo, wo_ref[row0:row0 + HW, :], preferred_element_type=F32)
                    + jnp.dot(hi, wo_ref[row0 + HW:row0 + W, :], preferred_element_type=F32))
        return jnp.dot(v, wo_ref[row0:row0 + W, :], preferred_element_type=F32)

    x1 = _tile_value(x_refs, TM_OUT) + g1_ref[...] * (branch(a_refs, 0) + branch(b_refs, W))
    x1_ref[...] = x1
    ms = jnp.mean(x1 * x1, axis=-1, keepdims=True)
    hf = x1 * lax.rsqrt(ms + EPS) * nw_ref[...]
    hf = hf * (1.0 + sc_ref[...]) + sh_ref[...]
    hf_hi = hf.astype(BF16)
    hf_hi32 = hf_hi.astype(F32)
    _store_tt(hf_ref, hf_hi32, TM_OUT, rounded=True)

    hf_lo = (hf - hf_hi32).astype(BF16)
    l2 = jnp.dot(hf_hi, rw_ref[...], preferred_element_type=F32)
    logits = (l2[:, :LANES] + l2[:, LANES:]
              + jnp.dot(hf_lo, rw_ref[:, :LANES], preferred_element_type=F32) + rb_ref[...])
    lane = lax.broadcasted_iota(jnp.int32, (TM_OUT, LANES), 1).astype(F32)
    neg = -jnp.inf
    gmask = lane < N_GROUPS
    mg = jnp.max(jnp.where(gmask, logits, neg), axis=-1, keepdims=True)
    gstar = jnp.min(jnp.where(gmask & (logits == mg), lane, float(LANES)), axis=-1, keepdims=True)
    denom = jnp.sum(jnp.where(gmask, jnp.exp(logits - mg), 0.0), axis=-1, keepdims=True)
    psel = 1.0 / denom
    lo = EXPERT_LANE0 + EXPERTS_PER_GROUP * gstar
    emask = (lane >= lo) & (lane < lo + EXPERTS_PER_GROUP)
    v1 = jnp.max(jnp.where(emask, logits, neg), axis=-1, keepdims=True)
    i1 = jnp.min(jnp.where(emask & (logits == v1), lane, float(LANES)), axis=-1, keepdims=True)
    em2 = emask & (lane != i1)
    v2 = jnp.max(jnp.where(em2, logits, neg), axis=-1, keepdims=True)
    i2 = jnp.min(jnp.where(em2 & (logits == v2), lane, float(LANES)), axis=-1, keepdims=True)
    e21 = jnp.exp(v2 - v1)
    w1 = psel / (1.0 + e21)
    w2 = psel * e21 / (1.0 + e21)

    memb = jnp.where((lane == i1) | (lane == i2), 1.0, 0.0)
    r_i = lax.broadcasted_iota(jnp.int32, (TM_OUT, TM_OUT), 0)
    c_i = lax.broadcasted_iota(jnp.int32, (TM_OUT, TM_OUT), 1)
    lower = jnp.where(r_i > c_i, 1.0, 0.0).astype(BF16)
    before = jnp.dot(lower, memb.astype(BF16), preferred_element_type=F32) + carry[0:1, :]
    rank1 = jnp.sum(jnp.where(lane == i1, before, 0.0), axis=-1, keepdims=True)
    rank2 = jnp.sum(jnp.where(lane == i2, before, 0.0), axis=-1, keepdims=True)
    new_carry = carry[...] + jnp.sum(memb, axis=0, keepdims=True)
    carry[...] = new_carry
    cnt_ref[...] = new_carry

    e1 = i1 - EXPERT_LANE0
    e2 = i2 - EXPERT_LANE0
    info = jnp.where(lane == 0, e1, jnp.where(lane == 1, e2, jnp.where(lane == 2, w1, jnp.where(
        lane == 3, w2, jnp.where(lane == 4, rank1, jnp.where(lane == 5, rank2, 0.0))))))
    info_ref[...] = info


def _out_proj(a_parts, b_parts, xs, mods, wo_bf16, nw, rw, rb, *, packed):
    tm = TM_OUT
    row = lambda n: pl.BlockSpec((tm, n), lambda i: (i, 0))
    vec = pl.BlockSpec((1, D), lambda i: (0, 0))
    return pl.pallas_call(
        functools.partial(_out_body, n_a=len(a_parts), n_b=len(b_parts), n_x=len(xs), packed=packed),
        grid=(T // tm,),
        in_specs=_row_specs(a_parts, tm, 1) + _row_specs(b_parts, tm, 1) + _row_specs(xs, tm, 1) + [
            _mod_spec(2, tm, 1),
            pl.BlockSpec((D, D), lambda i: (0, 0)), vec, _mod_spec(4, tm, 1), _mod_spec(3, tm, 1),
            pl.BlockSpec((D, 2 * LANES), lambda i: (0, 0)), pl.BlockSpec((1, LANES), lambda i: (0, 0))],
        out_specs=[row(D), pl.BlockSpec((tm * TT_SUB, LANES), lambda i: (i, 0)), row(LANES),
                   pl.BlockSpec((8, LANES), lambda i: (0, 0))],
        out_shape=[jax.ShapeDtypeStruct((T, D), F32), jax.ShapeDtypeStruct((T * TT_SUB, LANES), U32),
                   jax.ShapeDtypeStruct((T, LANES), F32), jax.ShapeDtypeStruct((8, LANES), F32)],
        scratch_shapes=[pltpu.VMEM((8, LANES), F32)],
        compiler_params=_params(("arbitrary",)),
        name="out_proj_router",
    )(*a_parts, *b_parts, *xs, mods, wo_bf16, nw.reshape(1, D), mods, mods, rw, rb)


COPY_UNROLL = 32


def _moe_body(be_ref, nused_ref, nv_ref, src_cur, src_nxt, dst_cur, hf_hbm, wg_ref, wu_ref, wd_ref, y_hbm,
              xbuf, obuf, wgc, wuc, wdc, sem_in, sem_out):
    i = pl.program_id(0)
    nused = nused_ref[0]
    slot = i % 2

    def tile(ref, row8):
        return ref.at[pl.ds(pl.multiple_of(row8, TT_SUB), TT_SUB)]

    def rows(ref, n):
        return ref.at[pl.ds(0, n * TT_SUB)]

    def for_rows(n, fn):
        groups = n // COPY_UNROLL

        def group(c, carry):
            for u in range(COPY_UNROLL):
                fn(c * COPY_UNROLL + u)
            return carry

        def single(r, carry):
            fn(r)
            return carry

        lax.fori_loop(0, groups, group, 0)
        lax.fori_loop(groups * COPY_UNROLL, n, single, 0)

    def gather(src_ref, s, n):
        def row(r):
            pltpu.make_async_copy(tile(hf_hbm, src_ref[0, r]), tile(xbuf.at[s], r * TT_SUB), sem_in.at[s]).start()

        for_rows(n, row)

    def wait_scatter(n):
        pltpu.make_async_copy(rows(obuf, n), rows(y_hbm, n), sem_out).wait()

    @pl.when(i < nused)
    def _():
        nv = nv_ref[i]

        @pl.when(i == 0)
        def _():
            xbuf[...] = jnp.zeros(xbuf.shape, U32)
            gather(src_cur, 0, nv)

        pltpu.make_async_copy(rows(hf_hbm, nv), rows(xbuf.at[slot], nv), sem_in.at[slot]).wait()

        @pl.when(i + 1 < nused)
        def _():
            gather(src_nxt, 1 - slot, nv_ref[i + 1])

        @pl.when((i == 0) | (be_ref[i] != be_ref[jnp.maximum(i - 1, 0)]))
        def _():
            wgc[...] = wg_ref[...].astype(BF16)
            wuc[...] = wu_ref[...].astype(BF16)
            wdc[...] = wd_ref[...].astype(BF16)

        x = _load_tt(xbuf.at[slot], TM_MOE, BF16)
        gate = jnp.dot(x, wgc[...], preferred_element_type=F32)
        up = jnp.dot(x, wuc[...], preferred_element_type=F32)
        hid = (gate * _sigmoid(gate) * up).astype(BF16)

        @pl.when(i > 0)
        def _():
            wait_scatter(nv_ref[i - 1])

        _store_tt(obuf, jnp.dot(hid, wdc[...], preferred_element_type=F32), TM_MOE)

        def scatter_row(r):
            pltpu.make_async_copy(tile(obuf, r * TT_SUB), tile(y_hbm, dst_cur[0, r]), sem_out).start()

        for_rows(nv, scatter_row)

        @pl.when(i == nused - 1)
        def _():
            wait_scatter(nv)


def _moe(hf, plan, wg, wu, wd, layer):
    src, dst, block_e, nused, nvalid = plan
    wspec = lambda shape: pl.BlockSpec((None, None) + shape, lambda i, be, nu, nv: (layer, be[i], 0, 0))
    rows = lambda f: pl.BlockSpec((None, 1, TM_MOE), lambda i, be, nu, nv: (f(i), 0, 0), memory_space=pltpu.SMEM)
    return pl.pallas_call(
        _moe_body,
        grid_spec=pltpu.PrefetchScalarGridSpec(
            num_scalar_prefetch=3,
            grid=(N_BLOCKS,),
            in_specs=[rows(lambda i: i), rows(lambda i: jnp.minimum(i + 1, N_BLOCKS - 1)), rows(lambda i: i),
                      pl.BlockSpec(memory_space=pl.ANY),
                      wspec((D, D_EXPERT)), wspec((D, D_EXPERT)), wspec((D_EXPERT, D))],
            out_specs=pl.BlockSpec(memory_space=pl.ANY),
            scratch_shapes=[pltpu.VMEM((2, TM_MOE * TT_SUB, LANES), U32),
                            pltpu.VMEM((TM_MOE * TT_SUB, LANES), U32),
                            pltpu.VMEM((D, D_EXPERT), BF16), pltpu.VMEM((D, D_EXPERT), BF16),
                            pltpu.VMEM((D_EXPERT, D), BF16),
                            pltpu.SemaphoreType.DMA((2,)), pltpu.SemaphoreType.DMA]),
        out_shape=jax.ShapeDtypeStruct((2 * T * TT_SUB, LANES), U32),
        compiler_params=_params(("arbitrary",), MOE_VMEM_LIMIT),
        name="moe_experts",
    )(block_e, nused, nvalid, src.reshape(N_BLOCKS, 1, TM_MOE), src.reshape(N_BLOCKS, 1, TM_MOE),
      dst.reshape(N_BLOCKS, 1, TM_MOE), hf, wg, wu, wd)


INV_ROWS = 2048


def _invert_body(dest_ref, zeros_hbm, codes_ref, sem):
    i = pl.program_id(0)

    @pl.when(i == 0)
    def _():
        clear = pltpu.make_async_copy(zeros_hbm, codes_ref, sem)
        clear.start()
        clear.wait()

    def put(j, carry):
        codes_ref[dest_ref[0, j]] = i * INV_ROWS + j
        return carry

    lax.fori_loop(0, INV_ROWS, put, 0, unroll=16)


def _invert_slots(dest3):
    return pl.pallas_call(
        _invert_body,
        grid=(dest3.shape[0],),
        in_specs=[pl.BlockSpec((None, 1, INV_ROWS), lambda i: (i, 0, 0), memory_space=pltpu.SMEM),
                  pl.BlockSpec(memory_space=pl.ANY)],
        out_specs=pl.BlockSpec(memory_space=pltpu.SMEM),
        out_shape=jax.ShapeDtypeStruct((N_SLOTS,), jnp.int32),
        scratch_shapes=[pltpu.SemaphoreType.DMA],
        compiler_params=_params(("arbitrary",)),
        name="invert_slots",
    )(dest3, jnp.zeros((N_SLOTS,), jnp.int32))


def _dispatch_plan(info, cnt):
    e = info[:, 0:2].astype(jnp.int32)
    rank = info[:, 4:6].astype(jnp.int32)
    counts = cnt[0, EXPERT_LANE0:EXPERT_LANE0 + N_EXPERTS].astype(jnp.int32)
    padded = ((counts + TM_MOE - 1) // TM_MOE) * TM_MOE
    pends = jnp.cumsum(padded)
    pstarts = pends - padded
    first = jnp.sum(jnp.where(e[..., None] == jnp.arange(N_EXPERTS, dtype=jnp.int32), pstarts, 0), axis=-1)
    dest = first + rank
    codes = _invert_slots(dest.reshape(2 * T // INV_ROWS, 1, INV_ROWS))
    src = (codes >> 1) * TT_SUB
    dst = ((codes & 1) * T + (codes >> 1)) * TT_SUB
    block0 = jnp.arange(N_BLOCKS, dtype=jnp.int32) * TM_MOE
    block_e = jnp.clip(jnp.searchsorted(pends, block0, side='right', method='compare_all'),
                       0, N_EXPERTS - 1).astype(jnp.int32)
    nvalid = jnp.clip(pstarts[block_e] + counts[block_e] - block0, 0, TM_MOE).astype(jnp.int32)
    nused = (pends[-1] // TM_MOE).astype(jnp.int32).reshape(1)
    return src, dst, block_e, nused, nvalid


def _combined(x_ref, y0_ref, y1_ref, info_ref, g2_ref, rows):
    info = info_ref[...]
    lane = lax.broadcasted_iota(jnp.int32, info.shape, 1)
    w1 = jnp.sum(jnp.where(lane == 2, info, 0.0), axis=-1, keepdims=True)
    w2 = jnp.sum(jnp.where(lane == 3, info, 0.0), axis=-1, keepdims=True)
    y0 = _load_tt(y0_ref, rows, F32)
    y1 = _load_tt(y1_ref, rows, F32)
    return x_ref[...] + g2_ref[...] * (y0 * w1 + y1 * w2)


def _combine_body(x_ref, y0_ref, y1_ref, info_ref, g2_ref, nw_ref, *o_refs, final):
    x = _combined(x_ref, y0_ref, y1_ref, info_ref, g2_ref, TM_CMB)
    if final:
        ms = jnp.mean(x * x, axis=-1, keepdims=True)
        x = x * lax.rsqrt(ms + EPS) * nw_ref[...]

    def emit(ref):
        ref[...] = x

    _for_tile(o_refs, TM_CMB, emit)


def _combine(x1, y, info, mods, nw, *, final):
    tm = TM_CMB
    nt = T // tm
    if final:
        out_shape = [jax.ShapeDtypeStruct((T_S, D), F32), jax.ShapeDtypeStruct((T_P, D), F32)]
    else:
        out_shape = [jax.ShapeDtypeStruct((T, D), F32)]
    return pl.pallas_call(
        functools.partial(_combine_body, final=final),
        grid=(nt,),
        in_specs=[pl.BlockSpec((tm, D), lambda i: (i, 0)),
                  pl.BlockSpec((tm * TT_SUB, LANES), lambda i: (i, 0)),
                  pl.BlockSpec((tm * TT_SUB, LANES), lambda i: (i + nt, 0)),
                  pl.BlockSpec((tm, LANES), lambda i: (i, 0)),
                  _mod_spec(5, tm, 1),
                  pl.BlockSpec((1, D), lambda i: (0, 0))],
        out_specs=_row_specs(out_shape, tm, 1),
        out_shape=out_shape,
        compiler_params=_params(("arbitrary",)),
        name="combine_final" if final else "combine",
    )(x1, y, y, info, mods, nw.reshape(1, D))


def kernel(x_prompt, x_sample, c, state_ret_fwd, state_ret_bwd, state_lru_fwd, state_lru_bwd, c_ctx, w_mod, b_mod, norm_mix_w, norm_ffn_w, norm_final_w, w_in_even, w_out_even, gmlp_norm_w, gmlp_w_s, gmlp_b_s, ret_decay_fwd, ret_decay_bwd, ret_gn_w, w_in_odd, w_out_odd, conv_w, conv_b, conv_ln_w, conv_ln_b, lru_conv_w, lru_conv_b, lru_wa_fwd, lru_ba_fwd, lru_wx_fwd, lru_bx_fwd, lru_lam_fwd, lru_wa_bwd, lru_ba_bwd, lru_wx_bwd, lru_bx_bwd, lru_lam_bwd, router_grp_w, router_grp_b, router_exp_w, router_exp_b, moe_w_gate, moe_w_up, moe_w_down):
    xs = (x_sample.reshape(T_S, D), x_prompt.reshape(T_P, D))
    cond = jnp.concatenate([c, c_ctx[None, :], jnp.zeros((N_COND_PAD - B_S - 1, D), F32)], axis=0)
    m = _modulation(cond, w_mod, b_mod)
    mods_all = m.reshape(DEPTH, N_COND_PAD, N_MOD, D).transpose(0, 2, 1, 3)[:, :, :, None, :]

    ret_f = ret_b = lru_f = lru_b = None
    for l in range(DEPTH):
        mods = mods_all[l]
        if l % 2 == 0:
            e = l // 2
            p = _norm_in(xs, norm_mix_w[l], mods, w_in_even[e].astype(BF16), pack=False)
            b_full = jnp.repeat(gmlp_b_s[e].T.astype(F32), A_GDIM, axis=1)
            out_a = (_gmlp(p, gmlp_norm_w[e], gmlp_w_s[e].astype(BF16), b_full),)
            tabs = _retention_tables(ret_decay_fwd[e], ret_decay_bwd[e])
            cos, sin = _rope_tables()
            ob_s, _, _ = _retention(p, state_ret_fwd[:, e], state_ret_bwd[:, e], tabs, ret_gn_w[e], cos, sin,
                                    base_chunk=0, nb=B_S, nc=N_S // CHUNK, rope=True)
            zero_state = jnp.zeros((B_P, HEADS, HD, HD), F32)
            ob_p, ret_f, ret_b = _retention(p, zero_state, zero_state, tabs, ret_gn_w[e], cos, sin,
                                            base_chunk=T_S // CHUNK, nb=B_P, nc=N_P // CHUNK, rope=False)
            out_b = (ob_s, ob_p)
            w_out = w_out_even[e]
        else:
            o = l // 2
            p = _norm_in(xs, norm_mix_w[l], mods, w_in_odd[o].astype(BF16), pack=True)
            cw3 = jnp.concatenate([conv_w[o], jnp.zeros((1, W), F32)], axis=0).reshape(
                C_KSIZE + 1, N_STRIPS, LANES).transpose(1, 0, 2)
            conv_args = (cw3, conv_b[o], conv_ln_w[o], conv_ln_b[o])
            prm_f = _lru_params(lru_conv_w[o], lru_conv_b[o], lru_wa_fwd[o], lru_ba_fwd[o], lru_wx_fwd[o],
                                lru_bx_fwd[o], lru_lam_fwd[o])
            prm_b = _lru_params(lru_conv_w[o], lru_conv_b[o], lru_wa_bwd[o], lru_ba_bwd[o], lru_wx_bwd[o],
                                lru_bx_bwd[o], lru_lam_bwd[o])
            latent = dict(seq_row0=0, nseq=B_S, n=N_S)
            context = dict(seq_row0=T_S, nseq=B_P, n=N_P)
            out_a = (_od_conv(p, *conv_args, **latent), _od_conv(p, *conv_args, **context))
            yd_s, _, _ = _od_rglru(p, prm_f, prm_b, state_lru_fwd[:, o], state_lru_bwd[:, o], **latent)
            zero_h = jnp.zeros((B_P, W), F32)
            yd_p, lru_f, lru_b = _od_rglru(p, prm_f, prm_b, zero_h, zero_h, **context)
            out_b = (yd_s, yd_p)
            w_out = w_out_odd[o]

        rw = jnp.concatenate([router_grp_w[l], router_exp_w[l],
                              jnp.zeros((D, LANES - N_GROUPS - N_EXPERTS), F32)], axis=1)
        rw_hi = rw.astype(BF16)
        rw = jnp.concatenate([rw_hi, (rw - rw_hi.astype(F32)).astype(BF16)], axis=1)
        rb = jnp.concatenate([router_grp_b[l], router_exp_b[l],
                              jnp.zeros((LANES - N_GROUPS - N_EXPERTS,), F32)]).reshape(1, LANES)
        x1, hf, info, cnt = _out_proj(out_a, out_b, xs, mods, w_out.astype(BF16), norm_ffn_w[l], rw, rb,
                                      packed=(l % 2 == 1))
        y = _moe(hf, _dispatch_plan(info, cnt), moe_w_gate, moe_w_up, moe_w_down, l)
        xs = tuple(_combine(x1, y, info, mods, norm_final_w, final=(l == DEPTH - 1)))

    y_sample = xs[0].reshape(B_S, N_S, D)
    y_prompt = xs[1].reshape(B_P, N_P, D)
    return (y_prompt, y_sample, ret_f[:, None], ret_b[:, None], lru_f[:, None], lru_b[:, None])
```

```python
import functools

import jax
import jax.numpy as jnp
from jax import lax
from jax.experimental import pallas as pl
from jax.experimental.pallas import tpu as pltpu

F32 = jnp.float32
BF16 = jnp.bfloat16
U32 = jnp.uint32

D = 2048
B_P, N_P = 16, 256
B_S, N_S = 8, 4096
T_S = B_S * N_S
T_P = B_P * N_P
T = T_S + T_P
ROWS_PER_COND = 4096
N_COND_PAD = 16
DEPTH = 2
N_MOD = 6
EPS = 1e-6
GRID_W = 64
ROPE_BASE = 10000.0

W = 1024
HW = W // 2
HEADS = 8
HD = 128
CHUNK = 128
A_GROUPS = 4
A_GDIM = W // A_GROUPS
C_KSIZE = 31
D_KSIZE = 4
LRU_C = 8.0

N_GROUPS = 4
EXPERTS_PER_GROUP = 8
N_EXPERTS = 32
D_EXPERT = 512
LANES = 128
N_STRIPS = W // LANES
EXPERT_LANE0 = N_GROUPS

TM_IN = 512
TN_IN = 2048
TM_OUT = 512
TM_MOE = 512
N_SLOTS = 2 * T + N_EXPERTS * TM_MOE
N_BLOCKS = N_SLOTS // TM_MOE
TM_CMB = 512

VMEM_LIMIT = 56 * 1024 * 1024
MOE_VMEM_LIMIT = 60 * 1024 * 1024


def _params(sem, vmem=VMEM_LIMIT):
    return pltpu.CompilerParams(dimension_semantics=sem, vmem_limit_bytes=vmem)


def _cond_row(i, tm):
    return (i * tm) // ROWS_PER_COND


def _sigmoid(x):
    return 0.5 * jnp.tanh(0.5 * x) + 0.5


def _pack_pair(lo, hi, rounded=False):
    if not rounded:
        lo, hi = lo.astype(BF16).astype(F32), hi.astype(BF16).astype(F32)
    return (lax.bitcast_convert_type(lo, U32) >> 16) | lax.bitcast_convert_type(hi, U32)


def _unpack_pair(w, dtype):
    return (lax.bitcast_convert_type(w << 16, F32).astype(dtype),
            lax.bitcast_convert_type(w & U32(0xFFFF0000), F32).astype(dtype))


def _pack_halves(x):
    k = x.shape[1] // 2
    return _pack_pair(x[:, :k], x[:, k:])


def _unpack_halves(w, dtype):
    return jnp.concatenate(_unpack_pair(w, dtype), axis=1)


TT_SUB = 8


def _store_tt(ref, x, rows, rounded=False):
    for c in range(TT_SUB):
        ref[pl.ds(c, rows, stride=TT_SUB), :] = _pack_pair(x[:, c * LANES:(c + 1) * LANES],
                                                           x[:, W + c * LANES:W + (c + 1) * LANES], rounded)


def _load_tt(ref, rows, dtype):
    lo, hi = [], []
    for c in range(TT_SUB):
        a, b = _unpack_pair(ref[pl.ds(c, rows, stride=TT_SUB), :], dtype)
        lo.append(a)
        hi.append(b)
    return jnp.concatenate(lo + hi, axis=1)


def _row_specs(arrs, tm, ngrid):
    if len(arrs) == 1:
        maps = [lambda i: i]
    else:
        n_s = T_S // tm
        maps = [lambda i: jnp.minimum(i, n_s - 1), lambda i: jnp.maximum(i - n_s, 0)]
    width = arrs[0].shape[1]
    if ngrid == 1:
        return [pl.BlockSpec((tm, width), lambda i, f=f: (f(i), 0)) for f in maps]
    return [pl.BlockSpec((tm, width), lambda i, j, f=f: (f(i), 0)) for f in maps]


def _tile_value(refs, tm):
    if len(refs) == 1:
        return refs[0][...]
    return jnp.where(pl.program_id(0) < T_S // tm, refs[0][...], refs[1][...])


def _for_tile(refs, tm, fn):
    if len(refs) == 1:
        fn(refs[0])
        return
    i = pl.program_id(0)

    @pl.when(i < T_S // tm)
    def _():
        fn(refs[0])

    @pl.when(i >= T_S // tm)
    def _():
        fn(refs[1])


def _mod_body(c_ref, w_ref, b_ref, o_ref):
    c = c_ref[...]
    a = (c * _sigmoid(c)).astype(BF16)
    o_ref[...] = jnp.dot(a, w_ref[...].astype(BF16), preferred_element_type=F32) + b_ref[...]


def _modulation(cond, w_mod, b_mod):
    tn = 1024
    return pl.pallas_call(
        _mod_body,
        grid=(DEPTH, N_MOD * D // tn),
        in_specs=[pl.BlockSpec((N_COND_PAD, D), lambda l, j: (0, 0)),
                  pl.BlockSpec((None, D, tn), lambda l, j: (l, 0, j)),
                  pl.BlockSpec((None, 1, tn), lambda l, j: (l, 0, j))],
        out_specs=pl.BlockSpec((None, N_COND_PAD, tn), lambda l, j: (l, 0, j)),
        out_shape=jax.ShapeDtypeStruct((DEPTH, N_COND_PAD, N_MOD * D), F32),
        compiler_params=_params(("arbitrary", "arbitrary")),
        name="modulation",
    )(cond, w_mod, b_mod.reshape(DEPTH, 1, N_MOD * D))


def _mod_spec(k, tm, ngrid):
    if ngrid == 1:
        return pl.BlockSpec((None, None, 1, D), lambda i: (k, _cond_row(i, tm), 0, 0))
    return pl.BlockSpec((None, None, 1, D), lambda i, j: (k, _cond_row(i, tm), 0, 0))


def _norm_in_body(*refs, pack):
    nw_ref, sc_ref, sh_ref, w_ref, p_ref, h_scr = refs[-6:]

    @pl.when(pl.program_id(1) == 0)
    def _():
        x = _tile_value(refs[:-6], TM_IN)
        ms = jnp.mean(x * x, axis=-1, keepdims=True)
        h = x * lax.rsqrt(ms + EPS) * nw_ref[...]
        h = h * (1.0 + sc_ref[...]) + sh_ref[...]
        h_scr[...] = h.astype(BF16)

    r = jnp.dot(h_scr[...], w_ref[...], preferred_element_type=F32)
    if pack:
        p_ref[...] = jnp.concatenate([_pack_halves(r[:, k * W:(k + 1) * W]) for k in range(TN_IN // W)], axis=1)
    else:
        p_ref[...] = r.astype(BF16)


def _norm_in(xs, nw, mods, w_bf16, *, pack):
    n = w_bf16.shape[1]
    if pack:
        out_spec = pl.BlockSpec((TM_IN, TN_IN // 2), lambda i, j: (i, j))
        out_shape = jax.ShapeDtypeStruct((T, n // 2), U32)
    else:
        out_spec = pl.BlockSpec((TM_IN, TN_IN), lambda i, j: (i, j))
        out_shape = jax.ShapeDtypeStruct((T, n), BF16)
    return pl.pallas_call(
        functools.partial(_norm_in_body, pack=pack),
        grid=(T // TM_IN, n // TN_IN),
        in_specs=_row_specs(xs, TM_IN, 2) + [
            pl.BlockSpec((1, D), lambda i, j: (0, 0)),
            _mod_spec(1, TM_IN, 2),
            _mod_spec(0, TM_IN, 2),
            pl.BlockSpec((D, TN_IN), lambda i, j: (0, j))],
        out_specs=out_spec,
        out_shape=out_shape,
        scratch_shapes=[pltpu.VMEM((TM_IN, D), BF16)],
        compiler_params=_params(("arbitrary", "arbitrary")),
        name="norm_in",
    )(*xs, nw.reshape(1, D), mods, mods, w_bf16)


GMLP_ROWS = 512


def _gmlp_body(u_ref, v_ref, lnw_ref, ws_ref, b_ref, o_ref):
    for c in range(GMLP_ROWS // CHUNK):
        rows = pl.ds(c * CHUNK, CHUNK)
        v = jax.nn.gelu(v_ref[rows, :].astype(F32))
        mu = jnp.mean(v, axis=-1, keepdims=True)
        var = jnp.mean(jnp.square(v - mu), axis=-1, keepdims=True)
        vn = ((v - mu) * lax.rsqrt(var + EPS) * lnw_ref[...]).astype(BF16)
        z = jnp.concatenate(
            [jnp.dot(ws_ref[g], vn[:, g * A_GDIM:(g + 1) * A_GDIM], preferred_element_type=F32)
             for g in range(A_GROUPS)], axis=1) + b_ref[...]
        u = jax.nn.gelu(u_ref[rows, :].astype(F32))
        o_ref[rows, :] = (u * z).astype(BF16)


def _gmlp(p, lnw, ws_bf16, b_full):
    return pl.pallas_call(
        _gmlp_body,
        grid=(T // GMLP_ROWS,),
        in_specs=[pl.BlockSpec((GMLP_ROWS, W), lambda i: (i, 0)),
                  pl.BlockSpec((GMLP_ROWS, W), lambda i: (i, 1)),
                  pl.BlockSpec((1, W), lambda i: (0, 0)),
                  pl.BlockSpec((A_GROUPS, CHUNK, CHUNK), lambda i: (0, 0, 0)),
                  pl.BlockSpec((CHUNK, W), lambda i: (0, 0))],
        out_specs=pl.BlockSpec((GMLP_ROWS, W), lambda i: (i, 0)),
        out_shape=jax.ShapeDtypeStruct((T, W), BF16),
        compiler_params=_params(("arbitrary",)),
        name="gmlp",
    )(p, p, lnw.reshape(1, W), ws_bf16, b_full)


RET_SUB = 2
RET_ROWS = RET_SUB * CHUNK


def _roped_qk(q_ref, k_ref, cos_ref, sin_ref, rows, h, rope):
    cols = slice(h * HD, (h + 1) * HD)
    q = q_ref[rows, cols].astype(F32)
    k = k_ref[rows, cols].astype(F32) * (HD ** -0.5)
    if rope:
        c = cos_ref[rows, :]
        s = sin_ref[rows, :]
        q = q * c + pltpu.roll(q, HD // 2, axis=1) * s
        k = k * c + pltpu.roll(k, HD // 2, axis=1) * s
    return q, k


def _kt_v(kd, v):
    return lax.dot_general(kd, v, (((0,), (0,)), ((), ())), preferred_element_type=F32)


def _ret_bwd_body(q_ref, k_ref, v_ref, cos_ref, sin_ref, s0_ref, dq_ref, dk_ref, dc_ref,
                  ob_ref, sfin_ref, s_scr, *, rope, nc):
    c = pl.program_id(1)

    @pl.when(c == 0)
    def _():
        s_scr[...] = s0_ref[...]

    for sub in reversed(range(RET_SUB)):
        rows = slice(sub * CHUNK, (sub + 1) * CHUNK)
        for h in range(HEADS):
            cols = slice(h * HD, (h + 1) * HD)
            q, k = _roped_qk(q_ref, k_ref, cos_ref, sin_ref, rows, h, rope)
            v = v_ref[rows, cols]
            s = s_scr[h]
            ob_ref[rows, cols] = jnp.dot(q.astype(BF16), s.astype(BF16), preferred_element_type=F32) * dq_ref[h]
            kd = (k * dk_ref[h]).astype(BF16)
            s_scr[h] = s * dc_ref[h] + _kt_v(kd, v)

    @pl.when(c == nc - 1)
    def _():
        sfin_ref[...] = s_scr[...]


def _ret_fwd_body(q_ref, k_ref, v_ref, g_ref, cos_ref, sin_ref, ob_ref, s0_ref, m_ref, dq_ref, dk_ref, dc_ref,
                  gn_ref, o_ref, sfin_ref, s_scr, *, rope, nc):
    c = pl.program_id(1)

    @pl.when(c == 0)
    def _():
        s_scr[...] = s0_ref[...]

    for sub in range(RET_SUB):
        rows = slice(sub * CHUNK, (sub + 1) * CHUNK)
        for h in range(HEADS):
            cols = slice(h * HD, (h + 1) * HD)
            q, k = _roped_qk(q_ref, k_ref, cos_ref, sin_ref, rows, h, rope)
            qb = q.astype(BF16)
            v = v_ref[rows, cols]
            s = s_scr[h]
            scores = lax.dot_general(qb, k.astype(BF16), (((1,), (1,)), ((), ())),
                                     preferred_element_type=F32) * m_ref[h]
            o = (jnp.dot(scores.astype(BF16), v, preferred_element_type=F32)
                 + jnp.dot(qb, s.astype(BF16), preferred_element_type=F32) * dq_ref[h]
                 + ob_ref[rows, cols])
            kd = (k * dk_ref[h]).astype(BF16)
            s_scr[h] = s * dc_ref[h] + _kt_v(kd, v)
            mu = jnp.mean(o, axis=-1, keepdims=True)
            var = jnp.mean(jnp.square(o - mu), axis=-1, keepdims=True)
            on = (o - mu) * lax.rsqrt(var + EPS) * gn_ref[:, cols]
            g = g_ref[rows, cols].astype(F32)
            o_ref[rows, cols] = (g * _sigmoid(g) * on).astype(BF16)

    @pl.when(c == nc - 1)
    def _():
        sfin_ref[...] = s_scr[...]


def _retention(p, s_f0, s_b0, tabs, gn_w, cos, sin, *, base_chunk, nb, nc, rope):
    m_tab, dq_f, dk_f, dc_f, dq_b, dk_b, dc_b = tabs
    ns = nc // RET_SUB
    base = base_chunk // RET_SUB
    state_spec = pl.BlockSpec((None, HEADS, HD, HD), lambda b, c: (b, 0, 0, 0))
    tab_spec = pl.BlockSpec((HEADS, HD, HD), lambda b, c: (0, 0, 0))
    dc_spec = pl.BlockSpec((HEADS, 1, HD), lambda b, c: (0, 0, 0))
    state_shape = jax.ShapeDtypeStruct((nb, HEADS, HD, HD), F32)

    def step(c, rev):
        return ns - 1 - c if rev else c

    def col(j, rev):
        return pl.BlockSpec((RET_ROWS, W), lambda b, c: (base + b * ns + step(c, rev), j))

    def rope_spec(rev):
        if not rope:
            return pl.BlockSpec((RET_ROWS, HD), lambda b, c: (0, 0))
        return pl.BlockSpec((RET_ROWS, HD), lambda b, c: (step(c, rev), 0))

    def local(rev):
        return pl.BlockSpec((RET_ROWS, W), lambda b, c: (b * ns + step(c, rev), 0))

    n_rows = nb * nc * CHUNK
    ob, s_b = pl.pallas_call(
        functools.partial(_ret_bwd_body, rope=rope, nc=ns),
        grid=(nb, ns),
        in_specs=[col(2, True), col(3, True), col(4, True), rope_spec(True), rope_spec(True),
                  state_spec, tab_spec, tab_spec, dc_spec],
        out_specs=[local(True), state_spec],
        out_shape=[jax.ShapeDtypeStruct((n_rows, W), F32), state_shape],
        scratch_shapes=[pltpu.VMEM((HEADS, HD, HD), F32)],
        compiler_params=_params(("arbitrary", "arbitrary")),
        name="retention_bwd",
    )(p, p, p, cos, sin, s_b0, dq_b, dk_b, dc_b)

    o, s_f = pl.pallas_call(
        functools.partial(_ret_fwd_body, rope=rope, nc=ns),
        grid=(nb, ns),
        in_specs=[col(2, False), col(3, False), col(4, False), col(5, False), rope_spec(False), rope_spec(False),
                  local(False), state_spec, tab_spec, tab_spec, tab_spec, dc_spec,
                  pl.BlockSpec((1, W), lambda b, c: (0, 0))],
        out_specs=[local(False), state_spec],
        out_shape=[jax.ShapeDtypeStruct((n_rows, W), BF16), state_shape],
        scratch_shapes=[pltpu.VMEM((HEADS, HD, HD), F32)],
        compiler_params=_params(("arbitrary", "arbitrary")),
        name="retention_fwd",
    )(p, p, p, p, cos, sin, ob, s_f0, m_tab, dq_f, dk_f, dc_f, gn_w.reshape(1, W))
    return o, s_f, s_b


def _retention_tables(decay_fwd, decay_bwd):
    lg_f = jax.nn.log_sigmoid(decay_fwd.astype(F32))[:, None, None]
    lg_b = jax.nn.log_sigmoid(decay_bwd.astype(F32))[:, None, None]
    pos = jnp.arange(CHUNK, dtype=F32)
    rel = pos[:, None] - pos[None, :]
    m_tab = (jnp.where(rel >= 0, jnp.exp(lg_f * jnp.maximum(rel, 0.0)), 0.0)
             + jnp.where(rel <= 0, jnp.exp(lg_b * jnp.maximum(-rel, 0.0)), 0.0))
    ones = jnp.ones((1, 1, HD), F32)
    col = pos[None, :, None]
    dq_f = jnp.exp(lg_f * (col + 1.0)) * ones
    dk_f = jnp.exp(lg_f * (CHUNK - 1.0 - col)) * ones
    dq_b = jnp.exp(lg_b * (CHUNK - col)) * ones
    dk_b = jnp.exp(lg_b * col) * ones
    dc_f = jnp.exp(lg_f * CHUNK) * ones
    dc_b = jnp.exp(lg_b * CHUNK) * ones
    return m_tab, dq_f, dk_f, dc_f, dq_b, dk_b, dc_b


def _rope_tables():
    rows = N_S // GRID_W
    row = jnp.repeat(jnp.arange(rows, dtype=F32), GRID_W)
    colp = jnp.tile(jnp.arange(GRID_W, dtype=F32), rows)
    quarter = HD // 4
    inv_freq = ROPE_BASE ** (-jnp.arange(quarter, dtype=F32) / quarter)
    ang = jnp.concatenate([row[:, None] * inv_freq, colp[:, None] * inv_freq], axis=-1)
    cos, sin = jnp.cos(ang), jnp.sin(ang)
    return jnp.concatenate([cos, cos], axis=-1), jnp.concatenate([-sin, sin], axis=-1)


OD_TT = 128
OD_G = 8
CONV_HALO = 16
LRU_HALO = 8


def _od_rows(seq_row0, n, g, tj):
    return [seq_row0 + (g * OD_G + b) * n + tj * OD_TT for b in range(OD_G)]


def _od_in_copies(p_hbm, buf, sem, *, col, halo, seq_row0, n, g, tj):
    nt = n // OD_TT
    out = []
    for b, base in enumerate(_od_rows(seq_row0, n, g, tj)):
        def src(r, k):
            return p_hbm.at[pl.ds(r, k), pl.ds(col * HW, HW)]

        out.append((None, pltpu.make_async_copy(src(base, OD_TT), buf.at[pl.ds(halo, OD_TT), b, :], sem)))
        if halo:
            out.append((tj > 0, pltpu.make_async_copy(src(base - halo, halo), buf.at[pl.ds(0, halo), b, :], sem)))
            out.append((tj < nt - 1, pltpu.make_async_copy(src(base + OD_TT, halo),
                                                           buf.at[pl.ds(halo + OD_TT, halo), b, :], sem)))
    return out


def _od_run(copies, op):
    for cond, cp in copies:
        fn = cp.start if op == "start" else cp.wait
        if cond is None:
            fn()
        else:
            pl.when(cond)(fn)


def _od_zero_halo(buf, halo, n, tj):
    nt = n // OD_TT
    zeros = jnp.zeros((halo, OD_G, HW), U32)

    @pl.when(tj == 0)
    def _():
        buf[0:halo] = zeros

    @pl.when(tj == nt - 1)
    def _():
        buf[halo + OD_TT:2 * halo + OD_TT] = zeros


def _od_out_copies(obuf, out_hbm, sem, *, seq_row0, n, g, tj):
    return [(None, pltpu.make_async_copy(obuf.at[:, b, :], out_hbm.at[pl.ds(base - seq_row0, OD_TT), :], sem))
            for b, base in enumerate(_od_rows(seq_row0, n, g, tj))]


def _od_unpack(buf, rows):
    return _unpack_halves(buf[...].reshape(rows, HW), F32)


def _od_pipeline(q, nq, in_copies, sem_slots=2):
    s = q % 2

    @pl.when(q == 0)
    def _():
        _od_run(in_copies(q, s), "start")

    _od_run(in_copies(q, s), "wait")

    @pl.when(q + 1 < nq)
    def _():
        _od_run(in_copies(q + 1, 1 - s), "start")

    return s


def _od_emit(q, nq, s, obuf, words, out_copies):
    @pl.when(q >= 2)
    def _():
        _od_run(out_copies(s), "wait")

    obuf[s] = words.reshape(OD_TT, OD_G, HW)
    _od_run(out_copies(s), "start")

    @pl.when(q == nq - 1)
    def _():
        _od_run(out_copies(s), "wait")

        @pl.when(q >= 1)
        def _():
            _od_run(out_copies(1 - s), "wait")


CONV_RB = 64


def _od_conv_body(p_hbm, cw_ref, cb_ref, lnw_ref, lnb_ref, yc_hbm, xin, glu, ybuf, obuf, sem_in, sem_out,
                  *, seq_row0, n, ng):
    nt = n // OD_TT
    nq = ng * nt
    q = pl.program_id(0)
    halo = CONV_HALO
    rows_in = (OD_TT + 2 * halo) * OD_G
    rows = OD_TT * OD_G

    def in_copies(step, slot):
        out = []
        for br in range(2):
            out += _od_in_copies(p_hbm, xin.at[br, slot], sem_in.at[slot], col=br, halo=halo,
                                 seq_row0=seq_row0, n=n, g=step // nt, tj=step % nt)
        return out

    s = _od_pipeline(q, nq, in_copies)
    g, tj = q // nt, q % nt
    for br in range(2):
        _od_zero_halo(xin.at[br, s], halo, n, tj)
    glu[...] = _od_unpack(xin.at[0, s], rows_in) * _sigmoid(_od_unpack(xin.at[1, s], rows_in))

    def strip(si, carry):
        lanes = pl.ds(pl.multiple_of(si * LANES, LANES), LANES)
        for rb in range(rows // CONV_RB):
            acc = jnp.zeros((CONV_RB // 8, 8, LANES), F32)
            for k in range(C_KSIZE):
                wk = cw_ref[si, pl.ds(k, 1), :]
                r0 = rb * CONV_RB + (halo - C_KSIZE // 2 + k) * OD_G
                acc = acc + glu[r0:r0 + CONV_RB, lanes].reshape(CONV_RB // 8, 8, LANES) * wk[None]
            ybuf[rb * CONV_RB:(rb + 1) * CONV_RB, lanes] = acc.reshape(CONV_RB, LANES)
        return carry

    lax.fori_loop(0, N_STRIPS, strip, 0)
    y = ybuf[...] + cb_ref[...]
    mu = jnp.mean(y, axis=-1, keepdims=True)
    var = jnp.mean(jnp.square(y - mu), axis=-1, keepdims=True)
    yn = (y - mu) * lax.rsqrt(var + EPS) * lnw_ref[...] + lnb_ref[...]
    yo = yn * _sigmoid(yn)

    def out_copies(slot):
        return _od_out_copies(obuf.at[slot], yc_hbm, sem_out.at[slot], seq_row0=seq_row0, n=n, g=g, tj=tj)

    _od_emit(q, nq, s, obuf, _pack_halves(yo), out_copies)


def _od_conv(p, cw3, cb, lnw, lnb, *, seq_row0, nseq, n):
    ng = nseq // OD_G
    nq = ng * (n // OD_TT)
    vec = pl.BlockSpec((1, W), lambda q: (0, 0))
    t_in = OD_TT + 2 * CONV_HALO
    return pl.pallas_call(
        functools.partial(_od_conv_body, seq_row0=seq_row0, n=n, ng=ng),
        grid=(nq,),
        in_specs=[pl.BlockSpec(memory_space=pl.ANY),
                  pl.BlockSpec((N_STRIPS, C_KSIZE + 1, LANES), lambda q: (0, 0, 0)), vec, vec, vec],
        out_specs=pl.BlockSpec(memory_space=pl.ANY),
        out_shape=jax.ShapeDtypeStruct((nseq * n, HW), U32),
        scratch_shapes=[pltpu.VMEM((2, 2, t_in, OD_G, HW), U32),
                        pltpu.VMEM((t_in * OD_G, W), F32),
                        pltpu.VMEM((OD_TT * OD_G, W), F32),
                        pltpu.VMEM((2, OD_TT, OD_G, HW), U32),
                        pltpu.SemaphoreType.DMA((2,)), pltpu.SemaphoreType.DMA((2,))],
        compiler_params=_params(("arbitrary",)),
        name="conv_module",
    )(p, cw3, cb.reshape(1, W), lnw.reshape(1, W), lnb.reshape(1, W))


def _od_gates(x, cw_ref, cb_ref, wa_ref, ba_ref, wx_ref, bx_ref, sp_ref, a_scr, b_scr):
    rows = OD_TT * OD_G
    xc = cb_ref[...]
    for k in range(D_KSIZE):
        r0 = (LRU_HALO - D_KSIZE // 2 + k) * OD_G
        xc = xc + cw_ref[k:k + 1, :] * x[r0:r0 + rows, :]
    xb = xc.astype(BF16)
    r = jnp.concatenate([jnp.dot(xb[:, h * HD:(h + 1) * HD], wa_ref[h], preferred_element_type=F32)
                         for h in range(HEADS)], axis=1) + ba_ref[...]
    g = jnp.concatenate([jnp.dot(xb[:, h * HD:(h + 1) * HD], wx_ref[h], preferred_element_type=F32)
                         for h in range(HEADS)], axis=1) + bx_ref[...]
    log_a = (-LRU_C) * _sigmoid(r) * sp_ref[...]
    th = jnp.tanh(log_a)
    one_minus_a2 = -2.0 * th / (1.0 - th)
    a_scr[...] = jnp.exp(log_a)
    b_scr[...] = jnp.sqrt(jnp.maximum(one_minus_a2, 0.0)) * (_sigmoid(g) * xc)


def _od_scan(a_scr, b_scr, out_ref, h, reverse):
    def step(t, h):
        tr = (OD_TT - 1 - t) if reverse else t
        rows = pl.ds(pl.multiple_of(tr * OD_G, OD_G), OD_G)
        h = a_scr[rows, :] * h + b_scr[rows, :]
        out_ref[rows, :] = h
        return h

    return lax.fori_loop(0, OD_TT, step, h, unroll=8)


def _od_lru_fwd_body(p_hbm, cw_ref, cb_ref, wa_ref, ba_ref, wx_ref, bx_ref, sp_ref, h0_ref,
                     hf_ref, hfin_ref, xin, a_scr, b_scr, h_scr, sem_in, *, seq_row0, n, ng):
    nt = n // OD_TT
    nq = ng * nt
    q = pl.program_id(0)

    def in_copies(step, slot):
        return _od_in_copies(p_hbm, xin.at[slot], sem_in.at[slot], col=2, halo=LRU_HALO,
                             seq_row0=seq_row0, n=n, g=step // nt, tj=step % nt)

    s = _od_pipeline(q, nq, in_copies)
    tj = q % nt
    _od_zero_halo(xin.at[s], LRU_HALO, n, tj)

    @pl.when(tj == 0)
    def _():
        h_scr[...] = h0_ref[...]

    x = _od_unpack(xin.at[s], (OD_TT + 2 * LRU_HALO) * OD_G)
    _od_gates(x, cw_ref, cb_ref, wa_ref, ba_ref, wx_ref, bx_ref, sp_ref, a_scr, b_scr)
    h = _od_scan(a_scr, b_scr, hf_ref, h_scr[...], False)
    h_scr[...] = h

    @pl.when(tj == nt - 1)
    def _():
        hfin_ref[...] = h


def _od_lru_bwd_body(p_hbm, hf_ref, cw_ref, cb_ref, wa_ref, ba_ref, wx_ref, bx_ref, sp_ref, h0_ref,
                     yd_hbm, hfin_ref, xin, gin, a_scr, b_scr, hb_scr, obuf, h_scr, sem_in, sem_out,
                     *, seq_row0, n, ng):
    nt = n // OD_TT
    nq = ng * nt
    q = pl.program_id(0)

    def in_copies(step, slot):
        where = dict(seq_row0=seq_row0, n=n, g=step // nt, tj=nt - 1 - step % nt)
        return (_od_in_copies(p_hbm, xin.at[slot], sem_in.at[slot], col=2, halo=LRU_HALO, **where)
                + _od_in_copies(p_hbm, gin.at[slot], sem_in.at[slot], col=3, halo=0, **where))

    s = _od_pipeline(q, nq, in_copies)
    g, tj = q // nt, nt - 1 - q % nt
    _od_zero_halo(xin.at[s], LRU_HALO, n, tj)

    @pl.when(tj == nt - 1)
    def _():
        h_scr[...] = h0_ref[...]

    x = _od_unpack(xin.at[s], (OD_TT + 2 * LRU_HALO) * OD_G)
    _od_gates(x, cw_ref, cb_ref, wa_ref, ba_ref, wx_ref, bx_ref, sp_ref, a_scr, b_scr)
    h = _od_scan(a_scr, b_scr, hb_scr, h_scr[...], True)
    h_scr[...] = h

    @pl.when(tj == 0)
    def _():
        hfin_ref[...] = h

    gd = _od_unpack(gin.at[s], OD_TT * OD_G)
    yd = (hf_ref[...] + hb_scr[...]) * jax.nn.gelu(gd)

    def out_copies(slot):
        return _od_out_copies(obuf.at[slot], yd_hbm, sem_out.at[slot], seq_row0=seq_row0, n=n, g=g, tj=tj)

    _od_emit(q, nq, s, obuf, _pack_halves(yd), out_copies)


def _od_rglru(p, prm_f, prm_b, h0_f, h0_b, *, seq_row0, nseq, n):
    ng = nseq // OD_G
    nt = n // OD_TT
    nq = ng * nt
    rows = OD_TT * OD_G
    vec = pl.BlockSpec((1, W), lambda q: (0, 0))
    wsp = pl.BlockSpec((HEADS, HD, HD), lambda q: (0, 0, 0))
    prm_specs = [pl.BlockSpec((8, W), lambda q: (0, 0)), vec, wsp, vec, wsp, vec, vec,
                 pl.BlockSpec((OD_G, W), lambda q: (q // nt, 0))]
    state_spec = pl.BlockSpec((OD_G, W), lambda q: (q // nt, 0))
    state_shape = jax.ShapeDtypeStruct((nseq, W), F32)
    t_in = OD_TT + 2 * LRU_HALO
    tile_f32 = pltpu.VMEM((rows, W), F32)

    hf, s_f = pl.pallas_call(
        functools.partial(_od_lru_fwd_body, seq_row0=seq_row0, n=n, ng=ng),
        grid=(nq,),
        in_specs=[pl.BlockSpec(memory_space=pl.ANY)] + prm_specs,
        out_specs=[pl.BlockSpec((rows, W), lambda q: (q, 0)), state_spec],
        out_shape=[jax.ShapeDtypeStruct((nq * rows, W), F32), state_shape],
        scratch_shapes=[pltpu.VMEM((2, t_in, OD_G, HW), U32), tile_f32, tile_f32, pltpu.VMEM((OD_G, W), F32),
                        pltpu.SemaphoreType.DMA((2,))],
        compiler_params=_params(("arbitrary",)),
        name="rglru_fwd",
    )(p, *prm_f, h0_f)

    yd, s_b = pl.pallas_call(
        functools.partial(_od_lru_bwd_body, seq_row0=seq_row0, n=n, ng=ng),
        grid=(nq,),
        in_specs=[pl.BlockSpec(memory_space=pl.ANY),
                  pl.BlockSpec((rows, W), lambda q: ((q // nt) * nt + nt - 1 - q % nt, 0))] + prm_specs,
        out_specs=[pl.BlockSpec(memory_space=pl.ANY), state_spec],
        out_shape=[jax.ShapeDtypeStruct((nseq * n, HW), U32), state_shape],
        scratch_shapes=[pltpu.VMEM((2, t_in, OD_G, HW), U32), pltpu.VMEM((2, OD_TT, OD_G, HW), U32),
                        tile_f32, tile_f32, tile_f32, pltpu.VMEM((2, OD_TT, OD_G, HW), U32),
                        pltpu.VMEM((OD_G, W), F32),
                        pltpu.SemaphoreType.DMA((2,)), pltpu.SemaphoreType.DMA((2,))],
        compiler_params=_params(("arbitrary",)),
        name="rglru_bwd",
    )(p, hf, *prm_b, h0_b)
    return yd, s_f, s_b


def _lru_params(cw, cb, wa, ba, wx, bx, lam):
    cw_pad = jnp.concatenate([cw, jnp.zeros((8 - D_KSIZE, W), F32)], axis=0)
    sp = jax.nn.softplus(-lam.astype(F32)).reshape(1, W)
    return (cw_pad, cb.reshape(1, W), wa.astype(BF16), ba.reshape(1, W), wx.astype(BF16), bx.reshape(1, W), sp)


def _out_body(*refs, n_a, n_b, n_x, packed):
    a_refs = refs[:n_a]
    b_refs = refs[n_a:n_a + n_b]
    x_refs = refs[n_a + n_b:n_a + n_b + n_x]
    (g1_ref, wo_ref, nw_ref, sc_ref, sh_ref, rw_ref, rb_ref,
     x1_ref, hf_ref, info_ref, cnt_ref, carry) = refs[n_a + n_b + n_x:]
    i = pl.program_id(0)

    @pl.when(i == 0)
    def _():
        carry[...] = jnp.zeros_like(carry)

    def branch(refs_, row0):
        v = _tile_value(refs_, TM_OUT)
        if packed:
            lo, hi = _unpack_pair(v, BF16)
            return (jnp.dot(lo, wo_ref[row0:row0 + HW, :], preferred_element_type=F32)
                    + jnp.dot(hi, wo_ref[row0 + HW:row0 + W, :], preferred_element_type=F32))
        return jnp.dot(v, wo_ref[row0:row0 + W, :], preferred_element_type=F32)

    x1 = _tile_value(x_refs, TM_OUT) + g1_ref[...] * (branch(a_refs, 0) + branch(b_refs, W))
    x1_ref[...] = x1
    ms = jnp.mean(x1 * x1, axis=-1, keepdims=True)
    hf = x1 * lax.rsqrt(ms + EPS) * nw_ref[...]
    hf = hf * (1.0 + sc_ref[...]) + sh_ref[...]
    hf_hi = hf.astype(BF16)
    hf_hi32 = hf_hi.astype(F32)
    _store_tt(hf_ref, hf_hi32, TM_OUT, rounded=True)

    hf_lo = (hf - hf_hi32).astype(BF16)
    l2 = jnp.dot(hf_hi, rw_ref[...], preferred_element_type=F32)
    logits = (l2[:, :LANES] + l2[:, LANES:]
              + jnp.dot(hf_lo, rw_ref[:, :LANES], preferred_element_type=F32) + rb_ref[...])
    lane = lax.broadcasted_iota(jnp.int32, (TM_OUT, LANES), 1).astype(F32)
    neg = -jnp.inf
    gmask = lane < N_GROUPS
    mg = jnp.max(jnp.where(gmask, logits, neg), axis=-1, keepdims=True)
    gstar = jnp.min(jnp.where(gmask & (logits == mg), lane, float(LANES)), axis=-1, keepdims=True)
    denom = jnp.sum(jnp.where(gmask, jnp.exp(logits - mg), 0.0), axis=-1, keepdims=True)
    psel = 1.0 / denom
    lo = EXPERT_LANE0 + EXPERTS_PER_GROUP * gstar
    emask = (lane >= lo) & (lane < lo + EXPERTS_PER_GROUP)
    v1 = jnp.max(jnp.where(emask, logits, neg), axis=-1, keepdims=True)
    i1 = jnp.min(jnp.where(emask & (logits == v1), lane, float(LANES)), axis=-1, keepdims=True)
    em2 = emask & (lane != i1)
    v2 = jnp.max(jnp.where(em2, logits, neg), axis=-1, keepdims=True)
    i2 = jnp.min(jnp.where(em2 & (logits == v2), lane, float(LANES)), axis=-1, keepdims=True)
    e21 = jnp.exp(v2 - v1)
    w1 = psel / (1.0 + e21)
    w2 = psel * e21 / (1.0 + e21)

    memb = jnp.where((lane == i1) | (lane == i2), 1.0, 0.0)
    r_i = lax.broadcasted_iota(jnp.int32, (TM_OUT, TM_OUT), 0)
    c_i = lax.broadcasted_iota(jnp.int32, (TM_OUT, TM_OUT), 1)
    lower = jnp.where(r_i > c_i, 1.0, 0.0).astype(BF16)
    before = jnp.dot(lower, memb.astype(BF16), preferred_element_type=F32) + carry[0:1, :]
    rank1 = jnp.sum(jnp.where(lane == i1, before, 0.0), axis=-1, keepdims=True)
    rank2 = jnp.sum(jnp.where(lane == i2, before, 0.0), axis=-1, keepdims=True)
    new_carry = carry[...] + jnp.sum(memb, axis=0, keepdims=True)
    carry[...] = new_carry
    cnt_ref[...] = new_carry

    e1 = i1 - EXPERT_LANE0
    e2 = i2 - EXPERT_LANE0
    info = jnp.where(lane == 0, e1, jnp.where(lane == 1, e2, jnp.where(lane == 2, w1, jnp.where(
        lane == 3, w2, jnp.where(lane == 4, rank1, jnp.where(lane == 5, rank2, 0.0))))))
    info_ref[...] = info


def _out_proj(a_parts, b_parts, xs, mods, wo_bf16, nw, rw, rb, *, packed):
    tm = TM_OUT
    row = lambda n: pl.BlockSpec((tm, n), lambda i: (i, 0))
    vec = pl.BlockSpec((1, D), lambda i: (0, 0))
    return pl.pallas_call(
        functools.partial(_out_body, n_a=len(a_parts), n_b=len(b_parts), n_x=len(xs), packed=packed),
        grid=(T // tm,),
        in_specs=_row_specs(a_parts, tm, 1) + _row_specs(b_parts, tm, 1) + _row_specs(xs, tm, 1) + [
            _mod_spec(2, tm, 1),
            pl.BlockSpec((D, D), lambda i: (0, 0)), vec, _mod_spec(4, tm, 1), _mod_spec(3, tm, 1),
            pl.BlockSpec((D, 2 * LANES), lambda i: (0, 0)), pl.BlockSpec((1, LANES), lambda i: (0, 0))],
        out_specs=[row(D), pl.BlockSpec((tm * TT_SUB, LANES), lambda i: (i, 0)), row(LANES),
                   pl.BlockSpec((8, LANES), lambda i: (0, 0))],
        out_shape=[jax.ShapeDtypeStruct((T, D), F32), jax.ShapeDtypeStruct((T * TT_SUB, LANES), U32),
                   jax.ShapeDtypeStruct((T, LANES), F32), jax.ShapeDtypeStruct((8, LANES), F32)],
        scratch_shapes=[pltpu.VMEM((8, LANES), F32)],
        compiler_params=_params(("arbitrary",)),
        name="out_proj_router",
    )(*a_parts, *b_parts, *xs, mods, wo_bf16, nw.reshape(1, D), mods, mods, rw, rb)


COPY_UNROLL = 32


def _for_rows(n, fn):
    groups = n // COPY_UNROLL

    def group(c, carry):
        for u in range(COPY_UNROLL):
            fn(c * COPY_UNROLL + u)
        return carry

    def single(r, carry):
        fn(r)
        return carry

    lax.fori_loop(0, groups, group, 0)
    lax.fori_loop(groups * COPY_UNROLL, n, single, 0)


def _moe_body(be_ref, nused_ref, nv_ref, src_cur, src_nxt, hf_hbm, wg_ref, wu_ref, wd_ref, y_ref,
              xbuf, wgc, wuc, wdc, sem_in):
    i = pl.program_id(0)
    nused = nused_ref[0]
    slot = i % 2

    def tile(ref, row8):
        return ref.at[pl.ds(pl.multiple_of(row8, TT_SUB), TT_SUB)]

    def rows(ref, n):
        return ref.at[pl.ds(0, n * TT_SUB)]

    def gather(src_ref, s, n):
        def row(r):
            pltpu.make_async_copy(tile(hf_hbm, src_ref[0, r]), tile(xbuf.at[s], r * TT_SUB), sem_in.at[s]).start()

        _for_rows(n, row)

    @pl.when(i < nused)
    def _():
        nv = nv_ref[i]

        @pl.when(i == 0)
        def _():
            xbuf[...] = jnp.zeros(xbuf.shape, U32)
            gather(src_cur, 0, nv)

        pltpu.make_async_copy(rows(hf_hbm, nv), rows(xbuf.at[slot], nv), sem_in.at[slot]).wait()

        @pl.when(i + 1 < nused)
        def _():
            gather(src_nxt, 1 - slot, nv_ref[i + 1])

        @pl.when((i == 0) | (be_ref[i] != be_ref[jnp.maximum(i - 1, 0)]))
        def _():
            wgc[...] = wg_ref[...].astype(BF16)
            wuc[...] = wu_ref[...].astype(BF16)
            wdc[...] = wd_ref[...].astype(BF16)

        x = _load_tt(xbuf.at[slot], TM_MOE, BF16)
        gate = jnp.dot(x, wgc[...], preferred_element_type=F32)
        up = jnp.dot(x, wuc[...], preferred_element_type=F32)
        hid = (gate * _sigmoid(gate) * up).astype(BF16)
        _store_tt(y_ref, jnp.dot(hid, wdc[...], preferred_element_type=F32), TM_MOE)

    @pl.when(i >= nused)
    def _():
        y_ref[...] = jnp.zeros(y_ref.shape, U32)


def _moe(hf, plan, wg, wu, wd, layer):
    src, block_e, nused, nvalid = plan
    wspec = lambda shape: pl.BlockSpec((None, None) + shape, lambda i, be, nu, nv: (layer, be[i], 0, 0))
    rows = lambda f: pl.BlockSpec((None, 1, TM_MOE), lambda i, be, nu, nv: (f(i), 0, 0), memory_space=pltpu.SMEM)
    return pl.pallas_call(
        _moe_body,
        grid_spec=pltpu.PrefetchScalarGridSpec(
            num_scalar_prefetch=3,
            grid=(N_BLOCKS,),
            in_specs=[rows(lambda i: i), rows(lambda i: jnp.minimum(i + 1, N_BLOCKS - 1)),
                      pl.BlockSpec(memory_space=pl.ANY),
                      wspec((D, D_EXPERT)), wspec((D, D_EXPERT)), wspec((D_EXPERT, D))],
            out_specs=pl.BlockSpec((TM_MOE * TT_SUB, LANES), lambda i, be, nu, nv: (i, 0)),
            scratch_shapes=[pltpu.VMEM((2, TM_MOE * TT_SUB, LANES), U32),
                            pltpu.VMEM((D, D_EXPERT), BF16), pltpu.VMEM((D, D_EXPERT), BF16),
                            pltpu.VMEM((D_EXPERT, D), BF16),
                            pltpu.SemaphoreType.DMA((2,))]),
        out_shape=jax.ShapeDtypeStruct((N_SLOTS * TT_SUB, LANES), U32),
        compiler_params=_params(("arbitrary",), MOE_VMEM_LIMIT),
        name="moe_experts",
    )(block_e, nused, nvalid, src.reshape(N_BLOCKS, 1, TM_MOE), src.reshape(N_BLOCKS, 1, TM_MOE), hf, wg, wu, wd)


INV_ROWS = 2048


def _invert_body(dest_ref, zeros_hbm, codes_ref, sem):
    i = pl.program_id(0)

    @pl.when(i == 0)
    def _():
        clear = pltpu.make_async_copy(zeros_hbm, codes_ref, sem)
        clear.start()
        clear.wait()

    def put(j, carry):
        codes_ref[dest_ref[0, j]] = i * INV_ROWS + j
        return carry

    lax.fori_loop(0, INV_ROWS, put, 0, unroll=16)


def _invert_slots(dest3):
    return pl.pallas_call(
        _invert_body,
        grid=(dest3.shape[0],),
        in_specs=[pl.BlockSpec((None, 1, INV_ROWS), lambda i: (i, 0, 0), memory_space=pltpu.SMEM),
                  pl.BlockSpec(memory_space=pl.ANY)],
        out_specs=pl.BlockSpec(memory_space=pltpu.SMEM),
        out_shape=jax.ShapeDtypeStruct((N_SLOTS,), jnp.int32),
        scratch_shapes=[pltpu.SemaphoreType.DMA],
        compiler_params=_params(("arbitrary",)),
        name="invert_slots",
    )(dest3, jnp.zeros((N_SLOTS,), jnp.int32))


def _dispatch_plan(info, cnt):
    e = info[:, 0:2].astype(jnp.int32)
    rank = info[:, 4:6].astype(jnp.int32)
    counts = cnt[0, EXPERT_LANE0:EXPERT_LANE0 + N_EXPERTS].astype(jnp.int32)
    padded = ((counts + TM_MOE - 1) // TM_MOE) * TM_MOE
    pends = jnp.cumsum(padded)
    pstarts = pends - padded
    first = jnp.sum(jnp.where(e[..., None] == jnp.arange(N_EXPERTS, dtype=jnp.int32), pstarts, 0), axis=-1)
    dest = first + rank
    codes = _invert_slots(dest.reshape(2 * T // INV_ROWS, 1, INV_ROWS))
    src = (codes >> 1) * TT_SUB
    block0 = jnp.arange(N_BLOCKS, dtype=jnp.int32) * TM_MOE
    block_e = jnp.clip(jnp.searchsorted(pends, block0, side='right', method='compare_all'),
                       0, N_EXPERTS - 1).astype(jnp.int32)
    nvalid = jnp.clip(pstarts[block_e] + counts[block_e] - block0, 0, TM_MOE).astype(jnp.int32)
    nused = (pends[-1] // TM_MOE).astype(jnp.int32).reshape(1)
    return (src, block_e, nused, nvalid), dest * TT_SUB


def _combined(x_ref, y0_ref, y1_ref, info_ref, g2_ref, rows):
    info = info_ref[...]
    lane = lax.broadcasted_iota(jnp.int32, info.shape, 1)
    w1 = jnp.sum(jnp.where(lane == 2, info, 0.0), axis=-1, keepdims=True)
    w2 = jnp.sum(jnp.where(lane == 3, info, 0.0), axis=-1, keepdims=True)
    y0 = _load_tt(y0_ref, rows, F32)
    y1 = _load_tt(y1_ref, rows, F32)
    return x_ref[...] + g2_ref[...] * (y0 * w1 + y1 * w2)


def _combine_body(d_cur, d_nxt, x_ref, ys_hbm, info_ref, g2_ref, nw_ref, *rest, final):
    o_refs, (ybuf, sem) = rest[:-2], rest[-2:]
    i = pl.program_id(0)
    s = i % 2
    half = TM_CMB * TT_SUB

    def tile(ref, row8):
        return ref.at[pl.ds(pl.multiple_of(row8, TT_SUB), TT_SUB)]

    def gather(d_ref, slot):
        def row(r):
            for k in range(2):
                pltpu.make_async_copy(tile(ys_hbm, d_ref[0, 2 * r + k]),
                                      tile(ybuf.at[slot], k * half + r * TT_SUB), sem.at[slot]).start()

        _for_rows(TM_CMB, row)

    @pl.when(i == 0)
    def _():
        gather(d_cur, 0)

    pltpu.make_async_copy(ys_hbm.at[pl.ds(0, 2 * half)], ybuf.at[s], sem.at[s]).wait()

    @pl.when(i + 1 < T // TM_CMB)
    def _():
        gather(d_nxt, 1 - s)

    x = _combined(x_ref, ybuf.at[s, pl.ds(0, half)], ybuf.at[s, pl.ds(half, half)], info_ref, g2_ref, TM_CMB)
    if final:
        ms = jnp.mean(x * x, axis=-1, keepdims=True)
        x = x * lax.rsqrt(ms + EPS) * nw_ref[...]

    def emit(ref):
        ref[...] = x

    _for_tile(o_refs, TM_CMB, emit)


def _combine(x1, ys, dest8, info, mods, nw, *, final):
    tm = TM_CMB
    nt = T // tm
    if final:
        out_shape = [jax.ShapeDtypeStruct((T_S, D), F32), jax.ShapeDtypeStruct((T_P, D), F32)]
    else:
        out_shape = [jax.ShapeDtypeStruct((T, D), F32)]
    dest3 = dest8.reshape(nt, 1, 2 * tm)
    dspec = lambda f: pl.BlockSpec((None, 1, 2 * tm), lambda i: (f(i), 0, 0), memory_space=pltpu.SMEM)
    return pl.pallas_call(
        functools.partial(_combine_body, final=final),
        grid=(nt,),
        in_specs=[dspec(lambda i: i), dspec(lambda i: jnp.minimum(i + 1, nt - 1)),
                  pl.BlockSpec((tm, D), lambda i: (i, 0)),
                  pl.BlockSpec(memory_space=pl.ANY),
                  pl.BlockSpec((tm, LANES), lambda i: (i, 0)),
                  _mod_spec(5, tm, 1),
                  pl.BlockSpec((1, D), lambda i: (0, 0))],
        out_specs=_row_specs(out_shape, tm, 1),
        out_shape=out_shape,
        scratch_shapes=[pltpu.VMEM((2, 2 * tm * TT_SUB, LANES), U32), pltpu.SemaphoreType.DMA((2,))],
        compiler_params=_params(("arbitrary",)),
        name="combine_final" if final else "combine",
    )(dest3, dest3, x1, ys, info, mods, nw.reshape(1, D))


def kernel(x_prompt, x_sample, c, state_ret_fwd, state_ret_bwd, state_lru_fwd, state_lru_bwd, c_ctx, w_mod, b_mod, norm_mix_w, norm_ffn_w, norm_final_w, w_in_even, w_out_even, gmlp_norm_w, gmlp_w_s, gmlp_b_s, ret_decay_fwd, ret_decay_bwd, ret_gn_w, w_in_odd, w_out_odd, conv_w, conv_b, conv_ln_w, conv_ln_b, lru_conv_w, lru_conv_b, lru_wa_fwd, lru_ba_fwd, lru_wx_fwd, lru_bx_fwd, lru_lam_fwd, lru_wa_bwd, lru_ba_bwd, lru_wx_bwd, lru_bx_bwd, lru_lam_bwd, router_grp_w, router_grp_b, router_exp_w, router_exp_b, moe_w_gate, moe_w_up, moe_w_down):
    xs = (x_sample.reshape(T_S, D), x_prompt.reshape(T_P, D))
    cond = jnp.concatenate([c, c_ctx[None, :], jnp.zeros((N_COND_PAD - B_S - 1, D), F32)], axis=0)
    m = _modulation(cond, w_mod, b_mod)
    mods_all = m.reshape(DEPTH, N_COND_PAD, N_MOD, D).transpose(0, 2, 1, 3)[:, :, :, None, :]

    ret_f = ret_b = lru_f = lru_b = None
    for l in range(DEPTH):
        mods = mods_all[l]
        if l % 2 == 0:
            e = l // 2
            p = _norm_in(xs, norm_mix_w[l], mods, w_in_even[e].astype(BF16), pack=False)
            b_full = jnp.repeat(gmlp_b_s[e].T.astype(F32), A_GDIM, axis=1)
            out_a = (_gmlp(p, gmlp_norm_w[e], gmlp_w_s[e].astype(BF16), b_full),)
            tabs = _retention_tables(ret_decay_fwd[e], ret_decay_bwd[e])
            cos, sin = _rope_tables()
            ob_s, _, _ = _retention(p, state_ret_fwd[:, e], state_ret_bwd[:, e], tabs, ret_gn_w[e], cos, sin,
                                    base_chunk=0, nb=B_S, nc=N_S // CHUNK, rope=True)
            zero_state = jnp.zeros((B_P, HEADS, HD, HD), F32)
            ob_p, ret_f, ret_b = _retention(p, zero_state, zero_state, tabs, ret_gn_w[e], cos, sin,
                                            base_chunk=T_S // CHUNK, nb=B_P, nc=N_P // CHUNK, rope=False)
            out_b = (ob_s, ob_p)
            w_out = w_out_even[e]
        else:
            o = l // 2
            p = _norm_in(xs, norm_mix_w[l], mods, w_in_odd[o].astype(BF16), pack=True)
            cw3 = jnp.concatenate([conv_w[o], jnp.zeros((1, W), F32)], axis=0).reshape(
                C_KSIZE + 1, N_STRIPS, LANES).transpose(1, 0, 2)
            conv_args = (cw3, conv_b[o], conv_ln_w[o], conv_ln_b[o])
            prm_f = _lru_params(lru_conv_w[o], lru_conv_b[o], lru_wa_fwd[o], lru_ba_fwd[o], lru_wx_fwd[o],
                                lru_bx_fwd[o], lru_lam_fwd[o])
            prm_b = _lru_params(lru_conv_w[o], lru_conv_b[o], lru_wa_bwd[o], lru_ba_bwd[o], lru_wx_bwd[o],
                                lru_bx_bwd[o], lru_lam_bwd[o])
            latent = dict(seq_row0=0, nseq=B_S, n=N_S)
            context = dict(seq_row0=T_S, nseq=B_P, n=N_P)
            out_a = (_od_conv(p, *conv_args, **latent), _od_conv(p, *conv_args, **context))
            yd_s, _, _ = _od_rglru(p, prm_f, prm_b, state_lru_fwd[:, o], state_lru_bwd[:, o], **latent)
            zero_h = jnp.zeros((B_P, W), F32)
            yd_p, lru_f, lru_b = _od_rglru(p, prm_f, prm_b, zero_h, zero_h, **context)
            out_b = (yd_s, yd_p)
            w_out = w_out_odd[o]

        rw = jnp.concatenate([router_grp_w[l], router_exp_w[l],
                              jnp.zeros((D, LANES - N_GROUPS - N_EXPERTS), F32)], axis=1)
        rw_hi = rw.astype(BF16)
        rw = jnp.concatenate([rw_hi, (rw - rw_hi.astype(F32)).astype(BF16)], axis=1)
        rb = jnp.concatenate([router_grp_b[l], router_exp_b[l],
                              jnp.zeros((LANES - N_GROUPS - N_EXPERTS,), F32)]).reshape(1, LANES)
        x1, hf, info, cnt = _out_proj(out_a, out_b, xs, mods, w_out.astype(BF16), norm_ffn_w[l], rw, rb,
                                      packed=(l % 2 == 1))
        plan, dest8 = _dispatch_plan(info, cnt)
        ys = _moe(hf, plan, moe_w_gate, moe_w_up, moe_w_down, l)
        xs = tuple(_combine(x1, ys, dest8, info, mods, norm_final_w, final=(l == DEPTH - 1)))

    y_sample = xs[0].reshape(B_S, N_S, D)
    y_prompt = xs[1].reshape(B_P, N_P, D)
    return (y_prompt, y_sample, ret_f[:, None], ret_b[:, None], lru_f[:, None], lru_b[:, None])
```

```python
import functools

import jax
import jax.numpy as jnp
from jax import lax
from jax.experimental import pallas as pl
from jax.experimental.pallas import tpu as pltpu

F32 = jnp.float32
BF16 = jnp.bfloat16
U32 = jnp.uint32

D = 2048
B_P, N_P = 16, 256
B_S, N_S = 8, 4096
T_S = B_S * N_S
T_P = B_P * N_P
T = T_S + T_P
ROWS_PER_COND = 4096
N_COND_PAD = 16
DEPTH = 2
N_MOD = 6
EPS = 1e-6
GRID_W = 64
ROPE_BASE = 10000.0

W = 1024
HW = W // 2
HEADS = 8
HD = 128
CHUNK = 128
A_GROUPS = 4
A_GDIM = W // A_GROUPS
C_KSIZE = 31
D_KSIZE = 4
LRU_C = 8.0

N_GROUPS = 4
EXPERTS_PER_GROUP = 8
N_EXPERTS = 32
D_EXPERT = 512
LANES = 128
N_STRIPS = W // LANES
EXPERT_LANE0 = N_GROUPS

TM_IN = 512
TN_IN = 2048
TM_OUT = 512
TM_MOE = 512
N_SLOTS = 2 * T + N_EXPERTS * TM_MOE
N_BLOCKS = N_SLOTS // TM_MOE
TM_CMB = 512

VMEM_LIMIT = 56 * 1024 * 1024
MOE_VMEM_LIMIT = 60 * 1024 * 1024


def _params(sem, vmem=VMEM_LIMIT):
    return pltpu.CompilerParams(dimension_semantics=sem, vmem_limit_bytes=vmem)


def _cond_row(i, tm):
    return (i * tm) // ROWS_PER_COND


def _sigmoid(x):
    return 0.5 * jnp.tanh(0.5 * x) + 0.5


def _pack_pair(lo, hi, rounded=False):
    if not rounded:
        lo, hi = lo.astype(BF16).astype(F32), hi.astype(BF16).astype(F32)
    return (lax.bitcast_convert_type(lo, U32) >> 16) | lax.bitcast_convert_type(hi, U32)


def _unpack_pair(w, dtype):
    return (lax.bitcast_convert_type(w << 16, F32).astype(dtype),
            lax.bitcast_convert_type(w & U32(0xFFFF0000), F32).astype(dtype))


def _pack_halves(x):
    k = x.shape[1] // 2
    return _pack_pair(x[:, :k], x[:, k:])


def _unpack_halves(w, dtype):
    return jnp.concatenate(_unpack_pair(w, dtype), axis=1)


TT_SUB = 8


def _store_tt(ref, x, rows, rounded=False):
    for c in range(TT_SUB):
        ref[pl.ds(c, rows, stride=TT_SUB), :] = _pack_pair(x[:, c * LANES:(c + 1) * LANES],
                                                           x[:, W + c * LANES:W + (c + 1) * LANES], rounded)


def _load_tt(ref, rows, dtype):
    lo, hi = [], []
    for c in range(TT_SUB):
        a, b = _unpack_pair(ref[pl.ds(c, rows, stride=TT_SUB), :], dtype)
        lo.append(a)
        hi.append(b)
    return jnp.concatenate(lo + hi, axis=1)


def _row_specs(arrs, tm, ngrid):
    if len(arrs) == 1:
        maps = [lambda i: i]
    else:
        n_s = T_S // tm
        maps = [lambda i: jnp.minimum(i, n_s - 1), lambda i: jnp.maximum(i - n_s, 0)]
    width = arrs[0].shape[1]
    if ngrid == 1:
        return [pl.BlockSpec((tm, width), lambda i, f=f: (f(i), 0)) for f in maps]
    return [pl.BlockSpec((tm, width), lambda i, j, f=f: (f(i), 0)) for f in maps]


def _tile_value(refs, tm):
    if len(refs) == 1:
        return refs[0][...]
    return jnp.where(pl.program_id(0) < T_S // tm, refs[0][...], refs[1][...])


def _for_tile(refs, tm, fn):
    if len(refs) == 1:
        fn(refs[0])
        return
    i = pl.program_id(0)

    @pl.when(i < T_S // tm)
    def _():
        fn(refs[0])

    @pl.when(i >= T_S // tm)
    def _():
        fn(refs[1])


def _mod_body(c_ref, w_ref, b_ref, o_ref):
    c = c_ref[...]
    a = (c * _sigmoid(c)).astype(BF16)
    o_ref[...] = jnp.dot(a, w_ref[...].astype(BF16), preferred_element_type=F32) + b_ref[...]


def _modulation(cond, w_mod, b_mod):
    tn = 1024
    return pl.pallas_call(
        _mod_body,
        grid=(DEPTH, N_MOD * D // tn),
        in_specs=[pl.BlockSpec((N_COND_PAD, D), lambda l, j: (0, 0)),
                  pl.BlockSpec((None, D, tn), lambda l, j: (l, 0, j)),
                  pl.BlockSpec((None, 1, tn), lambda l, j: (l, 0, j))],
        out_specs=pl.BlockSpec((None, N_COND_PAD, tn), lambda l, j: (l, 0, j)),
        out_shape=jax.ShapeDtypeStruct((DEPTH, N_COND_PAD, N_MOD * D), F32),
        compiler_params=_params(("arbitrary", "arbitrary")),
        name="modulation",
    )(cond, w_mod, b_mod.reshape(DEPTH, 1, N_MOD * D))


def _mod_spec(k, tm, ngrid):
    if ngrid == 1:
        return pl.BlockSpec((None, None, 1, D), lambda i: (k, _cond_row(i, tm), 0, 0))
    return pl.BlockSpec((None, None, 1, D), lambda i, j: (k, _cond_row(i, tm), 0, 0))


def _norm_in_body(*refs, pack):
    nw_ref, sc_ref, sh_ref, w_ref, p_ref, h_scr = refs[-6:]

    @pl.when(pl.program_id(1) == 0)
    def _():
        x = _tile_value(refs[:-6], TM_IN)
        ms = jnp.mean(x * x, axis=-1, keepdims=True)
        h = x * lax.rsqrt(ms + EPS) * nw_ref[...]
        h = h * (1.0 + sc_ref[...]) + sh_ref[...]
        h_scr[...] = h.astype(BF16)

    r = jnp.dot(h_scr[...], w_ref[...], preferred_element_type=F32)
    if pack:
        p_ref[...] = jnp.concatenate([_pack_halves(r[:, k * W:(k + 1) * W]) for k in range(TN_IN // W)], axis=1)
    else:
        p_ref[...] = r.astype(BF16)


def _norm_in(xs, nw, mods, w_bf16, *, pack):
    n = w_bf16.shape[1]
    if pack:
        out_spec = pl.BlockSpec((TM_IN, TN_IN // 2), lambda i, j: (i, j))
        out_shape = jax.ShapeDtypeStruct((T, n // 2), U32)
    else:
        out_spec = pl.BlockSpec((TM_IN, TN_IN), lambda i, j: (i, j))
        out_shape = jax.ShapeDtypeStruct((T, n), BF16)
    return pl.pallas_call(
        functools.partial(_norm_in_body, pack=pack),
        grid=(T // TM_IN, n // TN_IN),
        in_specs=_row_specs(xs, TM_IN, 2) + [
            pl.BlockSpec((1, D), lambda i, j: (0, 0)),
            _mod_spec(1, TM_IN, 2),
            _mod_spec(0, TM_IN, 2),
            pl.BlockSpec((D, TN_IN), lambda i, j: (0, j))],
        out_specs=out_spec,
        out_shape=out_shape,
        scratch_shapes=[pltpu.VMEM((TM_IN, D), BF16)],
        compiler_params=_params(("arbitrary", "arbitrary")),
        name="norm_in",
    )(*xs, nw.reshape(1, D), mods, mods, w_bf16)


GMLP_ROWS = 512


def _gmlp_body(u_ref, v_ref, lnw_ref, ws_ref, b_ref, o_ref):
    for c in range(GMLP_ROWS // CHUNK):
        rows = pl.ds(c * CHUNK, CHUNK)
        v = jax.nn.gelu(v_ref[rows, :].astype(F32))
        mu = jnp.mean(v, axis=-1, keepdims=True)
        var = jnp.mean(jnp.square(v - mu), axis=-1, keepdims=True)
        vn = ((v - mu) * lax.rsqrt(var + EPS) * lnw_ref[...]).astype(BF16)
        z = jnp.concatenate(
            [jnp.dot(ws_ref[g], vn[:, g * A_GDIM:(g + 1) * A_GDIM], preferred_element_type=F32)
             for g in range(A_GROUPS)], axis=1) + b_ref[...]
        u = jax.nn.gelu(u_ref[rows, :].astype(F32))
        o_ref[rows, :] = (u * z).astype(BF16)


def _gmlp(p, lnw, ws_bf16, b_full):
    return pl.pallas_call(
        _gmlp_body,
        grid=(T // GMLP_ROWS,),
        in_specs=[pl.BlockSpec((GMLP_ROWS, W), lambda i: (i, 0)),
                  pl.BlockSpec((GMLP_ROWS, W), lambda i: (i, 1)),
                  pl.BlockSpec((1, W), lambda i: (0, 0)),
                  pl.BlockSpec((A_GROUPS, CHUNK, CHUNK), lambda i: (0, 0, 0)),
                  pl.BlockSpec((CHUNK, W), lambda i: (0, 0))],
        out_specs=pl.BlockSpec((GMLP_ROWS, W), lambda i: (i, 0)),
        out_shape=jax.ShapeDtypeStruct((T, W), BF16),
        compiler_params=_params(("arbitrary",)),
        name="gmlp",
    )(p, p, lnw.reshape(1, W), ws_bf16, b_full)


RET_SUB = 2
RET_ROWS = RET_SUB * CHUNK


def _roped_qk(q_ref, k_ref, cos_ref, sin_ref, rows, h, rope):
    cols = slice(h * HD, (h + 1) * HD)
    q = q_ref[rows, cols].astype(F32)
    k = k_ref[rows, cols].astype(F32) * (HD ** -0.5)
    if rope:
        c = cos_ref[rows, :]
        s = sin_ref[rows, :]
        q = q * c + pltpu.roll(q, HD // 2, axis=1) * s
        k = k * c + pltpu.roll(k, HD // 2, axis=1) * s
    return q, k


def _kt_v(kd, v):
    return lax.dot_general(kd, v, (((0,), (0,)), ((), ())), preferred_element_type=F32)


def _ret_bwd_body(q_ref, k_ref, v_ref, cos_ref, sin_ref, s0_ref, dq_ref, dk_ref, dc_ref,
                  ob_ref, sfin_ref, s_scr, *, rope, nc):
    c = pl.program_id(1)

    @pl.when(c == 0)
    def _():
        s_scr[...] = s0_ref[...]

    for sub in reversed(range(RET_SUB)):
        rows = slice(sub * CHUNK, (sub + 1) * CHUNK)
        for h in range(HEADS):
            cols = slice(h * HD, (h + 1) * HD)
            q, k = _roped_qk(q_ref, k_ref, cos_ref, sin_ref, rows, h, rope)
            v = v_ref[rows, cols]
            s = s_scr[h]
            ob_ref[rows, cols] = jnp.dot(q.astype(BF16), s.astype(BF16), preferred_element_type=F32) * dq_ref[h]
            kd = (k * dk_ref[h]).astype(BF16)
            s_scr[h] = s * dc_ref[h] + _kt_v(kd, v)

    @pl.when(c == nc - 1)
    def _():
        sfin_ref[...] = s_scr[...]


def _ret_fwd_body(q_ref, k_ref, v_ref, g_ref, cos_ref, sin_ref, ob_ref, s0_ref, m_ref, dq_ref, dk_ref, dc_ref,
                  gn_ref, o_ref, sfin_ref, s_scr, *, rope, nc):
    c = pl.program_id(1)

    @pl.when(c == 0)
    def _():
        s_scr[...] = s0_ref[...]

    for sub in range(RET_SUB):
        rows = slice(sub * CHUNK, (sub + 1) * CHUNK)
        for h in range(HEADS):
            cols = slice(h * HD, (h + 1) * HD)
            q, k = _roped_qk(q_ref, k_ref, cos_ref, sin_ref, rows, h, rope)
            qb = q.astype(BF16)
            v = v_ref[rows, cols]
            s = s_scr[h]
            scores = lax.dot_general(qb, k.astype(BF16), (((1,), (1,)), ((), ())),
                                     preferred_element_type=F32) * m_ref[h]
            o = (jnp.dot(scores.astype(BF16), v, preferred_element_type=F32)
                 + jnp.dot(qb, s.astype(BF16), preferred_element_type=F32) * dq_ref[h]
                 + ob_ref[rows, cols])
            kd = (k * dk_ref[h]).astype(BF16)
            s_scr[h] = s * dc_ref[h] + _kt_v(kd, v)
            mu = jnp.mean(o, axis=-1, keepdims=True)
            var = jnp.mean(jnp.square(o - mu), axis=-1, keepdims=True)
            on = (o - mu) * lax.rsqrt(var + EPS) * gn_ref[:, cols]
            g = g_ref[rows, cols].astype(F32)
            o_ref[rows, cols] = (g * _sigmoid(g) * on).astype(BF16)

    @pl.when(c == nc - 1)
    def _():
        sfin_ref[...] = s_scr[...]


def _retention(p, s_f0, s_b0, tabs, gn_w, cos, sin, *, base_chunk, nb, nc, rope):
    m_tab, dq_f, dk_f, dc_f, dq_b, dk_b, dc_b = tabs
    ns = nc // RET_SUB
    base = base_chunk // RET_SUB
    state_spec = pl.BlockSpec((None, HEADS, HD, HD), lambda b, c: (b, 0, 0, 0))
    tab_spec = pl.BlockSpec((HEADS, HD, HD), lambda b, c: (0, 0, 0))
    dc_spec = pl.BlockSpec((HEADS, 1, HD), lambda b, c: (0, 0, 0))
    state_shape = jax.ShapeDtypeStruct((nb, HEADS, HD, HD), F32)

    def step(c, rev):
        return ns - 1 - c if rev else c

    def col(j, rev):
        return pl.BlockSpec((RET_ROWS, W), lambda b, c: (base + b * ns + step(c, rev), j))

    def rope_spec(rev):
        if not rope:
            return pl.BlockSpec((RET_ROWS, HD), lambda b, c: (0, 0))
        return pl.BlockSpec((RET_ROWS, HD), lambda b, c: (step(c, rev), 0))

    def local(rev):
        return pl.BlockSpec((RET_ROWS, W), lambda b, c: (b * ns + step(c, rev), 0))

    n_rows = nb * nc * CHUNK
    ob, s_b = pl.pallas_call(
        functools.partial(_ret_bwd_body, rope=rope, nc=ns),
        grid=(nb, ns),
        in_specs=[col(2, True), col(3, True), col(4, True), rope_spec(True), rope_spec(True),
                  state_spec, tab_spec, tab_spec, dc_spec],
        out_specs=[local(True), state_spec],
        out_shape=[jax.ShapeDtypeStruct((n_rows, W), F32), state_shape],
        scratch_shapes=[pltpu.VMEM((HEADS, HD, HD), F32)],
        compiler_params=_params(("arbitrary", "arbitrary")),
        name="retention_bwd",
    )(p, p, p, cos, sin, s_b0, dq_b, dk_b, dc_b)

    o, s_f = pl.pallas_call(
        functools.partial(_ret_fwd_body, rope=rope, nc=ns),
        grid=(nb, ns),
        in_specs=[col(2, False), col(3, False), col(4, False), col(5, False), rope_spec(False), rope_spec(False),
                  local(False), state_spec, tab_spec, tab_spec, tab_spec, dc_spec,
                  pl.BlockSpec((1, W), lambda b, c: (0, 0))],
        out_specs=[local(False), state_spec],
        out_shape=[jax.ShapeDtypeStruct((n_rows, W), BF16), state_shape],
        scratch_shapes=[pltpu.VMEM((HEADS, HD, HD), F32)],
        compiler_params=_params(("arbitrary", "arbitrary")),
        name="retention_fwd",
    )(p, p, p, p, cos, sin, ob, s_f0, m_tab, dq_f, dk_f, dc_f, gn_w.reshape(1, W))
    return o, s_f, s_b


def _retention_tables(decay_fwd, decay_bwd):
    lg_f = jax.nn.log_sigmoid(decay_fwd.astype(F32))[:, None, None]
    lg_b = jax.nn.log_sigmoid(decay_bwd.astype(F32))[:, None, None]
    pos = jnp.arange(CHUNK, dtype=F32)
    rel = pos[:, None] - pos[None, :]
    m_tab = (jnp.where(rel >= 0, jnp.exp(lg_f * jnp.maximum(rel, 0.0)), 0.0)
             + jnp.where(rel <= 0, jnp.exp(lg_b * jnp.maximum(-rel, 0.0)), 0.0))
    ones = jnp.ones((1, 1, HD), F32)
    col = pos[None, :, None]
    dq_f = jnp.exp(lg_f * (col + 1.0)) * ones
    dk_f = jnp.exp(lg_f * (CHUNK - 1.0 - col)) * ones
    dq_b = jnp.exp(lg_b * (CHUNK - col)) * ones
    dk_b = jnp.exp(lg_b * col) * ones
    dc_f = jnp.exp(lg_f * CHUNK) * ones
    dc_b = jnp.exp(lg_b * CHUNK) * ones
    return m_tab, dq_f, dk_f, dc_f, dq_b, dk_b, dc_b


def _rope_tables():
    rows = N_S // GRID_W
    row = jnp.repeat(jnp.arange(rows, dtype=F32), GRID_W)
    colp = jnp.tile(jnp.arange(GRID_W, dtype=F32), rows)
    quarter = HD // 4
    inv_freq = ROPE_BASE ** (-jnp.arange(quarter, dtype=F32) / quarter)
    ang = jnp.concatenate([row[:, None] * inv_freq, colp[:, None] * inv_freq], axis=-1)
    cos, sin = jnp.cos(ang), jnp.sin(ang)
    return jnp.concatenate([cos, cos], axis=-1), jnp.concatenate([-sin, sin], axis=-1)


OD_TT = 128
OD_G = 8
CONV_HALO = 16
LRU_HALO = 8


def _od_rows(seq_row0, n, g, tj):
    return [seq_row0 + (g * OD_G + b) * n + tj * OD_TT for b in range(OD_G)]


def _od_in_copies(p_hbm, buf, sem, *, col, halo, seq_row0, n, g, tj):
    nt = n // OD_TT
    out = []
    for b, base in enumerate(_od_rows(seq_row0, n, g, tj)):
        def src(r, k):
            return p_hbm.at[pl.ds(r, k), pl.ds(col * HW, HW)]

        out.append((None, pltpu.make_async_copy(src(base, OD_TT), buf.at[pl.ds(halo, OD_TT), b, :], sem)))
        if halo:
            out.append((tj > 0, pltpu.make_async_copy(src(base - halo, halo), buf.at[pl.ds(0, halo), b, :], sem)))
            out.append((tj < nt - 1, pltpu.make_async_copy(src(base + OD_TT, halo),
                                                           buf.at[pl.ds(halo + OD_TT, halo), b, :], sem)))
    return out


def _od_run(copies, op):
    for cond, cp in copies:
        fn = cp.start if op == "start" else cp.wait
        if cond is None:
            fn()
        else:
            pl.when(cond)(fn)


def _od_zero_halo(buf, halo, n, tj):
    nt = n // OD_TT
    zeros = jnp.zeros((halo, OD_G, HW), U32)

    @pl.when(tj == 0)
    def _():
        buf[0:halo] = zeros

    @pl.when(tj == nt - 1)
    def _():
        buf[halo + OD_TT:2 * halo + OD_TT] = zeros


def _od_out_copies(obuf, out_hbm, sem, *, seq_row0, n, g, tj):
    return [(None, pltpu.make_async_copy(obuf.at[:, b, :], out_hbm.at[pl.ds(base - seq_row0, OD_TT), :], sem))
            for b, base in enumerate(_od_rows(seq_row0, n, g, tj))]


def _od_unpack(buf, rows):
    return _unpack_halves(buf[...].reshape(rows, HW), F32)


def _od_pipeline(q, nq, in_copies, sem_slots=2):
    s = q % 2

    @pl.when(q == 0)
    def _():
        _od_run(in_copies(q, s), "start")

    _od_run(in_copies(q, s), "wait")

    @pl.when(q + 1 < nq)
    def _():
        _od_run(in_copies(q + 1, 1 - s), "start")

    return s


def _od_emit(q, nq, s, obuf, words, out_copies):
    @pl.when(q >= 2)
    def _():
        _od_run(out_copies(s), "wait")

    obuf[s] = words.reshape(OD_TT, OD_G, HW)
    _od_run(out_copies(s), "start")

    @pl.when(q == nq - 1)
    def _():
        _od_run(out_copies(s), "wait")

        @pl.when(q >= 1)
        def _():
            _od_run(out_copies(1 - s), "wait")


CONV_RB = 64


def _od_conv_body(p_hbm, cw_ref, cb_ref, lnw_ref, lnb_ref, yc_hbm, xin, glu, ybuf, obuf, sem_in, sem_out,
                  *, seq_row0, n, ng):
    nt = n // OD_TT
    nq = ng * nt
    q = pl.program_id(0)
    halo = CONV_HALO
    rows_in = (OD_TT + 2 * halo) * OD_G
    rows = OD_TT * OD_G

    def in_copies(step, slot):
        out = []
        for br in range(2):
            out += _od_in_copies(p_hbm, xin.at[br, slot], sem_in.at[slot], col=br, halo=halo,
                                 seq_row0=seq_row0, n=n, g=step // nt, tj=step % nt)
        return out

    s = _od_pipeline(q, nq, in_copies)
    g, tj = q // nt, q % nt
    for br in range(2):
        _od_zero_halo(xin.at[br, s], halo, n, tj)
    glu[...] = _od_unpack(xin.at[0, s], rows_in) * _sigmoid(_od_unpack(xin.at[1, s], rows_in))

    def strip(si, carry):
        lanes = pl.ds(pl.multiple_of(si * LANES, LANES), LANES)
        for rb in range(rows // CONV_RB):
            acc = jnp.zeros((CONV_RB // 8, 8, LANES), F32)
            for k in range(C_KSIZE):
                wk = cw_ref[si, pl.ds(k, 1), :]
                r0 = rb * CONV_RB + (halo - C_KSIZE // 2 + k) * OD_G
                acc = acc + glu[r0:r0 + CONV_RB, lanes].reshape(CONV_RB // 8, 8, LANES) * wk[None]
            ybuf[rb * CONV_RB:(rb + 1) * CONV_RB, lanes] = acc.reshape(CONV_RB, LANES)
        return carry

    lax.fori_loop(0, N_STRIPS, strip, 0)
    y = ybuf[...] + cb_ref[...]
    mu = jnp.mean(y, axis=-1, keepdims=True)
    var = jnp.mean(jnp.square(y - mu), axis=-1, keepdims=True)
    yn = (y - mu) * lax.rsqrt(var + EPS) * lnw_ref[...] + lnb_ref[...]
    yo = yn * _sigmoid(yn)

    def out_copies(slot):
        return _od_out_copies(obuf.at[slot], yc_hbm, sem_out.at[slot], seq_row0=seq_row0, n=n, g=g, tj=tj)

    _od_emit(q, nq, s, obuf, _pack_halves(yo), out_copies)


def _od_conv(p, cw3, cb, lnw, lnb, *, seq_row0, nseq, n):
    ng = nseq // OD_G
    nq = ng * (n // OD_TT)
    vec = pl.BlockSpec((1, W), lambda q: (0, 0))
    t_in = OD_TT + 2 * CONV_HALO
    return pl.pallas_call(
        functools.partial(_od_conv_body, seq_row0=seq_row0, n=n, ng=ng),
        grid=(nq,),
        in_specs=[pl.BlockSpec(memory_space=pl.ANY),
                  pl.BlockSpec((N_STRIPS, C_KSIZE + 1, LANES), lambda q: (0, 0, 0)), vec, vec, vec],
        out_specs=pl.BlockSpec(memory_space=pl.ANY),
        out_shape=jax.ShapeDtypeStruct((nseq * n, HW), U32),
        scratch_shapes=[pltpu.VMEM((2, 2, t_in, OD_G, HW), U32),
                        pltpu.VMEM((t_in * OD_G, W), F32),
                        pltpu.VMEM((OD_TT * OD_G, W), F32),
                        pltpu.VMEM((2, OD_TT, OD_G, HW), U32),
                        pltpu.SemaphoreType.DMA((2,)), pltpu.SemaphoreType.DMA((2,))],
        compiler_params=_params(("arbitrary",)),
        name="conv_module",
    )(p, cw3, cb.reshape(1, W), lnw.reshape(1, W), lnb.reshape(1, W))


def _od_gates(x, cw_ref, cb_ref, wa_ref, ba_ref, wx_ref, bx_ref, sp_ref, a_scr, b_scr):
    rows = OD_TT * OD_G
    xc = cb_ref[...]
    for k in range(D_KSIZE):
        r0 = (LRU_HALO - D_KSIZE // 2 + k) * OD_G
        xc = xc + cw_ref[k:k + 1, :] * x[r0:r0 + rows, :]
    xb = xc.astype(BF16)
    r = jnp.concatenate([jnp.dot(xb[:, h * HD:(h + 1) * HD], wa_ref[h], preferred_element_type=F32)
                         for h in range(HEADS)], axis=1) + ba_ref[...]
    g = jnp.concatenate([jnp.dot(xb[:, h * HD:(h + 1) * HD], wx_ref[h], preferred_element_type=F32)
                         for h in range(HEADS)], axis=1) + bx_ref[...]
    log_a = (-LRU_C) * _sigmoid(r) * sp_ref[...]
    th = jnp.tanh(log_a)
    one_minus_a2 = -2.0 * th / (1.0 - th)
    a_scr[...] = jnp.exp(log_a)
    b_scr[...] = jnp.sqrt(jnp.maximum(one_minus_a2, 0.0)) * (_sigmoid(g) * xc)


def _od_scan(a_scr, b_scr, out_ref, h, reverse):
    def step(t, h):
        tr = (OD_TT - 1 - t) if reverse else t
        rows = pl.ds(pl.multiple_of(tr * OD_G, OD_G), OD_G)
        h = a_scr[rows, :] * h + b_scr[rows, :]
        out_ref[rows, :] = h
        return h

    return lax.fori_loop(0, OD_TT, step, h, unroll=8)


def _od_lru_fwd_body(p_hbm, cw_ref, cb_ref, wa_ref, ba_ref, wx_ref, bx_ref, sp_ref, h0_ref,
                     hf_ref, hfin_ref, xin, a_scr, b_scr, h_scr, sem_in, *, seq_row0, n, ng):
    nt = n // OD_TT
    nq = ng * nt
    q = pl.program_id(0)

    def in_copies(step, slot):
        return _od_in_copies(p_hbm, xin.at[slot], sem_in.at[slot], col=2, halo=LRU_HALO,
                             seq_row0=seq_row0, n=n, g=step // nt, tj=step % nt)

    s = _od_pipeline(q, nq, in_copies)
    tj = q % nt
    _od_zero_halo(xin.at[s], LRU_HALO, n, tj)

    @pl.when(tj == 0)
    def _():
        h_scr[...] = h0_ref[...]

    x = _od_unpack(xin.at[s], (OD_TT + 2 * LRU_HALO) * OD_G)
    _od_gates(x, cw_ref, cb_ref, wa_ref, ba_ref, wx_ref, bx_ref, sp_ref, a_scr, b_scr)
    h = _od_scan(a_scr, b_scr, hf_ref, h_scr[...], False)
    h_scr[...] = h

    @pl.when(tj == nt - 1)
    def _():
        hfin_ref[...] = h


def _od_lru_bwd_body(p_hbm, hf_ref, cw_ref, cb_ref, wa_ref, ba_ref, wx_ref, bx_ref, sp_ref, h0_ref,
                     yd_hbm, hfin_ref, xin, gin, a_scr, b_scr, hb_scr, obuf, h_scr, sem_in, sem_out,
                     *, seq_row0, n, ng):
    nt = n // OD_TT
    nq = ng * nt
    q = pl.program_id(0)

    def in_copies(step, slot):
        where = dict(seq_row0=seq_row0, n=n, g=step // nt, tj=nt - 1 - step % nt)
        return (_od_in_copies(p_hbm, xin.at[slot], sem_in.at[slot], col=2, halo=LRU_HALO, **where)
                + _od_in_copies(p_hbm, gin.at[slot], sem_in.at[slot], col=3, halo=0, **where))

    s = _od_pipeline(q, nq, in_copies)
    g, tj = q // nt, nt - 1 - q % nt
    _od_zero_halo(xin.at[s], LRU_HALO, n, tj)

    @pl.when(tj == nt - 1)
    def _():
        h_scr[...] = h0_ref[...]

    x = _od_unpack(xin.at[s], (OD_TT + 2 * LRU_HALO) * OD_G)
    _od_gates(x, cw_ref, cb_ref, wa_ref, ba_ref, wx_ref, bx_ref, sp_ref, a_scr, b_scr)
    h = _od_scan(a_scr, b_scr, hb_scr, h_scr[...], True)
    h_scr[...] = h

    @pl.when(tj == 0)
    def _():
        hfin_ref[...] = h

    gd = _od_unpack(gin.at[s], OD_TT * OD_G)
    yd = (hf_ref[...] + hb_scr[...]) * jax.nn.gelu(gd)

    def out_copies(slot):
        return _od_out_copies(obuf.at[slot], yd_hbm, sem_out.at[slot], seq_row0=seq_row0, n=n, g=g, tj=tj)

    _od_emit(q, nq, s, obuf, _pack_halves(yd), out_copies)


def _od_rglru(p, prm_f, prm_b, h0_f, h0_b, *, seq_row0, nseq, n):
    ng = nseq // OD_G
    nt = n // OD_TT
    nq = ng * nt
    rows = OD_TT * OD_G
    vec = pl.BlockSpec((1, W), lambda q: (0, 0))
    wsp = pl.BlockSpec((HEADS, HD, HD), lambda q: (0, 0, 0))
    prm_specs = [pl.BlockSpec((8, W), lambda q: (0, 0)), vec, wsp, vec, wsp, vec, vec,
                 pl.BlockSpec((OD_G, W), lambda q: (q // nt, 0))]
    state_spec = pl.BlockSpec((OD_G, W), lambda q: (q // nt, 0))
    state_shape = jax.ShapeDtypeStruct((nseq, W), F32)
    t_in = OD_TT + 2 * LRU_HALO
    tile_f32 = pltpu.VMEM((rows, W), F32)

    hf, s_f = pl.pallas_call(
        functools.partial(_od_lru_fwd_body, seq_row0=seq_row0, n=n, ng=ng),
        grid=(nq,),
        in_specs=[pl.BlockSpec(memory_space=pl.ANY)] + prm_specs,
        out_specs=[pl.BlockSpec((rows, W), lambda q: (q, 0)), state_spec],
        out_shape=[jax.ShapeDtypeStruct((nq * rows, W), F32), state_shape],
        scratch_shapes=[pltpu.VMEM((2, t_in, OD_G, HW), U32), tile_f32, tile_f32, pltpu.VMEM((OD_G, W), F32),
                        pltpu.SemaphoreType.DMA((2,))],
        compiler_params=_params(("arbitrary",)),
        name="rglru_fwd",
    )(p, *prm_f, h0_f)

    yd, s_b = pl.pallas_call(
        functools.partial(_od_lru_bwd_body, seq_row0=seq_row0, n=n, ng=ng),
        grid=(nq,),
        in_specs=[pl.BlockSpec(memory_space=pl.ANY),
                  pl.BlockSpec((rows, W), lambda q: ((q // nt) * nt + nt - 1 - q % nt, 0))] + prm_specs,
        out_specs=[pl.BlockSpec(memory_space=pl.ANY), state_spec],
        out_shape=[jax.ShapeDtypeStruct((nseq * n, HW), U32), state_shape],
        scratch_shapes=[pltpu.VMEM((2, t_in, OD_G, HW), U32), pltpu.VMEM((2, OD_TT, OD_G, HW), U32),
                        tile_f32, tile_f32, tile_f32, pltpu.VMEM((2, OD_TT, OD_G, HW), U32),
                        pltpu.VMEM((OD_G, W), F32),
                        pltpu.SemaphoreType.DMA((2,)), pltpu.SemaphoreType.DMA((2,))],
        compiler_params=_params(("arbitrary",)),
        name="rglru_bwd",
    )(p, hf, *prm_b, h0_b)
    return yd, s_f, s_b


def _lru_params(cw, cb, wa, ba, wx, bx, lam):
    cw_pad = jnp.concatenate([cw, jnp.zeros((8 - D_KSIZE, W), F32)], axis=0)
    sp = jax.nn.softplus(-lam.astype(F32)).reshape(1, W)
    return (cw_pad, cb.reshape(1, W), wa.astype(BF16), ba.reshape(1, W), wx.astype(BF16), bx.reshape(1, W), sp)


def _out_body(*refs, n_a, n_b, n_x, packed):
    a_refs = refs[:n_a]
    b_refs = refs[n_a:n_a + n_b]
    x_refs = refs[n_a + n_b:n_a + n_b + n_x]
    (g1_ref, wo_ref, nw_ref, sc_ref, sh_ref, rw_ref, rb_ref,
     x1_ref, hf_ref, info_ref, cnt_ref, carry) = refs[n_a + n_b + n_x:]
    i = pl.program_id(0)

    @pl.when(i == 0)
    def _():
        carry[...] = jnp.zeros_like(carry)

    def branch(refs_, row0):
        v = _tile_value(refs_, TM_OUT)
        if packed:
            lo, hi = _unpack_pair(v, BF16)
            return (jnp.dot(lo, wo_ref[row0:row0 + HW, :], preferred_element_type=F32)
                    + jnp.dot(hi, wo_ref[row0 + HW:row0 + W, :], preferred_element_type=F32))
        return jnp.dot(v, wo_ref[row0:row0 + W, :], preferred_element_type=F32)

    x1 = _tile_value(x_refs, TM_OUT) + g1_ref[...] * (branch(a_refs, 0) + branch(b_refs, W))
    x1_ref[...] = x1
    ms = jnp.mean(x1 * x1, axis=-1, keepdims=True)
    hf = x1 * lax.rsqrt(ms + EPS) * nw_ref[...]
    hf = hf * (1.0 + sc_ref[...]) + sh_ref[...]
    hf_hi = hf.astype(BF16)
    hf_hi32 = hf_hi.astype(F32)
    _store_tt(hf_ref, hf_hi32, TM_OUT, rounded=True)

    l2 = jnp.dot(hf_hi, rw_ref[...], preferred_element_type=F32)
    logits = l2[:, :LANES] + l2[:, LANES:] + rb_ref[...]
    lane = lax.broadcasted_iota(jnp.int32, (TM_OUT, LANES), 1).astype(F32)
    neg = -jnp.inf
    gmask = lane < N_GROUPS
    mg = jnp.max(jnp.where(gmask, logits, neg), axis=-1, keepdims=True)
    gstar = jnp.min(jnp.where(gmask & (logits == mg), lane, float(LANES)), axis=-1, keepdims=True)
    denom = jnp.sum(jnp.where(gmask, jnp.exp(logits - mg), 0.0), axis=-1, keepdims=True)
    psel = 1.0 / denom
    lo = EXPERT_LANE0 + EXPERTS_PER_GROUP * gstar
    emask = (lane >= lo) & (lane < lo + EXPERTS_PER_GROUP)
    v1 = jnp.max(jnp.where(emask, logits, neg), axis=-1, keepdims=True)
    i1 = jnp.min(jnp.where(emask & (logits == v1), lane, float(LANES)), axis=-1, keepdims=True)
    em2 = emask & (lane != i1)
    v2 = jnp.max(jnp.where(em2, logits, neg), axis=-1, keepdims=True)
    i2 = jnp.min(jnp.where(em2 & (logits == v2), lane, float(LANES)), axis=-1, keepdims=True)
    e21 = jnp.exp(v2 - v1)
    w1 = psel / (1.0 + e21)
    w2 = psel * e21 / (1.0 + e21)

    memb = jnp.where((lane == i1) | (lane == i2), 1.0, 0.0)
    r_i = lax.broadcasted_iota(jnp.int32, (TM_OUT, TM_OUT), 0)
    c_i = lax.broadcasted_iota(jnp.int32, (TM_OUT, TM_OUT), 1)
    lower = jnp.where(r_i > c_i, 1.0, 0.0).astype(BF16)
    before = jnp.dot(lower, memb.astype(BF16), preferred_element_type=F32) + carry[0:1, :]
    rank1 = jnp.sum(jnp.where(lane == i1, before, 0.0), axis=-1, keepdims=True)
    rank2 = jnp.sum(jnp.where(lane == i2, before, 0.0), axis=-1, keepdims=True)
    new_carry = carry[...] + jnp.sum(memb, axis=0, keepdims=True)
    carry[...] = new_carry
    cnt_ref[...] = new_carry

    e1 = i1 - EXPERT_LANE0
    e2 = i2 - EXPERT_LANE0
    info = jnp.where(lane == 0, e1, jnp.where(lane == 1, e2, jnp.where(lane == 2, w1, jnp.where(
        lane == 3, w2, jnp.where(lane == 4, rank1, jnp.where(lane == 5, rank2, 0.0))))))
    info_ref[...] = info


def _out_proj(a_parts, b_parts, xs, mods, wo_bf16, nw, rw, rb, *, packed):
    tm = TM_OUT
    row = lambda n: pl.BlockSpec((tm, n), lambda i: (i, 0))
    vec = pl.BlockSpec((1, D), lambda i: (0, 0))
    return pl.pallas_call(
        functools.partial(_out_body, n_a=len(a_parts), n_b=len(b_parts), n_x=len(xs), packed=packed),
        grid=(T // tm,),
        in_specs=_row_specs(a_parts, tm, 1) + _row_specs(b_parts, tm, 1) + _row_specs(xs, tm, 1) + [
            _mod_spec(2, tm, 1),
            pl.BlockSpec((D, D), lambda i: (0, 0)), vec, _mod_spec(4, tm, 1), _mod_spec(3, tm, 1),
            pl.BlockSpec((D, 2 * LANES), lambda i: (0, 0)), pl.BlockSpec((1, LANES), lambda i: (0, 0))],
        out_specs=[row(D), pl.BlockSpec((tm * TT_SUB, LANES), lambda i: (i, 0)), row(LANES),
                   pl.BlockSpec((8, LANES), lambda i: (0, 0))],
        out_shape=[jax.ShapeDtypeStruct((T, D), F32), jax.ShapeDtypeStruct((T * TT_SUB, LANES), U32),
                   jax.ShapeDtypeStruct((T, LANES), F32), jax.ShapeDtypeStruct((8, LANES), F32)],
        scratch_shapes=[pltpu.VMEM((8, LANES), F32)],
        compiler_params=_params(("arbitrary",)),
        name="out_proj_router",
    )(*a_parts, *b_parts, *xs, mods, wo_bf16, nw.reshape(1, D), mods, mods, rw, rb)


COPY_UNROLL = 64


def _for_rows(n, fn):
    groups = n // COPY_UNROLL

    def group(c, carry):
        for u in range(COPY_UNROLL):
            fn(c * COPY_UNROLL + u)
        return carry

    def single(r, carry):
        fn(r)
        return carry

    lax.fori_loop(0, groups, group, 0)
    lax.fori_loop(groups * COPY_UNROLL, n, single, 0)


def _moe_body(be_ref, nused_ref, nv_ref, src_cur, src_nxt, dst_cur, hf_hbm, wg_ref, wu_ref, wd_ref, y_hbm,
              xbuf, obuf, wgc, wuc, wdc, sem_in, sem_out):
    i = pl.program_id(0)
    nused = nused_ref[0]
    slot = i % 2

    def tile(ref, row8):
        return ref.at[pl.ds(pl.multiple_of(row8, TT_SUB), TT_SUB)]

    def rows(ref, n):
        return ref.at[pl.ds(0, n * TT_SUB)]

    def gather(src_ref, s, n):
        def row(r):
            pltpu.make_async_copy(tile(hf_hbm, src_ref[0, r]), tile(xbuf.at[s], r * TT_SUB), sem_in.at[s]).start()

        _for_rows(n, row)

    def wait_scatter(n):
        pltpu.make_async_copy(rows(obuf, n), rows(y_hbm, n), sem_out).wait()

    @pl.when(i < nused)
    def _():
        nv = nv_ref[i]

        @pl.when(i == 0)
        def _():
            xbuf[...] = jnp.zeros(xbuf.shape, U32)
            gather(src_cur, 0, nv)

        pltpu.make_async_copy(rows(hf_hbm, nv), rows(xbuf.at[slot], nv), sem_in.at[slot]).wait()

        @pl.when(i + 1 < nused)
        def _():
            gather(src_nxt, 1 - slot, nv_ref[i + 1])

        @pl.when((i == 0) | (be_ref[i] != be_ref[jnp.maximum(i - 1, 0)]))
        def _():
            wgc[...] = wg_ref[...].astype(BF16)
            wuc[...] = wu_ref[...].astype(BF16)
            wdc[...] = wd_ref[...].astype(BF16)

        x = _load_tt(xbuf.at[slot], TM_MOE, BF16)
        gate = jnp.dot(x, wgc[...], preferred_element_type=F32)
        up = jnp.dot(x, wuc[...], preferred_element_type=F32)
        hid = (gate * _sigmoid(gate) * up).astype(BF16)

        @pl.when(i > 0)
        def _():
            wait_scatter(nv_ref[i - 1])

        _store_tt(obuf, jnp.dot(hid, wdc[...], preferred_element_type=F32), TM_MOE)

        def scatter_row(r):
            pltpu.make_async_copy(tile(obuf, r * TT_SUB), tile(y_hbm, dst_cur[0, r]), sem_out).start()

        _for_rows(nv, scatter_row)

        @pl.when(i == nused - 1)
        def _():
            wait_scatter(nv)


def _moe(hf, plan, wg, wu, wd, layer):
    src, dst, block_e, nused, nvalid = plan
    wspec = lambda shape: pl.BlockSpec((None, None) + shape, lambda i, be, nu, nv: (layer, be[i], 0, 0))
    rows = lambda f: pl.BlockSpec((None, 1, TM_MOE), lambda i, be, nu, nv: (f(i), 0, 0), memory_space=pltpu.SMEM)
    return pl.pallas_call(
        _moe_body,
        grid_spec=pltpu.PrefetchScalarGridSpec(
            num_scalar_prefetch=3,
            grid=(N_BLOCKS,),
            in_specs=[rows(lambda i: i), rows(lambda i: jnp.minimum(i + 1, N_BLOCKS - 1)), rows(lambda i: i),
                      pl.BlockSpec(memory_space=pl.ANY),
                      wspec((D, D_EXPERT)), wspec((D, D_EXPERT)), wspec((D_EXPERT, D))],
            out_specs=pl.BlockSpec(memory_space=pl.ANY),
            scratch_shapes=[pltpu.VMEM((2, TM_MOE * TT_SUB, LANES), U32),
                            pltpu.VMEM((TM_MOE * TT_SUB, LANES), U32),
                            pltpu.VMEM((D, D_EXPERT), BF16), pltpu.VMEM((D, D_EXPERT), BF16),
                            pltpu.VMEM((D_EXPERT, D), BF16),
                            pltpu.SemaphoreType.DMA((2,)), pltpu.SemaphoreType.DMA]),
        out_shape=jax.ShapeDtypeStruct((2 * T * TT_SUB, LANES), U32),
        compiler_params=_params(("arbitrary",), MOE_VMEM_LIMIT),
        name="moe_experts",
    )(block_e, nused, nvalid, src.reshape(N_BLOCKS, 1, TM_MOE), src.reshape(N_BLOCKS, 1, TM_MOE),
      dst.reshape(N_BLOCKS, 1, TM_MOE), hf, wg, wu, wd)


INV_ROWS = 2048


def _invert_body(dest_ref, zeros_hbm, codes_ref, sem):
    i = pl.program_id(0)

    @pl.when(i == 0)
    def _():
        clear = pltpu.make_async_copy(zeros_hbm, codes_ref, sem)
        clear.start()
        clear.wait()

    def put(j, carry):
        codes_ref[dest_ref[0, j]] = i * INV_ROWS + j
        return carry

    lax.fori_loop(0, INV_ROWS, put, 0, unroll=16)


def _invert_slots(dest3):
    return pl.pallas_call(
        _invert_body,
        grid=(dest3.shape[0],),
        in_specs=[pl.BlockSpec((None, 1, INV_ROWS), lambda i: (i, 0, 0), memory_space=pltpu.SMEM),
                  pl.BlockSpec(memory_space=pl.ANY)],
        out_specs=pl.BlockSpec(memory_space=pltpu.SMEM),
        out_shape=jax.ShapeDtypeStruct((N_SLOTS,), jnp.int32),
        scratch_shapes=[pltpu.SemaphoreType.DMA],
        compiler_params=_params(("arbitrary",)),
        name="invert_slots",
    )(dest3, jnp.zeros((N_SLOTS,), jnp.int32))


def _dispatch_plan(info, cnt):
    e = info[:, 0:2].astype(jnp.int32)
    rank = info[:, 4:6].astype(jnp.int32)
    counts = cnt[0, EXPERT_LANE0:EXPERT_LANE0 + N_EXPERTS].astype(jnp.int32)
    padded = ((counts + TM_MOE - 1) // TM_MOE) * TM_MOE
    pends = jnp.cumsum(padded)
    pstarts = pends - padded
    first = jnp.sum(jnp.where(e[..., None] == jnp.arange(N_EXPERTS, dtype=jnp.int32), pstarts, 0), axis=-1)
    dest = first + rank
    codes = _invert_slots(dest.reshape(2 * T // INV_ROWS, 1, INV_ROWS))
    src = (codes >> 1) * TT_SUB
    dst = ((codes & 1) * T + (codes >> 1)) * TT_SUB
    block0 = jnp.arange(N_BLOCKS, dtype=jnp.int32) * TM_MOE
    block_e = jnp.clip(jnp.searchsorted(pends, block0, side='right', method='compare_all'),
                       0, N_EXPERTS - 1).astype(jnp.int32)
    nvalid = jnp.clip(pstarts[block_e] + counts[block_e] - block0, 0, TM_MOE).astype(jnp.int32)
    nused = (pends[-1] // TM_MOE).astype(jnp.int32).reshape(1)
    return src, dst, block_e, nused, nvalid


def _combined(x_ref, y0_ref, y1_ref, info_ref, g2_ref, rows):
    info = info_ref[...]
    lane = lax.broadcasted_iota(jnp.int32, info.shape, 1)
    w1 = jnp.sum(jnp.where(lane == 2, info, 0.0), axis=-1, keepdims=True)
    w2 = jnp.sum(jnp.where(lane == 3, info, 0.0), axis=-1, keepdims=True)
    y0 = _load_tt(y0_ref, rows, F32)
    y1 = _load_tt(y1_ref, rows, F32)
    return x_ref[...] + g2_ref[...] * (y0 * w1 + y1 * w2)


def _combine_body(x_ref, y0_ref, y1_ref, info_ref, g2_ref, nw_ref, *o_refs, final):
    x = _combined(x_ref, y0_ref, y1_ref, info_ref, g2_ref, TM_CMB)
    if final:
        ms = jnp.mean(x * x, axis=-1, keepdims=True)
        x = x * lax.rsqrt(ms + EPS) * nw_ref[...]

    def emit(ref):
        ref[...] = x

    _for_tile(o_refs, TM_CMB, emit)


def _combine(x1, y, info, mods, nw, *, final):
    tm = TM_CMB
    nt = T // tm
    if final:
        out_shape = [jax.ShapeDtypeStruct((T_S, D), F32), jax.ShapeDtypeStruct((T_P, D), F32)]
    else:
        out_shape = [jax.ShapeDtypeStruct((T, D), F32)]
    return pl.pallas_call(
        functools.partial(_combine_body, final=final),
        grid=(nt,),
        in_specs=[pl.BlockSpec((tm, D), lambda i: (i, 0)),
                  pl.BlockSpec((tm * TT_SUB, LANES), lambda i: (i, 0)),
                  pl.BlockSpec((tm * TT_SUB, LANES), lambda i: (i + nt, 0)),
                  pl.BlockSpec((tm, LANES), lambda i: (i, 0)),
                  _mod_spec(5, tm, 1),
                  pl.BlockSpec((1, D), lambda i: (0, 0))],
        out_specs=_row_specs(out_shape, tm, 1),
        out_shape=out_shape,
        compiler_params=_params(("arbitrary",)),
        name="combine_final" if final else "combine",
    )(x1, y, y, info, mods, nw.reshape(1, D))


def kernel(x_prompt, x_sample, c, state_ret_fwd, state_ret_bwd, state_lru_fwd, state_lru_bwd, c_ctx, w_mod, b_mod, norm_mix_w, norm_ffn_w, norm_final_w, w_in_even, w_out_even, gmlp_norm_w, gmlp_w_s, gmlp_b_s, ret_decay_fwd, ret_decay_bwd, ret_gn_w, w_in_odd, w_out_odd, conv_w, conv_b, conv_ln_w, conv_ln_b, lru_conv_w, lru_conv_b, lru_wa_fwd, lru_ba_fwd, lru_wx_fwd, lru_bx_fwd, lru_lam_fwd, lru_wa_bwd, lru_ba_bwd, lru_wx_bwd, lru_bx_bwd, lru_lam_bwd, router_grp_w, router_grp_b, router_exp_w, router_exp_b, moe_w_gate, moe_w_up, moe_w_down):
    xs = (x_sample.reshape(T_S, D), x_prompt.reshape(T_P, D))
    cond = jnp.concatenate([c, c_ctx[None, :], jnp.zeros((N_COND_PAD - B_S - 1, D), F32)], axis=0)
    m = _modulation(cond, w_mod, b_mod)
    mods_all = m.reshape(DEPTH, N_COND_PAD, N_MOD, D).transpose(0, 2, 1, 3)[:, :, :, None, :]

    ret_f = ret_b = lru_f = lru_b = None
    for l in range(DEPTH):
        mods = mods_all[l]
        if l % 2 == 0:
            e = l // 2
            p = _norm_in(xs, norm_mix_w[l], mods, w_in_even[e].astype(BF16), pack=False)
            b_full = jnp.repeat(gmlp_b_s[e].T.astype(F32), A_GDIM, axis=1)
            out_a = (_gmlp(p, gmlp_norm_w[e], gmlp_w_s[e].astype(BF16), b_full),)
            tabs = _retention_tables(ret_decay_fwd[e], ret_decay_bwd[e])
            cos, sin = _rope_tables()
            ob_s, _, _ = _retention(p, state_ret_fwd[:, e], state_ret_bwd[:, e], tabs, ret_gn_w[e], cos, sin,
                                    base_chunk=0, nb=B_S, nc=N_S // CHUNK, rope=True)
            zero_state = jnp.zeros((B_P, HEADS, HD, HD), F32)
            ob_p, ret_f, ret_b = _retention(p, zero_state, zero_state, tabs, ret_gn_w[e], cos, sin,
                                            base_chunk=T_S // CHUNK, nb=B_P, nc=N_P // CHUNK, rope=False)
            out_b = (ob_s, ob_p)
            w_out = w_out_even[e]
        else:
            o = l // 2
            p = _norm_in(xs, norm_mix_w[l], mods, w_in_odd[o].astype(BF16), pack=True)
            cw3 = jnp.concatenate([conv_w[o], jnp.zeros((1, W), F32)], axis=0).reshape(
                C_KSIZE + 1, N_STRIPS, LANES).transpose(1, 0, 2)
            conv_args = (cw3, conv_b[o], conv_ln_w[o], conv_ln_b[o])
            prm_f = _lru_params(lru_conv_w[o], lru_conv_b[o], lru_wa_fwd[o], lru_ba_fwd[o], lru_wx_fwd[o],
                                lru_bx_fwd[o], lru_lam_fwd[o])
            prm_b = _lru_params(lru_conv_w[o], lru_conv_b[o], lru_wa_bwd[o], lru_ba_bwd[o], lru_wx_bwd[o],
                                lru_bx_bwd[o], lru_lam_bwd[o])
            latent = dict(seq_row0=0, nseq=B_S, n=N_S)
            context = dict(seq_row0=T_S, nseq=B_P, n=N_P)
            out_a = (_od_conv(p, *conv_args, **latent), _od_conv(p, *conv_args, **context))
            yd_s, _, _ = _od_rglru(p, prm_f, prm_b, state_lru_fwd[:, o], state_lru_bwd[:, o], **latent)
            zero_h = jnp.zeros((B_P, W), F32)
            yd_p, lru_f, lru_b = _od_rglru(p, prm_f, prm_b, zero_h, zero_h, **context)
            out_b = (yd_s, yd_p)
            w_out = w_out_odd[o]

        rw = jnp.concatenate([router_grp_w[l], router_exp_w[l],
                              jnp.zeros((D, LANES - N_GROUPS - N_EXPERTS), F32)], axis=1)
        rw_hi = rw.astype(BF16)
        rw = jnp.concatenate([rw_hi, (rw - rw_hi.astype(F32)).astype(BF16)], axis=1)
        rb = jnp.concatenate([router_grp_b[l], router_exp_b[l],
                              jnp.zeros((LANES - N_GROUPS - N_EXPERTS,), F32)]).reshape(1, LANES)
        x1, hf, info, cnt = _out_proj(out_a, out_b, xs, mods, w_out.astype(BF16), norm_ffn_w[l], rw, rb,
                                      packed=(l % 2 == 1))
        y = _moe(hf, _dispatch_plan(info, cnt), moe_w_gate, moe_w_up, moe_w_down, l)
        xs = tuple(_combine(x1, y, info, mods, norm_final_w, final=(l == DEPTH - 1)))

    y_sample = xs[0].reshape(B_S, N_S, D)
    y_prompt = xs[1].reshape(B_P, N_P, D)
    return (y_prompt, y_sample, ret_f[:, None], ret_b[:, None], lru_f[:, None], lru_b[:, None])
```

```python
import functools

import jax
import jax.numpy as jnp
from jax import lax
from jax.experimental import pallas as pl
from jax.experimental.pallas import tpu as pltpu

F32 = jnp.float32
BF16 = jnp.bfloat16
U32 = jnp.uint32

D = 2048
B_P, N_P = 16, 256
B_S, N_S = 8, 4096
T_S = B_S * N_S
T_P = B_P * N_P
T = T_S + T_P
ROWS_PER_COND = 4096
N_COND_PAD = 16
DEPTH = 2
N_MOD = 6
EPS = 1e-6
GRID_W = 64
ROPE_BASE = 10000.0

W = 1024
HW = W // 2
HEADS = 8
HD = 128
CHUNK = 128
A_GROUPS = 4
A_GDIM = W // A_GROUPS
C_KSIZE = 31
D_KSIZE = 4
LRU_C = 8.0

N_GROUPS = 4
EXPERTS_PER_GROUP = 8
N_EXPERTS = 32
D_EXPERT = 512
LANES = 128
N_STRIPS = W // LANES
EXPERT_LANE0 = N_GROUPS

TM_IN = 512
TN_IN = 2048
TM_OUT = 512
TM_MOE = 512
N_SLOTS = 2 * T + N_EXPERTS * TM_MOE
N_BLOCKS = N_SLOTS // TM_MOE
TM_CMB = 512

VMEM_LIMIT = 56 * 1024 * 1024
MOE_VMEM_LIMIT = 60 * 1024 * 1024


def _params(sem, vmem=VMEM_LIMIT):
    return pltpu.CompilerParams(dimension_semantics=sem, vmem_limit_bytes=vmem)


def _cond_row(i, tm):
    return (i * tm) // ROWS_PER_COND


def _sigmoid(x):
    return 0.5 * jnp.tanh(0.5 * x) + 0.5


def _pack_pair(lo, hi, rounded=False):
    if not rounded:
        lo, hi = lo.astype(BF16).astype(F32), hi.astype(BF16).astype(F32)
    return (lax.bitcast_convert_type(lo, U32) >> 16) | lax.bitcast_convert_type(hi, U32)


def _unpack_pair(w, dtype):
    return (lax.bitcast_convert_type(w << 16, F32).astype(dtype),
            lax.bitcast_convert_type(w & U32(0xFFFF0000), F32).astype(dtype))


def _pack_halves(x):
    k = x.shape[1] // 2
    return _pack_pair(x[:, :k], x[:, k:])


def _unpack_halves(w, dtype):
    return jnp.concatenate(_unpack_pair(w, dtype), axis=1)


TT_SUB = 8


def _store_tt(ref, x, rows, rounded=False):
    for c in range(TT_SUB):
        ref[pl.ds(c, rows, stride=TT_SUB), :] = _pack_pair(x[:, c * LANES:(c + 1) * LANES],
                                                           x[:, W + c * LANES:W + (c + 1) * LANES], rounded)


def _load_tt(ref, rows, dtype):
    lo, hi = [], []
    for c in range(TT_SUB):
        a, b = _unpack_pair(ref[pl.ds(c, rows, stride=TT_SUB), :], dtype)
        lo.append(a)
        hi.append(b)
    return jnp.concatenate(lo + hi, axis=1)


def _row_specs(arrs, tm, ngrid):
    if len(arrs) == 1:
        maps = [lambda i: i]
    else:
        n_s = T_S // tm
        maps = [lambda i: jnp.minimum(i, n_s - 1), lambda i: jnp.maximum(i - n_s, 0)]
    width = arrs[0].shape[1]
    if ngrid == 1:
        return [pl.BlockSpec((tm, width), lambda i, f=f: (f(i), 0)) for f in maps]
    return [pl.BlockSpec((tm, width), lambda i, j, f=f: (f(i), 0)) for f in maps]


def _tile_value(refs, tm):
    if len(refs) == 1:
        return refs[0][...]
    return jnp.where(pl.program_id(0) < T_S // tm, refs[0][...], refs[1][...])


def _for_tile(refs, tm, fn):
    if len(refs) == 1:
        fn(refs[0])
        return
    i = pl.program_id(0)

    @pl.when(i < T_S // tm)
    def _():
        fn(refs[0])

    @pl.when(i >= T_S // tm)
    def _():
        fn(refs[1])


def _mod_body(c_ref, w_ref, b_ref, o_ref):
    c = c_ref[...]
    a = (c * _sigmoid(c)).astype(BF16)
    o_ref[...] = jnp.dot(a, w_ref[...].astype(BF16), preferred_element_type=F32) + b_ref[...]


def _modulation(cond, w_mod, b_mod):
    tn = 1024
    return pl.pallas_call(
        _mod_body,
        grid=(DEPTH, N_MOD * D // tn),
        in_specs=[pl.BlockSpec((N_COND_PAD, D), lambda l, j: (0, 0)),
                  pl.BlockSpec((None, D, tn), lambda l, j: (l, 0, j)),
                  pl.BlockSpec((None, 1, tn), lambda l, j: (l, 0, j))],
        out_specs=pl.BlockSpec((None, N_COND_PAD, tn), lambda l, j: (l, 0, j)),
        out_shape=jax.ShapeDtypeStruct((DEPTH, N_COND_PAD, N_MOD * D), F32),
        compiler_params=_params(("arbitrary", "arbitrary")),
        name="modulation",
    )(cond, w_mod, b_mod.reshape(DEPTH, 1, N_MOD * D))


def _mod_spec(k, tm, ngrid):
    if ngrid == 1:
        return pl.BlockSpec((None, None, 1, D), lambda i: (k, _cond_row(i, tm), 0, 0))
    return pl.BlockSpec((None, None, 1, D), lambda i, j: (k, _cond_row(i, tm), 0, 0))


def _norm_in_body(*refs, pack):
    nw_ref, sc_ref, sh_ref, w_ref, p_ref, h_scr = refs[-6:]

    @pl.when(pl.program_id(1) == 0)
    def _():
        x = _tile_value(refs[:-6], TM_IN)
        ms = jnp.mean(x * x, axis=-1, keepdims=True)
        h = x * lax.rsqrt(ms + EPS) * nw_ref[...]
        h = h * (1.0 + sc_ref[...]) + sh_ref[...]
        h_scr[...] = h.astype(BF16)

    r = jnp.dot(h_scr[...], w_ref[...], preferred_element_type=F32)
    if pack:
        p_ref[...] = jnp.concatenate([_pack_halves(r[:, k * W:(k + 1) * W]) for k in range(TN_IN // W)], axis=1)
    else:
        p_ref[...] = r.astype(BF16)


def _norm_in(xs, nw, mods, w_bf16, *, pack):
    n = w_bf16.shape[1]
    if pack:
        out_spec = pl.BlockSpec((TM_IN, TN_IN // 2), lambda i, j: (i, j))
        out_shape = jax.ShapeDtypeStruct((T, n // 2), U32)
    else:
        out_spec = pl.BlockSpec((TM_IN, TN_IN), lambda i, j: (i, j))
        out_shape = jax.ShapeDtypeStruct((T, n), BF16)
    return pl.pallas_call(
        functools.partial(_norm_in_body, pack=pack),
        grid=(T // TM_IN, n // TN_IN),
        in_specs=_row_specs(xs, TM_IN, 2) + [
            pl.BlockSpec((1, D), lambda i, j: (0, 0)),
            _mod_spec(1, TM_IN, 2),
            _mod_spec(0, TM_IN, 2),
            pl.BlockSpec((D, TN_IN), lambda i, j: (0, j))],
        out_specs=out_spec,
        out_shape=out_shape,
        scratch_shapes=[pltpu.VMEM((TM_IN, D), BF16)],
        compiler_params=_params(("arbitrary", "arbitrary")),
        name="norm_in",
    )(*xs, nw.reshape(1, D), mods, mods, w_bf16)


GMLP_ROWS = 512


def _gmlp_body(u_ref, v_ref, lnw_ref, ws_ref, b_ref, o_ref):
    for c in range(GMLP_ROWS // CHUNK):
        rows = pl.ds(c * CHUNK, CHUNK)
        v = jax.nn.gelu(v_ref[rows, :].astype(F32))
        mu = jnp.mean(v, axis=-1, keepdims=True)
        var = jnp.mean(jnp.square(v - mu), axis=-1, keepdims=True)
        vn = ((v - mu) * lax.rsqrt(var + EPS) * lnw_ref[...]).astype(BF16)
        z = jnp.concatenate(
            [jnp.dot(ws_ref[g], vn[:, g * A_GDIM:(g + 1) * A_GDIM], preferred_element_type=F32)
             for g in range(A_GROUPS)], axis=1) + b_ref[...]
        u = jax.nn.gelu(u_ref[rows, :].astype(F32))
        o_ref[rows, :] = (u * z).astype(BF16)


def _gmlp(p, lnw, ws_bf16, b_full):
    return pl.pallas_call(
        _gmlp_body,
        grid=(T // GMLP_ROWS,),
        in_specs=[pl.BlockSpec((GMLP_ROWS, W), lambda i: (i, 0)),
                  pl.BlockSpec((GMLP_ROWS, W), lambda i: (i, 1)),
                  pl.BlockSpec((1, W), lambda i: (0, 0)),
                  pl.BlockSpec((A_GROUPS, CHUNK, CHUNK), lambda i: (0, 0, 0)),
                  pl.BlockSpec((CHUNK, W), lambda i: (0, 0))],
        out_specs=pl.BlockSpec((GMLP_ROWS, W), lambda i: (i, 0)),
        out_shape=jax.ShapeDtypeStruct((T, W), BF16),
        compiler_params=_params(("arbitrary",)),
        name="gmlp",
    )(p, p, lnw.reshape(1, W), ws_bf16, b_full)


RET_SUB = 4


def _roped_qk(q_ref, k_ref, cos_ref, sin_ref, rows, h, rope):
    cols = slice(h * HD, (h + 1) * HD)
    q = q_ref[rows, cols].astype(F32)
    k = k_ref[rows, cols].astype(F32) * (HD ** -0.5)
    if rope:
        c = cos_ref[rows, :]
        s = sin_ref[rows, :]
        q = q * c + pltpu.roll(q, HD // 2, axis=1) * s
        k = k * c + pltpu.roll(k, HD // 2, axis=1) * s
    return q, k


def _kt_v(kd, v):
    return lax.dot_general(kd, v, (((0,), (0,)), ((), ())), preferred_element_type=F32)


def _ret_bwd_body(q_ref, k_ref, v_ref, cos_ref, sin_ref, s0_ref, dq_ref, dk_ref, dc_ref,
                  ob_ref, sfin_ref, s_scr, *, rope, nc, nsub):
    c = pl.program_id(1)

    @pl.when(c == 0)
    def _():
        s_scr[...] = s0_ref[...]

    for sub in reversed(range(nsub)):
        rows = slice(sub * CHUNK, (sub + 1) * CHUNK)
        for h in range(HEADS):
            cols = slice(h * HD, (h + 1) * HD)
            q, k = _roped_qk(q_ref, k_ref, cos_ref, sin_ref, rows, h, rope)
            v = v_ref[rows, cols]
            s = s_scr[h]
            ob_ref[rows, cols] = jnp.dot(q.astype(BF16), s.astype(BF16), preferred_element_type=F32) * dq_ref[h]
            kd = (k * dk_ref[h]).astype(BF16)
            s_scr[h] = s * dc_ref[h] + _kt_v(kd, v)

    @pl.when(c == nc - 1)
    def _():
        sfin_ref[...] = s_scr[...]


def _ret_fwd_body(q_ref, k_ref, v_ref, g_ref, cos_ref, sin_ref, ob_ref, s0_ref, m_ref, dq_ref, dk_ref, dc_ref,
                  gn_ref, o_ref, sfin_ref, s_scr, *, rope, nc, nsub):
    c = pl.program_id(1)

    @pl.when(c == 0)
    def _():
        s_scr[...] = s0_ref[...]

    for sub in range(nsub):
        rows = slice(sub * CHUNK, (sub + 1) * CHUNK)
        for h in range(HEADS):
            cols = slice(h * HD, (h + 1) * HD)
            q, k = _roped_qk(q_ref, k_ref, cos_ref, sin_ref, rows, h, rope)
            qb = q.astype(BF16)
            v = v_ref[rows, cols]
            s = s_scr[h]
            scores = lax.dot_general(qb, k.astype(BF16), (((1,), (1,)), ((), ())),
                                     preferred_element_type=F32) * m_ref[h]
            o = (jnp.dot(scores.astype(BF16), v, preferred_element_type=F32)
                 + jnp.dot(qb, s.astype(BF16), preferred_element_type=F32) * dq_ref[h]
                 + ob_ref[rows, cols])
            kd = (k * dk_ref[h]).astype(BF16)
            s_scr[h] = s * dc_ref[h] + _kt_v(kd, v)
            mu = jnp.mean(o, axis=-1, keepdims=True)
            var = jnp.mean(jnp.square(o - mu), axis=-1, keepdims=True)
            on = (o - mu) * lax.rsqrt(var + EPS) * gn_ref[:, cols]
            g = g_ref[rows, cols].astype(F32)
            o_ref[rows, cols] = (g * _sigmoid(g) * on).astype(BF16)

    @pl.when(c == nc - 1)
    def _():
        sfin_ref[...] = s_scr[...]


def _retention(p, s_f0, s_b0, tabs, gn_w, cos, sin, *, base_chunk, nb, nc, rope):
    m_tab, dq_f, dk_f, dc_f, dq_b, dk_b, dc_b = tabs
    nsub = min(RET_SUB, nc)
    rows = nsub * CHUNK
    ns = nc // nsub
    base = base_chunk // nsub
    state_spec = pl.BlockSpec((None, HEADS, HD, HD), lambda b, c: (b, 0, 0, 0))
    tab_spec = pl.BlockSpec((HEADS, HD, HD), lambda b, c: (0, 0, 0))
    dc_spec = pl.BlockSpec((HEADS, 1, HD), lambda b, c: (0, 0, 0))
    state_shape = jax.ShapeDtypeStruct((nb, HEADS, HD, HD), F32)

    def step(c, rev):
        return ns - 1 - c if rev else c

    def col(j, rev):
        return pl.BlockSpec((rows,W), lambda b, c: (base + b * ns + step(c, rev), j))

    def rope_spec(rev):
        if not rope:
            return pl.BlockSpec((rows,HD), lambda b, c: (0, 0))
        return pl.BlockSpec((rows,HD), lambda b, c: (step(c, rev), 0))

    def local(rev):
        return pl.BlockSpec((rows,W), lambda b, c: (b * ns + step(c, rev), 0))

    n_rows = nb * nc * CHUNK
    ob, s_b = pl.pallas_call(
        functools.partial(_ret_bwd_body, rope=rope, nc=ns, nsub=nsub),
        grid=(nb, ns),
        in_specs=[col(2, True), col(3, True), col(4, True), rope_spec(True), rope_spec(True),
                  state_spec, tab_spec, tab_spec, dc_spec],
        out_specs=[local(True), state_spec],
        out_shape=[jax.ShapeDtypeStruct((n_rows, W), F32), state_shape],
        scratch_shapes=[pltpu.VMEM((HEADS, HD, HD), F32)],
        compiler_params=_params(("arbitrary", "arbitrary")),
        name="retention_bwd",
    )(p, p, p, cos, sin, s_b0, dq_b, dk_b, dc_b)

    o, s_f = pl.pallas_call(
        functools.partial(_ret_fwd_body, rope=rope, nc=ns, nsub=nsub),
        grid=(nb, ns),
        in_specs=[col(2, False), col(3, False), col(4, False), col(5, False), rope_spec(False), rope_spec(False),
                  local(False), state_spec, tab_spec, tab_spec, tab_spec, dc_spec,
                  pl.BlockSpec((1, W), lambda b, c: (0, 0))],
        out_specs=[local(False), state_spec],
        out_shape=[jax.ShapeDtypeStruct((n_rows, W), BF16), state_shape],
        scratch_shapes=[pltpu.VMEM((HEADS, HD, HD), F32)],
        compiler_params=_params(("arbitrary", "arbitrary")),
        name="retention_fwd",
    )(p, p, p, p, cos, sin, ob, s_f0, m_tab, dq_f, dk_f, dc_f, gn_w.reshape(1, W))
    return o, s_f, s_b


def _retention_tables(decay_fwd, decay_bwd):
    lg_f = jax.nn.log_sigmoid(decay_fwd.astype(F32))[:, None, None]
    lg_b = jax.nn.log_sigmoid(decay_bwd.astype(F32))[:, None, None]
    pos = jnp.arange(CHUNK, dtype=F32)
    rel = pos[:, None] - pos[None, :]
    m_tab = (jnp.where(rel >= 0, jnp.exp(lg_f * jnp.maximum(rel, 0.0)), 0.0)
             + jnp.where(rel <= 0, jnp.exp(lg_b * jnp.maximum(-rel, 0.0)), 0.0))
    ones = jnp.ones((1, 1, HD), F32)
    col = pos[None, :, None]
    dq_f = jnp.exp(lg_f * (col + 1.0)) * ones
    dk_f = jnp.exp(lg_f * (CHUNK - 1.0 - col)) * ones
    dq_b = jnp.exp(lg_b * (CHUNK - col)) * ones
    dk_b = jnp.exp(lg_b * col) * ones
    dc_f = jnp.exp(lg_f * CHUNK) * ones
    dc_b = jnp.exp(lg_b * CHUNK) * ones
    return m_tab, dq_f, dk_f, dc_f, dq_b, dk_b, dc_b


def _rope_tables():
    rows = N_S // GRID_W
    row = jnp.repeat(jnp.arange(rows, dtype=F32), GRID_W)
    colp = jnp.tile(jnp.arange(GRID_W, dtype=F32), rows)
    quarter = HD // 4
    inv_freq = ROPE_BASE ** (-jnp.arange(quarter, dtype=F32) / quarter)
    ang = jnp.concatenate([row[:, None] * inv_freq, colp[:, None] * inv_freq], axis=-1)
    cos, sin = jnp.cos(ang), jnp.sin(ang)
    return jnp.concatenate([cos, cos], axis=-1), jnp.concatenate([-sin, sin], axis=-1)


OD_TT = 128
OD_G = 8
CONV_HALO = 16
LRU_HALO = 8


def _od_rows(seq_row0, n, g, tj):
    return [seq_row0 + (g * OD_G + b) * n + tj * OD_TT for b in range(OD_G)]


def _od_in_copies(p_hbm, buf, sem, *, col, halo, seq_row0, n, g, tj):
    nt = n // OD_TT
    out = []
    for b, base in enumerate(_od_rows(seq_row0, n, g, tj)):
        def src(r, k):
            return p_hbm.at[pl.ds(r, k), pl.ds(col * HW, HW)]

        out.append((None, pltpu.make_async_copy(src(base, OD_TT), buf.at[pl.ds(halo, OD_TT), b, :], sem)))
        if halo:
            out.append((tj > 0, pltpu.make_async_copy(src(base - halo, halo), buf.at[pl.ds(0, halo), b, :], sem)))
            out.append((tj < nt - 1, pltpu.make_async_copy(src(base + OD_TT, halo),
                                                           buf.at[pl.ds(halo + OD_TT, halo), b, :], sem)))
    return out


def _od_run(copies, op):
    for cond, cp in copies:
        fn = cp.start if op == "start" else cp.wait
        if cond is None:
            fn()
        else:
            pl.when(cond)(fn)


def _od_zero_halo(buf, halo, n, tj):
    nt = n // OD_TT
    zeros = jnp.zeros((halo, OD_G, HW), U32)

    @pl.when(tj == 0)
    def _():
        buf[0:halo] = zeros

    @pl.when(tj == nt - 1)
    def _():
        buf[halo + OD_TT:2 * halo + OD_TT] = zeros


def _od_out_copies(obuf, out_hbm, sem, *, seq_row0, n, g, tj):
    return [(None, pltpu.make_async_copy(obuf.at[:, b, :], out_hbm.at[pl.ds(base - seq_row0, OD_TT), :], sem))
            for b, base in enumerate(_od_rows(seq_row0, n, g, tj))]


def _od_unpack(buf, rows):
    return _unpack_halves(buf[...].reshape(rows, HW), F32)


def _od_pipeline(q, nq, in_copies, sem_slots=2):
    s = q % 2

    @pl.when(q == 0)
    def _():
        _od_run(in_copies(q, s), "start")

    _od_run(in_copies(q, s), "wait")

    @pl.when(q + 1 < nq)
    def _():
        _od_run(in_copies(q + 1, 1 - s), "start")

    return s


def _od_emit(q, nq, s, obuf, words, out_copies):
    @pl.when(q >= 2)
    def _():
        _od_run(out_copies(s), "wait")

    obuf[s] = words.reshape(OD_TT, OD_G, HW)
    _od_run(out_copies(s), "start")

    @pl.when(q == nq - 1)
    def _():
        _od_run(out_copies(s), "wait")

        @pl.when(q >= 1)
        def _():
            _od_run(out_copies(1 - s), "wait")


CONV_RB = 64


def _od_conv_body(p_hbm, cw_ref, cb_ref, lnw_ref, lnb_ref, yc_hbm, xin, glu, ybuf, obuf, sem_in, sem_out,
                  *, seq_row0, n, ng):
    nt = n // OD_TT
    nq = ng * nt
    q = pl.program_id(0)
    halo = CONV_HALO
    rows_in = (OD_TT + 2 * halo) * OD_G
    rows = OD_TT * OD_G

    def in_copies(step, slot):
        out = []
        for br in range(2):
            out += _od_in_copies(p_hbm, xin.at[br, slot], sem_in.at[slot], col=br, halo=halo,
                                 seq_row0=seq_row0, n=n, g=step // nt, tj=step % nt)
        return out

    s = _od_pipeline(q, nq, in_copies)
    g, tj = q // nt, q % nt
    for br in range(2):
        _od_zero_halo(xin.at[br, s], halo, n, tj)
    glu[...] = _od_unpack(xin.at[0, s], rows_in) * _sigmoid(_od_unpack(xin.at[1, s], rows_in))

    def strip(si, carry):
        lanes = pl.ds(pl.multiple_of(si * LANES, LANES), LANES)
        for rb in range(rows // CONV_RB):
            acc = jnp.zeros((CONV_RB // 8, 8, LANES), F32)
            for k in range(C_KSIZE):
                wk = cw_ref[si, pl.ds(k, 1), :]
                r0 = rb * CONV_RB + (halo - C_KSIZE // 2 + k) * OD_G
                acc = acc + glu[r0:r0 + CONV_RB, lanes].reshape(CONV_RB // 8, 8, LANES) * wk[None]
            ybuf[rb * CONV_RB:(rb + 1) * CONV_RB, lanes] = acc.reshape(CONV_RB, LANES)
        return carry

    lax.fori_loop(0, N_STRIPS, strip, 0)
    y = ybuf[...] + cb_ref[...]
    mu = jnp.mean(y, axis=-1, keepdims=True)
    var = jnp.mean(jnp.square(y - mu), axis=-1, keepdims=True)
    yn = (y - mu) * lax.rsqrt(var + EPS) * lnw_ref[...] + lnb_ref[...]
    yo = yn * _sigmoid(yn)

    def out_copies(slot):
        return _od_out_copies(obuf.at[slot], yc_hbm, sem_out.at[slot], seq_row0=seq_row0, n=n, g=g, tj=tj)

    _od_emit(q, nq, s, obuf, _pack_halves(yo), out_copies)


def _od_conv(p, cw3, cb, lnw, lnb, *, seq_row0, nseq, n):
    ng = nseq // OD_G
    nq = ng * (n // OD_TT)
    vec = pl.BlockSpec((1, W), lambda q: (0, 0))
    t_in = OD_TT + 2 * CONV_HALO
    return pl.pallas_call(
        functools.partial(_od_conv_body, seq_row0=seq_row0, n=n, ng=ng),
        grid=(nq,),
        in_specs=[pl.BlockSpec(memory_space=pl.ANY),
                  pl.BlockSpec((N_STRIPS, C_KSIZE + 1, LANES), lambda q: (0, 0, 0)), vec, vec, vec],
        out_specs=pl.BlockSpec(memory_space=pl.ANY),
        out_shape=jax.ShapeDtypeStruct((nseq * n, HW), U32),
        scratch_shapes=[pltpu.VMEM((2, 2, t_in, OD_G, HW), U32),
                        pltpu.VMEM((t_in * OD_G, W), F32),
                        pltpu.VMEM((OD_TT * OD_G, W), F32),
                        pltpu.VMEM((2, OD_TT, OD_G, HW), U32),
                        pltpu.SemaphoreType.DMA((2,)), pltpu.SemaphoreType.DMA((2,))],
        compiler_params=_params(("arbitrary",)),
        name="conv_module",
    )(p, cw3, cb.reshape(1, W), lnw.reshape(1, W), lnb.reshape(1, W))


def _od_gates(x, cw_ref, cb_ref, wa_ref, ba_ref, wx_ref, bx_ref, sp_ref, a_scr, b_scr):
    rows = OD_TT * OD_G
    xc = cb_ref[...]
    for k in range(D_KSIZE):
        r0 = (LRU_HALO - D_KSIZE // 2 + k) * OD_G
        xc = xc + cw_ref[k:k + 1, :] * x[r0:r0 + rows, :]
    xb = xc.astype(BF16)
    r = jnp.concatenate([jnp.dot(xb[:, h * HD:(h + 1) * HD], wa_ref[h], preferred_element_type=F32)
                         for h in range(HEADS)], axis=1) + ba_ref[...]
    g = jnp.concatenate([jnp.dot(xb[:, h * HD:(h + 1) * HD], wx_ref[h], preferred_element_type=F32)
                         for h in range(HEADS)], axis=1) + bx_ref[...]
    log_a = (-LRU_C) * _sigmoid(r) * sp_ref[...]
    th = jnp.tanh(log_a)
    one_minus_a2 = -2.0 * th / (1.0 - th)
    a_scr[...] = jnp.exp(log_a)
    b_scr[...] = jnp.sqrt(jnp.maximum(one_minus_a2, 0.0)) * (_sigmoid(g) * xc)


def _od_scan(a_scr, b_scr, out_ref, h, reverse):
    def step(t, h):
        tr = (OD_TT - 1 - t) if reverse else t
        rows = pl.ds(pl.multiple_of(tr * OD_G, OD_G), OD_G)
        h = a_scr[rows, :] * h + b_scr[rows, :]
        out_ref[rows, :] = h
        return h

    return lax.fori_loop(0, OD_TT, step, h, unroll=8)


def _od_lru_fwd_body(p_hbm, cw_ref, cb_ref, wa_ref, ba_ref, wx_ref, bx_ref, sp_ref, h0_ref,
                     hf_ref, hfin_ref, xin, a_scr, b_scr, h_scr, sem_in, *, seq_row0, n, ng):
    nt = n // OD_TT
    nq = ng * nt
    q = pl.program_id(0)

    def in_copies(step, slot):
        return _od_in_copies(p_hbm, xin.at[slot], sem_in.at[slot], col=2, halo=LRU_HALO,
                             seq_row0=seq_row0, n=n, g=step // nt, tj=step % nt)

    s = _od_pipeline(q, nq, in_copies)
    tj = q % nt
    _od_zero_halo(xin.at[s], LRU_HALO, n, tj)

    @pl.when(tj == 0)
    def _():
        h_scr[...] = h0_ref[...]

    x = _od_unpack(xin.at[s], (OD_TT + 2 * LRU_HALO) * OD_G)
    _od_gates(x, cw_ref, cb_ref, wa_ref, ba_ref, wx_ref, bx_ref, sp_ref, a_scr, b_scr)
    h = _od_scan(a_scr, b_scr, hf_ref, h_scr[...], False)
    h_scr[...] = h

    @pl.when(tj == nt - 1)
    def _():
        hfin_ref[...] = h


def _od_lru_bwd_body(p_hbm, hf_ref, cw_ref, cb_ref, wa_ref, ba_ref, wx_ref, bx_ref, sp_ref, h0_ref,
                     yd_hbm, hfin_ref, xin, gin, a_scr, b_scr, hb_scr, obuf, h_scr, sem_in, sem_out,
                     *, seq_row0, n, ng):
    nt = n // OD_TT
    nq = ng * nt
    q = pl.program_id(0)

    def in_copies(step, slot):
        where = dict(seq_row0=seq_row0, n=n, g=step // nt, tj=nt - 1 - step % nt)
        return (_od_in_copies(p_hbm, xin.at[slot], sem_in.at[slot], col=2, halo=LRU_HALO, **where)
                + _od_in_copies(p_hbm, gin.at[slot], sem_in.at[slot], col=3, halo=0, **where))

    s = _od_pipeline(q, nq, in_copies)
    g, tj = q // nt, nt - 1 - q % nt
    _od_zero_halo(xin.at[s], LRU_HALO, n, tj)

    @pl.when(tj == nt - 1)
    def _():
        h_scr[...] = h0_ref[...]

    x = _od_unpack(xin.at[s], (OD_TT + 2 * LRU_HALO) * OD_G)
    _od_gates(x, cw_ref, cb_ref, wa_ref, ba_ref, wx_ref, bx_ref, sp_ref, a_scr, b_scr)
    h = _od_scan(a_scr, b_scr, hb_scr, h_scr[...], True)
    h_scr[...] = h

    @pl.when(tj == 0)
    def _():
        hfin_ref[...] = h

    gd = _od_unpack(gin.at[s], OD_TT * OD_G)
    yd = (hf_ref[...] + hb_scr[...]) * jax.nn.gelu(gd)

    def out_copies(slot):
        return _od_out_copies(obuf.at[slot], yd_hbm, sem_out.at[slot], seq_row0=seq_row0, n=n, g=g, tj=tj)

    _od_emit(q, nq, s, obuf, _pack_halves(yd), out_copies)


def _od_rglru(p, prm_f, prm_b, h0_f, h0_b, *, seq_row0, nseq, n):
    ng = nseq // OD_G
    nt = n // OD_TT
    nq = ng * nt
    rows = OD_TT * OD_G
    vec = pl.BlockSpec((1, W), lambda q: (0, 0))
    wsp = pl.BlockSpec((HEADS, HD, HD), lambda q: (0, 0, 0))
    prm_specs = [pl.BlockSpec((8, W), lambda q: (0, 0)), vec, wsp, vec, wsp, vec, vec,
                 pl.BlockSpec((OD_G, W), lambda q: (q // nt, 0))]
    state_spec = pl.BlockSpec((OD_G, W), lambda q: (q // nt, 0))
    state_shape = jax.ShapeDtypeStruct((nseq, W), F32)
    t_in = OD_TT + 2 * LRU_HALO
    tile_f32 = pltpu.VMEM((rows, W), F32)

    hf, s_f = pl.pallas_call(
        functools.partial(_od_lru_fwd_body, seq_row0=seq_row0, n=n, ng=ng),
        grid=(nq,),
        in_specs=[pl.BlockSpec(memory_space=pl.ANY)] + prm_specs,
        out_specs=[pl.BlockSpec((rows, W), lambda q: (q, 0)), state_spec],
        out_shape=[jax.ShapeDtypeStruct((nq * rows, W), F32), state_shape],
        scratch_shapes=[pltpu.VMEM((2, t_in, OD_G, HW), U32), tile_f32, tile_f32, pltpu.VMEM((OD_G, W), F32),
                        pltpu.SemaphoreType.DMA((2,))],
        compiler_params=_params(("arbitrary",)),
        name="rglru_fwd",
    )(p, *prm_f, h0_f)

    yd, s_b = pl.pallas_call(
        functools.partial(_od_lru_bwd_body, seq_row0=seq_row0, n=n, ng=ng),
        grid=(nq,),
        in_specs=[pl.BlockSpec(memory_space=pl.ANY),
                  pl.BlockSpec((rows, W), lambda q: ((q // nt) * nt + nt - 1 - q % nt, 0))] + prm_specs,
        out_specs=[pl.BlockSpec(memory_space=pl.ANY), state_spec],
        out_shape=[jax.ShapeDtypeStruct((nseq * n, HW), U32), state_shape],
        scratch_shapes=[pltpu.VMEM((2, t_in, OD_G, HW), U32), pltpu.VMEM((2, OD_TT, OD_G, HW), U32),
                        tile_f32, tile_f32, tile_f32, pltpu.VMEM((2, OD_TT, OD_G, HW), U32),
                        pltpu.VMEM((OD_G, W), F32),
                        pltpu.SemaphoreType.DMA((2,)), pltpu.SemaphoreType.DMA((2,))],
        compiler_params=_params(("arbitrary",)),
        name="rglru_bwd",
    )(p, hf, *prm_b, h0_b)
    return yd, s_f, s_b


def _lru_params(cw, cb, wa, ba, wx, bx, lam):
    cw_pad = jnp.concatenate([cw, jnp.zeros((8 - D_KSIZE, W), F32)], axis=0)
    sp = jax.nn.softplus(-lam.astype(F32)).reshape(1, W)
    return (cw_pad, cb.reshape(1, W), wa.astype(BF16), ba.reshape(1, W), wx.astype(BF16), bx.reshape(1, W), sp)


def _out_body(*refs, n_a, n_b, n_x, packed):
    a_refs = refs[:n_a]
    b_refs = refs[n_a:n_a + n_b]
    x_refs = refs[n_a + n_b:n_a + n_b + n_x]
    (g1_ref, wo_ref, nw_ref, sc_ref, sh_ref, rw_ref, rb_ref,
     x1_ref, hf_ref, info_ref, cnt_ref, carry) = refs[n_a + n_b + n_x:]
    i = pl.program_id(0)

    @pl.when(i == 0)
    def _():
        carry[...] = jnp.zeros_like(carry)

    def branch(refs_, row0):
        v = _tile_value(refs_, TM_OUT)
        if packed:
            lo, hi = _unpack_pair(v, BF16)
            return (jnp.dot(lo, wo_ref[row0:row0 + HW, :], preferred_element_type=F32)
                    + jnp.dot(hi, wo_ref[row0 + HW:row0 + W, :], preferred_element_type=F32))
        return jnp.dot(v, wo_ref[row0:row0 + W, :], preferred_element_type=F32)

    x1 = _tile_value(x_refs, TM_OUT) + g1_ref[...] * (branch(a_refs, 0) + branch(b_refs, W))
    x1_ref[...] = x1
    ms = jnp.mean(x1 * x1, axis=-1, keepdims=True)
    hf = x1 * lax.rsqrt(ms + EPS) * nw_ref[...]
    hf = hf * (1.0 + sc_ref[...]) + sh_ref[...]
    hf_hi = hf.astype(BF16)
    hf_hi32 = hf_hi.astype(F32)
    _store_tt(hf_ref, hf_hi32, TM_OUT, rounded=True)

    hf_lo = (hf - hf_hi32).astype(BF16)
    l2 = jnp.dot(hf_hi, rw_ref[...], preferred_element_type=F32)
    logits = (l2[:, :LANES] + l2[:, LANES:]
              + jnp.dot(hf_lo, rw_ref[:, :LANES], preferred_element_type=F32) + rb_ref[...])
    lane = lax.broadcasted_iota(jnp.int32, (TM_OUT, LANES), 1).astype(F32)
    neg = -jnp.inf
    gmask = lane < N_GROUPS
    mg = jnp.max(jnp.where(gmask, logits, neg), axis=-1, keepdims=True)
    gstar = jnp.min(jnp.where(gmask & (logits == mg), lane, float(LANES)), axis=-1, keepdims=True)
    denom = jnp.sum(jnp.where(gmask, jnp.exp(logits - mg), 0.0), axis=-1, keepdims=True)
    psel = 1.0 / denom
    lo = EXPERT_LANE0 + EXPERTS_PER_GROUP * gstar
    emask = (lane >= lo) & (lane < lo + EXPERTS_PER_GROUP)
    v1 = jnp.max(jnp.where(emask, logits, neg), axis=-1, keepdims=True)
    i1 = jnp.min(jnp.where(emask & (logits == v1), lane, float(LANES)), axis=-1, keepdims=True)
    em2 = emask & (lane != i1)
    v2 = jnp.max(jnp.where(em2, logits, neg), axis=-1, keepdims=True)
    i2 = jnp.min(jnp.where(em2 & (logits == v2), lane, float(LANES)), axis=-1, keepdims=True)
    e21 = jnp.exp(v2 - v1)
    w1 = psel / (1.0 + e21)
    w2 = psel * e21 / (1.0 + e21)

    memb = jnp.where((lane == i1) | (lane == i2), 1.0, 0.0)
    r_i = lax.broadcasted_iota(jnp.int32, (TM_OUT, TM_OUT), 0)
    c_i = lax.broadcasted_iota(jnp.int32, (TM_OUT, TM_OUT), 1)
    lower = jnp.where(r_i > c_i, 1.0, 0.0).astype(BF16)
    before = jnp.dot(lower, memb.astype(BF16), preferred_element_type=F32) + carry[0:1, :]
    rank1 = jnp.sum(jnp.where(lane == i1, before, 0.0), axis=-1, keepdims=True)
    rank2 = jnp.sum(jnp.where(lane == i2, before, 0.0), axis=-1, keepdims=True)
    new_carry = carry[...] + jnp.sum(memb, axis=0, keepdims=True)
    carry[...] = new_carry
    cnt_ref[...] = new_carry

    e1 = i1 - EXPERT_LANE0
    e2 = i2 - EXPERT_LANE0
    info = jnp.where(lane == 0, e1, jnp.where(lane == 1, e2, jnp.where(lane == 2, w1, jnp.where(
        lane == 3, w2, jnp.where(lane == 4, rank1, jnp.where(lane == 5, rank2, 0.0))))))
    info_ref[...] = info


def _out_proj(a_parts, b_parts, xs, mods, wo_bf16, nw, rw, rb, *, packed):
    tm = TM_OUT
    row = lambda n: pl.BlockSpec((tm, n), lambda i: (i, 0))
    vec = pl.BlockSpec((1, D), lambda i: (0, 0))
    return pl.pallas_call(
        functools.partial(_out_body, n_a=len(a_parts), n_b=len(b_parts), n_x=len(xs), packed=packed),
        grid=(T // tm,),
        in_specs=_row_specs(a_parts, tm, 1) + _row_specs(b_parts, tm, 1) + _row_specs(xs, tm, 1) + [
            _mod_spec(2, tm, 1),
            pl.BlockSpec((D, D), lambda i: (0, 0)), vec, _mod_spec(4, tm, 1), _mod_spec(3, tm, 1),
            pl.BlockSpec((D, 2 * LANES), lambda i: (0, 0)), pl.BlockSpec((1, LANES), lambda i: (0, 0))],
        out_specs=[row(D), pl.BlockSpec((tm * TT_SUB, LANES), lambda i: (i, 0)), row(LANES),
                   pl.BlockSpec((8, LANES), lambda i: (0, 0))],
        out_shape=[jax.ShapeDtypeStruct((T, D), F32), jax.ShapeDtypeStruct((T * TT_SUB, LANES), U32),
                   jax.ShapeDtypeStruct((T, LANES), F32), jax.ShapeDtypeStruct((8, LANES), F32)],
        scratch_shapes=[pltpu.VMEM((8, LANES), F32)],
        compiler_params=_params(("arbitrary",)),
        name="out_proj_router",
    )(*a_parts, *b_parts, *xs, mods, wo_bf16, nw.reshape(1, D), mods, mods, rw, rb)


COPY_UNROLL = 32


def _moe_body(be_ref, nused_ref, nv_ref, src_cur, src_nxt, dst_cur, hf_hbm, wg_ref, wu_ref, wd_ref, y_hbm,
              xbuf, obuf, wgc, wuc, wdc, sem_in, sem_out):
    i = pl.program_id(0)
    nused = nused_ref[0]
    slot = i % 2

    def tile(ref, row8):
        return ref.at[pl.ds(pl.multiple_of(row8, TT_SUB), TT_SUB)]

    def rows(ref, n):
        return ref.at[pl.ds(0, n * TT_SUB)]

    def for_rows(n, fn):
        groups = n // COPY_UNROLL

        def group(c, carry):
            for u in range(COPY_UNROLL):
                fn(c * COPY_UNROLL + u)
            return carry

        def single(r, carry):
            fn(r)
            return carry

        lax.fori_loop(0, groups, group, 0)
        lax.fori_loop(groups * COPY_UNROLL, n, single, 0)

    def gather(src_ref, s, n):
        def row(r):
            pltpu.make_async_copy(tile(hf_hbm, src_ref[0, r]), tile(xbuf.at[s], r * TT_SUB), sem_in.at[s]).start()

        for_rows(n, row)

    def wait_scatter(n):
        pltpu.make_async_copy(rows(obuf, n), rows(y_hbm, n), sem_out).wait()

    @pl.when(i < nused)
    def _():
        nv = nv_ref[i]

        @pl.when(i == 0)
        def _():
            xbuf[...] = jnp.zeros(xbuf.shape, U32)
            gather(src_cur, 0, nv)

        pltpu.make_async_copy(rows(hf_hbm, nv), rows(xbuf.at[slot], nv), sem_in.at[slot]).wait()

        @pl.when(i + 1 < nused)
        def _():
            gather(src_nxt, 1 - slot, nv_ref[i + 1])

        @pl.when((i == 0) | (be_ref[i] != be_ref[jnp.maximum(i - 1, 0)]))
        def _():
            wgc[...] = wg_ref[...].astype(BF16)
            wuc[...] = wu_ref[...].astype(BF16)
            wdc[...] = wd_ref[...].astype(BF16)

        x = _load_tt(xbuf.at[slot], TM_MOE, BF16)
        gate = jnp.dot(x, wgc[...], preferred_element_type=F32)
        up = jnp.dot(x, wuc[...], preferred_element_type=F32)
        hid = (gate * _sigmoid(gate) * up).astype(BF16)

        @pl.when(i > 0)
        def _():
            wait_scatter(nv_ref[i - 1])

        _store_tt(obuf, jnp.dot(hid, wdc[...], preferred_element_type=F32), TM_MOE)

        def scatter_row(r):
            pltpu.make_async_copy(tile(obuf, r * TT_SUB), tile(y_hbm, dst_cur[0, r]), sem_out).start()

        for_rows(nv, scatter_row)

        @pl.when(i == nused - 1)
        def _():
            wait_scatter(nv)


def _moe(hf, plan, wg, wu, wd, layer):
    src, dst, block_e, nused, nvalid = plan
    wspec = lambda shape: pl.BlockSpec((None, None) + shape, lambda i, be, nu, nv: (layer, be[i], 0, 0))
    rows = lambda f: pl.BlockSpec((None, 1, TM_MOE), lambda i, be, nu, nv: (f(i), 0, 0), memory_space=pltpu.SMEM)
    return pl.pallas_call(
        _moe_body,
        grid_spec=pltpu.PrefetchScalarGridSpec(
            num_scalar_prefetch=3,
            grid=(N_BLOCKS,),
            in_specs=[rows(lambda i: i), rows(lambda i: jnp.minimum(i + 1, N_BLOCKS - 1)), rows(lambda i: i),
                      pl.BlockSpec(memory_space=pl.ANY),
                      wspec((D, D_EXPERT)), wspec((D, D_EXPERT)), wspec((D_EXPERT, D))],
            out_specs=pl.BlockSpec(memory_space=pl.ANY),
            scratch_shapes=[pltpu.VMEM((2, TM_MOE * TT_SUB, LANES), U32),
                            pltpu.VMEM((TM_MOE * TT_SUB, LANES), U32),
                            pltpu.VMEM((D, D_EXPERT), BF16), pltpu.VMEM((D, D_EXPERT), BF16),
                            pltpu.VMEM((D_EXPERT, D), BF16),
                            pltpu.SemaphoreType.DMA((2,)), pltpu.SemaphoreType.DMA]),
        out_shape=jax.ShapeDtypeStruct((2 * T * TT_SUB, LANES), U32),
        compiler_params=_params(("arbitrary",), MOE_VMEM_LIMIT),
        name="moe_experts",
    )(block_e, nused, nvalid, src.reshape(N_BLOCKS, 1, TM_MOE), src.reshape(N_BLOCKS, 1, TM_MOE),
      dst.reshape(N_BLOCKS, 1, TM_MOE), hf, wg, wu, wd)


INV_ROWS = 2048


def _invert_body(dest_ref, zeros_hbm, codes_ref, sem):
    i = pl.program_id(0)

    @pl.when(i == 0)
    def _():
        clear = pltpu.make_async_copy(zeros_hbm, codes_ref, sem)
        clear.start()
        clear.wait()

    def put(j, carry):
        codes_ref[dest_ref[0, j]] = i * INV_ROWS + j
        return carry

    lax.fori_loop(0, INV_ROWS, put, 0, unroll=16)


def _invert_slots(dest3):
    return pl.pallas_call(
        _invert_body,
        grid=(dest3.shape[0],),
        in_specs=[pl.BlockSpec((None, 1, INV_ROWS), lambda i: (i, 0, 0), memory_space=pltpu.SMEM),
                  pl.BlockSpec(memory_space=pl.ANY)],
        out_specs=pl.BlockSpec(memory_space=pltpu.SMEM),
        out_shape=jax.ShapeDtypeStruct((N_SLOTS,), jnp.int32),
        scratch_shapes=[pltpu.SemaphoreType.DMA],
        compiler_params=_params(("arbitrary",)),
        name="invert_slots",
    )(dest3, jnp.zeros((N_SLOTS,), jnp.int32))


def _dispatch_plan(info, cnt):
    e = info[:, 0:2].astype(jnp.int32)
    rank = info[:, 4:6].astype(jnp.int32)
    counts = cnt[0, EXPERT_LANE0:EXPERT_LANE0 + N_EXPERTS].astype(jnp.int32)
    padded = ((counts + TM_MOE - 1) // TM_MOE) * TM_MOE
    pends = jnp.cumsum(padded)
    pstarts = pends - padded
    first = jnp.sum(jnp.where(e[..., None] == jnp.arange(N_EXPERTS, dtype=jnp.int32), pstarts, 0), axis=-1)
    dest = first + rank
    codes = _invert_slots(dest.reshape(2 * T // INV_ROWS, 1, INV_ROWS))
    src = (codes >> 1) * TT_SUB
    dst = ((codes & 1) * T + (codes >> 1)) * TT_SUB
    block0 = jnp.arange(N_BLOCKS, dtype=jnp.int32) * TM_MOE
    block_e = jnp.clip(jnp.searchsorted(pends, block0, side='right', method='compare_all'),
                       0, N_EXPERTS - 1).astype(jnp.int32)
    nvalid = jnp.clip(pstarts[block_e] + counts[block_e] - block0, 0, TM_MOE).astype(jnp.int32)
    nused = (pends[-1] // TM_MOE).astype(jnp.int32).reshape(1)
    return src, dst, block_e, nused, nvalid


def _combined(x_ref, y0_ref, y1_ref, info_ref, g2_ref, rows):
    info = info_ref[...]
    lane = lax.broadcasted_iota(jnp.int32, info.shape, 1)
    w1 = jnp.sum(jnp.where(lane == 2, info, 0.0), axis=-1, keepdims=True)
    w2 = jnp.sum(jnp.where(lane == 3, info, 0.0), axis=-1, keepdims=True)
    y0 = _load_tt(y0_ref, rows, F32)
    y1 = _load_tt(y1_ref, rows, F32)
    return x_ref[...] + g2_ref[...] * (y0 * w1 + y1 * w2)


def _combine_body(x_ref, y0_ref, y1_ref, info_ref, g2_ref, nw_ref, *o_refs, final):
    x = _combined(x_ref, y0_ref, y1_ref, info_ref, g2_ref, TM_CMB)
    if final:
        ms = jnp.mean(x * x, axis=-1, keepdims=True)
        x = x * lax.rsqrt(ms + EPS) * nw_ref[...]

    def emit(ref):
        ref[...] = x

    _for_tile(o_refs, TM_CMB, emit)


def _combine(x1, y, info, mods, nw, *, final):
    tm = TM_CMB
    nt = T // tm
    if final:
        out_shape = [jax.ShapeDtypeStruct((T_S, D), F32), jax.ShapeDtypeStruct((T_P, D), F32)]
    else:
        out_shape = [jax.ShapeDtypeStruct((T, D), F32)]
    return pl.pallas_call(
        functools.partial(_combine_body, final=final),
        grid=(nt,),
        in_specs=[pl.BlockSpec((tm, D), lambda i: (i, 0)),
                  pl.BlockSpec((tm * TT_SUB, LANES), lambda i: (i, 0)),
                  pl.BlockSpec((tm * TT_SUB, LANES), lambda i: (i + nt, 0)),
                  pl.BlockSpec((tm, LANES), lambda i: (i, 0)),
                  _mod_spec(5, tm, 1),
                  pl.BlockSpec((1, D), lambda i: (0, 0))],
        out_specs=_row_specs(out_shape, tm, 1),
        out_shape=out_shape,
        compiler_params=_params(("arbitrary",)),
        name="combine_final" if final else "combine",
    )(x1, y, y, info, mods, nw.reshape(1, D))


def kernel(x_prompt, x_sample, c, state_ret_fwd, state_ret_bwd, state_lru_fwd, state_lru_bwd, c_ctx, w_mod, b_mod, norm_mix_w, norm_ffn_w, norm_final_w, w_in_even, w_out_even, gmlp_norm_w, gmlp_w_s, gmlp_b_s, ret_decay_fwd, ret_decay_bwd, ret_gn_w, w_in_odd, w_out_odd, conv_w, conv_b, conv_ln_w, conv_ln_b, lru_conv_w, lru_conv_b, lru_wa_fwd, lru_ba_fwd, lru_wx_fwd, lru_bx_fwd, lru_lam_fwd, lru_wa_bwd, lru_ba_bwd, lru_wx_bwd, lru_bx_bwd, lru_lam_bwd, router_grp_w, router_grp_b, router_exp_w, router_exp_b, moe_w_gate, moe_w_up, moe_w_down):
    xs = (x_sample.reshape(T_S, D), x_prompt.reshape(T_P, D))
    cond = jnp.concatenate([c, c_ctx[None, :], jnp.zeros((N_COND_PAD - B_S - 1, D), F32)], axis=0)
    m = _modulation(cond, w_mod, b_mod)
    mods_all = m.reshape(DEPTH, N_COND_PAD, N_MOD, D).transpose(0, 2, 1, 3)[:, :, :, None, :]

    ret_f = ret_b = lru_f = lru_b = None
    for l in range(DEPTH):
        mods = mods_all[l]
        if l % 2 == 0:
            e = l // 2
            p = _norm_in(xs, norm_mix_w[l], mods, w_in_even[e].astype(BF16), pack=False)
            b_full = jnp.repeat(gmlp_b_s[e].T.astype(F32), A_GDIM, axis=1)
            out_a = (_gmlp(p, gmlp_norm_w[e], gmlp_w_s[e].astype(BF16), b_full),)
            tabs = _retention_tables(ret_decay_fwd[e], ret_decay_bwd[e])
            cos, sin = _rope_tables()
            ob_s, _, _ = _retention(p, state_ret_fwd[:, e], state_ret_bwd[:, e], tabs, ret_gn_w[e], cos, sin,
                                    base_chunk=0, nb=B_S, nc=N_S // CHUNK, rope=True)
            zero_state = jnp.zeros((B_P, HEADS, HD, HD), F32)
            ob_p, ret_f, ret_b = _retention(p, zero_state, zero_state, tabs, ret_gn_w[e], cos, sin,
                                            base_chunk=T_S // CHUNK, nb=B_P, nc=N_P // CHUNK, rope=False)
            out_b = (ob_s, ob_p)
            w_out = w_out_even[e]
        else:
            o = l // 2
            p = _norm_in(xs, norm_mix_w[l], mods, w_in_odd[o].astype(BF16), pack=True)
            cw3 = jnp.concatenate([conv_w[o], jnp.zeros((1, W), F32)], axis=0).reshape(
                C_KSIZE + 1, N_STRIPS, LANES).transpose(1, 0, 2)
            conv_args = (cw3, conv_b[o], conv_ln_w[o], conv_ln_b[o])
            prm_f = _lru_params(lru_conv_w[o], lru_conv_b[o], lru_wa_fwd[o], lru_ba_fwd[o], lru_wx_fwd[o],
                                lru_bx_fwd[o], lru_lam_fwd[o])
            prm_b = _lru_params(lru_conv_w[o], lru_conv_b[o], lru_wa_bwd[o], lru_ba_bwd[o], lru_wx_bwd[o],
                                lru_bx_bwd[o], lru_lam_bwd[o])
            latent = dict(seq_row0=0, nseq=B_S, n=N_S)
            context = dict(seq_row0=T_S, nseq=B_P, n=N_P)
            out_a = (_od_conv(p, *conv_args, **latent), _od_conv(p, *conv_args, **context))
            yd_s, _, _ = _od_rglru(p, prm_f, prm_b, state_lru_fwd[:, o], state_lru_bwd[:, o], **latent)
            zero_h = jnp.zeros((B_P, W), F32)
            yd_p, lru_f, lru_b = _od_rglru(p, prm_f, prm_b, zero_h, zero_h, **context)
            out_b = (yd_s, yd_p)
            w_out = w_out_odd[o]

        rw = jnp.concatenate([router_grp_w[l], router_exp_w[l],
                              jnp.zeros((D, LANES - N_GROUPS - N_EXPERTS), F32)], axis=1)
        rw_hi = rw.astype(BF16)
        rw = jnp.concatenate([rw_hi, (rw - rw_hi.astype(F32)).astype(BF16)], axis=1)
        rb = jnp.concatenate([router_grp_b[l], router_exp_b[l],
                              jnp.zeros((LANES - N_GROUPS - N_EXPERTS,), F32)]).reshape(1, LANES)
        x1, hf, info, cnt = _out_proj(out_a, out_b, xs, mods, w_out.astype(BF16), norm_ffn_w[l], rw, rb,
                                      packed=(l % 2 == 1))
        y = _moe(hf, _dispatch_plan(info, cnt), moe_w_gate, moe_w_up, moe_w_down, l)
        xs = tuple(_combine(x1, y, info, mods, norm_final_w, final=(l == DEPTH - 1)))

    y_sample = xs[0].reshape(B_S, N_S, D)
    y_prompt = xs[1].reshape(B_P, N_P, D)
    return (y_prompt, y_sample, ret_f[:, None], ret_b[:, None], lru_f[:, None], lru_b[:, None])
```

```python
import functools

import jax
import jax.numpy as jnp
from jax import lax
from jax.experimental import pallas as pl
from jax.experimental.pallas import tpu as pltpu

F32 = jnp.float32
BF16 = jnp.bfloat16
U32 = jnp.uint32

D = 2048
B_P, N_P = 16, 256
B_S, N_S = 8, 4096
T_S = B_S * N_S
T_P = B_P * N_P
T = T_S + T_P
ROWS_PER_COND = 4096
N_COND_PAD = 16
DEPTH = 2
N_MOD = 6
EPS = 1e-6
GRID_W = 64
ROPE_BASE = 10000.0

W = 1024
HW = W // 2
HEADS = 8
HD = 128
CHUNK = 128
A_GROUPS = 4
A_GDIM = W // A_GROUPS
C_KSIZE = 31
D_KSIZE = 4
LRU_C = 8.0

N_GROUPS = 4
EXPERTS_PER_GROUP = 8
N_EXPERTS = 32
D_EXPERT = 512
LANES = 128
N_STRIPS = W // LANES
EXPERT_LANE0 = N_GROUPS

TM_IN = 512
TN_IN = 2048
TM_OUT = 512
TM_MOE = 512
N_SLOTS = 2 * T + N_EXPERTS * TM_MOE
N_BLOCKS = N_SLOTS // TM_MOE
TM_CMB = 512

VMEM_LIMIT = 56 * 1024 * 1024
MOE_VMEM_LIMIT = 60 * 1024 * 1024


def _params(sem, vmem=VMEM_LIMIT):
    return pltpu.CompilerParams(dimension_semantics=sem, vmem_limit_bytes=vmem)


def _cond_row(i, tm):
    return (i * tm) // ROWS_PER_COND


def _sigmoid(x):
    return 0.5 * jnp.tanh(0.5 * x) + 0.5


def _pack_pair(lo, hi, rounded=False):
    if not rounded:
        lo, hi = lo.astype(BF16).astype(F32), hi.astype(BF16).astype(F32)
    return (lax.bitcast_convert_type(lo, U32) >> 16) | lax.bitcast_convert_type(hi, U32)


def _unpack_pair(w, dtype):
    return (lax.bitcast_convert_type(w << 16, F32).astype(dtype),
            lax.bitcast_convert_type(w & U32(0xFFFF0000), F32).astype(dtype))


def _pack_halves(x):
    k = x.shape[1] // 2
    return _pack_pair(x[:, :k], x[:, k:])


def _unpack_halves(w, dtype):
    return jnp.concatenate(_unpack_pair(w, dtype), axis=1)


TT_SUB = 8


def _store_tt(ref, x, rows, rounded=False):
    for c in range(TT_SUB):
        ref[pl.ds(c, rows, stride=TT_SUB), :] = _pack_pair(x[:, c * LANES:(c + 1) * LANES],
                                                           x[:, W + c * LANES:W + (c + 1) * LANES], rounded)


def _load_tt(ref, rows, dtype):
    lo, hi = [], []
    for c in range(TT_SUB):
        a, b = _unpack_pair(ref[pl.ds(c, rows, stride=TT_SUB), :], dtype)
        lo.append(a)
        hi.append(b)
    return jnp.concatenate(lo + hi, axis=1)


def _row_specs(arrs, tm, ngrid):
    if len(arrs) == 1:
        maps = [lambda i: i]
    else:
        n_s = T_S // tm
        maps = [lambda i: jnp.minimum(i, n_s - 1), lambda i: jnp.maximum(i - n_s, 0)]
    width = arrs[0].shape[1]
    if ngrid == 1:
        return [pl.BlockSpec((tm, width), lambda i, f=f: (f(i), 0)) for f in maps]
    return [pl.BlockSpec((tm, width), lambda i, j, f=f: (f(i), 0)) for f in maps]


def _tile_value(refs, tm):
    if len(refs) == 1:
        return refs[0][...]
    return jnp.where(pl.program_id(0) < T_S // tm, refs[0][...], refs[1][...])


def _for_tile(refs, tm, fn):
    if len(refs) == 1:
        fn(refs[0])
        return
    i = pl.program_id(0)

    @pl.when(i < T_S // tm)
    def _():
        fn(refs[0])

    @pl.when(i >= T_S // tm)
    def _():
        fn(refs[1])


def _mod_body(c_ref, w_ref, b_ref, o_ref):
    c = c_ref[...]
    a = (c * _sigmoid(c)).astype(BF16)
    o_ref[...] = jnp.dot(a, w_ref[...].astype(BF16), preferred_element_type=F32) + b_ref[...]


def _modulation(cond, w_mod, b_mod):
    tn = 1024
    return pl.pallas_call(
        _mod_body,
        grid=(DEPTH, N_MOD * D // tn),
        in_specs=[pl.BlockSpec((N_COND_PAD, D), lambda l, j: (0, 0)),
                  pl.BlockSpec((None, D, tn), lambda l, j: (l, 0, j)),
                  pl.BlockSpec((None, 1, tn), lambda l, j: (l, 0, j))],
        out_specs=pl.BlockSpec((None, N_COND_PAD, tn), lambda l, j: (l, 0, j)),
        out_shape=jax.ShapeDtypeStruct((DEPTH, N_COND_PAD, N_MOD * D), F32),
        compiler_params=_params(("arbitrary", "arbitrary")),
        name="modulation",
    )(cond, w_mod, b_mod.reshape(DEPTH, 1, N_MOD * D))


def _mod_spec(k, tm, ngrid):
    if ngrid == 1:
        return pl.BlockSpec((None, None, 1, D), lambda i: (k, _cond_row(i, tm), 0, 0))
    return pl.BlockSpec((None, None, 1, D), lambda i, j: (k, _cond_row(i, tm), 0, 0))


def _norm_in_body(*refs, pack):
    nw_ref, sc_ref, sh_ref, w_ref, p_ref, h_scr = refs[-6:]

    @pl.when(pl.program_id(1) == 0)
    def _():
        x = _tile_value(refs[:-6], TM_IN)
        ms = jnp.mean(x * x, axis=-1, keepdims=True)
        h = x * lax.rsqrt(ms + EPS) * nw_ref[...]
        h = h * (1.0 + sc_ref[...]) + sh_ref[...]
        h_scr[...] = h.astype(BF16)

    r = jnp.dot(h_scr[...], w_ref[...], preferred_element_type=F32)
    if pack:
        p_ref[...] = jnp.concatenate([_pack_halves(r[:, k * W:(k + 1) * W]) for k in range(TN_IN // W)], axis=1)
    else:
        p_ref[...] = r.astype(BF16)


def _norm_in(xs, nw, mods, w_bf16, *, pack):
    n = w_bf16.shape[1]
    if pack:
        out_spec = pl.BlockSpec((TM_IN, TN_IN // 2), lambda i, j: (i, j))
        out_shape = jax.ShapeDtypeStruct((T, n // 2), U32)
    else:
        out_spec = pl.BlockSpec((TM_IN, TN_IN), lambda i, j: (i, j))
        out_shape = jax.ShapeDtypeStruct((T, n), BF16)
    return pl.pallas_call(
        functools.partial(_norm_in_body, pack=pack),
        grid=(T // TM_IN, n // TN_IN),
        in_specs=_row_specs(xs, TM_IN, 2) + [
            pl.BlockSpec((1, D), lambda i, j: (0, 0)),
            _mod_spec(1, TM_IN, 2),
            _mod_spec(0, TM_IN, 2),
            pl.BlockSpec((D, TN_IN), lambda i, j: (0, j))],
        out_specs=out_spec,
        out_shape=out_shape,
        scratch_shapes=[pltpu.VMEM((TM_IN, D), BF16)],
        compiler_params=_params(("arbitrary", "arbitrary")),
        name="norm_in",
    )(*xs, nw.reshape(1, D), mods, mods, w_bf16)


GMLP_ROWS = 512


def _gmlp_body(u_ref, v_ref, lnw_ref, ws_ref, b_ref, o_ref):
    for c in range(GMLP_ROWS // CHUNK):
        rows = pl.ds(c * CHUNK, CHUNK)
        v = jax.nn.gelu(v_ref[rows, :].astype(F32))
        mu = jnp.mean(v, axis=-1, keepdims=True)
        var = jnp.mean(jnp.square(v - mu), axis=-1, keepdims=True)
        vn = ((v - mu) * lax.rsqrt(var + EPS) * lnw_ref[...]).astype(BF16)
        z = jnp.concatenate(
            [jnp.dot(ws_ref[g], vn[:, g * A_GDIM:(g + 1) * A_GDIM], preferred_element_type=F32)
             for g in range(A_GROUPS)], axis=1) + b_ref[...]
        u = jax.nn.gelu(u_ref[rows, :].astype(F32))
        o_ref[rows, :] = (u * z).astype(BF16)


def _gmlp(p, lnw, ws_bf16, b_full):
    return pl.pallas_call(
        _gmlp_body,
        grid=(T // GMLP_ROWS,),
        in_specs=[pl.BlockSpec((GMLP_ROWS, W), lambda i: (i, 0)),
                  pl.BlockSpec((GMLP_ROWS, W), lambda i: (i, 1)),
                  pl.BlockSpec((1, W), lambda i: (0, 0)),
                  pl.BlockSpec((A_GROUPS, CHUNK, CHUNK), lambda i: (0, 0, 0)),
                  pl.BlockSpec((CHUNK, W), lambda i: (0, 0))],
        out_specs=pl.BlockSpec((GMLP_ROWS, W), lambda i: (i, 0)),
        out_shape=jax.ShapeDtypeStruct((T, W), BF16),
        compiler_params=_params(("arbitrary",)),
        name="gmlp",
    )(p, p, lnw.reshape(1, W), ws_bf16, b_full)


RET_SUB = 8


def _roped_qk(q_ref, k_ref, cos_ref, sin_ref, rows, h, rope):
    cols = slice(h * HD, (h + 1) * HD)
    q = q_ref[rows, cols].astype(F32)
    k = k_ref[rows, cols].astype(F32) * (HD ** -0.5)
    if rope:
        c = cos_ref[rows, :]
        s = sin_ref[rows, :]
        q = q * c + pltpu.roll(q, HD // 2, axis=1) * s
        k = k * c + pltpu.roll(k, HD // 2, axis=1) * s
    return q, k


def _kt_v(kd, v):
    return lax.dot_general(kd, v, (((0,), (0,)), ((), ())), preferred_element_type=F32)


def _ret_bwd_body(q_ref, k_ref, v_ref, cos_ref, sin_ref, s0_ref, dq_ref, dk_ref, dc_ref,
                  ob_ref, sfin_ref, s_scr, *, rope, nc, nsub):
    c = pl.program_id(1)

    @pl.when(c == 0)
    def _():
        s_scr[...] = s0_ref[...]

    for sub in reversed(range(nsub)):
        rows = slice(sub * CHUNK, (sub + 1) * CHUNK)
        for h in range(HEADS):
            cols = slice(h * HD, (h + 1) * HD)
            q, k = _roped_qk(q_ref, k_ref, cos_ref, sin_ref, rows, h, rope)
            v = v_ref[rows, cols]
            s = s_scr[h]
            ob_ref[rows, cols] = jnp.dot(q.astype(BF16), s.astype(BF16), preferred_element_type=F32) * dq_ref[h]
            kd = (k * dk_ref[h]).astype(BF16)
            s_scr[h] = s * dc_ref[h] + _kt_v(kd, v)

    @pl.when(c == nc - 1)
    def _():
        sfin_ref[...] = s_scr[...]


def _ret_fwd_body(q_ref, k_ref, v_ref, g_ref, cos_ref, sin_ref, ob_ref, s0_ref, m_ref, dq_ref, dk_ref, dc_ref,
                  gn_ref, o_ref, sfin_ref, s_scr, *, rope, nc, nsub):
    c = pl.program_id(1)

    @pl.when(c == 0)
    def _():
        s_scr[...] = s0_ref[...]

    for sub in range(nsub):
        rows = slice(sub * CHUNK, (sub + 1) * CHUNK)
        for h in range(HEADS):
            cols = slice(h * HD, (h + 1) * HD)
            q, k = _roped_qk(q_ref, k_ref, cos_ref, sin_ref, rows, h, rope)
            qb = q.astype(BF16)
            v = v_ref[rows, cols]
            s = s_scr[h]
            scores = lax.dot_general(qb, k.astype(BF16), (((1,), (1,)), ((), ())),
                                     preferred_element_type=F32) * m_ref[h]
            o = (jnp.dot(scores.astype(BF16), v, preferred_element_type=F32)
                 + jnp.dot(qb, s.astype(BF16), preferred_element_type=F32) * dq_ref[h]
                 + ob_ref[rows, cols])
            kd = (k * dk_ref[h]).astype(BF16)
            s_scr[h] = s * dc_ref[h] + _kt_v(kd, v)
            mu = jnp.mean(o, axis=-1, keepdims=True)
            var = jnp.mean(jnp.square(o - mu), axis=-1, keepdims=True)
            on = (o - mu) * lax.rsqrt(var + EPS) * gn_ref[:, cols]
            g = g_ref[rows, cols].astype(F32)
            o_ref[rows, cols] = (g * _sigmoid(g) * on).astype(BF16)

    @pl.when(c == nc - 1)
    def _():
        sfin_ref[...] = s_scr[...]


def _retention(p, s_f0, s_b0, tabs, gn_w, cos, sin, *, base_chunk, nb, nc, rope):
    m_tab, dq_f, dk_f, dc_f, dq_b, dk_b, dc_b = tabs
    nsub = min(RET_SUB, nc)
    rows = nsub * CHUNK
    ns = nc // nsub
    base = base_chunk // nsub
    state_spec = pl.BlockSpec((None, HEADS, HD, HD), lambda b, c: (b, 0, 0, 0))
    tab_spec = pl.BlockSpec((HEADS, HD, HD), lambda b, c: (0, 0, 0))
    dc_spec = pl.BlockSpec((HEADS, 1, HD), lambda b, c: (0, 0, 0))
    state_shape = jax.ShapeDtypeStruct((nb, HEADS, HD, HD), F32)

    def step(c, rev):
        return ns - 1 - c if rev else c

    def col(j, rev):
        return pl.BlockSpec((rows,W), lambda b, c: (base + b * ns + step(c, rev), j))

    def rope_spec(rev):
        if not rope:
            return pl.BlockSpec((rows,HD), lambda b, c: (0, 0))
        return pl.BlockSpec((rows,HD), lambda b, c: (step(c, rev), 0))

    def local(rev):
        return pl.BlockSpec((rows,W), lambda b, c: (b * ns + step(c, rev), 0))

    n_rows = nb * nc * CHUNK
    ob, s_b = pl.pallas_call(
        functools.partial(_ret_bwd_body, rope=rope, nc=ns, nsub=nsub),
        grid=(nb, ns),
        in_specs=[col(2, True), col(3, True), col(4, True), rope_spec(True), rope_spec(True),
                  state_spec, tab_spec, tab_spec, dc_spec],
        out_specs=[local(True), state_spec],
        out_shape=[jax.ShapeDtypeStruct((n_rows, W), F32), state_shape],
        scratch_shapes=[pltpu.VMEM((HEADS, HD, HD), F32)],
        compiler_params=_params(("arbitrary", "arbitrary")),
        name="retention_bwd",
    )(p, p, p, cos, sin, s_b0, dq_b, dk_b, dc_b)

    o, s_f = pl.pallas_call(
        functools.partial(_ret_fwd_body, rope=rope, nc=ns, nsub=nsub),
        grid=(nb, ns),
        in_specs=[col(2, False), col(3, False), col(4, False), col(5, False), rope_spec(False), rope_spec(False),
                  local(False), state_spec, tab_spec, tab_spec, tab_spec, dc_spec,
                  pl.BlockSpec((1, W), lambda b, c: (0, 0))],
        out_specs=[local(False), state_spec],
        out_shape=[jax.ShapeDtypeStruct((n_rows, W), BF16), state_shape],
        scratch_shapes=[pltpu.VMEM((HEADS, HD, HD), F32)],
        compiler_params=_params(("arbitrary", "arbitrary")),
        name="retention_fwd",
    )(p, p, p, p, cos, sin, ob, s_f0, m_tab, dq_f, dk_f, dc_f, gn_w.reshape(1, W))
    return o, s_f, s_b


def _retention_tables(decay_fwd, decay_bwd):
    lg_f = jax.nn.log_sigmoid(decay_fwd.astype(F32))[:, None, None]
    lg_b = jax.nn.log_sigmoid(decay_bwd.astype(F32))[:, None, None]
    pos = jnp.arange(CHUNK, dtype=F32)
    rel = pos[:, None] - pos[None, :]
    m_tab = (jnp.where(rel >= 0, jnp.exp(lg_f * jnp.maximum(rel, 0.0)), 0.0)
             + jnp.where(rel <= 0, jnp.exp(lg_b * jnp.maximum(-rel, 0.0)), 0.0))
    ones = jnp.ones((1, 1, HD), F32)
    col = pos[None, :, None]
    dq_f = jnp.exp(lg_f * (col + 1.0)) * ones
    dk_f = jnp.exp(lg_f * (CHUNK - 1.0 - col)) * ones
    dq_b = jnp.exp(lg_b * (CHUNK - col)) * ones
    dk_b = jnp.exp(lg_b * col) * ones
    dc_f = jnp.exp(lg_f * CHUNK) * ones
    dc_b = jnp.exp(lg_b * CHUNK) * ones
    return m_tab, dq_f, dk_f, dc_f, dq_b, dk_b, dc_b


def _rope_tables():
    rows = N_S // GRID_W
    row = jnp.repeat(jnp.arange(rows, dtype=F32), GRID_W)
    colp = jnp.tile(jnp.arange(GRID_W, dtype=F32), rows)
    quarter = HD // 4
    inv_freq = ROPE_BASE ** (-jnp.arange(quarter, dtype=F32) / quarter)
    ang = jnp.concatenate([row[:, None] * inv_freq, colp[:, None] * inv_freq], axis=-1)
    cos, sin = jnp.cos(ang), jnp.sin(ang)
    return jnp.concatenate([cos, cos], axis=-1), jnp.concatenate([-sin, sin], axis=-1)


OD_TT = 128
OD_G = 8
CONV_HALO = 16
LRU_HALO = 8


def _od_rows(seq_row0, n, g, tj):
    return [seq_row0 + (g * OD_G + b) * n + tj * OD_TT for b in range(OD_G)]


def _od_in_copies(p_hbm, buf, sem, *, col, halo, seq_row0, n, g, tj):
    nt = n // OD_TT
    out = []
    for b, base in enumerate(_od_rows(seq_row0, n, g, tj)):
        def src(r, k):
            return p_hbm.at[pl.ds(r, k), pl.ds(col * HW, HW)]

        out.append((None, pltpu.make_async_copy(src(base, OD_TT), buf.at[pl.ds(halo, OD_TT), b, :], sem)))
        if halo:
            out.append((tj > 0, pltpu.make_async_copy(src(base - halo, halo), buf.at[pl.ds(0, halo), b, :], sem)))
            out.append((tj < nt - 1, pltpu.make_async_copy(src(base + OD_TT, halo),
                                                           buf.at[pl.ds(halo + OD_TT, halo), b, :], sem)))
    return out


def _od_run(copies, op):
    for cond, cp in copies:
        fn = cp.start if op == "start" else cp.wait
        if cond is None:
            fn()
        else:
            pl.when(cond)(fn)


def _od_zero_halo(buf, halo, n, tj):
    nt = n // OD_TT
    zeros = jnp.zeros((halo, OD_G, HW), U32)

    @pl.when(tj == 0)
    def _():
        buf[0:halo] = zeros

    @pl.when(tj == nt - 1)
    def _():
        buf[halo + OD_TT:2 * halo + OD_TT] = zeros


def _od_out_copies(obuf, out_hbm, sem, *, seq_row0, n, g, tj):
    return [(None, pltpu.make_async_copy(obuf.at[:, b, :], out_hbm.at[pl.ds(base - seq_row0, OD_TT), :], sem))
            for b, base in enumerate(_od_rows(seq_row0, n, g, tj))]


def _od_unpack(buf, rows):
    return _unpack_halves(buf[...].reshape(rows, HW), F32)


def _od_pipeline(q, nq, in_copies, sem_slots=2):
    s = q % 2

    @pl.when(q == 0)
    def _():
        _od_run(in_copies(q, s), "start")

    _od_run(in_copies(q, s), "wait")

    @pl.when(q + 1 < nq)
    def _():
        _od_run(in_copies(q + 1, 1 - s), "start")

    return s


def _od_emit(q, nq, s, obuf, words, out_copies):
    @pl.when(q >= 2)
    def _():
        _od_run(out_copies(s), "wait")

    obuf[s] = words.reshape(OD_TT, OD_G, HW)
    _od_run(out_copies(s), "start")

    @pl.when(q == nq - 1)
    def _():
        _od_run(out_copies(s), "wait")

        @pl.when(q >= 1)
        def _():
            _od_run(out_copies(1 - s), "wait")


CONV_RB = 64


def _od_conv_body(p_hbm, cw_ref, cb_ref, lnw_ref, lnb_ref, yc_hbm, xin, glu, ybuf, obuf, sem_in, sem_out,
                  *, seq_row0, n, ng):
    nt = n // OD_TT
    nq = ng * nt
    q = pl.program_id(0)
    halo = CONV_HALO
    rows_in = (OD_TT + 2 * halo) * OD_G
    rows = OD_TT * OD_G

    def in_copies(step, slot):
        out = []
        for br in range(2):
            out += _od_in_copies(p_hbm, xin.at[br, slot], sem_in.at[slot], col=br, halo=halo,
                                 seq_row0=seq_row0, n=n, g=step // nt, tj=step % nt)
        return out

    s = _od_pipeline(q, nq, in_copies)
    g, tj = q // nt, q % nt
    for br in range(2):
        _od_zero_halo(xin.at[br, s], halo, n, tj)
    glu[...] = _od_unpack(xin.at[0, s], rows_in) * _sigmoid(_od_unpack(xin.at[1, s], rows_in))

    def strip(si, carry):
        lanes = pl.ds(pl.multiple_of(si * LANES, LANES), LANES)
        for rb in range(rows // CONV_RB):
            acc = jnp.zeros((CONV_RB // 8, 8, LANES), F32)
            for k in range(C_KSIZE):
                wk = cw_ref[si, pl.ds(k, 1), :]
                r0 = rb * CONV_RB + (halo - C_KSIZE // 2 + k) * OD_G
                acc = acc + glu[r0:r0 + CONV_RB, lanes].reshape(CONV_RB // 8, 8, LANES) * wk[None]
            ybuf[rb * CONV_RB:(rb + 1) * CONV_RB, lanes] = acc.reshape(CONV_RB, LANES)
        return carry

    lax.fori_loop(0, N_STRIPS, strip, 0)
    y = ybuf[...] + cb_ref[...]
    mu = jnp.mean(y, axis=-1, keepdims=True)
    var = jnp.mean(jnp.square(y - mu), axis=-1, keepdims=True)
    yn = (y - mu) * lax.rsqrt(var + EPS) * lnw_ref[...] + lnb_ref[...]
    yo = yn * _sigmoid(yn)

    def out_copies(slot):
        return _od_out_copies(obuf.at[slot], yc_hbm, sem_out.at[slot], seq_row0=seq_row0, n=n, g=g, tj=tj)

    _od_emit(q, nq, s, obuf, _pack_halves(yo), out_copies)


def _od_conv(p, cw3, cb, lnw, lnb, *, seq_row0, nseq, n):
    ng = nseq // OD_G
    nq = ng * (n // OD_TT)
    vec = pl.BlockSpec((1, W), lambda q: (0, 0))
    t_in = OD_TT + 2 * CONV_HALO
    return pl.pallas_call(
        functools.partial(_od_conv_body, seq_row0=seq_row0, n=n, ng=ng),
        grid=(nq,),
        in_specs=[pl.BlockSpec(memory_space=pl.ANY),
                  pl.BlockSpec((N_STRIPS, C_KSIZE + 1, LANES), lambda q: (0, 0, 0)), vec, vec, vec],
        out_specs=pl.BlockSpec(memory_space=pl.ANY),
        out_shape=jax.ShapeDtypeStruct((nseq * n, HW), U32),
        scratch_shapes=[pltpu.VMEM((2, 2, t_in, OD_G, HW), U32),
                        pltpu.VMEM((t_in * OD_G, W), F32),
                        pltpu.VMEM((OD_TT * OD_G, W), F32),
                        pltpu.VMEM((2, OD_TT, OD_G, HW), U32),
                        pltpu.SemaphoreType.DMA((2,)), pltpu.SemaphoreType.DMA((2,))],
        compiler_params=_params(("arbitrary",)),
        name="conv_module",
    )(p, cw3, cb.reshape(1, W), lnw.reshape(1, W), lnb.reshape(1, W))


def _od_gates(x, cw_ref, cb_ref, wa_ref, ba_ref, wx_ref, bx_ref, sp_ref, a_scr, b_scr):
    rows = OD_TT * OD_G
    xc = cb_ref[...]
    for k in range(D_KSIZE):
        r0 = (LRU_HALO - D_KSIZE // 2 + k) * OD_G
        xc = xc + cw_ref[k:k + 1, :] * x[r0:r0 + rows, :]
    xb = xc.astype(BF16)
    r = jnp.concatenate([jnp.dot(xb[:, h * HD:(h + 1) * HD], wa_ref[h], preferred_element_type=F32)
                         for h in range(HEADS)], axis=1) + ba_ref[...]
    g = jnp.concatenate([jnp.dot(xb[:, h * HD:(h + 1) * HD], wx_ref[h], preferred_element_type=F32)
                         for h in range(HEADS)], axis=1) + bx_ref[...]
    log_a = (-LRU_C) * _sigmoid(r) * sp_ref[...]
    th = jnp.tanh(log_a)
    one_minus_a2 = -2.0 * th / (1.0 - th)
    a_scr[...] = jnp.exp(log_a)
    b_scr[...] = jnp.sqrt(jnp.maximum(one_minus_a2, 0.0)) * (_sigmoid(g) * xc)


def _od_scan(a_scr, b_scr, out_ref, h, reverse):
    def step(t, h):
        tr = (OD_TT - 1 - t) if reverse else t
        rows = pl.ds(pl.multiple_of(tr * OD_G, OD_G), OD_G)
        h = a_scr[rows, :] * h + b_scr[rows, :]
        out_ref[rows, :] = h
        return h

    return lax.fori_loop(0, OD_TT, step, h, unroll=8)


def _od_lru_fwd_body(p_hbm, cw_ref, cb_ref, wa_ref, ba_ref, wx_ref, bx_ref, sp_ref, h0_ref,
                     hf_ref, hfin_ref, xin, a_scr, b_scr, h_scr, sem_in, *, seq_row0, n, ng):
    nt = n // OD_TT
    nq = ng * nt
    q = pl.program_id(0)

    def in_copies(step, slot):
        return _od_in_copies(p_hbm, xin.at[slot], sem_in.at[slot], col=2, halo=LRU_HALO,
                             seq_row0=seq_row0, n=n, g=step // nt, tj=step % nt)

    s = _od_pipeline(q, nq, in_copies)
    tj = q % nt
    _od_zero_halo(xin.at[s], LRU_HALO, n, tj)

    @pl.when(tj == 0)
    def _():
        h_scr[...] = h0_ref[...]

    x = _od_unpack(xin.at[s], (OD_TT + 2 * LRU_HALO) * OD_G)
    _od_gates(x, cw_ref, cb_ref, wa_ref, ba_ref, wx_ref, bx_ref, sp_ref, a_scr, b_scr)
    h = _od_scan(a_scr, b_scr, hf_ref, h_scr[...], False)
    h_scr[...] = h

    @pl.when(tj == nt - 1)
    def _():
        hfin_ref[...] = h


def _od_lru_bwd_body(p_hbm, hf_ref, cw_ref, cb_ref, wa_ref, ba_ref, wx_ref, bx_ref, sp_ref, h0_ref,
                     yd_hbm, hfin_ref, xin, gin, a_scr, b_scr, hb_scr, obuf, h_scr, sem_in, sem_out,
                     *, seq_row0, n, ng):
    nt = n // OD_TT
    nq = ng * nt
    q = pl.program_id(0)

    def in_copies(step, slot):
        where = dict(seq_row0=seq_row0, n=n, g=step // nt, tj=nt - 1 - step % nt)
        return (_od_in_copies(p_hbm, xin.at[slot], sem_in.at[slot], col=2, halo=LRU_HALO, **where)
                + _od_in_copies(p_hbm, gin.at[slot], sem_in.at[slot], col=3, halo=0, **where))

    s = _od_pipeline(q, nq, in_copies)
    g, tj = q // nt, nt - 1 - q % nt
    _od_zero_halo(xin.at[s], LRU_HALO, n, tj)

    @pl.when(tj == nt - 1)
    def _():
        h_scr[...] = h0_ref[...]

    x = _od_unpack(xin.at[s], (OD_TT + 2 * LRU_HALO) * OD_G)
    _od_gates(x, cw_ref, cb_ref, wa_ref, ba_ref, wx_ref, bx_ref, sp_ref, a_scr, b_scr)
    h = _od_scan(a_scr, b_scr, hb_scr, h_scr[...], True)
    h_scr[...] = h

    @pl.when(tj == 0)
    def _():
        hfin_ref[...] = h

    gd = _od_unpack(gin.at[s], OD_TT * OD_G)
    yd = (hf_ref[...] + hb_scr[...]) * jax.nn.gelu(gd)

    def out_copies(slot):
        return _od_out_copies(obuf.at[slot], yd_hbm, sem_out.at[slot], seq_row0=seq_row0, n=n, g=g, tj=tj)

    _od_emit(q, nq, s, obuf, _pack_halves(yd), out_copies)


def _od_rglru(p, prm_f, prm_b, h0_f, h0_b, *, seq_row0, nseq, n):
    ng = nseq // OD_G
    nt = n // OD_TT
    nq = ng * nt
    rows = OD_TT * OD_G
    vec = pl.BlockSpec((1, W), lambda q: (0, 0))
    wsp = pl.BlockSpec((HEADS, HD, HD), lambda q: (0, 0, 0))
    prm_specs = [pl.BlockSpec((8, W), lambda q: (0, 0)), vec, wsp, vec, wsp, vec, vec,
                 pl.BlockSpec((OD_G, W), lambda q: (q // nt, 0))]
    state_spec = pl.BlockSpec((OD_G, W), lambda q: (q // nt, 0))
    state_shape = jax.ShapeDtypeStruct((nseq, W), F32)
    t_in = OD_TT + 2 * LRU_HALO
    tile_f32 = pltpu.VMEM((rows, W), F32)

    hf, s_f = pl.pallas_call(
        functools.partial(_od_lru_fwd_body, seq_row0=seq_row0, n=n, ng=ng),
        grid=(nq,),
        in_specs=[pl.BlockSpec(memory_space=pl.ANY)] + prm_specs,
        out_specs=[pl.BlockSpec((rows, W), lambda q: (q, 0)), state_spec],
        out_shape=[jax.ShapeDtypeStruct((nq * rows, W), F32), state_shape],
        scratch_shapes=[pltpu.VMEM((2, t_in, OD_G, HW), U32), tile_f32, tile_f32, pltpu.VMEM((OD_G, W), F32),
                        pltpu.SemaphoreType.DMA((2,))],
        compiler_params=_params(("arbitrary",)),
        name="rglru_fwd",
    )(p, *prm_f, h0_f)

    yd, s_b = pl.pallas_call(
        functools.partial(_od_lru_bwd_body, seq_row0=seq_row0, n=n, ng=ng),
        grid=(nq,),
        in_specs=[pl.BlockSpec(memory_space=pl.ANY),
                  pl.BlockSpec((rows, W), lambda q: ((q // nt) * nt + nt - 1 - q % nt, 0))] + prm_specs,
        out_specs=[pl.BlockSpec(memory_space=pl.ANY), state_spec],
        out_shape=[jax.ShapeDtypeStruct((nseq * n, HW), U32), state_shape],
        scratch_shapes=[pltpu.VMEM((2, t_in, OD_G, HW), U32), pltpu.VMEM((2, OD_TT, OD_G, HW), U32),
                        tile_f32, tile_f32, tile_f32, pltpu.VMEM((2, OD_TT, OD_G, HW), U32),
                        pltpu.VMEM((OD_G, W), F32),
                        pltpu.SemaphoreType.DMA((2,)), pltpu.SemaphoreType.DMA((2,))],
        compiler_params=_params(("arbitrary",)),
        name="rglru_bwd",
    )(p, hf, *prm_b, h0_b)
    return yd, s_f, s_b


def _lru_params(cw, cb, wa, ba, wx, bx, lam):
    cw_pad = jnp.concatenate([cw, jnp.zeros((8 - D_KSIZE, W), F32)], axis=0)
    sp = jax.nn.softplus(-lam.astype(F32)).reshape(1, W)
    return (cw_pad, cb.reshape(1, W), wa.astype(BF16), ba.reshape(1, W), wx.astype(BF16), bx.reshape(1, W), sp)


def _out_body(*refs, n_a, n_b, n_x, packed):
    a_refs = refs[:n_a]
    b_refs = refs[n_a:n_a + n_b]
    x_refs = refs[n_a + n_b:n_a + n_b + n_x]
    (g1_ref, wo_ref, nw_ref, sc_ref, sh_ref, rw_ref, rb_ref,
     x1_ref, hf_ref, info_ref, cnt_ref, carry) = refs[n_a + n_b + n_x:]
    i = pl.program_id(0)

    @pl.when(i == 0)
    def _():
        carry[...] = jnp.zeros_like(carry)

    def branch(refs_, row0):
        v = _tile_value(refs_, TM_OUT)
        if packed:
            lo, hi = _unpack_pair(v, BF16)
            return (jnp.dot(lo, wo_ref[row0:row0 + HW, :], preferred_element_type=F32)
                    + jnp.dot(hi, wo_ref[row0 + HW:row0 + W, :], preferred_element_type=F32))
        return jnp.dot(v, wo_ref[row0:row0 + W, :], preferred_element_type=F32)

    x1 = _tile_value(x_refs, TM_OUT) + g1_ref[...] * (branch(a_refs, 0) + branch(b_refs, W))
    x1_ref[...] = x1
    ms = jnp.mean(x1 * x1, axis=-1, keepdims=True)
    hf = x1 * lax.rsqrt(ms + EPS) * nw_ref[...]
    hf = hf * (1.0 + sc_ref[...]) + sh_ref[...]
    hf_hi = hf.astype(BF16)
    hf_hi32 = hf_hi.astype(F32)
    _store_tt(hf_ref, hf_hi32, TM_OUT, rounded=True)

    hf_lo = (hf - hf_hi32).astype(BF16)
    l2 = jnp.dot(hf_hi, rw_ref[...], preferred_element_type=F32)
    logits = (l2[:, :LANES] + l2[:, LANES:]
              + jnp.dot(hf_lo, rw_ref[:, :LANES], preferred_element_type=F32) + rb_ref[...])
    lane = lax.broadcasted_iota(jnp.int32, (TM_OUT, LANES), 1).astype(F32)
    neg = -jnp.inf
    gmask = lane < N_GROUPS
    mg = jnp.max(jnp.where(gmask, logits, neg), axis=-1, keepdims=True)
    gstar = jnp.min(jnp.where(gmask & (logits == mg), lane, float(LANES)), axis=-1, keepdims=True)
    denom = jnp.sum(jnp.where(gmask, jnp.exp(logits - mg), 0.0), axis=-1, keepdims=True)
    psel = 1.0 / denom
    lo = EXPERT_LANE0 + EXPERTS_PER_GROUP * gstar
    emask = (lane >= lo) & (lane < lo + EXPERTS_PER_GROUP)
    v1 = jnp.max(jnp.where(emask, logits, neg), axis=-1, keepdims=True)
    i1 = jnp.min(jnp.where(emask & (logits == v1), lane, float(LANES)), axis=-1, keepdims=True)
    em2 = emask & (lane != i1)
    v2 = jnp.max(jnp.where(em2, logits, neg), axis=-1, keepdims=True)
    i2 = jnp.min(jnp.where(em2 & (logits == v2), lane, float(LANES)), axis=-1, keepdims=True)
    e21 = jnp.exp(v2 - v1)
    w1 = psel / (1.0 + e21)
    w2 = psel * e21 / (1.0 + e21)

    memb = jnp.where((lane == i1) | (lane == i2), 1.0, 0.0)
    r_i = lax.broadcasted_iota(jnp.int32, (TM_OUT, TM_OUT), 0)
    c_i = lax.broadcasted_iota(jnp.int32, (TM_OUT, TM_OUT), 1)
    lower = jnp.where(r_i > c_i, 1.0, 0.0).astype(BF16)
    before = jnp.dot(lower, memb.astype(BF16), preferred_element_type=F32) + carry[0:1, :]
    rank1 = jnp.sum(jnp.where(lane == i1, before, 0.0), axis=-1, keepdims=True)
    rank2 = jnp.sum(jnp.where(lane == i2, before, 0.0), axis=-1, keepdims=True)
    new_carry = carry[...] + jnp.sum(memb, axis=0, keepdims=True)
    carry[...] = new_carry
    cnt_ref[...] = new_carry

    e1 = i1 - EXPERT_LANE0
    e2 = i2 - EXPERT_LANE0
    info = jnp.where(lane == 0, e1, jnp.where(lane == 1, e2, jnp.where(lane == 2, w1, jnp.where(
        lane == 3, w2, jnp.where(lane == 4, rank1, jnp.where(lane == 5, rank2, 0.0))))))
    info_ref[...] = info


def _out_proj(a_parts, b_parts, xs, mods, wo_bf16, nw, rw, rb, *, packed):
    tm = TM_OUT
    row = lambda n: pl.BlockSpec((tm, n), lambda i: (i, 0))
    vec = pl.BlockSpec((1, D), lambda i: (0, 0))
    return pl.pallas_call(
        functools.partial(_out_body, n_a=len(a_parts), n_b=len(b_parts), n_x=len(xs), packed=packed),
        grid=(T // tm,),
        in_specs=_row_specs(a_parts, tm, 1) + _row_specs(b_parts, tm, 1) + _row_specs(xs, tm, 1) + [
            _mod_spec(2, tm, 1),
            pl.BlockSpec((D, D), lambda i: (0, 0)), vec, _mod_spec(4, tm, 1), _mod_spec(3, tm, 1),
            pl.BlockSpec((D, 2 * LANES), lambda i: (0, 0)), pl.BlockSpec((1, LANES), lambda i: (0, 0))],
        out_specs=[row(D), pl.BlockSpec((tm * TT_SUB, LANES), lambda i: (i, 0)), row(LANES),
                   pl.BlockSpec((8, LANES), lambda i: (0, 0))],
        out_shape=[jax.ShapeDtypeStruct((T, D), F32), jax.ShapeDtypeStruct((T * TT_SUB, LANES), U32),
                   jax.ShapeDtypeStruct((T, LANES), F32), jax.ShapeDtypeStruct((8, LANES), F32)],
        scratch_shapes=[pltpu.VMEM((8, LANES), F32)],
        compiler_params=_params(("arbitrary",)),
        name="out_proj_router",
    )(*a_parts, *b_parts, *xs, mods, wo_bf16, nw.reshape(1, D), mods, mods, rw, rb)


COPY_UNROLL = 32


def _moe_body(be_ref, nused_ref, nv_ref, src_cur, src_nxt, dst_cur, hf_hbm, wg_ref, wu_ref, wd_ref, y_hbm,
              xbuf, obuf, wgc, wuc, wdc, sem_in, sem_out):
    i = pl.program_id(0)
    nused = nused_ref[0]
    slot = i % 2

    def tile(ref, row8):
        return ref.at[pl.ds(pl.multiple_of(row8, TT_SUB), TT_SUB)]

    def rows(ref, n):
        return ref.at[pl.ds(0, n * TT_SUB)]

    def for_rows(n, fn):
        groups = n // COPY_UNROLL

        def group(c, carry):
            for u in range(COPY_UNROLL):
                fn(c * COPY_UNROLL + u)
            return carry

        def single(r, carry):
            fn(r)
            return carry

        lax.fori_loop(0, groups, group, 0)
        lax.fori_loop(groups * COPY_UNROLL, n, single, 0)

    def gather(src_ref, s, n):
        def row(r):
            pltpu.make_async_copy(tile(hf_hbm, src_ref[0, r]), tile(xbuf.at[s], r * TT_SUB), sem_in.at[s]).start()

        for_rows(n, row)

    def wait_scatter(n):
        pltpu.make_async_copy(rows(obuf, n), rows(y_hbm, n), sem_out).wait()

    @pl.when(i < nused)
    def _():
        nv = nv_ref[i]

        @pl.when(i == 0)
        def _():
            xbuf[...] = jnp.zeros(xbuf.shape, U32)
            gather(src_cur, 0, nv)

        pltpu.make_async_copy(rows(hf_hbm, nv), rows(xbuf.at[slot], nv), sem_in.at[slot]).wait()

        @pl.when(i + 1 < nused)
        def _():
            gather(src_nxt, 1 - slot, nv_ref[i + 1])

        @pl.when((i == 0) | (be_ref[i] != be_ref[jnp.maximum(i - 1, 0)]))
        def _():
            wgc[...] = wg_ref[...].astype(BF16)
            wuc[...] = wu_ref[...].astype(BF16)
            wdc[...] = wd_ref[...].astype(BF16)

        x = _load_tt(xbuf.at[slot], TM_MOE, BF16)
        gate = jnp.dot(x, wgc[...], preferred_element_type=F32)
        up = jnp.dot(x, wuc[...], preferred_element_type=F32)
        hid = (gate * _sigmoid(gate) * up).astype(BF16)

        @pl.when(i > 0)
        def _():
            wait_scatter(nv_ref[i - 1])

        _store_tt(obuf, jnp.dot(hid, wdc[...], preferred_element_type=F32), TM_MOE)

        def scatter_row(r):
            pltpu.make_async_copy(tile(obuf, r * TT_SUB), tile(y_hbm, dst_cur[0, r]), sem_out).start()

        for_rows(nv, scatter_row)

        @pl.when(i == nused - 1)
        def _():
            wait_scatter(nv)


def _moe(hf, plan, wg, wu, wd, layer):
    src, dst, block_e, nused, nvalid = plan
    wspec = lambda shape: pl.BlockSpec((None, None) + shape, lambda i, be, nu, nv: (layer, be[i], 0, 0))
    rows = lambda f: pl.BlockSpec((None, 1, TM_MOE), lambda i, be, nu, nv: (f(i), 0, 0), memory_space=pltpu.SMEM)
    return pl.pallas_call(
        _moe_body,
        grid_spec=pltpu.PrefetchScalarGridSpec(
            num_scalar_prefetch=3,
            grid=(N_BLOCKS,),
            in_specs=[rows(lambda i: i), rows(lambda i: jnp.minimum(i + 1, N_BLOCKS - 1)), rows(lambda i: i),
                      pl.BlockSpec(memory_space=pl.ANY),
                      wspec((D, D_EXPERT)), wspec((D, D_EXPERT)), wspec((D_EXPERT, D))],
            out_specs=pl.BlockSpec(memory_space=pl.ANY),
            scratch_shapes=[pltpu.VMEM((2, TM_MOE * TT_SUB, LANES), U32),
                            pltpu.VMEM((TM_MOE * TT_SUB, LANES), U32),
                            pltpu.VMEM((D, D_EXPERT), BF16), pltpu.VMEM((D, D_EXPERT), BF16),
                            pltpu.VMEM((D_EXPERT, D), BF16),
                            pltpu.SemaphoreType.DMA((2,)), pltpu.SemaphoreType.DMA]),
        out_shape=jax.ShapeDtypeStruct((2 * T * TT_SUB, LANES), U32),
        compiler_params=_params(("arbitrary",), MOE_VMEM_LIMIT),
        name="moe_experts",
    )(block_e, nused, nvalid, src.reshape(N_BLOCKS, 1, TM_MOE), src.reshape(N_BLOCKS, 1, TM_MOE),
      dst.reshape(N_BLOCKS, 1, TM_MOE), hf, wg, wu, wd)


INV_ROWS = 2048


def _invert_body(dest_ref, zeros_hbm, codes_ref, sem):
    i = pl.program_id(0)

    @pl.when(i == 0)
    def _():
        clear = pltpu.make_async_copy(zeros_hbm, codes_ref, sem)
        clear.start()
        clear.wait()

    def put(j, carry):
        codes_ref[dest_ref[0, j]] = i * INV_ROWS + j
        return carry

    lax.fori_loop(0, INV_ROWS, put, 0, unroll=16)


def _invert_slots(dest3):
    return pl.pallas_call(
        _invert_body,
        grid=(dest3.shape[0],),
        in_specs=[pl.BlockSpec((None, 1, INV_ROWS), lambda i: (i, 0, 0), memory_space=pltpu.SMEM),
                  pl.BlockSpec(memory_space=pl.ANY)],
        out_specs=pl.BlockSpec(memory_space=pltpu.SMEM),
        out_shape=jax.ShapeDtypeStruct((N_SLOTS,), jnp.int32),
        scratch_shapes=[pltpu.SemaphoreType.DMA],
        compiler_params=_params(("arbitrary",)),
        name="invert_slots",
    )(dest3, jnp.zeros((N_SLOTS,), jnp.int32))


def _dispatch_plan(info, cnt):
    e = info[:, 0:2].astype(jnp.int32)
    rank = info[:, 4:6].astype(jnp.int32)
    counts = cnt[0, EXPERT_LANE0:EXPERT_LANE0 + N_EXPERTS].astype(jnp.int32)
    padded = ((counts + TM_MOE - 1) // TM_MOE) * TM_MOE
    pends = jnp.cumsum(padded)
    pstarts = pends - padded
    first = jnp.sum(jnp.where(e[..., None] == jnp.arange(N_EXPERTS, dtype=jnp.int32), pstarts, 0), axis=-1)
    dest = first + rank
    codes = _invert_slots(dest.reshape(2 * T // INV_ROWS, 1, INV_ROWS))
    src = (codes >> 1) * TT_SUB
    dst = ((codes & 1) * T + (codes >> 1)) * TT_SUB
    block0 = jnp.arange(N_BLOCKS, dtype=jnp.int32) * TM_MOE
    block_e = jnp.clip(jnp.searchsorted(pends, block0, side='right', method='compare_all'),
                       0, N_EXPERTS - 1).astype(jnp.int32)
    nvalid = jnp.clip(pstarts[block_e] + counts[block_e] - block0, 0, TM_MOE).astype(jnp.int32)
    nused = (pends[-1] // TM_MOE).astype(jnp.int32).reshape(1)
    return src, dst, block_e, nused, nvalid


def _combined(x_ref, y0_ref, y1_ref, info_ref, g2_ref, rows):
    info = info_ref[...]
    lane = lax.broadcasted_iota(jnp.int32, info.shape, 1)
    w1 = jnp.sum(jnp.where(lane == 2, info, 0.0), axis=-1, keepdims=True)
    w2 = jnp.sum(jnp.where(lane == 3, info, 0.0), axis=-1, keepdims=True)
    y0 = _load_tt(y0_ref, rows, F32)
    y1 = _load_tt(y1_ref, rows, F32)
    return x_ref[...] + g2_ref[...] * (y0 * w1 + y1 * w2)


def _combine_body(x_ref, y0_ref, y1_ref, info_ref, g2_ref, nw_ref, *o_refs, final):
    x = _combined(x_ref, y0_ref, y1_ref, info_ref, g2_ref, TM_CMB)
    if final:
        ms = jnp.mean(x * x, axis=-1, keepdims=True)
        x = x * lax.rsqrt(ms + EPS) * nw_ref[...]

    def emit(ref):
        ref[...] = x

    _for_tile(o_refs, TM_CMB, emit)


def _combine(x1, y, info, mods, nw, *, final):
    tm = TM_CMB
    nt = T // tm
    if final:
        out_shape = [jax.ShapeDtypeStruct((T_S, D), F32), jax.ShapeDtypeStruct((T_P, D), F32)]
    else:
        out_shape = [jax.ShapeDtypeStruct((T, D), F32)]
    return pl.pallas_call(
        functools.partial(_combine_body, final=final),
        grid=(nt,),
        in_specs=[pl.BlockSpec((tm, D), lambda i: (i, 0)),
                  pl.BlockSpec((tm * TT_SUB, LANES), lambda i: (i, 0)),
                  pl.BlockSpec((tm * TT_SUB, LANES), lambda i: (i + nt, 0)),
                  pl.BlockSpec((tm, LANES), lambda i: (i, 0)),
                  _mod_spec(5, tm, 1),
                  pl.BlockSpec((1, D), lambda i: (0, 0))],
        out_specs=_row_specs(out_shape, tm, 1),
        out_shape=out_shape,
        compiler_params=_params(("arbitrary",)),
        name="combine_final" if final else "combine",
    )(x1, y, y, info, mods, nw.reshape(1, D))


def kernel(x_prompt, x_sample, c, state_ret_fwd, state_ret_bwd, state_lru_fwd, state_lru_bwd, c_ctx, w_mod, b_mod, norm_mix_w, norm_ffn_w, norm_final_w, w_in_even, w_out_even, gmlp_norm_w, gmlp_w_s, gmlp_b_s, ret_decay_fwd, ret_decay_bwd, ret_gn_w, w_in_odd, w_out_odd, conv_w, conv_b, conv_ln_w, conv_ln_b, lru_conv_w, lru_conv_b, lru_wa_fwd, lru_ba_fwd, lru_wx_fwd, lru_bx_fwd, lru_lam_fwd, lru_wa_bwd, lru_ba_bwd, lru_wx_bwd, lru_bx_bwd, lru_lam_bwd, router_grp_w, router_grp_b, router_exp_w, router_exp_b, moe_w_gate, moe_w_up, moe_w_down):
    xs = (x_sample.reshape(T_S, D), x_prompt.reshape(T_P, D))
    cond = jnp.concatenate([c, c_ctx[None, :], jnp.zeros((N_COND_PAD - B_S - 1, D), F32)], axis=0)
    m = _modulation(cond, w_mod, b_mod)
    mods_all = m.reshape(DEPTH, N_COND_PAD, N_MOD, D).transpose(0, 2, 1, 3)[:, :, :, None, :]

    ret_f = ret_b = lru_f = lru_b = None
    for l in range(DEPTH):
        mods = mods_all[l]
        if l % 2 == 0:
            e = l // 2
            p = _norm_in(xs, norm_mix_w[l], mods, w_in_even[e].astype(BF16), pack=False)
            b_full = jnp.repeat(gmlp_b_s[e].T.astype(F32), A_GDIM, axis=1)
            out_a = (_gmlp(p, gmlp_norm_w[e], gmlp_w_s[e].astype(BF16), b_full),)
            tabs = _retention_tables(ret_decay_fwd[e], ret_decay_bwd[e])
            cos, sin = _rope_tables()
            ob_s, _, _ = _retention(p, state_ret_fwd[:, e], state_ret_bwd[:, e], tabs, ret_gn_w[e], cos, sin,
                                    base_chunk=0, nb=B_S, nc=N_S // CHUNK, rope=True)
            zero_state = jnp.zeros((B_P, HEADS, HD, HD), F32)
            ob_p, ret_f, ret_b = _retention(p, zero_state, zero_state, tabs, ret_gn_w[e], cos, sin,
                                            base_chunk=T_S // CHUNK, nb=B_P, nc=N_P // CHUNK, rope=False)
            out_b = (ob_s, ob_p)
            w_out = w_out_even[e]
        else:
            o = l // 2
            p = _norm_in(xs, norm_mix_w[l], mods, w_in_odd[o].astype(BF16), pack=True)
            cw3 = jnp.concatenate([conv_w[o], jnp.zeros((1, W), F32)], axis=0).reshape(
                C_KSIZE + 1, N_STRIPS, LANES).transpose(1, 0, 2)
            conv_args = (cw3, conv_b[o], conv_ln_w[o], conv_ln_b[o])
            prm_f = _lru_params(lru_conv_w[o], lru_conv_b[o], lru_wa_fwd[o], lru_ba_fwd[o], lru_wx_fwd[o],
                                lru_bx_fwd[o], lru_lam_fwd[o])
            prm_b = _lru_params(lru_conv_w[o], lru_conv_b[o], lru_wa_bwd[o], lru_ba_bwd[o], lru_wx_bwd[o],
                                lru_bx_bwd[o], lru_lam_bwd[o])
            latent = dict(seq_row0=0, nseq=B_S, n=N_S)
            context = dict(seq_row0=T_S, nseq=B_P, n=N_P)
            out_a = (_od_conv(p, *conv_args, **latent), _od_conv(p, *conv_args, **context))
            yd_s, _, _ = _od_rglru(p, prm_f, prm_b, state_lru_fwd[:, o], state_lru_bwd[:, o], **latent)
            zero_h = jnp.zeros((B_P, W), F32)
            yd_p, lru_f, lru_b = _od_rglru(p, prm_f, prm_b, zero_h, zero_h, **context)
            out_b = (yd_s, yd_p)
            w_out = w_out_odd[o]

        rw = jnp.concatenate([router_grp_w[l], router_exp_w[l],
                              jnp.zeros((D, LANES - N_GROUPS - N_EXPERTS), F32)], axis=1)
        rw_hi = rw.astype(BF16)
        rw = jnp.concatenate([rw_hi, (rw - rw_hi.astype(F32)).astype(BF16)], axis=1)
        rb = jnp.concatenate([router_grp_b[l], router_exp_b[l],
                              jnp.zeros((LANES - N_GROUPS - N_EXPERTS,), F32)]).reshape(1, LANES)
        x1, hf, info, cnt = _out_proj(out_a, out_b, xs, mods, w_out.astype(BF16), norm_ffn_w[l], rw, rb,
                                      packed=(l % 2 == 1))
        y = _moe(hf, _dispatch_plan(info, cnt), moe_w_gate, moe_w_up, moe_w_down, l)
        xs = tuple(_combine(x1, y, info, mods, norm_final_w, final=(l == DEPTH - 1)))

    y_sample = xs[0].reshape(B_S, N_S, D)
    y_prompt = xs[1].reshape(B_P, N_P, D)
    return (y_prompt, y_sample, ret_f[:, None], ret_b[:, None], lru_f[:, None], lru_b[:, None])
```

```python
import functools

import jax
import jax.numpy as jnp
from jax import lax
from jax.experimental import pallas as pl
from jax.experimental.pallas import tpu as pltpu

F32 = jnp.float32
BF16 = jnp.bfloat16
U32 = jnp.uint32

D = 2048
B_P, N_P = 16, 256
B_S, N_S = 8, 4096
T_S = B_S * N_S
T_P = B_P * N_P
T = T_S + T_P
ROWS_PER_COND = 4096
N_COND_PAD = 16
DEPTH = 2
N_MOD = 6
EPS = 1e-6
GRID_W = 64
ROPE_BASE = 10000.0

W = 1024
HW = W // 2
HEADS = 8
HD = 128
CHUNK = 128
A_GROUPS = 4
A_GDIM = W // A_GROUPS
C_KSIZE = 31
D_KSIZE = 4
LRU_C = 8.0

N_GROUPS = 4
EXPERTS_PER_GROUP = 8
N_EXPERTS = 32
D_EXPERT = 512
LANES = 128
N_STRIPS = W // LANES
EXPERT_LANE0 = N_GROUPS

TM_IN = 512
TN_IN = 2048
TM_OUT = 512
TM_MOE = 512
N_SLOTS = 2 * T + N_EXPERTS * TM_MOE
N_BLOCKS = N_SLOTS // TM_MOE
TM_CMB = 512

VMEM_LIMIT = 56 * 1024 * 1024
MOE_VMEM_LIMIT = 60 * 1024 * 1024


def _params(sem, vmem=VMEM_LIMIT):
    return pltpu.CompilerParams(dimension_semantics=sem, vmem_limit_bytes=vmem)


def _cond_row(i, tm):
    return (i * tm) // ROWS_PER_COND


def _sigmoid(x):
    return 0.5 * jnp.tanh(0.5 * x) + 0.5


def _pack_pair(lo, hi, rounded=False):
    if not rounded:
        lo, hi = lo.astype(BF16).astype(F32), hi.astype(BF16).astype(F32)
    return (lax.bitcast_convert_type(lo, U32) >> 16) | lax.bitcast_convert_type(hi, U32)


def _unpack_pair(w, dtype):
    return (lax.bitcast_convert_type(w << 16, F32).astype(dtype),
            lax.bitcast_convert_type(w & U32(0xFFFF0000), F32).astype(dtype))


def _pack_halves(x):
    k = x.shape[1] // 2
    return _pack_pair(x[:, :k], x[:, k:])


def _unpack_halves(w, dtype):
    return jnp.concatenate(_unpack_pair(w, dtype), axis=1)


TT_SUB = 8


def _store_tt(ref, x, rows, rounded=False):
    for c in range(TT_SUB):
        ref[pl.ds(c, rows, stride=TT_SUB), :] = _pack_pair(x[:, c * LANES:(c + 1) * LANES],
                                                           x[:, W + c * LANES:W + (c + 1) * LANES], rounded)


def _load_tt(ref, rows, dtype):
    lo, hi = [], []
    for c in range(TT_SUB):
        a, b = _unpack_pair(ref[pl.ds(c, rows, stride=TT_SUB), :], dtype)
        lo.append(a)
        hi.append(b)
    return jnp.concatenate(lo + hi, axis=1)


def _row_specs(arrs, tm, ngrid):
    if len(arrs) == 1:
        maps = [lambda i: i]
    else:
        n_s = T_S // tm
        maps = [lambda i: jnp.minimum(i, n_s - 1), lambda i: jnp.maximum(i - n_s, 0)]
    width = arrs[0].shape[1]
    if ngrid == 1:
        return [pl.BlockSpec((tm, width), lambda i, f=f: (f(i), 0)) for f in maps]
    return [pl.BlockSpec((tm, width), lambda i, j, f=f: (f(i), 0)) for f in maps]


def _tile_value(refs, tm):
    if len(refs) == 1:
        return refs[0][...]
    return jnp.where(pl.program_id(0) < T_S // tm, refs[0][...], refs[1][...])


def _for_tile(refs, tm, fn):
    if len(refs) == 1:
        fn(refs[0])
        return
    i = pl.program_id(0)

    @pl.when(i < T_S // tm)
    def _():
        fn(refs[0])

    @pl.when(i >= T_S // tm)
    def _():
        fn(refs[1])


def _mod_body(c_ref, w_ref, b_ref, o_ref):
    c = c_ref[...]
    a = (c * _sigmoid(c)).astype(BF16)
    o_ref[...] = jnp.dot(a, w_ref[...].astype(BF16), preferred_element_type=F32) + b_ref[...]


def _modulation(cond, w_mod, b_mod):
    tn = 1024
    return pl.pallas_call(
        _mod_body,
        grid=(DEPTH, N_MOD * D // tn),
        in_specs=[pl.BlockSpec((N_COND_PAD, D), lambda l, j: (0, 0)),
                  pl.BlockSpec((None, D, tn), lambda l, j: (l, 0, j)),
                  pl.BlockSpec((None, 1, tn), lambda l, j: (l, 0, j))],
        out_specs=pl.BlockSpec((None, N_COND_PAD, tn), lambda l, j: (l, 0, j)),
        out_shape=jax.ShapeDtypeStruct((DEPTH, N_COND_PAD, N_MOD * D), F32),
        compiler_params=_params(("arbitrary", "arbitrary")),
        name="modulation",
    )(cond, w_mod, b_mod.reshape(DEPTH, 1, N_MOD * D))


def _mod_spec(k, tm, ngrid):
    if ngrid == 1:
        return pl.BlockSpec((None, None, 1, D), lambda i: (k, _cond_row(i, tm), 0, 0))
    return pl.BlockSpec((None, None, 1, D), lambda i, j: (k, _cond_row(i, tm), 0, 0))


def _norm_in_body(*refs, pack):
    nw_ref, sc_ref, sh_ref, w_ref, p_ref, h_scr = refs[-6:]

    def normalise(x_ref):
        x = x_ref[...]
        ms = jnp.mean(x * x, axis=-1, keepdims=True)
        h = x * lax.rsqrt(ms + EPS) * nw_ref[...]
        h = h * (1.0 + sc_ref[...]) + sh_ref[...]
        h_scr[...] = h.astype(BF16)

    @pl.when(pl.program_id(1) == 0)
    def _():
        _for_tile(refs[:-6], TM_IN, normalise)

    r = jnp.dot(h_scr[...], w_ref[...], preferred_element_type=F32)
    if pack:
        p_ref[...] = jnp.concatenate([_pack_halves(r[:, k * W:(k + 1) * W]) for k in range(TN_IN // W)], axis=1)
    else:
        p_ref[...] = r.astype(BF16)


def _norm_in(xs, nw, mods, w_bf16, *, pack):
    n = w_bf16.shape[1]
    if pack:
        out_spec = pl.BlockSpec((TM_IN, TN_IN // 2), lambda i, j: (i, j))
        out_shape = jax.ShapeDtypeStruct((T, n // 2), U32)
    else:
        out_spec = pl.BlockSpec((TM_IN, TN_IN), lambda i, j: (i, j))
        out_shape = jax.ShapeDtypeStruct((T, n), BF16)
    return pl.pallas_call(
        functools.partial(_norm_in_body, pack=pack),
        grid=(T // TM_IN, n // TN_IN),
        in_specs=_row_specs(xs, TM_IN, 2) + [
            pl.BlockSpec((1, D), lambda i, j: (0, 0)),
            _mod_spec(1, TM_IN, 2),
            _mod_spec(0, TM_IN, 2),
            pl.BlockSpec((D, TN_IN), lambda i, j: (0, j))],
        out_specs=out_spec,
        out_shape=out_shape,
        scratch_shapes=[pltpu.VMEM((TM_IN, D), BF16)],
        compiler_params=_params(("arbitrary", "arbitrary")),
        name="norm_in",
    )(*xs, nw.reshape(1, D), mods, mods, w_bf16)


GMLP_ROWS = 512


def _gmlp_body(u_ref, v_ref, lnw_ref, ws_ref, b_ref, o_ref):
    for c in range(GMLP_ROWS // CHUNK):
        rows = pl.ds(c * CHUNK, CHUNK)
        v = jax.nn.gelu(v_ref[rows, :].astype(F32))
        mu = jnp.mean(v, axis=-1, keepdims=True)
        var = jnp.mean(jnp.square(v - mu), axis=-1, keepdims=True)
        vn = ((v - mu) * lax.rsqrt(var + EPS) * lnw_ref[...]).astype(BF16)
        z = jnp.concatenate(
            [jnp.dot(ws_ref[g], vn[:, g * A_GDIM:(g + 1) * A_GDIM], preferred_element_type=F32)
             for g in range(A_GROUPS)], axis=1) + b_ref[...]
        u = jax.nn.gelu(u_ref[rows, :].astype(F32))
        o_ref[rows, :] = (u * z).astype(BF16)


def _gmlp(p, lnw, ws_bf16, b_full):
    return pl.pallas_call(
        _gmlp_body,
        grid=(T // GMLP_ROWS,),
        in_specs=[pl.BlockSpec((GMLP_ROWS, W), lambda i: (i, 0)),
                  pl.BlockSpec((GMLP_ROWS, W), lambda i: (i, 1)),
                  pl.BlockSpec((1, W), lambda i: (0, 0)),
                  pl.BlockSpec((A_GROUPS, CHUNK, CHUNK), lambda i: (0, 0, 0)),
                  pl.BlockSpec((CHUNK, W), lambda i: (0, 0))],
        out_specs=pl.BlockSpec((GMLP_ROWS, W), lambda i: (i, 0)),
        out_shape=jax.ShapeDtypeStruct((T, W), BF16),
        compiler_params=_params(("arbitrary",)),
        name="gmlp",
    )(p, p, lnw.reshape(1, W), ws_bf16, b_full)


RET_SUB = 8


def _roped_qk(q_ref, k_ref, cos_ref, sin_ref, rows, h, rope):
    cols = slice(h * HD, (h + 1) * HD)
    q = q_ref[rows, cols].astype(F32)
    k = k_ref[rows, cols].astype(F32) * (HD ** -0.5)
    if rope:
        c = cos_ref[rows, :]
        s = sin_ref[rows, :]
        q = q * c + pltpu.roll(q, HD // 2, axis=1) * s
        k = k * c + pltpu.roll(k, HD // 2, axis=1) * s
    return q, k


def _kt_v(kd, v):
    return lax.dot_general(kd, v, (((0,), (0,)), ((), ())), preferred_element_type=F32)


def _ret_bwd_body(q_ref, k_ref, v_ref, cos_ref, sin_ref, s0_ref, dq_ref, dk_ref, dc_ref,
                  ob_ref, sfin_ref, s_scr, *, rope, nc, nsub):
    c = pl.program_id(1)

    @pl.when(c == 0)
    def _():
        s_scr[...] = s0_ref[...]

    for sub in reversed(range(nsub)):
        rows = slice(sub * CHUNK, (sub + 1) * CHUNK)
        for h in range(HEADS):
            cols = slice(h * HD, (h + 1) * HD)
            q, k = _roped_qk(q_ref, k_ref, cos_ref, sin_ref, rows, h, rope)
            v = v_ref[rows, cols]
            s = s_scr[h]
            ob_ref[rows, cols] = jnp.dot(q.astype(BF16), s.astype(BF16), preferred_element_type=F32) * dq_ref[h]
            kd = (k * dk_ref[h]).astype(BF16)
            s_scr[h] = s * dc_ref[h] + _kt_v(kd, v)

    @pl.when(c == nc - 1)
    def _():
        sfin_ref[...] = s_scr[...]


def _ret_fwd_body(q_ref, k_ref, v_ref, g_ref, cos_ref, sin_ref, ob_ref, s0_ref, m_ref, dq_ref, dk_ref, dc_ref,
                  gn_ref, o_ref, sfin_ref, s_scr, *, rope, nc, nsub):
    c = pl.program_id(1)

    @pl.when(c == 0)
    def _():
        s_scr[...] = s0_ref[...]

    for sub in range(nsub):
        rows = slice(sub * CHUNK, (sub + 1) * CHUNK)
        for h in range(HEADS):
            cols = slice(h * HD, (h + 1) * HD)
            q, k = _roped_qk(q_ref, k_ref, cos_ref, sin_ref, rows, h, rope)
            qb = q.astype(BF16)
            v = v_ref[rows, cols]
            s = s_scr[h]
            scores = lax.dot_general(qb, k.astype(BF16), (((1,), (1,)), ((), ())),
                                     preferred_element_type=F32) * m_ref[h]
            o = (jnp.dot(scores.astype(BF16), v, preferred_element_type=F32)
                 + jnp.dot(qb, s.astype(BF16), preferred_element_type=F32) * dq_ref[h]
                 + ob_ref[rows, cols])
            kd = (k * dk_ref[h]).astype(BF16)
            s_scr[h] = s * dc_ref[h] + _kt_v(kd, v)
            mu = jnp.mean(o, axis=-1, keepdims=True)
            var = jnp.mean(jnp.square(o - mu), axis=-1, keepdims=True)
            on = (o - mu) * lax.rsqrt(var + EPS) * gn_ref[:, cols]
            g = g_ref[rows, cols].astype(F32)
            o_ref[rows, cols] = (g * _sigmoid(g) * on).astype(BF16)

    @pl.when(c == nc - 1)
    def _():
        sfin_ref[...] = s_scr[...]


def _retention(p, s_f0, s_b0, tabs, gn_w, cos, sin, *, base_chunk, nb, nc, rope):
    m_tab, dq_f, dk_f, dc_f, dq_b, dk_b, dc_b = tabs
    nsub = min(RET_SUB, nc)
    rows = nsub * CHUNK
    ns = nc // nsub
    base = base_chunk // nsub
    state_spec = pl.BlockSpec((None, HEADS, HD, HD), lambda b, c: (b, 0, 0, 0))
    tab_spec = pl.BlockSpec((HEADS, HD, HD), lambda b, c: (0, 0, 0))
    dc_spec = pl.BlockSpec((HEADS, 1, HD), lambda b, c: (0, 0, 0))
    state_shape = jax.ShapeDtypeStruct((nb, HEADS, HD, HD), F32)

    def step(c, rev):
        return ns - 1 - c if rev else c

    def col(j, rev):
        return pl.BlockSpec((rows,W), lambda b, c: (base + b * ns + step(c, rev), j))

    def rope_spec(rev):
        if not rope:
            return pl.BlockSpec((rows,HD), lambda b, c: (0, 0))
        return pl.BlockSpec((rows,HD), lambda b, c: (step(c, rev), 0))

    def local(rev):
        return pl.BlockSpec((rows,W), lambda b, c: (b * ns + step(c, rev), 0))

    n_rows = nb * nc * CHUNK
    ob, s_b = pl.pallas_call(
        functools.partial(_ret_bwd_body, rope=rope, nc=ns, nsub=nsub),
        grid=(nb, ns),
        in_specs=[col(2, True), col(3, True), col(4, True), rope_spec(True), rope_spec(True),
                  state_spec, tab_spec, tab_spec, dc_spec],
        out_specs=[local(True), state_spec],
        out_shape=[jax.ShapeDtypeStruct((n_rows, W), F32), state_shape],
        scratch_shapes=[pltpu.VMEM((HEADS, HD, HD), F32)],
        compiler_params=_params(("arbitrary", "arbitrary")),
        name="retention_bwd",
    )(p, p, p, cos, sin, s_b0, dq_b, dk_b, dc_b)

    o, s_f = pl.pallas_call(
        functools.partial(_ret_fwd_body, rope=rope, nc=ns, nsub=nsub),
        grid=(nb, ns),
        in_specs=[col(2, False), col(3, False), col(4, False), col(5, False), rope_spec(False), rope_spec(False),
                  local(False), state_spec, tab_spec, tab_spec, tab_spec, dc_spec,
                  pl.BlockSpec((1, W), lambda b, c: (0, 0))],
        out_specs=[local(False), state_spec],
        out_shape=[jax.ShapeDtypeStruct((n_rows, W), BF16), state_shape],
        scratch_shapes=[pltpu.VMEM((HEADS, HD, HD), F32)],
        compiler_params=_params(("arbitrary", "arbitrary")),
        name="retention_fwd",
    )(p, p, p, p, cos, sin, ob, s_f0, m_tab, dq_f, dk_f, dc_f, gn_w.reshape(1, W))
    return o, s_f, s_b


def _retention_tables(decay_fwd, decay_bwd):
    lg_f = jax.nn.log_sigmoid(decay_fwd.astype(F32))[:, None, None]
    lg_b = jax.nn.log_sigmoid(decay_bwd.astype(F32))[:, None, None]
    pos = jnp.arange(CHUNK, dtype=F32)
    rel = pos[:, None] - pos[None, :]
    m_tab = (jnp.where(rel >= 0, jnp.exp(lg_f * jnp.maximum(rel, 0.0)), 0.0)
             + jnp.where(rel <= 0, jnp.exp(lg_b * jnp.maximum(-rel, 0.0)), 0.0))
    ones = jnp.ones((1, 1, HD), F32)
    col = pos[None, :, None]
    dq_f = jnp.exp(lg_f * (col + 1.0)) * ones
    dk_f = jnp.exp(lg_f * (CHUNK - 1.0 - col)) * ones
    dq_b = jnp.exp(lg_b * (CHUNK - col)) * ones
    dk_b = jnp.exp(lg_b * col) * ones
    dc_f = jnp.exp(lg_f * CHUNK) * ones
    dc_b = jnp.exp(lg_b * CHUNK) * ones
    return m_tab, dq_f, dk_f, dc_f, dq_b, dk_b, dc_b


def _rope_tables():
    rows = N_S // GRID_W
    row = jnp.repeat(jnp.arange(rows, dtype=F32), GRID_W)
    colp = jnp.tile(jnp.arange(GRID_W, dtype=F32), rows)
    quarter = HD // 4
    inv_freq = ROPE_BASE ** (-jnp.arange(quarter, dtype=F32) / quarter)
    ang = jnp.concatenate([row[:, None] * inv_freq, colp[:, None] * inv_freq], axis=-1)
    cos, sin = jnp.cos(ang), jnp.sin(ang)
    return jnp.concatenate([cos, cos], axis=-1), jnp.concatenate([-sin, sin], axis=-1)


OD_TT = 128
OD_G = 8
CONV_HALO = 16
LRU_HALO = 8


def _od_rows(seq_row0, n, g, tj):
    return [seq_row0 + (g * OD_G + b) * n + tj * OD_TT for b in range(OD_G)]


def _od_in_copies(p_hbm, buf, sem, *, col, halo, seq_row0, n, g, tj):
    nt = n // OD_TT
    out = []
    for b, base in enumerate(_od_rows(seq_row0, n, g, tj)):
        def src(r, k):
            return p_hbm.at[pl.ds(r, k), pl.ds(col * HW, HW)]

        out.append((None, pltpu.make_async_copy(src(base, OD_TT), buf.at[pl.ds(halo, OD_TT), b, :], sem)))
        if halo:
            out.append((tj > 0, pltpu.make_async_copy(src(base - halo, halo), buf.at[pl.ds(0, halo), b, :], sem)))
            out.append((tj < nt - 1, pltpu.make_async_copy(src(base + OD_TT, halo),
                                                           buf.at[pl.ds(halo + OD_TT, halo), b, :], sem)))
    return out


def _od_run(copies, op):
    for cond, cp in copies:
        fn = cp.start if op == "start" else cp.wait
        if cond is None:
            fn()
        else:
            pl.when(cond)(fn)


def _od_zero_halo(buf, halo, n, tj):
    nt = n // OD_TT
    zeros = jnp.zeros((halo, OD_G, HW), U32)

    @pl.when(tj == 0)
    def _():
        buf[0:halo] = zeros

    @pl.when(tj == nt - 1)
    def _():
        buf[halo + OD_TT:2 * halo + OD_TT] = zeros


def _od_out_copies(obuf, out_hbm, sem, *, seq_row0, n, g, tj):
    return [(None, pltpu.make_async_copy(obuf.at[:, b, :], out_hbm.at[pl.ds(base - seq_row0, OD_TT), :], sem))
            for b, base in enumerate(_od_rows(seq_row0, n, g, tj))]


def _od_unpack(buf, rows):
    return _unpack_halves(buf[...].reshape(rows, HW), F32)


def _od_pipeline(q, nq, in_copies, sem_slots=2):
    s = q % 2

    @pl.when(q == 0)
    def _():
        _od_run(in_copies(q, s), "start")

    _od_run(in_copies(q, s), "wait")

    @pl.when(q + 1 < nq)
    def _():
        _od_run(in_copies(q + 1, 1 - s), "start")

    return s


def _od_emit(q, nq, s, obuf, words, out_copies):
    @pl.when(q >= 2)
    def _():
        _od_run(out_copies(s), "wait")

    obuf[s] = words.reshape(OD_TT, OD_G, HW)
    _od_run(out_copies(s), "start")

    @pl.when(q == nq - 1)
    def _():
        _od_run(out_copies(s), "wait")

        @pl.when(q >= 1)
        def _():
            _od_run(out_copies(1 - s), "wait")


CONV_RB = 64


def _od_conv_body(p_hbm, cw_ref, cb_ref, lnw_ref, lnb_ref, yc_hbm, xin, glu, ybuf, obuf, sem_in, sem_out,
                  *, seq_row0, n, ng):
    nt = n // OD_TT
    nq = ng * nt
    q = pl.program_id(0)
    halo = CONV_HALO
    rows_in = (OD_TT + 2 * halo) * OD_G
    rows = OD_TT * OD_G

    def in_copies(step, slot):
        out = []
        for br in range(2):
            out += _od_in_copies(p_hbm, xin.at[br, slot], sem_in.at[slot], col=br, halo=halo,
                                 seq_row0=seq_row0, n=n, g=step // nt, tj=step % nt)
        return out

    s = _od_pipeline(q, nq, in_copies)
    g, tj = q // nt, q % nt
    for br in range(2):
        _od_zero_halo(xin.at[br, s], halo, n, tj)
    glu[...] = _od_unpack(xin.at[0, s], rows_in) * _sigmoid(_od_unpack(xin.at[1, s], rows_in))

    def strip(si, carry):
        lanes = pl.ds(pl.multiple_of(si * LANES, LANES), LANES)
        for rb in range(rows // CONV_RB):
            acc = jnp.zeros((CONV_RB // 8, 8, LANES), F32)
            for k in range(C_KSIZE):
                wk = cw_ref[si, pl.ds(k, 1), :]
                r0 = rb * CONV_RB + (halo - C_KSIZE // 2 + k) * OD_G
                acc = acc + glu[r0:r0 + CONV_RB, lanes].reshape(CONV_RB // 8, 8, LANES) * wk[None]
            ybuf[rb * CONV_RB:(rb + 1) * CONV_RB, lanes] = acc.reshape(CONV_RB, LANES)
        return carry

    lax.fori_loop(0, N_STRIPS, strip, 0)
    y = ybuf[...] + cb_ref[...]
    mu = jnp.mean(y, axis=-1, keepdims=True)
    var = jnp.mean(jnp.square(y - mu), axis=-1, keepdims=True)
    yn = (y - mu) * lax.rsqrt(var + EPS) * lnw_ref[...] + lnb_ref[...]
    yo = yn * _sigmoid(yn)

    def out_copies(slot):
        return _od_out_copies(obuf.at[slot], yc_hbm, sem_out.at[slot], seq_row0=seq_row0, n=n, g=g, tj=tj)

    _od_emit(q, nq, s, obuf, _pack_halves(yo), out_copies)


def _od_conv(p, cw3, cb, lnw, lnb, *, seq_row0, nseq, n):
    ng = nseq // OD_G
    nq = ng * (n // OD_TT)
    vec = pl.BlockSpec((1, W), lambda q: (0, 0))
    t_in = OD_TT + 2 * CONV_HALO
    return pl.pallas_call(
        functools.partial(_od_conv_body, seq_row0=seq_row0, n=n, ng=ng),
        grid=(nq,),
        in_specs=[pl.BlockSpec(memory_space=pl.ANY),
                  pl.BlockSpec((N_STRIPS, C_KSIZE + 1, LANES), lambda q: (0, 0, 0)), vec, vec, vec],
        out_specs=pl.BlockSpec(memory_space=pl.ANY),
        out_shape=jax.ShapeDtypeStruct((nseq * n, HW), U32),
        scratch_shapes=[pltpu.VMEM((2, 2, t_in, OD_G, HW), U32),
                        pltpu.VMEM((t_in * OD_G, W), F32),
                        pltpu.VMEM((OD_TT * OD_G, W), F32),
                        pltpu.VMEM((2, OD_TT, OD_G, HW), U32),
                        pltpu.SemaphoreType.DMA((2,)), pltpu.SemaphoreType.DMA((2,))],
        compiler_params=_params(("arbitrary",)),
        name="conv_module",
    )(p, cw3, cb.reshape(1, W), lnw.reshape(1, W), lnb.reshape(1, W))


def _od_gates(x, cw_ref, cb_ref, wa_ref, ba_ref, wx_ref, bx_ref, sp_ref, a_scr, b_scr):
    rows = OD_TT * OD_G
    xc = cb_ref[...]
    for k in range(D_KSIZE):
        r0 = (LRU_HALO - D_KSIZE // 2 + k) * OD_G
        xc = xc + cw_ref[k:k + 1, :] * x[r0:r0 + rows, :]
    xb = xc.astype(BF16)
    r = jnp.concatenate([jnp.dot(xb[:, h * HD:(h + 1) * HD], wa_ref[h], preferred_element_type=F32)
                         for h in range(HEADS)], axis=1) + ba_ref[...]
    g = jnp.concatenate([jnp.dot(xb[:, h * HD:(h + 1) * HD], wx_ref[h], preferred_element_type=F32)
                         for h in range(HEADS)], axis=1) + bx_ref[...]
    log_a = (-LRU_C) * _sigmoid(r) * sp_ref[...]
    th = jnp.tanh(log_a)
    one_minus_a2 = -2.0 * th / (1.0 - th)
    a_scr[...] = jnp.exp(log_a)
    b_scr[...] = jnp.sqrt(jnp.maximum(one_minus_a2, 0.0)) * (_sigmoid(g) * xc)


def _od_scan(a_scr, b_scr, out_ref, h, reverse):
    def step(t, h):
        tr = (OD_TT - 1 - t) if reverse else t
        rows = pl.ds(pl.multiple_of(tr * OD_G, OD_G), OD_G)
        h = a_scr[rows, :] * h + b_scr[rows, :]
        out_ref[rows, :] = h
        return h

    return lax.fori_loop(0, OD_TT, step, h, unroll=8)


def _od_lru_fwd_body(p_hbm, cw_ref, cb_ref, wa_ref, ba_ref, wx_ref, bx_ref, sp_ref, h0_ref,
                     hf_ref, hfin_ref, xin, a_scr, b_scr, h_scr, sem_in, *, seq_row0, n, ng):
    nt = n // OD_TT
    nq = ng * nt
    q = pl.program_id(0)

    def in_copies(step, slot):
        return _od_in_copies(p_hbm, xin.at[slot], sem_in.at[slot], col=2, halo=LRU_HALO,
                             seq_row0=seq_row0, n=n, g=step // nt, tj=step % nt)

    s = _od_pipeline(q, nq, in_copies)
    tj = q % nt
    _od_zero_halo(xin.at[s], LRU_HALO, n, tj)

    @pl.when(tj == 0)
    def _():
        h_scr[...] = h0_ref[...]

    x = _od_unpack(xin.at[s], (OD_TT + 2 * LRU_HALO) * OD_G)
    _od_gates(x, cw_ref, cb_ref, wa_ref, ba_ref, wx_ref, bx_ref, sp_ref, a_scr, b_scr)
    h = _od_scan(a_scr, b_scr, hf_ref, h_scr[...], False)
    h_scr[...] = h

    @pl.when(tj == nt - 1)
    def _():
        hfin_ref[...] = h


def _od_lru_bwd_body(p_hbm, hf_ref, cw_ref, cb_ref, wa_ref, ba_ref, wx_ref, bx_ref, sp_ref, h0_ref,
                     yd_hbm, hfin_ref, xin, gin, a_scr, b_scr, hb_scr, obuf, h_scr, sem_in, sem_out,
                     *, seq_row0, n, ng):
    nt = n // OD_TT
    nq = ng * nt
    q = pl.program_id(0)

    def in_copies(step, slot):
        where = dict(seq_row0=seq_row0, n=n, g=step // nt, tj=nt - 1 - step % nt)
        return (_od_in_copies(p_hbm, xin.at[slot], sem_in.at[slot], col=2, halo=LRU_HALO, **where)
                + _od_in_copies(p_hbm, gin.at[slot], sem_in.at[slot], col=3, halo=0, **where))

    s = _od_pipeline(q, nq, in_copies)
    g, tj = q // nt, nt - 1 - q % nt
    _od_zero_halo(xin.at[s], LRU_HALO, n, tj)

    @pl.when(tj == nt - 1)
    def _():
        h_scr[...] = h0_ref[...]

    x = _od_unpack(xin.at[s], (OD_TT + 2 * LRU_HALO) * OD_G)
    _od_gates(x, cw_ref, cb_ref, wa_ref, ba_ref, wx_ref, bx_ref, sp_ref, a_scr, b_scr)
    h = _od_scan(a_scr, b_scr, hb_scr, h_scr[...], True)
    h_scr[...] = h

    @pl.when(tj == 0)
    def _():
        hfin_ref[...] = h

    gd = _od_unpack(gin.at[s], OD_TT * OD_G)
    yd = (hf_ref[...] + hb_scr[...]) * jax.nn.gelu(gd)

    def out_copies(slot):
        return _od_out_copies(obuf.at[slot], yd_hbm, sem_out.at[slot], seq_row0=seq_row0, n=n, g=g, tj=tj)

    _od_emit(q, nq, s, obuf, _pack_halves(yd), out_copies)


def _od_rglru(p, prm_f, prm_b, h0_f, h0_b, *, seq_row0, nseq, n):
    ng = nseq // OD_G
    nt = n // OD_TT
    nq = ng * nt
    rows = OD_TT * OD_G
    vec = pl.BlockSpec((1, W), lambda q: (0, 0))
    wsp = pl.BlockSpec((HEADS, HD, HD), lambda q: (0, 0, 0))
    prm_specs = [pl.BlockSpec((8, W), lambda q: (0, 0)), vec, wsp, vec, wsp, vec, vec,
                 pl.BlockSpec((OD_G, W), lambda q: (q // nt, 0))]
    state_spec = pl.BlockSpec((OD_G, W), lambda q: (q // nt, 0))
    state_shape = jax.ShapeDtypeStruct((nseq, W), F32)
    t_in = OD_TT + 2 * LRU_HALO
    tile_f32 = pltpu.VMEM((rows, W), F32)

    hf, s_f = pl.pallas_call(
        functools.partial(_od_lru_fwd_body, seq_row0=seq_row0, n=n, ng=ng),
        grid=(nq,),
        in_specs=[pl.BlockSpec(memory_space=pl.ANY)] + prm_specs,
        out_specs=[pl.BlockSpec((rows, W), lambda q: (q, 0)), state_spec],
        out_shape=[jax.ShapeDtypeStruct((nq * rows, W), F32), state_shape],
        scratch_shapes=[pltpu.VMEM((2, t_in, OD_G, HW), U32), tile_f32, tile_f32, pltpu.VMEM((OD_G, W), F32),
                        pltpu.SemaphoreType.DMA((2,))],
        compiler_params=_params(("arbitrary",)),
        name="rglru_fwd",
    )(p, *prm_f, h0_f)

    yd, s_b = pl.pallas_call(
        functools.partial(_od_lru_bwd_body, seq_row0=seq_row0, n=n, ng=ng),
        grid=(nq,),
        in_specs=[pl.BlockSpec(memory_space=pl.ANY),
                  pl.BlockSpec((rows, W), lambda q: ((q // nt) * nt + nt - 1 - q % nt, 0))] + prm_specs,
        out_specs=[pl.BlockSpec(memory_space=pl.ANY), state_spec],
        out_shape=[jax.ShapeDtypeStruct((nseq * n, HW), U32), state_shape],
        scratch_shapes=[pltpu.VMEM((2, t_in, OD_G, HW), U32), pltpu.VMEM((2, OD_TT, OD_G, HW), U32),
                        tile_f32, tile_f32, tile_f32, pltpu.VMEM((2, OD_TT, OD_G, HW), U32),
                        pltpu.VMEM((OD_G, W), F32),
                        pltpu.SemaphoreType.DMA((2,)), pltpu.SemaphoreType.DMA((2,))],
        compiler_params=_params(("arbitrary",)),
        name="rglru_bwd",
    )(p, hf, *prm_b, h0_b)
    return yd, s_f, s_b


def _lru_params(cw, cb, wa, ba, wx, bx, lam):
    cw_pad = jnp.concatenate([cw, jnp.zeros((8 - D_KSIZE, W), F32)], axis=0)
    sp = jax.nn.softplus(-lam.astype(F32)).reshape(1, W)
    return (cw_pad, cb.reshape(1, W), wa.astype(BF16), ba.reshape(1, W), wx.astype(BF16), bx.reshape(1, W), sp)


def _out_body(*refs, n_a, n_b, n_x, packed):
    a_refs = refs[:n_a]
    b_refs = refs[n_a:n_a + n_b]
    x_refs = refs[n_a + n_b:n_a + n_b + n_x]
    (g1_ref, wo_ref, nw_ref, sc_ref, sh_ref, rw_ref, rb_ref,
     x1_ref, hf_ref, info_ref, cnt_ref, carry) = refs[n_a + n_b + n_x:]
    i = pl.program_id(0)

    @pl.when(i == 0)
    def _():
        carry[...] = jnp.zeros_like(carry)

    def branch(refs_, row0):
        v = _tile_value(refs_, TM_OUT)
        if packed:
            lo, hi = _unpack_pair(v, BF16)
            return (jnp.dot(lo, wo_ref[row0:row0 + HW, :], preferred_element_type=F32)
                    + jnp.dot(hi, wo_ref[row0 + HW:row0 + W, :], preferred_element_type=F32))
        return jnp.dot(v, wo_ref[row0:row0 + W, :], preferred_element_type=F32)

    x1 = _tile_value(x_refs, TM_OUT) + g1_ref[...] * (branch(a_refs, 0) + branch(b_refs, W))
    x1_ref[...] = x1
    ms = jnp.mean(x1 * x1, axis=-1, keepdims=True)
    hf = x1 * lax.rsqrt(ms + EPS) * nw_ref[...]
    hf = hf * (1.0 + sc_ref[...]) + sh_ref[...]
    hf_hi = hf.astype(BF16)
    hf_hi32 = hf_hi.astype(F32)
    _store_tt(hf_ref, hf_hi32, TM_OUT, rounded=True)

    hf_lo = (hf - hf_hi32).astype(BF16)
    l2 = jnp.dot(hf_hi, rw_ref[...], preferred_element_type=F32)
    logits = (l2[:, :LANES] + l2[:, LANES:]
              + jnp.dot(hf_lo, rw_ref[:, :LANES], preferred_element_type=F32) + rb_ref[...])
    lane = lax.broadcasted_iota(jnp.int32, (TM_OUT, LANES), 1).astype(F32)
    neg = -jnp.inf
    gmask = lane < N_GROUPS
    mg = jnp.max(jnp.where(gmask, logits, neg), axis=-1, keepdims=True)
    gstar = jnp.min(jnp.where(gmask & (logits == mg), lane, float(LANES)), axis=-1, keepdims=True)
    denom = jnp.sum(jnp.where(gmask, jnp.exp(logits - mg), 0.0), axis=-1, keepdims=True)
    psel = 1.0 / denom
    lo = EXPERT_LANE0 + EXPERTS_PER_GROUP * gstar
    emask = (lane >= lo) & (lane < lo + EXPERTS_PER_GROUP)
    v1 = jnp.max(jnp.where(emask, logits, neg), axis=-1, keepdims=True)
    i1 = jnp.min(jnp.where(emask & (logits == v1), lane, float(LANES)), axis=-1, keepdims=True)
    em2 = emask & (lane != i1)
    v2 = jnp.max(jnp.where(em2, logits, neg), axis=-1, keepdims=True)
    i2 = jnp.min(jnp.where(em2 & (logits == v2), lane, float(LANES)), axis=-1, keepdims=True)
    e21 = jnp.exp(v2 - v1)
    w1 = psel / (1.0 + e21)
    w2 = psel * e21 / (1.0 + e21)

    memb = jnp.where((lane == i1) | (lane == i2), 1.0, 0.0)
    r_i = lax.broadcasted_iota(jnp.int32, (TM_OUT, TM_OUT), 0)
    c_i = lax.broadcasted_iota(jnp.int32, (TM_OUT, TM_OUT), 1)
    lower = jnp.where(r_i > c_i, 1.0, 0.0).astype(BF16)
    before = jnp.dot(lower, memb.astype(BF16), preferred_element_type=F32) + carry[0:1, :]
    rank1 = jnp.sum(jnp.where(lane == i1, before, 0.0), axis=-1, keepdims=True)
    rank2 = jnp.sum(jnp.where(lane == i2, before, 0.0), axis=-1, keepdims=True)
    new_carry = carry[...] + jnp.sum(memb, axis=0, keepdims=True)
    carry[...] = new_carry
    cnt_ref[...] = new_carry

    e1 = i1 - EXPERT_LANE0
    e2 = i2 - EXPERT_LANE0
    info = jnp.where(lane == 0, e1, jnp.where(lane == 1, e2, jnp.where(lane == 2, w1, jnp.where(
        lane == 3, w2, jnp.where(lane == 4, rank1, jnp.where(lane == 5, rank2, 0.0))))))
    info_ref[...] = info


def _out_proj(a_parts, b_parts, xs, mods, wo_bf16, nw, rw, rb, *, packed):
    tm = TM_OUT
    row = lambda n: pl.BlockSpec((tm, n), lambda i: (i, 0))
    vec = pl.BlockSpec((1, D), lambda i: (0, 0))
    return pl.pallas_call(
        functools.partial(_out_body, n_a=len(a_parts), n_b=len(b_parts), n_x=len(xs), packed=packed),
        grid=(T // tm,),
        in_specs=_row_specs(a_parts, tm, 1) + _row_specs(b_parts, tm, 1) + _row_specs(xs, tm, 1) + [
            _mod_spec(2, tm, 1),
            pl.BlockSpec((D, D), lambda i: (0, 0)), vec, _mod_spec(4, tm, 1), _mod_spec(3, tm, 1),
            pl.BlockSpec((D, 2 * LANES), lambda i: (0, 0)), pl.BlockSpec((1, LANES), lambda i: (0, 0))],
        out_specs=[row(D), pl.BlockSpec((tm * TT_SUB, LANES), lambda i: (i, 0)), row(LANES),
                   pl.BlockSpec((8, LANES), lambda i: (0, 0))],
        out_shape=[jax.ShapeDtypeStruct((T, D), F32), jax.ShapeDtypeStruct((T * TT_SUB, LANES), U32),
                   jax.ShapeDtypeStruct((T, LANES), F32), jax.ShapeDtypeStruct((8, LANES), F32)],
        scratch_shapes=[pltpu.VMEM((8, LANES), F32)],
        compiler_params=_params(("arbitrary",)),
        name="out_proj_router",
    )(*a_parts, *b_parts, *xs, mods, wo_bf16, nw.reshape(1, D), mods, mods, rw, rb)


COPY_UNROLL = 32


def _moe_body(be_ref, nused_ref, nv_ref, src_cur, src_nxt, dst_cur, hf_hbm, wg_ref, wu_ref, wd_ref, y_hbm,
              xbuf, obuf, wgc, wuc, wdc, sem_in, sem_out):
    i = pl.program_id(0)
    nused = nused_ref[0]
    slot = i % 2

    def tile(ref, row8):
        return ref.at[pl.ds(pl.multiple_of(row8, TT_SUB), TT_SUB)]

    def rows(ref, n):
        return ref.at[pl.ds(0, n * TT_SUB)]

    def for_rows(n, fn):
        groups = n // COPY_UNROLL

        def group(c, carry):
            for u in range(COPY_UNROLL):
                fn(c * COPY_UNROLL + u)
            return carry

        def single(r, carry):
            fn(r)
            return carry

        lax.fori_loop(0, groups, group, 0)
        lax.fori_loop(groups * COPY_UNROLL, n, single, 0)

    def gather(src_ref, s, n):
        def row(r):
            pltpu.make_async_copy(tile(hf_hbm, src_ref[0, r]), tile(xbuf.at[s], r * TT_SUB), sem_in.at[s]).start()

        for_rows(n, row)

    def wait_scatter(n):
        pltpu.make_async_copy(rows(obuf, n), rows(y_hbm, n), sem_out).wait()

    @pl.when(i < nused)
    def _():
        nv = nv_ref[i]

        @pl.when(i == 0)
        def _():
            xbuf[...] = jnp.zeros(xbuf.shape, U32)
            gather(src_cur, 0, nv)

        pltpu.make_async_copy(rows(hf_hbm, nv), rows(xbuf.at[slot], nv), sem_in.at[slot]).wait()

        @pl.when(i + 1 < nused)
        def _():
            gather(src_nxt, 1 - slot, nv_ref[i + 1])

        @pl.when((i == 0) | (be_ref[i] != be_ref[jnp.maximum(i - 1, 0)]))
        def _():
            wgc[...] = wg_ref[...].astype(BF16)
            wuc[...] = wu_ref[...].astype(BF16)
            wdc[...] = wd_ref[...].astype(BF16)

        x = _load_tt(xbuf.at[slot], TM_MOE, BF16)
        gate = jnp.dot(x, wgc[...], preferred_element_type=F32)
        up = jnp.dot(x, wuc[...], preferred_element_type=F32)
        hid = (gate * _sigmoid(gate) * up).astype(BF16)

        @pl.when(i > 0)
        def _():
            wait_scatter(nv_ref[i - 1])

        _store_tt(obuf, jnp.dot(hid, wdc[...], preferred_element_type=F32), TM_MOE)

        def scatter_row(r):
            pltpu.make_async_copy(tile(obuf, r * TT_SUB), tile(y_hbm, dst_cur[0, r]), sem_out).start()

        for_rows(nv, scatter_row)

        @pl.when(i == nused - 1)
        def _():
            wait_scatter(nv)


def _moe(hf, plan, wg, wu, wd, layer):
    src, dst, block_e, nused, nvalid = plan
    wspec = lambda shape: pl.BlockSpec((None, None) + shape, lambda i, be, nu, nv: (layer, be[i], 0, 0))
    rows = lambda f: pl.BlockSpec((None, 1, TM_MOE), lambda i, be, nu, nv: (f(i), 0, 0), memory_space=pltpu.SMEM)
    return pl.pallas_call(
        _moe_body,
        grid_spec=pltpu.PrefetchScalarGridSpec(
            num_scalar_prefetch=3,
            grid=(N_BLOCKS,),
            in_specs=[rows(lambda i: i), rows(lambda i: jnp.minimum(i + 1, N_BLOCKS - 1)), rows(lambda i: i),
                      pl.BlockSpec(memory_space=pl.ANY),
                      wspec((D, D_EXPERT)), wspec((D, D_EXPERT)), wspec((D_EXPERT, D))],
            out_specs=pl.BlockSpec(memory_space=pl.ANY),
            scratch_shapes=[pltpu.VMEM((2, TM_MOE * TT_SUB, LANES), U32),
                            pltpu.VMEM((TM_MOE * TT_SUB, LANES), U32),
                            pltpu.VMEM((D, D_EXPERT), BF16), pltpu.VMEM((D, D_EXPERT), BF16),
                            pltpu.VMEM((D_EXPERT, D), BF16),
                            pltpu.SemaphoreType.DMA((2,)), pltpu.SemaphoreType.DMA]),
        out_shape=jax.ShapeDtypeStruct((2 * T * TT_SUB, LANES), U32),
        compiler_params=_params(("arbitrary",), MOE_VMEM_LIMIT),
        name="moe_experts",
    )(block_e, nused, nvalid, src.reshape(N_BLOCKS, 1, TM_MOE), src.reshape(N_BLOCKS, 1, TM_MOE),
      dst.reshape(N_BLOCKS, 1, TM_MOE), hf, wg, wu, wd)


INV_ROWS = 2048


def _invert_body(dest_ref, zeros_hbm, codes_ref, sem):
    i = pl.program_id(0)

    @pl.when(i == 0)
    def _():
        clear = pltpu.make_async_copy(zeros_hbm, codes_ref, sem)
        clear.start()
        clear.wait()

    def put(j, carry):
        codes_ref[dest_ref[0, j]] = i * INV_ROWS + j
        return carry

    lax.fori_loop(0, INV_ROWS, put, 0, unroll=16)


def _invert_slots(dest3):
    return pl.pallas_call(
        _invert_body,
        grid=(dest3.shape[0],),
        in_specs=[pl.BlockSpec((None, 1, INV_ROWS), lambda i: (i, 0, 0), memory_space=pltpu.SMEM),
                  pl.BlockSpec(memory_space=pl.ANY)],
        out_specs=pl.BlockSpec(memory_space=pltpu.SMEM),
        out_shape=jax.ShapeDtypeStruct((N_SLOTS,), jnp.int32),
        scratch_shapes=[pltpu.SemaphoreType.DMA],
        compiler_params=_params(("arbitrary",)),
        name="invert_slots",
    )(dest3, jnp.zeros((N_SLOTS,), jnp.int32))


def _dispatch_plan(info, cnt):
    e = info[:, 0:2].astype(jnp.int32)
    rank = info[:, 4:6].astype(jnp.int32)
    counts = cnt[0, EXPERT_LANE0:EXPERT_LANE0 + N_EXPERTS].astype(jnp.int32)
    padded = ((counts + TM_MOE - 1) // TM_MOE) * TM_MOE
    pends = jnp.cumsum(padded)
    pstarts = pends - padded
    first = jnp.sum(jnp.where(e[..., None] == jnp.arange(N_EXPERTS, dtype=jnp.int32), pstarts, 0), axis=-1)
    dest = first + rank
    codes = _invert_slots(dest.reshape(2 * T // INV_ROWS, 1, INV_ROWS))
    src = (codes >> 1) * TT_SUB
    dst = ((codes & 1) * T + (codes >> 1)) * TT_SUB
    block0 = jnp.arange(N_BLOCKS, dtype=jnp.int32) * TM_MOE
    block_e = jnp.clip(jnp.searchsorted(pends, block0, side='right', method='compare_all'),
                       0, N_EXPERTS - 1).astype(jnp.int32)
    nvalid = jnp.clip(pstarts[block_e] + counts[block_e] - block0, 0, TM_MOE).astype(jnp.int32)
    nused = (pends[-1] // TM_MOE).astype(jnp.int32).reshape(1)
    return src, dst, block_e, nused, nvalid


def _combined(x_ref, y0_ref, y1_ref, info_ref, g2_ref, rows):
    info = info_ref[...]
    lane = lax.broadcasted_iota(jnp.int32, info.shape, 1)
    w1 = jnp.sum(jnp.where(lane == 2, info, 0.0), axis=-1, keepdims=True)
    w2 = jnp.sum(jnp.where(lane == 3, info, 0.0), axis=-1, keepdims=True)
    y0 = _load_tt(y0_ref, rows, F32)
    y1 = _load_tt(y1_ref, rows, F32)
    return x_ref[...] + g2_ref[...] * (y0 * w1 + y1 * w2)


def _combine_body(x_ref, y0_ref, y1_ref, info_ref, g2_ref, nw_ref, *o_refs, final):
    x = _combined(x_ref, y0_ref, y1_ref, info_ref, g2_ref, TM_CMB)
    if final:
        ms = jnp.mean(x * x, axis=-1, keepdims=True)
        x = x * lax.rsqrt(ms + EPS) * nw_ref[...]

    def emit(ref):
        ref[...] = x

    _for_tile(o_refs, TM_CMB, emit)


def _combine(x1, y, info, mods, nw, *, final):
    tm = TM_CMB
    nt = T // tm
    if final:
        out_shape = [jax.ShapeDtypeStruct((T_S, D), F32), jax.ShapeDtypeStruct((T_P, D), F32)]
    else:
        out_shape = [jax.ShapeDtypeStruct((T, D), F32)]
    return pl.pallas_call(
        functools.partial(_combine_body, final=final),
        grid=(nt,),
        in_specs=[pl.BlockSpec((tm, D), lambda i: (i, 0)),
                  pl.BlockSpec((tm * TT_SUB, LANES), lambda i: (i, 0)),
                  pl.BlockSpec((tm * TT_SUB, LANES), lambda i: (i + nt, 0)),
                  pl.BlockSpec((tm, LANES), lambda i: (i, 0)),
                  _mod_spec(5, tm, 1),
                  pl.BlockSpec((1, D), lambda i: (0, 0))],
        out_specs=_row_specs(out_shape, tm, 1),
        out_shape=out_shape,
        compiler_params=_params(("arbitrary",)),
        name="combine_final" if final else "combine",
    )(x1, y, y, info, mods, nw.reshape(1, D))


def kernel(x_prompt, x_sample, c, state_ret_fwd, state_ret_bwd, state_lru_fwd, state_lru_bwd, c_ctx, w_mod, b_mod, norm_mix_w, norm_ffn_w, norm_final_w, w_in_even, w_out_even, gmlp_norm_w, gmlp_w_s, gmlp_b_s, ret_decay_fwd, ret_decay_bwd, ret_gn_w, w_in_odd, w_out_odd, conv_w, conv_b, conv_ln_w, conv_ln_b, lru_conv_w, lru_conv_b, lru_wa_fwd, lru_ba_fwd, lru_wx_fwd, lru_bx_fwd, lru_lam_fwd, lru_wa_bwd, lru_ba_bwd, lru_wx_bwd, lru_bx_bwd, lru_lam_bwd, router_grp_w, router_grp_b, router_exp_w, router_exp_b, moe_w_gate, moe_w_up, moe_w_down):
    xs = (x_sample.reshape(T_S, D), x_prompt.reshape(T_P, D))
    cond = jnp.concatenate([c, c_ctx[None, :], jnp.zeros((N_COND_PAD - B_S - 1, D), F32)], axis=0)
    m = _modulation(cond, w_mod, b_mod)
    mods_all = m.reshape(DEPTH, N_COND_PAD, N_MOD, D).transpose(0, 2, 1, 3)[:, :, :, None, :]

    ret_f = ret_b = lru_f = lru_b = None
    for l in range(DEPTH):
        mods = mods_all[l]
        if l % 2 == 0:
            e = l // 2
            p = _norm_in(xs, norm_mix_w[l], mods, w_in_even[e].astype(BF16), pack=False)
            b_full = jnp.repeat(gmlp_b_s[e].T.astype(F32), A_GDIM, axis=1)
            out_a = (_gmlp(p, gmlp_norm_w[e], gmlp_w_s[e].astype(BF16), b_full),)
            tabs = _retention_tables(ret_decay_fwd[e], ret_decay_bwd[e])
            cos, sin = _rope_tables()
            ob_s, _, _ = _retention(p, state_ret_fwd[:, e], state_ret_bwd[:, e], tabs, ret_gn_w[e], cos, sin,
                                    base_chunk=0, nb=B_S, nc=N_S // CHUNK, rope=True)
            zero_state = jnp.zeros((B_P, HEADS, HD, HD), F32)
            ob_p, ret_f, ret_b = _retention(p, zero_state, zero_state, tabs, ret_gn_w[e], cos, sin,
                                            base_chunk=T_S // CHUNK, nb=B_P, nc=N_P // CHUNK, rope=False)
            out_b = (ob_s, ob_p)
            w_out = w_out_even[e]
        else:
            o = l // 2
            p = _norm_in(xs, norm_mix_w[l], mods, w_in_odd[o].astype(BF16), pack=True)
            cw3 = jnp.concatenate([conv_w[o], jnp.zeros((1, W), F32)], axis=0).reshape(
                C_KSIZE + 1, N_STRIPS, LANES).transpose(1, 0, 2)
            conv_args = (cw3, conv_b[o], conv_ln_w[o], conv_ln_b[o])
            prm_f = _lru_params(lru_conv_w[o], lru_conv_b[o], lru_wa_fwd[o], lru_ba_fwd[o], lru_wx_fwd[o],
                                lru_bx_fwd[o], lru_lam_fwd[o])
            prm_b = _lru_params(lru_conv_w[o], lru_conv_b[o], lru_wa_bwd[o], lru_ba_bwd[o], lru_wx_bwd[o],
                                lru_bx_bwd[o], lru_lam_bwd[o])
            latent = dict(seq_row0=0, nseq=B_S, n=N_S)
            context = dict(seq_row0=T_S, nseq=B_P, n=N_P)
            out_a = (_od_conv(p, *conv_args, **latent), _od_conv(p, *conv_args, **context))
            yd_s, _, _ = _od_rglru(p, prm_f, prm_b, state_lru_fwd[:, o], state_lru_bwd[:, o], **latent)
            zero_h = jnp.zeros((B_P, W), F32)
            yd_p, lru_f, lru_b = _od_rglru(p, prm_f, prm_b, zero_h, zero_h, **context)
            out_b = (yd_s, yd_p)
            w_out = w_out_odd[o]

        rw = jnp.concatenate([router_grp_w[l], router_exp_w[l],
                              jnp.zeros((D, LANES - N_GROUPS - N_EXPERTS), F32)], axis=1)
        rw_hi = rw.astype(BF16)
        rw = jnp.concatenate([rw_hi, (rw - rw_hi.astype(F32)).astype(BF16)], axis=1)
        rb = jnp.concatenate([router_grp_b[l], router_exp_b[l],
                              jnp.zeros((LANES - N_GROUPS - N_EXPERTS,), F32)]).reshape(1, LANES)
        x1, hf, info, cnt = _out_proj(out_a, out_b, xs, mods, w_out.astype(BF16), norm_ffn_w[l], rw, rb,
                                      packed=(l % 2 == 1))
        y = _moe(hf, _dispatch_plan(info, cnt), moe_w_gate, moe_w_up, moe_w_down, l)
        xs = tuple(_combine(x1, y, info, mods, norm_final_w, final=(l == DEPTH - 1)))

    y_sample = xs[0].reshape(B_S, N_S, D)
    y_prompt = xs[1].reshape(B_P, N_P, D)
    return (y_prompt, y_sample, ret_f[:, None], ret_b[:, None], lru_f[:, None], lru_b[:, None])
```

```python
import functools

import jax
import jax.numpy as jnp
from jax import lax
from jax.experimental import pallas as pl
from jax.experimental.pallas import tpu as pltpu

F32 = jnp.float32
BF16 = jnp.bfloat16
U32 = jnp.uint32

D = 2048
B_P, N_P = 16, 256
B_S, N_S = 8, 4096
T_S = B_S * N_S
T_P = B_P * N_P
T = T_S + T_P
ROWS_PER_COND = 4096
N_COND_PAD = 16
DEPTH = 2
N_MOD = 6
EPS = 1e-6
GRID_W = 64
ROPE_BASE = 10000.0

W = 1024
HW = W // 2
HEADS = 8
HD = 128
CHUNK = 128
A_GROUPS = 4
A_GDIM = W // A_GROUPS
C_KSIZE = 31
D_KSIZE = 4
LRU_C = 8.0

N_GROUPS = 4
EXPERTS_PER_GROUP = 8
N_EXPERTS = 32
D_EXPERT = 512
LANES = 128
N_STRIPS = W // LANES
EXPERT_LANE0 = N_GROUPS

TM_IN = 512
TN_IN = 2048
TM_OUT = 512
TM_MOE = 512
N_SLOTS = 2 * T + N_EXPERTS * TM_MOE
N_BLOCKS = N_SLOTS // TM_MOE
TM_CMB = 512

VMEM_LIMIT = 56 * 1024 * 1024
MOE_VMEM_LIMIT = 60 * 1024 * 1024


def _params(sem, vmem=VMEM_LIMIT):
    return pltpu.CompilerParams(dimension_semantics=sem, vmem_limit_bytes=vmem)


def _cond_row(i, tm):
    return (i * tm) // ROWS_PER_COND


def _sigmoid(x):
    return 0.5 * jnp.tanh(0.5 * x) + 0.5


def _pack_pair(lo, hi, rounded=False):
    if not rounded:
        lo, hi = lo.astype(BF16).astype(F32), hi.astype(BF16).astype(F32)
    return (lax.bitcast_convert_type(lo, U32) >> 16) | lax.bitcast_convert_type(hi, U32)


def _unpack_pair(w, dtype):
    return (lax.bitcast_convert_type(w << 16, F32).astype(dtype),
            lax.bitcast_convert_type(w & U32(0xFFFF0000), F32).astype(dtype))


def _pack_halves(x):
    k = x.shape[1] // 2
    return _pack_pair(x[:, :k], x[:, k:])


def _unpack_halves(w, dtype):
    return jnp.concatenate(_unpack_pair(w, dtype), axis=1)


TT_SUB = 8


def _store_tt(ref, x, rows, rounded=False):
    for c in range(TT_SUB):
        ref[pl.ds(c, rows, stride=TT_SUB), :] = _pack_pair(x[:, c * LANES:(c + 1) * LANES],
                                                           x[:, W + c * LANES:W + (c + 1) * LANES], rounded)


def _load_tt(ref, rows, dtype):
    lo, hi = [], []
    for c in range(TT_SUB):
        a, b = _unpack_pair(ref[pl.ds(c, rows, stride=TT_SUB), :], dtype)
        lo.append(a)
        hi.append(b)
    return jnp.concatenate(lo + hi, axis=1)


def _row_specs(arrs, tm, ngrid):
    if len(arrs) == 1:
        maps = [lambda i: i]
    else:
        n_s = T_S // tm
        maps = [lambda i: jnp.minimum(i, n_s - 1), lambda i: jnp.maximum(i - n_s, 0)]
    width = arrs[0].shape[1]
    if ngrid == 1:
        return [pl.BlockSpec((tm, width), lambda i, f=f: (f(i), 0)) for f in maps]
    return [pl.BlockSpec((tm, width), lambda i, j, f=f: (f(i), 0)) for f in maps]


def _tile_value(refs, tm):
    if len(refs) == 1:
        return refs[0][...]
    return jnp.where(pl.program_id(0) < T_S // tm, refs[0][...], refs[1][...])


def _for_tile(refs, tm, fn):
    if len(refs) == 1:
        fn(refs[0])
        return
    i = pl.program_id(0)

    @pl.when(i < T_S // tm)
    def _():
        fn(refs[0])

    @pl.when(i >= T_S // tm)
    def _():
        fn(refs[1])


def _mod_body(c_ref, w_ref, b_ref, o_ref):
    c = c_ref[...]
    a = (c * _sigmoid(c)).astype(BF16)
    o_ref[...] = jnp.dot(a, w_ref[...].astype(BF16), preferred_element_type=F32) + b_ref[...]


def _modulation(cond, w_mod, b_mod):
    tn = 1024
    return pl.pallas_call(
        _mod_body,
        grid=(DEPTH, N_MOD * D // tn),
        in_specs=[pl.BlockSpec((N_COND_PAD, D), lambda l, j: (0, 0)),
                  pl.BlockSpec((None, D, tn), lambda l, j: (l, 0, j)),
                  pl.BlockSpec((None, 1, tn), lambda l, j: (l, 0, j))],
        out_specs=pl.BlockSpec((None, N_COND_PAD, tn), lambda l, j: (l, 0, j)),
        out_shape=jax.ShapeDtypeStruct((DEPTH, N_COND_PAD, N_MOD * D), F32),
        compiler_params=_params(("arbitrary", "arbitrary")),
        name="modulation",
    )(cond, w_mod, b_mod.reshape(DEPTH, 1, N_MOD * D))


def _mod_spec(k, tm, ngrid):
    if ngrid == 1:
        return pl.BlockSpec((None, None, 1, D), lambda i: (k, _cond_row(i, tm), 0, 0))
    return pl.BlockSpec((None, None, 1, D), lambda i, j: (k, _cond_row(i, tm), 0, 0))


def _norm_in_body(*refs, pack):
    nw_ref, sc_ref, sh_ref, w_ref, p_ref, h_scr = refs[-6:]

    @pl.when(pl.program_id(1) == 0)
    def _():
        x = _tile_value(refs[:-6], TM_IN)
        ms = jnp.mean(x * x, axis=-1, keepdims=True)
        h = x * lax.rsqrt(ms + EPS) * nw_ref[...]
        h = h * (1.0 + sc_ref[...]) + sh_ref[...]
        h_scr[...] = h.astype(BF16)

    r = jnp.dot(h_scr[...], w_ref[...], preferred_element_type=F32)
    if pack:
        p_ref[...] = jnp.concatenate([_pack_halves(r[:, k * W:(k + 1) * W]) for k in range(TN_IN // W)], axis=1)
    else:
        p_ref[...] = r.astype(BF16)


def _norm_in(xs, nw, mods, w_bf16, *, pack):
    n = w_bf16.shape[1]
    if pack:
        out_spec = pl.BlockSpec((TM_IN, TN_IN // 2), lambda i, j: (i, j))
        out_shape = jax.ShapeDtypeStruct((T, n // 2), U32)
    else:
        out_spec = pl.BlockSpec((TM_IN, TN_IN), lambda i, j: (i, j))
        out_shape = jax.ShapeDtypeStruct((T, n), BF16)
    return pl.pallas_call(
        functools.partial(_norm_in_body, pack=pack),
        grid=(T // TM_IN, n // TN_IN),
        in_specs=_row_specs(xs, TM_IN, 2) + [
            pl.BlockSpec((1, D), lambda i, j: (0, 0)),
            _mod_spec(1, TM_IN, 2),
            _mod_spec(0, TM_IN, 2),
            pl.BlockSpec((D, TN_IN), lambda i, j: (0, j))],
        out_specs=out_spec,
        out_shape=out_shape,
        scratch_shapes=[pltpu.VMEM((TM_IN, D), BF16)],
        compiler_params=_params(("arbitrary", "arbitrary")),
        name="norm_in",
    )(*xs, nw.reshape(1, D), mods, mods, w_bf16)


GMLP_ROWS = 512


def _gmlp_body(u_ref, v_ref, lnw_ref, ws_ref, b_ref, o_ref):
    for c in range(GMLP_ROWS // CHUNK):
        rows = pl.ds(c * CHUNK, CHUNK)
        v = jax.nn.gelu(v_ref[rows, :].astype(F32))
        mu = jnp.mean(v, axis=-1, keepdims=True)
        var = jnp.mean(jnp.square(v - mu), axis=-1, keepdims=True)
        vn = ((v - mu) * lax.rsqrt(var + EPS) * lnw_ref[...]).astype(BF16)
        z = jnp.concatenate(
            [jnp.dot(ws_ref[g], vn[:, g * A_GDIM:(g + 1) * A_GDIM], preferred_element_type=F32)
             for g in range(A_GROUPS)], axis=1) + b_ref[...]
        u = jax.nn.gelu(u_ref[rows, :].astype(F32))
        o_ref[rows, :] = (u * z).astype(BF16)


def _gmlp(p, lnw, ws_bf16, b_full):
    return pl.pallas_call(
        _gmlp_body,
        grid=(T // GMLP_ROWS,),
        in_specs=[pl.BlockSpec((GMLP_ROWS, W), lambda i: (i, 0)),
                  pl.BlockSpec((GMLP_ROWS, W), lambda i: (i, 1)),
                  pl.BlockSpec((1, W), lambda i: (0, 0)),
                  pl.BlockSpec((A_GROUPS, CHUNK, CHUNK), lambda i: (0, 0, 0)),
                  pl.BlockSpec((CHUNK, W), lambda i: (0, 0))],
        out_specs=pl.BlockSpec((GMLP_ROWS, W), lambda i: (i, 0)),
        out_shape=jax.ShapeDtypeStruct((T, W), BF16),
        compiler_params=_params(("arbitrary",)),
        name="gmlp",
    )(p, p, lnw.reshape(1, W), ws_bf16, b_full)


RET_SUB = 8


def _roped_qk(q_ref, k_ref, cos_ref, sin_ref, rows, h, rope):
    cols = slice(h * HD, (h + 1) * HD)
    q = q_ref[rows, cols].astype(F32)
    k = k_ref[rows, cols].astype(F32) * (HD ** -0.5)
    if rope:
        c = cos_ref[rows, :]
        s = sin_ref[rows, :]
        q = q * c + pltpu.roll(q, HD // 2, axis=1) * s
        k = k * c + pltpu.roll(k, HD // 2, axis=1) * s
    return q, k


def _kt_v(kd, v):
    return lax.dot_general(kd, v, (((0,), (0,)), ((), ())), preferred_element_type=F32)


def _ret_bwd_body(q_ref, k_ref, v_ref, cos_ref, sin_ref, s0_ref, dq_ref, dk_ref, dc_ref,
                  ob_ref, sfin_ref, s_scr, *, rope, nc, nsub):
    c = pl.program_id(1)

    @pl.when(c == 0)
    def _():
        s_scr[...] = s0_ref[...]

    for sub in reversed(range(nsub)):
        rows = slice(sub * CHUNK, (sub + 1) * CHUNK)
        for h in range(HEADS):
            cols = slice(h * HD, (h + 1) * HD)
            q, k = _roped_qk(q_ref, k_ref, cos_ref, sin_ref, rows, h, rope)
            v = v_ref[rows, cols]
            s = s_scr[h]
            ob_ref[rows, cols] = jnp.dot(q.astype(BF16), s.astype(BF16), preferred_element_type=F32) * dq_ref[h]
            kd = (k * dk_ref[h]).astype(BF16)
            s_scr[h] = s * dc_ref[h] + _kt_v(kd, v)

    @pl.when(c == nc - 1)
    def _():
        sfin_ref[...] = s_scr[...]


def _ret_fwd_body(q_ref, k_ref, v_ref, g_ref, cos_ref, sin_ref, ob_ref, s0_ref, m_ref, dq_ref, dk_ref, dc_ref,
                  gn_ref, o_ref, sfin_ref, s_scr, *, rope, nc, nsub):
    c = pl.program_id(1)

    @pl.when(c == 0)
    def _():
        s_scr[...] = s0_ref[...]

    for sub in range(nsub):
        rows = slice(sub * CHUNK, (sub + 1) * CHUNK)
        for h in range(HEADS):
            cols = slice(h * HD, (h + 1) * HD)
            q, k = _roped_qk(q_ref, k_ref, cos_ref, sin_ref, rows, h, rope)
            qb = q.astype(BF16)
            v = v_ref[rows, cols]
            s = s_scr[h]
            scores = lax.dot_general(qb, k.astype(BF16), (((1,), (1,)), ((), ())),
                                     preferred_element_type=F32) * m_ref[h]
            o = (jnp.dot(scores.astype(BF16), v, preferred_element_type=F32)
                 + jnp.dot(qb, s.astype(BF16), preferred_element_type=F32) * dq_ref[h]
                 + ob_ref[rows, cols])
            kd = (k * dk_ref[h]).astype(BF16)
            s_scr[h] = s * dc_ref[h] + _kt_v(kd, v)
            mu = jnp.mean(o, axis=-1, keepdims=True)
            var = jnp.mean(jnp.square(o - mu), axis=-1, keepdims=True)
            on = (o - mu) * lax.rsqrt(var + EPS) * gn_ref[:, cols]
            g = g_ref[rows, cols].astype(F32)
            o_ref[rows, cols] = (g * _sigmoid(g) * on).astype(BF16)

    @pl.when(c == nc - 1)
    def _():
        sfin_ref[...] = s_scr[...]


def _retention(p, s_f0, s_b0, tabs, gn_w, cos, sin, *, base_chunk, nb, nc, rope):
    m_tab, dq_f, dk_f, dc_f, dq_b, dk_b, dc_b = tabs
    nsub = min(RET_SUB, nc)
    rows = nsub * CHUNK
    ns = nc // nsub
    base = base_chunk // nsub
    state_spec = pl.BlockSpec((None, HEADS, HD, HD), lambda b, c: (b, 0, 0, 0))
    tab_spec = pl.BlockSpec((HEADS, HD, HD), lambda b, c: (0, 0, 0))
    dc_spec = pl.BlockSpec((HEADS, 1, HD), lambda b, c: (0, 0, 0))
    state_shape = jax.ShapeDtypeStruct((nb, HEADS, HD, HD), F32)

    def step(c, rev):
        return ns - 1 - c if rev else c

    def col(j, rev):
        return pl.BlockSpec((rows,W), lambda b, c: (base + b * ns + step(c, rev), j))

    def rope_spec(rev):
        if not rope:
            return pl.BlockSpec((rows,HD), lambda b, c: (0, 0))
        return pl.BlockSpec((rows,HD), lambda b, c: (step(c, rev), 0))

    def local(rev):
        return pl.BlockSpec((rows,W), lambda b, c: (b * ns + step(c, rev), 0))

    n_rows = nb * nc * CHUNK
    ob, s_b = pl.pallas_call(
        functools.partial(_ret_bwd_body, rope=rope, nc=ns, nsub=nsub),
        grid=(nb, ns),
        in_specs=[col(2, True), col(3, True), col(4, True), rope_spec(True), rope_spec(True),
                  state_spec, tab_spec, tab_spec, dc_spec],
        out_specs=[local(True), state_spec],
        out_shape=[jax.ShapeDtypeStruct((n_rows, W), F32), state_shape],
        scratch_shapes=[pltpu.VMEM((HEADS, HD, HD), F32)],
        compiler_params=_params(("arbitrary", "arbitrary")),
        name="retention_bwd",
    )(p, p, p, cos, sin, s_b0, dq_b, dk_b, dc_b)

    o, s_f = pl.pallas_call(
        functools.partial(_ret_fwd_body, rope=rope, nc=ns, nsub=nsub),
        grid=(nb, ns),
        in_specs=[col(2, False), col(3, False), col(4, False), col(5, False), rope_spec(False), rope_spec(False),
                  local(False), state_spec, tab_spec, tab_spec, tab_spec, dc_spec,
                  pl.BlockSpec((1, W), lambda b, c: (0, 0))],
        out_specs=[local(False), state_spec],
        out_shape=[jax.ShapeDtypeStruct((n_rows, W), BF16), state_shape],
        scratch_shapes=[pltpu.VMEM((HEADS, HD, HD), F32)],
        compiler_params=_params(("arbitrary", "arbitrary")),
        name="retention_fwd",
    )(p, p, p, p, cos, sin, ob, s_f0, m_tab, dq_f, dk_f, dc_f, gn_w.reshape(1, W))
    return o, s_f, s_b


def _retention_tables(decay_fwd, decay_bwd):
    lg_f = jax.nn.log_sigmoid(decay_fwd.astype(F32))[:, None, None]
    lg_b = jax.nn.log_sigmoid(decay_bwd.astype(F32))[:, None, None]
    pos = jnp.arange(CHUNK, dtype=F32)
    rel = pos[:, None] - pos[None, :]
    m_tab = (jnp.where(rel >= 0, jnp.exp(lg_f * jnp.maximum(rel, 0.0)), 0.0)
             + jnp.where(rel <= 0, jnp.exp(lg_b * jnp.maximum(-rel, 0.0)), 0.0))
    ones = jnp.ones((1, 1, HD), F32)
    col = pos[None, :, None]
    dq_f = jnp.exp(lg_f * (col + 1.0)) * ones
    dk_f = jnp.exp(lg_f * (CHUNK - 1.0 - col)) * ones
    dq_b = jnp.exp(lg_b * (CHUNK - col)) * ones
    dk_b = jnp.exp(lg_b * col) * ones
    dc_f = jnp.exp(lg_f * CHUNK) * ones
    dc_b = jnp.exp(lg_b * CHUNK) * ones
    return m_tab, dq_f, dk_f, dc_f, dq_b, dk_b, dc_b


def _rope_tables():
    rows = N_S // GRID_W
    row = jnp.repeat(jnp.arange(rows, dtype=F32), GRID_W)
    colp = jnp.tile(jnp.arange(GRID_W, dtype=F32), rows)
    quarter = HD // 4
    inv_freq = ROPE_BASE ** (-jnp.arange(quarter, dtype=F32) / quarter)
    ang = jnp.concatenate([row[:, None] * inv_freq, colp[:, None] * inv_freq], axis=-1)
    cos, sin = jnp.cos(ang), jnp.sin(ang)
    return jnp.concatenate([cos, cos], axis=-1), jnp.concatenate([-sin, sin], axis=-1)


OD_TT = 128
OD_G = 8
CONV_HALO = 16
LRU_HALO = 8


def _od_rows(seq_row0, n, g, tj):
    return [seq_row0 + (g * OD_G + b) * n + tj * OD_TT for b in range(OD_G)]


def _od_in_copies(p_hbm, buf, sem, *, col, halo, seq_row0, n, g, tj):
    nt = n // OD_TT
    out = []
    for b, base in enumerate(_od_rows(seq_row0, n, g, tj)):
        def src(r, k):
            return p_hbm.at[pl.ds(r, k), pl.ds(col * HW, HW)]

        out.append((None, pltpu.make_async_copy(src(base, OD_TT), buf.at[pl.ds(halo, OD_TT), b, :], sem)))
        if halo:
            out.append((tj > 0, pltpu.make_async_copy(src(base - halo, halo), buf.at[pl.ds(0, halo), b, :], sem)))
            out.append((tj < nt - 1, pltpu.make_async_copy(src(base + OD_TT, halo),
                                                           buf.at[pl.ds(halo + OD_TT, halo), b, :], sem)))
    return out


def _od_run(copies, op):
    for cond, cp in copies:
        fn = cp.start if op == "start" else cp.wait
        if cond is None:
            fn()
        else:
            pl.when(cond)(fn)


def _od_zero_halo(buf, halo, n, tj):
    nt = n // OD_TT
    zeros = jnp.zeros((halo, OD_G, HW), U32)

    @pl.when(tj == 0)
    def _():
        buf[0:halo] = zeros

    @pl.when(tj == nt - 1)
    def _():
        buf[halo + OD_TT:2 * halo + OD_TT] = zeros


def _od_out_copies(obuf, out_hbm, sem, *, seq_row0, n, g, tj):
    return [(None, pltpu.make_async_copy(obuf.at[:, b, :], out_hbm.at[pl.ds(base - seq_row0, OD_TT), :], sem))
            for b, base in enumerate(_od_rows(seq_row0, n, g, tj))]


def _od_unpack(buf, rows):
    return _unpack_halves(buf[...].reshape(rows, HW), F32)


def _od_pipeline(q, nq, in_copies, sem_slots=2):
    s = q % 2

    @pl.when(q == 0)
    def _():
        _od_run(in_copies(q, s), "start")

    _od_run(in_copies(q, s), "wait")

    @pl.when(q + 1 < nq)
    def _():
        _od_run(in_copies(q + 1, 1 - s), "start")

    return s


def _od_emit(q, nq, s, obuf, words, out_copies):
    @pl.when(q >= 2)
    def _():
        _od_run(out_copies(s), "wait")

    obuf[s] = words.reshape(OD_TT, OD_G, HW)
    _od_run(out_copies(s), "start")

    @pl.when(q == nq - 1)
    def _():
        _od_run(out_copies(s), "wait")

        @pl.when(q >= 1)
        def _():
            _od_run(out_copies(1 - s), "wait")


CONV_RB = 64


def _od_conv_body(p_hbm, cw_ref, cb_ref, lnw_ref, lnb_ref, yc_hbm, xin, glu, ybuf, obuf, sem_in, sem_out,
                  *, seq_row0, n, ng):
    nt = n // OD_TT
    nq = ng * nt
    q = pl.program_id(0)
    halo = CONV_HALO
    rows_in = (OD_TT + 2 * halo) * OD_G
    rows = OD_TT * OD_G

    def in_copies(step, slot):
        out = []
        for br in range(2):
            out += _od_in_copies(p_hbm, xin.at[br, slot], sem_in.at[slot], col=br, halo=halo,
                                 seq_row0=seq_row0, n=n, g=step // nt, tj=step % nt)
        return out

    s = _od_pipeline(q, nq, in_copies)
    g, tj = q // nt, q % nt
    for br in range(2):
        _od_zero_halo(xin.at[br, s], halo, n, tj)
    glu[...] = _od_unpack(xin.at[0, s], rows_in) * _sigmoid(_od_unpack(xin.at[1, s], rows_in))

    def strip(si, carry):
        lanes = pl.ds(pl.multiple_of(si * LANES, LANES), LANES)
        for rb in range(rows // CONV_RB):
            acc = jnp.zeros((CONV_RB // 8, 8, LANES), F32)
            for k in range(C_KSIZE):
                wk = cw_ref[si, pl.ds(k, 1), :]
                r0 = rb * CONV_RB + (halo - C_KSIZE // 2 + k) * OD_G
                acc = acc + glu[r0:r0 + CONV_RB, lanes].reshape(CONV_RB // 8, 8, LANES) * wk[None]
            ybuf[rb * CONV_RB:(rb + 1) * CONV_RB, lanes] = acc.reshape(CONV_RB, LANES)
        return carry

    lax.fori_loop(0, N_STRIPS, strip, 0)
    y = ybuf[...] + cb_ref[...]
    mu = jnp.mean(y, axis=-1, keepdims=True)
    var = jnp.mean(jnp.square(y - mu), axis=-1, keepdims=True)
    yn = (y - mu) * lax.rsqrt(var + EPS) * lnw_ref[...] + lnb_ref[...]
    yo = yn * _sigmoid(yn)

    def out_copies(slot):
        return _od_out_copies(obuf.at[slot], yc_hbm, sem_out.at[slot], seq_row0=seq_row0, n=n, g=g, tj=tj)

    _od_emit(q, nq, s, obuf, _pack_halves(yo), out_copies)


def _od_conv(p, cw3, cb, lnw, lnb, *, seq_row0, nseq, n):
    ng = nseq // OD_G
    nq = ng * (n // OD_TT)
    vec = pl.BlockSpec((1, W), lambda q: (0, 0))
    t_in = OD_TT + 2 * CONV_HALO
    return pl.pallas_call(
        functools.partial(_od_conv_body, seq_row0=seq_row0, n=n, ng=ng),
        grid=(nq,),
        in_specs=[pl.BlockSpec(memory_space=pl.ANY),
                  pl.BlockSpec((N_STRIPS, C_KSIZE + 1, LANES), lambda q: (0, 0, 0)), vec, vec, vec],
        out_specs=pl.BlockSpec(memory_space=pl.ANY),
        out_shape=jax.ShapeDtypeStruct((nseq * n, HW), U32),
        scratch_shapes=[pltpu.VMEM((2, 2, t_in, OD_G, HW), U32),
                        pltpu.VMEM((t_in * OD_G, W), F32),
                        pltpu.VMEM((OD_TT * OD_G, W), F32),
                        pltpu.VMEM((2, OD_TT, OD_G, HW), U32),
                        pltpu.SemaphoreType.DMA((2,)), pltpu.SemaphoreType.DMA((2,))],
        compiler_params=_params(("arbitrary",)),
        name="conv_module",
    )(p, cw3, cb.reshape(1, W), lnw.reshape(1, W), lnb.reshape(1, W))


def _od_gates(x, cw_ref, cb_ref, wa_ref, ba_ref, wx_ref, bx_ref, sp_ref, a_scr, b_scr):
    rows = OD_TT * OD_G
    xc = cb_ref[...]
    for k in range(D_KSIZE):
        r0 = (LRU_HALO - D_KSIZE // 2 + k) * OD_G
        xc = xc + cw_ref[k:k + 1, :] * x[r0:r0 + rows, :]
    xb = xc.astype(BF16)
    r = jnp.concatenate([jnp.dot(xb[:, h * HD:(h + 1) * HD], wa_ref[h], preferred_element_type=F32)
                         for h in range(HEADS)], axis=1) + ba_ref[...]
    g = jnp.concatenate([jnp.dot(xb[:, h * HD:(h + 1) * HD], wx_ref[h], preferred_element_type=F32)
                         for h in range(HEADS)], axis=1) + bx_ref[...]
    log_a = (-LRU_C) * _sigmoid(r) * sp_ref[...]
    th = jnp.tanh(log_a)
    one_minus_a2 = -2.0 * th / (1.0 - th)
    a_scr[...] = jnp.exp(log_a)
    b_scr[...] = jnp.sqrt(jnp.maximum(one_minus_a2, 0.0)) * (_sigmoid(g) * xc)


def _od_scan(a_scr, b_scr, out_ref, h, reverse):
    def step(t, h):
        tr = (OD_TT - 1 - t) if reverse else t
        rows = pl.ds(pl.multiple_of(tr * OD_G, OD_G), OD_G)
        h = a_scr[rows, :] * h + b_scr[rows, :]
        out_ref[rows, :] = h
        return h

    return lax.fori_loop(0, OD_TT, step, h, unroll=8)


def _od_lru_fwd_body(p_hbm, cw_ref, cb_ref, wa_ref, ba_ref, wx_ref, bx_ref, sp_ref, h0_ref,
                     hf_ref, hfin_ref, xin, a_scr, b_scr, h_scr, sem_in, *, seq_row0, n, ng):
    nt = n // OD_TT
    nq = ng * nt
    q = pl.program_id(0)

    def in_copies(step, slot):
        return _od_in_copies(p_hbm, xin.at[slot], sem_in.at[slot], col=2, halo=LRU_HALO,
                             seq_row0=seq_row0, n=n, g=step // nt, tj=step % nt)

    s = _od_pipeline(q, nq, in_copies)
    tj = q % nt
    _od_zero_halo(xin.at[s], LRU_HALO, n, tj)

    @pl.when(tj == 0)
    def _():
        h_scr[...] = h0_ref[...]

    x = _od_unpack(xin.at[s], (OD_TT + 2 * LRU_HALO) * OD_G)
    _od_gates(x, cw_ref, cb_ref, wa_ref, ba_ref, wx_ref, bx_ref, sp_ref, a_scr, b_scr)
    h = _od_scan(a_scr, b_scr, hf_ref, h_scr[...], False)
    h_scr[...] = h

    @pl.when(tj == nt - 1)
    def _():
        hfin_ref[...] = h


def _od_lru_bwd_body(p_hbm, hf_ref, cw_ref, cb_ref, wa_ref, ba_ref, wx_ref, bx_ref, sp_ref, h0_ref,
                     yd_hbm, hfin_ref, xin, gin, a_scr, b_scr, hb_scr, obuf, h_scr, sem_in, sem_out,
                     *, seq_row0, n, ng):
    nt = n // OD_TT
    nq = ng * nt
    q = pl.program_id(0)

    def in_copies(step, slot):
        where = dict(seq_row0=seq_row0, n=n, g=step // nt, tj=nt - 1 - step % nt)
        return (_od_in_copies(p_hbm, xin.at[slot], sem_in.at[slot], col=2, halo=LRU_HALO, **where)
                + _od_in_copies(p_hbm, gin.at[slot], sem_in.at[slot], col=3, halo=0, **where))

    s = _od_pipeline(q, nq, in_copies)
    g, tj = q // nt, nt - 1 - q % nt
    _od_zero_halo(xin.at[s], LRU_HALO, n, tj)

    @pl.when(tj == nt - 1)
    def _():
        h_scr[...] = h0_ref[...]

    x = _od_unpack(xin.at[s], (OD_TT + 2 * LRU_HALO) * OD_G)
    _od_gates(x, cw_ref, cb_ref, wa_ref, ba_ref, wx_ref, bx_ref, sp_ref, a_scr, b_scr)
    h = _od_scan(a_scr, b_scr, hb_scr, h_scr[...], True)
    h_scr[...] = h

    @pl.when(tj == 0)
    def _():
        hfin_ref[...] = h

    gd = _od_unpack(gin.at[s], OD_TT * OD_G)
    yd = (hf_ref[...] + hb_scr[...]) * jax.nn.gelu(gd)

    def out_copies(slot):
        return _od_out_copies(obuf.at[slot], yd_hbm, sem_out.at[slot], seq_row0=seq_row0, n=n, g=g, tj=tj)

    _od_emit(q, nq, s, obuf, _pack_halves(yd), out_copies)


def _od_rglru(p, prm_f, prm_b, h0_f, h0_b, *, seq_row0, nseq, n):
    ng = nseq // OD_G
    nt = n // OD_TT
    nq = ng * nt
    rows = OD_TT * OD_G
    vec = pl.BlockSpec((1, W), lambda q: (0, 0))
    wsp = pl.BlockSpec((HEADS, HD, HD), lambda q: (0, 0, 0))
    prm_specs = [pl.BlockSpec((8, W), lambda q: (0, 0)), vec, wsp, vec, wsp, vec, vec,
                 pl.BlockSpec((OD_G, W), lambda q: (q // nt, 0))]
    state_spec = pl.BlockSpec((OD_G, W), lambda q: (q // nt, 0))
    state_shape = jax.ShapeDtypeStruct((nseq, W), F32)
    t_in = OD_TT + 2 * LRU_HALO
    tile_f32 = pltpu.VMEM((rows, W), F32)

    hf, s_f = pl.pallas_call(
        functools.partial(_od_lru_fwd_body, seq_row0=seq_row0, n=n, ng=ng),
        grid=(nq,),
        in_specs=[pl.BlockSpec(memory_space=pl.ANY)] + prm_specs,
        out_specs=[pl.BlockSpec((rows, W), lambda q: (q, 0)), state_spec],
        out_shape=[jax.ShapeDtypeStruct((nq * rows, W), F32), state_shape],
        scratch_shapes=[pltpu.VMEM((2, t_in, OD_G, HW), U32), tile_f32, tile_f32, pltpu.VMEM((OD_G, W), F32),
                        pltpu.SemaphoreType.DMA((2,))],
        compiler_params=_params(("arbitrary",)),
        name="rglru_fwd",
    )(p, *prm_f, h0_f)

    yd, s_b = pl.pallas_call(
        functools.partial(_od_lru_bwd_body, seq_row0=seq_row0, n=n, ng=ng),
        grid=(nq,),
        in_specs=[pl.BlockSpec(memory_space=pl.ANY),
                  pl.BlockSpec((rows, W), lambda q: ((q // nt) * nt + nt - 1 - q % nt, 0))] + prm_specs,
        out_specs=[pl.BlockSpec(memory_space=pl.ANY), state_spec],
        out_shape=[jax.ShapeDtypeStruct((nseq * n, HW), U32), state_shape],
        scratch_shapes=[pltpu.VMEM((2, t_in, OD_G, HW), U32), pltpu.VMEM((2, OD_TT, OD_G, HW), U32),
                        tile_f32, tile_f32, tile_f32, pltpu.VMEM((2, OD_TT, OD_G, HW), U32),
                        pltpu.VMEM((OD_G, W), F32),
                        pltpu.SemaphoreType.DMA((2,)), pltpu.SemaphoreType.DMA((2,))],
        compiler_params=_params(("arbitrary",)),
        name="rglru_bwd",
    )(p, hf, *prm_b, h0_b)
    return yd, s_f, s_b


def _lru_params(cw, cb, wa, ba, wx, bx, lam):
    cw_pad = jnp.concatenate([cw, jnp.zeros((8 - D_KSIZE, W), F32)], axis=0)
    sp = jax.nn.softplus(-lam.astype(F32)).reshape(1, W)
    return (cw_pad, cb.reshape(1, W), wa.astype(BF16), ba.reshape(1, W), wx.astype(BF16), bx.reshape(1, W), sp)


def _out_body(*refs, n_a, n_b, n_x, packed):
    a_refs = refs[:n_a]
    b_refs = refs[n_a:n_a + n_b]
    x_refs = refs[n_a + n_b:n_a + n_b + n_x]
    (g1_ref, wo_ref, nw_ref, sc_ref, sh_ref, rw_ref, rb_ref,
     x1_ref, hf_ref, info_ref, cnt_ref, carry) = refs[n_a + n_b + n_x:]
    i = pl.program_id(0)

    @pl.when(i == 0)
    def _():
        carry[...] = jnp.zeros_like(carry)

    def branch(refs_, row0):
        v = _tile_value(refs_, TM_OUT)
        if packed:
            lo, hi = _unpack_pair(v, BF16)
            return (jnp.dot(lo, wo_ref[row0:row0 + HW, :], preferred_element_type=F32)
                    + jnp.dot(hi, wo_ref[row0 + HW:row0 + W, :], preferred_element_type=F32))
        return jnp.dot(v, wo_ref[row0:row0 + W, :], preferred_element_type=F32)

    x1 = _tile_value(x_refs, TM_OUT) + g1_ref[...] * (branch(a_refs, 0) + branch(b_refs, W))
    x1_ref[...] = x1
    ms = jnp.mean(x1 * x1, axis=-1, keepdims=True)
    hf = x1 * lax.rsqrt(ms + EPS) * nw_ref[...]
    hf = hf * (1.0 + sc_ref[...]) + sh_ref[...]
    hf_hi = hf.astype(BF16)
    hf_hi32 = hf_hi.astype(F32)
    _store_tt(hf_ref, hf_hi32, TM_OUT, rounded=True)

    hf_lo = (hf - hf_hi32).astype(BF16)
    l2 = jnp.dot(hf_hi, rw_ref[...], preferred_element_type=F32)
    logits = (l2[:, :LANES] + l2[:, LANES:]
              + jnp.dot(hf_lo, rw_ref[:, :LANES], preferred_element_type=F32) + rb_ref[...])
    lane = lax.broadcasted_iota(jnp.int32, (TM_OUT, LANES), 1).astype(F32)
    neg = -jnp.inf
    gmask = lane < N_GROUPS
    mg = jnp.max(jnp.where(gmask, logits, neg), axis=-1, keepdims=True)
    gstar = jnp.min(jnp.where(gmask & (logits == mg), lane, float(LANES)), axis=-1, keepdims=True)
    denom = jnp.sum(jnp.where(gmask, jnp.exp(logits - mg), 0.0), axis=-1, keepdims=True)
    psel = 1.0 / denom
    lo = EXPERT_LANE0 + EXPERTS_PER_GROUP * gstar
    emask = (lane >= lo) & (lane < lo + EXPERTS_PER_GROUP)
    v1 = jnp.max(jnp.where(emask, logits, neg), axis=-1, keepdims=True)
    i1 = jnp.min(jnp.where(emask & (logits == v1), lane, float(LANES)), axis=-1, keepdims=True)
    em2 = emask & (lane != i1)
    v2 = jnp.max(jnp.where(em2, logits, neg), axis=-1, keepdims=True)
    i2 = jnp.min(jnp.where(em2 & (logits == v2), lane, float(LANES)), axis=-1, keepdims=True)
    e21 = jnp.exp(v2 - v1)
    w1 = psel / (1.0 + e21)
    w2 = psel * e21 / (1.0 + e21)

    memb = jnp.where((lane == i1) | (lane == i2), 1.0, 0.0)
    r_i = lax.broadcasted_iota(jnp.int32, (TM_OUT, TM_OUT), 0)
    c_i = lax.broadcasted_iota(jnp.int32, (TM_OUT, TM_OUT), 1)
    lower = jnp.where(r_i > c_i, 1.0, 0.0).astype(BF16)
    before = jnp.dot(lower, memb.astype(BF16), preferred_element_type=F32) + carry[0:1, :]
    rank1 = jnp.sum(jnp.where(lane == i1, before, 0.0), axis=-1, keepdims=True)
    rank2 = jnp.sum(jnp.where(lane == i2, before, 0.0), axis=-1, keepdims=True)
    new_carry = carry[...] + jnp.sum(memb, axis=0, keepdims=True)
    carry[...] = new_carry
    cnt_ref[...] = new_carry

    e1 = i1 - EXPERT_LANE0
    e2 = i2 - EXPERT_LANE0
    info = jnp.where(lane == 0, e1, jnp.where(lane == 1, e2, jnp.where(lane == 2, w1, jnp.where(
        lane == 3, w2, jnp.where(lane == 4, rank1, jnp.where(lane == 5, rank2, 0.0))))))
    info_ref[...] = info


def _out_proj(a_parts, b_parts, xs, mods, wo_bf16, nw, rw, rb, *, packed):
    tm = TM_OUT
    row = lambda n: pl.BlockSpec((tm, n), lambda i: (i, 0))
    vec = pl.BlockSpec((1, D), lambda i: (0, 0))
    return pl.pallas_call(
        functools.partial(_out_body, n_a=len(a_parts), n_b=len(b_parts), n_x=len(xs), packed=packed),
        grid=(T // tm,),
        in_specs=_row_specs(a_parts, tm, 1) + _row_specs(b_parts, tm, 1) + _row_specs(xs, tm, 1) + [
            _mod_spec(2, tm, 1),
            pl.BlockSpec((D, D), lambda i: (0, 0)), vec, _mod_spec(4, tm, 1), _mod_spec(3, tm, 1),
            pl.BlockSpec((D, 2 * LANES), lambda i: (0, 0)), pl.BlockSpec((1, LANES), lambda i: (0, 0))],
        out_specs=[row(D), pl.BlockSpec((tm * TT_SUB, LANES), lambda i: (i, 0)), row(LANES),
                   pl.BlockSpec((8, LANES), lambda i: (0, 0))],
        out_shape=[jax.ShapeDtypeStruct((T, D), F32), jax.ShapeDtypeStruct((T * TT_SUB, LANES), U32),
                   jax.ShapeDtypeStruct((T, LANES), F32), jax.ShapeDtypeStruct((8, LANES), F32)],
        scratch_shapes=[pltpu.VMEM((8, LANES), F32)],
        compiler_params=_params(("arbitrary",)),
        name="out_proj_router",
    )(*a_parts, *b_parts, *xs, mods, wo_bf16, nw.reshape(1, D), mods, mods, rw, rb)


COPY_UNROLL = 32


def _moe_body(be_ref, nused_ref, nv_ref, src_cur, src_nxt, dst_cur, hf_hbm, wg_ref, wu_ref, wd_ref, y_hbm,
              xbuf, obuf, wgc, wuc, wdc, sem_in, sem_out):
    i = pl.program_id(0)
    nused = nused_ref[0]
    slot = i % 2

    def tile(ref, row8):
        return ref.at[pl.ds(pl.multiple_of(row8, TT_SUB), TT_SUB)]

    def rows(ref, n):
        return ref.at[pl.ds(0, n * TT_SUB)]

    def for_rows(n, fn):
        groups = n // COPY_UNROLL

        def group(c, carry):
            for u in range(COPY_UNROLL):
                fn(c * COPY_UNROLL + u, u % 2)
            return carry

        def single(r, carry):
            fn(r, 0)
            return carry

        lax.fori_loop(0, groups, group, 0)
        lax.fori_loop(groups * COPY_UNROLL, n, single, 0)

    def gather(src_ref, s, n):
        def row(r, lane):
            pltpu.make_async_copy(tile(hf_hbm, src_ref[0, r]), tile(xbuf.at[s], r * TT_SUB), sem_in.at[s]).start()

        for_rows(n, row)

    def wait_scatter(n):
        pltpu.make_async_copy(rows(obuf, n), rows(y_hbm, n), sem_out).wait()

    @pl.when(i < nused)
    def _():
        nv = nv_ref[i]

        @pl.when(i == 0)
        def _():
            xbuf[...] = jnp.zeros(xbuf.shape, U32)
            gather(src_cur, 0, nv)

        pltpu.make_async_copy(rows(hf_hbm, nv), rows(xbuf.at[slot], nv), sem_in.at[slot]).wait()

        @pl.when(i + 1 < nused)
        def _():
            gather(src_nxt, 1 - slot, nv_ref[i + 1])

        @pl.when((i == 0) | (be_ref[i] != be_ref[jnp.maximum(i - 1, 0)]))
        def _():
            wgc[...] = wg_ref[...].astype(BF16)
            wuc[...] = wu_ref[...].astype(BF16)
            wdc[...] = wd_ref[...].astype(BF16)

        x = _load_tt(xbuf.at[slot], TM_MOE, BF16)
        gate = jnp.dot(x, wgc[...], preferred_element_type=F32)
        up = jnp.dot(x, wuc[...], preferred_element_type=F32)
        hid = (gate * _sigmoid(gate) * up).astype(BF16)

        @pl.when(i > 0)
        def _():
            wait_scatter(nv_ref[i - 1])

        _store_tt(obuf, jnp.dot(hid, wdc[...], preferred_element_type=F32), TM_MOE)

        def scatter_row(r, lane):
            pltpu.make_async_copy(tile(obuf, r * TT_SUB), tile(y_hbm, dst_cur[0, r]), sem_out).start(priority=lane)

        for_rows(nv, scatter_row)

        @pl.when(i == nused - 1)
        def _():
            wait_scatter(nv)


def _moe(hf, plan, wg, wu, wd, layer):
    src, dst, block_e, nused, nvalid = plan
    wspec = lambda shape: pl.BlockSpec((None, None) + shape, lambda i, be, nu, nv: (layer, be[i], 0, 0))
    rows = lambda f: pl.BlockSpec((None, 1, TM_MOE), lambda i, be, nu, nv: (f(i), 0, 0), memory_space=pltpu.SMEM)
    return pl.pallas_call(
        _moe_body,
        grid_spec=pltpu.PrefetchScalarGridSpec(
            num_scalar_prefetch=3,
            grid=(N_BLOCKS,),
            in_specs=[rows(lambda i: i), rows(lambda i: jnp.minimum(i + 1, N_BLOCKS - 1)), rows(lambda i: i),
                      pl.BlockSpec(memory_space=pl.ANY),
                      wspec((D, D_EXPERT)), wspec((D, D_EXPERT)), wspec((D_EXPERT, D))],
            out_specs=pl.BlockSpec(memory_space=pl.ANY),
            scratch_shapes=[pltpu.VMEM((2, TM_MOE * TT_SUB, LANES), U32),
                            pltpu.VMEM((TM_MOE * TT_SUB, LANES), U32),
                            pltpu.VMEM((D, D_EXPERT), BF16), pltpu.VMEM((D, D_EXPERT), BF16),
                            pltpu.VMEM((D_EXPERT, D), BF16),
                            pltpu.SemaphoreType.DMA((2,)), pltpu.SemaphoreType.DMA]),
        out_shape=jax.ShapeDtypeStruct((2 * T * TT_SUB, LANES), U32),
        compiler_params=_params(("arbitrary",), MOE_VMEM_LIMIT),
        name="moe_experts",
    )(block_e, nused, nvalid, src.reshape(N_BLOCKS, 1, TM_MOE), src.reshape(N_BLOCKS, 1, TM_MOE),
      dst.reshape(N_BLOCKS, 1, TM_MOE), hf, wg, wu, wd)


INV_ROWS = 2048


def _invert_body(dest_ref, zeros_hbm, codes_ref, sem):
    i = pl.program_id(0)

    @pl.when(i == 0)
    def _():
        clear = pltpu.make_async_copy(zeros_hbm, codes_ref, sem)
        clear.start()
        clear.wait()

    def put(j, carry):
        codes_ref[dest_ref[0, j]] = i * INV_ROWS + j
        return carry

    lax.fori_loop(0, INV_ROWS, put, 0, unroll=16)


def _invert_slots(dest3):
    return pl.pallas_call(
        _invert_body,
        grid=(dest3.shape[0],),
        in_specs=[pl.BlockSpec((None, 1, INV_ROWS), lambda i: (i, 0, 0), memory_space=pltpu.SMEM),
                  pl.BlockSpec(memory_space=pl.ANY)],
        out_specs=pl.BlockSpec(memory_space=pltpu.SMEM),
        out_shape=jax.ShapeDtypeStruct((N_SLOTS,), jnp.int32),
        scratch_shapes=[pltpu.SemaphoreType.DMA],
        compiler_params=_params(("arbitrary",)),
        name="invert_slots",
    )(dest3, jnp.zeros((N_SLOTS,), jnp.int32))


def _dispatch_plan(info, cnt):
    e = info[:, 0:2].astype(jnp.int32)
    rank = info[:, 4:6].astype(jnp.int32)
    counts = cnt[0, EXPERT_LANE0:EXPERT_LANE0 + N_EXPERTS].astype(jnp.int32)
    padded = ((counts + TM_MOE - 1) // TM_MOE) * TM_MOE
    pends = jnp.cumsum(padded)
    pstarts = pends - padded
    first = jnp.sum(jnp.where(e[..., None] == jnp.arange(N_EXPERTS, dtype=jnp.int32), pstarts, 0), axis=-1)
    dest = first + rank
    codes = _invert_slots(dest.reshape(2 * T // INV_ROWS, 1, INV_ROWS))
    src = (codes >> 1) * TT_SUB
    dst = ((codes & 1) * T + (codes >> 1)) * TT_SUB
    block0 = jnp.arange(N_BLOCKS, dtype=jnp.int32) * TM_MOE
    block_e = jnp.clip(jnp.searchsorted(pends, block0, side='right', method='compare_all'),
                       0, N_EXPERTS - 1).astype(jnp.int32)
    nvalid = jnp.clip(pstarts[block_e] + counts[block_e] - block0, 0, TM_MOE).astype(jnp.int32)
    nused = (pends[-1] // TM_MOE).astype(jnp.int32).reshape(1)
    return src, dst, block_e, nused, nvalid


def _combined(x_ref, y0_ref, y1_ref, info_ref, g2_ref, rows):
    info = info_ref[...]
    lane = lax.broadcasted_iota(jnp.int32, info.shape, 1)
    w1 = jnp.sum(jnp.where(lane == 2, info, 0.0), axis=-1, keepdims=True)
    w2 = jnp.sum(jnp.where(lane == 3, info, 0.0), axis=-1, keepdims=True)
    y0 = _load_tt(y0_ref, rows, F32)
    y1 = _load_tt(y1_ref, rows, F32)
    return x_ref[...] + g2_ref[...] * (y0 * w1 + y1 * w2)


def _combine_body(x_ref, y0_ref, y1_ref, info_ref, g2_ref, nw_ref, *o_refs, final):
    x = _combined(x_ref, y0_ref, y1_ref, info_ref, g2_ref, TM_CMB)
    if final:
        ms = jnp.mean(x * x, axis=-1, keepdims=True)
        x = x * lax.rsqrt(ms + EPS) * nw_ref[...]

    def emit(ref):
        ref[...] = x

    _for_tile(o_refs, TM_CMB, emit)


def _combine(x1, y, info, mods, nw, *, final):
    tm = TM_CMB
    nt = T // tm
    if final:
        out_shape = [jax.ShapeDtypeStruct((T_S, D), F32), jax.ShapeDtypeStruct((T_P, D), F32)]
    else:
        out_shape = [jax.ShapeDtypeStruct((T, D), F32)]
    return pl.pallas_call(
        functools.partial(_combine_body, final=final),
        grid=(nt,),
        in_specs=[pl.BlockSpec((tm, D), lambda i: (i, 0)),
                  pl.BlockSpec((tm * TT_SUB, LANES), lambda i: (i, 0)),
                  pl.BlockSpec((tm * TT_SUB, LANES), lambda i: (i + nt, 0)),
                  pl.BlockSpec((tm, LANES), lambda i: (i, 0)),
                  _mod_spec(5, tm, 1),
                  pl.BlockSpec((1, D), lambda i: (0, 0))],
        out_specs=_row_specs(out_shape, tm, 1),
        out_shape=out_shape,
        compiler_params=_params(("arbitrary",)),
        name="combine_final" if final else "combine",
    )(x1, y, y, info, mods, nw.reshape(1, D))


def kernel(x_prompt, x_sample, c, state_ret_fwd, state_ret_bwd, state_lru_fwd, state_lru_bwd, c_ctx, w_mod, b_mod, norm_mix_w, norm_ffn_w, norm_final_w, w_in_even, w_out_even, gmlp_norm_w, gmlp_w_s, gmlp_b_s, ret_decay_fwd, ret_decay_bwd, ret_gn_w, w_in_odd, w_out_odd, conv_w, conv_b, conv_ln_w, conv_ln_b, lru_conv_w, lru_conv_b, lru_wa_fwd, lru_ba_fwd, lru_wx_fwd, lru_bx_fwd, lru_lam_fwd, lru_wa_bwd, lru_ba_bwd, lru_wx_bwd, lru_bx_bwd, lru_lam_bwd, router_grp_w, router_grp_b, router_exp_w, router_exp_b, moe_w_gate, moe_w_up, moe_w_down):
    xs = (x_sample.reshape(T_S, D), x_prompt.reshape(T_P, D))
    cond = jnp.concatenate([c, c_ctx[None, :], jnp.zeros((N_COND_PAD - B_S - 1, D), F32)], axis=0)
    m = _modulation(cond, w_mod, b_mod)
    mods_all = m.reshape(DEPTH, N_COND_PAD, N_MOD, D).transpose(0, 2, 1, 3)[:, :, :, None, :]

    ret_f = ret_b = lru_f = lru_b = None
    for l in range(DEPTH):
        mods = mods_all[l]
        if l % 2 == 0:
            e = l // 2
            p = _norm_in(xs, norm_mix_w[l], mods, w_in_even[e].astype(BF16), pack=False)
            b_full = jnp.repeat(gmlp_b_s[e].T.astype(F32), A_GDIM, axis=1)
            out_a = (_gmlp(p, gmlp_norm_w[e], gmlp_w_s[e].astype(BF16), b_full),)
            tabs = _retention_tables(ret_decay_fwd[e], ret_decay_bwd[e])
            cos, sin = _rope_tables()
            ob_s, _, _ = _retention(p, state_ret_fwd[:, e], state_ret_bwd[:, e], tabs, ret_gn_w[e], cos, sin,
                                    base_chunk=0, nb=B_S, nc=N_S // CHUNK, rope=True)
            zero_state = jnp.zeros((B_P, HEADS, HD, HD), F32)
            ob_p, ret_f, ret_b = _retention(p, zero_state, zero_state, tabs, ret_gn_w[e], cos, sin,
                                            base_chunk=T_S // CHUNK, nb=B_P, nc=N_P // CHUNK, rope=False)
            out_b = (ob_s, ob_p)
            w_out = w_out_even[e]
        else:
            o = l // 2
            p = _norm_in(xs, norm_mix_w[l], mods, w_in_odd[o].astype(BF16), pack=True)
            cw3 = jnp.concatenate([conv_w[o], jnp.zeros((1, W), F32)], axis=0).reshape(
                C_KSIZE + 1, N_STRIPS, LANES).transpose(1, 0, 2)
            conv_args = (cw3, conv_b[o], conv_ln_w[o], conv_ln_b[o])
            prm_f = _lru_params(lru_conv_w[o], lru_conv_b[o], lru_wa_fwd[o], lru_ba_fwd[o], lru_wx_fwd[o],
                                lru_bx_fwd[o], lru_lam_fwd[o])
            prm_b = _lru_params(lru_conv_w[o], lru_conv_b[o], lru_wa_bwd[o], lru_ba_bwd[o], lru_wx_bwd[o],
                                lru_bx_bwd[o], lru_lam_bwd[o])
            latent = dict(seq_row0=0, nseq=B_S, n=N_S)
            context = dict(seq_row0=T_S, nseq=B_P, n=N_P)
            out_a = (_od_conv(p, *conv_args, **latent), _od_conv(p, *conv_args, **context))
            yd_s, _, _ = _od_rglru(p, prm_f, prm_b, state_lru_fwd[:, o], state_lru_bwd[:, o], **latent)
            zero_h = jnp.zeros((B_P, W), F32)
            yd_p, lru_f, lru_b = _od_rglru(p, prm_f, prm_b, zero_h, zero_h, **context)
            out_b = (yd_s, yd_p)
            w_out = w_out_odd[o]

        rw = jnp.concatenate([router_grp_w[l], router_exp_w[l],
                              jnp.zeros((D, LANES - N_GROUPS - N_EXPERTS), F32)], axis=1)
        rw_hi = rw.astype(BF16)
        rw = jnp.concatenate([rw_hi, (rw - rw_hi.astype(F32)).astype(BF16)], axis=1)
        rb = jnp.concatenate([router_grp_b[l], router_exp_b[l],
                              jnp.zeros((LANES - N_GROUPS - N_EXPERTS,), F32)]).reshape(1, LANES)
        x1, hf, info, cnt = _out_proj(out_a, out_b, xs, mods, w_out.astype(BF16), norm_ffn_w[l], rw, rb,
                                      packed=(l % 2 == 1))
        y = _moe(hf, _dispatch_plan(info, cnt), moe_w_gate, moe_w_up, moe_w_down, l)
        xs = tuple(_combine(x1, y, info, mods, norm_final_w, final=(l == DEPTH - 1)))

    y_sample = xs[0].reshape(B_S, N_S, D)
    y_prompt = xs[1].reshape(B_P, N_P, D)
    return (y_prompt, y_sample, ret_f[:, None], ret_b[:, None], lru_f[:, None], lru_b[:, None])
```
